```python
import math
import jax, jax.numpy as jnp
from jax import lax
import numpy as np

D_MODEL = 2048
BATCH = 8
SEQ = 2048
DEPTH = 1

MOBA_HEADS = 8
MOBA_HEAD_DIM = 128
MOBA_W = MOBA_HEADS * MOBA_HEAD_DIM
MOBA_BLOCK = 256
MOBA_TOPK = 3
MOBA_Q_CHUNK = 16
REL_BUCKETS = 32
REL_MAX_DIST = 128
GDN_QK_HEADS = 8
GDN_V_HEADS = 16
GDN_HEAD_DIM = 128
GDN_QK_W = GDN_QK_HEADS * GDN_HEAD_DIM
GDN_V_W = GDN_V_HEADS * GDN_HEAD_DIM
GDN_CONV_CH = 2 * GDN_QK_W + GDN_V_W
GDN_CONV = 4
GDN_CHUNK = 64
N_IN = 3 * MOBA_W + GDN_CONV_CH + GDN_V_W + 2 * GDN_V_HEADS + 2 * D_MODEL
D_FF = -(-8 * D_MODEL // (3 * 256)) * 256
DEEPNORM_ALPHA = (2.0 * DEPTH) ** 0.25
DEEPNORM_BETA = (8.0 * DEPTH) ** -0.25
LN_EPS = 1e-5
RMS_EPS = 1e-6
NEG_INF = -1e30

kernel_name = 'hybrid_moba_gdn_deepnorm_adaln_layer'


def layer_norm(x, gain, bias):
    xf = x.astype(jnp.float32)
    mu = xf.mean(-1, keepdims=True)
    var = jnp.square(xf - mu).mean(-1, keepdims=True)
    return ((xf - mu) * lax.rsqrt(var + LN_EPS) * gain + bias).astype(x.dtype)


def l2norm(x):
    return x * lax.rsqrt(jnp.sum(x * x, -1, keepdims=True) + RMS_EPS)


def rel_bucket(dist):
    max_exact = REL_BUCKETS // 2
    n = jnp.maximum(dist, 0)
    nf = jnp.maximum(n, 1).astype(jnp.float32)
    large = max_exact + (jnp.log(nf / max_exact) / math.log(REL_MAX_DIST / max_exact)
                         * (REL_BUCKETS - max_exact)).astype(jnp.int32)
    large = jnp.minimum(large, REL_BUCKETS - 1)
    return jnp.where(n < max_exact, n, large)


def moba_attention(q, k, v, rel_bias):
    B, S, H, Dh = q.shape
    s_pad = -(-S // MOBA_BLOCK) * MOBA_BLOCK
    pad = ((0, 0), (0, s_pad - S), (0, 0), (0, 0))
    q, k, v = [jnp.pad(t, pad).transpose(0, 2, 1, 3) for t in (q, k, v)]
    nb = s_pad // MOBA_BLOCK
    topk = min(MOBA_TOPK, nb)
    kb = k.reshape(B, H, nb, MOBA_BLOCK, Dh)
    vb = v.reshape(B, H, nb, MOBA_BLOCK, Dh)
    k_mean = kb.astype(jnp.float32).mean(3)
    scale = Dh ** -0.5
    tab_h = rel_bias.T
    b_idx = jnp.arange(B)[:, None, None, None]
    h_idx = jnp.arange(H)[None, :, None, None]
    offs = jnp.arange(MOBA_BLOCK)

    def chunk(ci):
        start = ci * MOBA_Q_CHUNK
        qc = lax.dynamic_slice_in_dim(q, start, MOBA_Q_CHUNK, axis=2)
        qpos = start + jnp.arange(MOBA_Q_CHUNK)
        qblk = start // MOBA_BLOCK
        route = jnp.einsum('bhqd,bhnd->bhqn', qc.astype(jnp.float32), k_mean)
        route = jnp.where(jnp.arange(nb) < qblk, route, NEG_INF)
        _, sel = lax.top_k(route, topk)
        valid = sel < qblk
        k_sel = kb[b_idx, h_idx, sel]
        v_sel = vb[b_idx, h_idx, sel]
        kpos_sel = sel[..., None] * MOBA_BLOCK + offs
        s_sel = jnp.einsum('bhqd,bhqnkd->bhqnk', qc, k_sel).astype(jnp.float32) * scale
        s_sel = s_sel + tab_h[h_idx[..., None], rel_bucket(qpos[:, None, None] - kpos_sel)]
        s_sel = jnp.where(valid[..., None], s_sel, NEG_INF)
        k_own = lax.dynamic_slice_in_dim(k, qblk * MOBA_BLOCK, MOBA_BLOCK, axis=2)
        v_own = lax.dynamic_slice_in_dim(v, qblk * MOBA_BLOCK, MOBA_BLOCK, axis=2)
        dist = qpos[:, None] - (qblk * MOBA_BLOCK + offs)[None, :]
        s_own = jnp.einsum('bhqd,bhkd->bhqk', qc, k_own).astype(jnp.float32) * scale
        s_own = jnp.where(dist >= 0, s_own + tab_h[:, rel_bucket(dist)], NEG_INF)
        logits = jnp.concatenate([s_own, s_sel.reshape(B, H, MOBA_Q_CHUNK, topk * MOBA_BLOCK)], -1)
        p = jax.nn.softmax(logits, axis=-1).astype(v.dtype)
        p_own = p[..., :MOBA_BLOCK]
        p_sel = p[..., MOBA_BLOCK:].reshape(B, H, MOBA_Q_CHUNK, topk, MOBA_BLOCK)
        return (jnp.einsum('bhqk,bhkd->bhqd', p_own, v_own)
                + jnp.einsum('bhqnk,bhqnkd->bhqd', p_sel, v_sel))

    out = lax.map(chunk, jnp.arange(S // MOBA_Q_CHUNK))
    return out.transpose(1, 0, 3, 2, 4).reshape(B, S, H * Dh)


def causal_dwconv_silu(x, w):
    y = lax.conv_general_dilated(x, w[:, None, :], window_strides=(1,),
                                 padding=((GDN_CONV - 1, 0),),
                                 dimension_numbers=('NWC', 'WIO', 'NWC'),
                                 feature_group_count=x.shape[-1])
    return jax.nn.silu(y)


def gated_delta_rule(q, k, v, g, beta):
    B, H, S, Dk = k.shape
    Dv = v.shape[-1]
    C = GDN_CHUNK
    N = S // C
    q = q.reshape(B, H, N, C, Dk)
    k = k.reshape(B, H, N, C, Dk)
    v = v.reshape(B, H, N, C, Dv)
    beta = beta.reshape(B, H, N, C)
    g_cum = jnp.cumsum(g.reshape(B, H, N, C), axis=-1)
    tril = jnp.tril(jnp.ones((C, C), bool))
    strict = jnp.tril(jnp.ones((C, C), bool), -1)
    diff = g_cum[..., :, None] - g_cum[..., None, :]
    decay = jnp.where(tril, jnp.exp(jnp.where(tril, diff, 0.0)), 0.0)
    k_beta = k * beta[..., None]
    v_beta = v * beta[..., None]
    L = jnp.where(strict, jnp.einsum('bhncd,bhnjd->bhncj', k_beta, k) * decay, 0.0)
    A = L + jnp.eye(C, dtype=jnp.float32)
    rhs = jnp.concatenate([v_beta, k_beta * jnp.exp(g_cum)[..., None]], -1)
    sol = lax.linalg.triangular_solve(A, rhs, left_side=True, lower=True, unit_diagonal=True)
    u = sol[..., :Dv]
    w = sol[..., Dv:]
    attn = jnp.where(tril, jnp.einsum('bhncd,bhnjd->bhncj', q, k) * decay, 0.0)
    q_dec = q * jnp.exp(g_cum)[..., None]
    k_dec = k * jnp.exp(g_cum[..., -1:] - g_cum)[..., None]
    chunk_decay = jnp.exp(g_cum[..., -1])

    def step(state, xs):
        u_n, w_n, a_n, qd_n, kd_n, cd_n = xs
        v_new = u_n - jnp.einsum('bhcd,bhde->bhce', w_n, state)
        o = jnp.einsum('bhcd,bhde->bhce', qd_n, state) + jnp.einsum('bhcj,bhje->bhce', a_n, v_new)
        state = state * cd_n[..., None, None] + jnp.einsum('bhcd,bhce->bhde', kd_n, v_new)
        return state, o

    xs = tuple(jnp.moveaxis(t, 2, 0) for t in (u, w, attn, q_dec, k_dec, chunk_decay))
    state0 = jnp.zeros((B, H, Dk, Dv), jnp.float32)
    _, o = lax.scan(step, state0, xs)
    return jnp.moveaxis(o, 0, 2).reshape(B, H, S, Dv)


def token_mixer(h, w_in, conv_w, a_log, dt_bias, gdn_norm_w, rel_bias, w_proj_moba, w_proj_gdn, w_out):
    B, S, _ = h.shape
    f32 = jnp.float32
    proj = h @ w_in
    o1 = 3 * MOBA_W
    o2 = o1 + GDN_CONV_CH
    o3 = o2 + GDN_V_W
    o4 = o3 + GDN_V_HEADS
    o5 = o4 + GDN_V_HEADS
    moba_qkv, gdn_qkv, gdn_z, gdn_b, gdn_a, gate_raw = jnp.split(proj, [o1, o2, o3, o4, o5], axis=-1)
    qa, ka, va = [t.reshape(B, S, MOBA_HEADS, MOBA_HEAD_DIM) for t in jnp.split(moba_qkv, 3, axis=-1)]
    y_a = moba_attention(qa, ka, va, rel_bias)
    qkv = causal_dwconv_silu(gdn_qkv, conv_w)
    qb, kb, vb = jnp.split(qkv, [GDN_QK_W, 2 * GDN_QK_W], axis=-1)
    rep = GDN_V_HEADS // GDN_QK_HEADS
    qb = l2norm(qb.reshape(B, S, GDN_QK_HEADS, GDN_HEAD_DIM).astype(f32)) * GDN_HEAD_DIM ** -0.5
    kb = l2norm(kb.reshape(B, S, GDN_QK_HEADS, GDN_HEAD_DIM).astype(f32))
    qb = jnp.repeat(qb, rep, axis=2)
    kb = jnp.repeat(kb, rep, axis=2)
    vb = vb.reshape(B, S, GDN_V_HEADS, GDN_HEAD_DIM).astype(f32)
    beta = jax.nn.sigmoid(gdn_b.astype(f32))
    g = -jnp.exp(a_log.astype(f32)) * jax.nn.softplus(gdn_a.astype(f32) + dt_bias.astype(f32))
    o = gated_delta_rule(qb.transpose(0, 2, 1, 3), kb.transpose(0, 2, 1, 3), vb.transpose(0, 2, 1, 3),
                         g.transpose(0, 2, 1), beta.transpose(0, 2, 1)).transpose(0, 2, 1, 3)
    z = gdn_z.reshape(B, S, GDN_V_HEADS, GDN_HEAD_DIM).astype(f32)
    o = (o * lax.rsqrt(jnp.mean(o * o, -1, keepdims=True) + RMS_EPS) * gdn_norm_w.astype(f32)
         * jax.nn.silu(z))
    y_b = o.reshape(B, S, GDN_V_W).astype(h.dtype)
    gate_a, gate_b = jnp.split(gate_raw, 2, axis=-1)
    merged = jax.nn.sigmoid(gate_a) * (y_a @ w_proj_moba) + jax.nn.sigmoid(gate_b) * (y_b @ w_proj_gdn)
    return merged @ w_out


def swiglu(h, w_ffn_in, w_ffn_out):
    gate, up = jnp.split(h @ w_ffn_in, 2, axis=-1)
    return (jax.nn.silu(gate) * up) @ w_ffn_out


def setup_inputs(seed: int = 0) -> dict:
    key = jax.random.key(seed)
    ks = jax.random.split(key, 20)
    f32 = jnp.float32

    def nrm(k, shape, scale):
        return jax.random.normal(k, shape, f32) * scale

    x = nrm(ks[0], (BATCH, SEQ, D_MODEL), 1.0)
    c = nrm(ks[1], (BATCH, D_MODEL), 1.0)
    w_ada = nrm(ks[2], (DEPTH, D_MODEL, 6 * D_MODEL), D_MODEL ** -0.5)
    b_ada = nrm(ks[3], (DEPTH, 6 * D_MODEL), 0.02)
    w_in = nrm(ks[4], (DEPTH, D_MODEL, N_IN), D_MODEL ** -0.5)
    conv_w = nrm(ks[5], (DEPTH, GDN_CONV, GDN_CONV_CH), GDN_CONV ** -0.5)
    a_log = jnp.log(jax.random.uniform(ks[6], (DEPTH, GDN_V_HEADS), f32, 1.0, 16.0))
    dt = jnp.exp(jax.random.uniform(ks[7], (DEPTH, GDN_V_HEADS), f32, math.log(1e-3), math.log(1e-1)))
    dt_bias = dt + jnp.log(-jnp.expm1(-dt))
    gdn_norm_w = 1.0 + nrm(ks[8], (DEPTH, GDN_HEAD_DIM), 0.02)
    rel_bias = nrm(ks[9], (REL_BUCKETS, MOBA_HEADS), 0.5)
    w_proj_moba = nrm(ks[10], (DEPTH, MOBA_W, D_MODEL), MOBA_W ** -0.5)
    w_proj_gdn = nrm(ks[11], (DEPTH, GDN_V_W, D_MODEL), GDN_V_W ** -0.5)
    w_out = nrm(ks[12], (DEPTH, D_MODEL, D_MODEL), DEEPNORM_BETA * D_MODEL ** -0.5)
    ln1_g = 1.0 + nrm(ks[13], (DEPTH, D_MODEL), 0.02)
    ln1_b = nrm(ks[14], (DEPTH, D_MODEL), 0.02)
    w_ffn_in = nrm(ks[15], (DEPTH, D_MODEL, 2 * D_FF), D_MODEL ** -0.5)
    w_ffn_out = nrm(ks[16], (DEPTH, D_FF, D_MODEL), DEEPNORM_BETA * D_FF ** -0.5)
    ln2_g = 1.0 + nrm(ks[17], (DEPTH, D_MODEL), 0.02)
    ln2_b = nrm(ks[18], (DEPTH, D_MODEL), 0.02)
    return {'x': x, 'c': c, 'w_ada': w_ada, 'b_ada': b_ada, 'w_in': w_in, 'conv_w': conv_w,
            'a_log': a_log, 'dt_bias': dt_bias, 'gdn_norm_w': gdn_norm_w, 'rel_bias': rel_bias,
            'w_proj_moba': w_proj_moba, 'w_proj_gdn': w_proj_gdn, 'w_out': w_out,
            'ln1_g': ln1_g, 'ln1_b': ln1_b, 'w_ffn_in': w_ffn_in, 'w_ffn_out': w_ffn_out,
            'ln2_g': ln2_g, 'ln2_b': ln2_b}


def reference(x, c, w_ada, b_ada, w_in, conv_w, a_log, dt_bias, gdn_norm_w, rel_bias,
              w_proj_moba, w_proj_gdn, w_out, ln1_g, ln1_b, w_ffn_in, w_ffn_out, ln2_g, ln2_b):
    for l in range(DEPTH):
        mod = jax.nn.silu(c) @ w_ada[l] + b_ada[l]
        sh1, sc1, g1, sh2, sc2, g2 = jnp.split(mod[:, None, :], 6, axis=-1)
        h = x * (1.0 + sc1) + sh1
        y = token_mixer(h, w_in[l], conv_w[l], a_log[l], dt_bias[l], gdn_norm_w[l], rel_bias,
                        w_proj_moba[l], w_proj_gdn[l], w_out[l])
        x = layer_norm(DEEPNORM_ALPHA * x + g1 * y, ln1_g[l], ln1_b[l])
        h = x * (1.0 + sc2) + sh2
        y = swiglu(h, w_ffn_in[l], w_ffn_out[l])
        x = layer_norm(DEEPNORM_ALPHA * x + g2 * y, ln2_g[l], ln2_b[l])
    return x
```

```python
import functools
import math

import jax
import jax.numpy as jnp
from jax import lax
from jax.experimental import pallas as pl
from jax.experimental.pallas import tpu as pltpu

F32 = jnp.float32
BF16 = jnp.bfloat16

D_MODEL = 2048
MOBA_HEADS = 8
HEAD_DIM = 128
MOBA_W = MOBA_HEADS * HEAD_DIM
MOBA_BLOCK = 256
MOBA_TOPK = 3
REL_BUCKETS = 32
REL_MAX_DIST = 128
GDN_QK_HEADS = 8
GDN_V_HEADS = 16
GDN_QK_W = GDN_QK_HEADS * HEAD_DIM
GDN_V_W = GDN_V_HEADS * HEAD_DIM
GDN_CONV = 4
GDN_CHUNK = 64
D_FF = 5632
DEEPNORM_ALPHA = 2.0 ** 0.25
LN_EPS = 1e-5
RMS_EPS = 1e-6
NEG_INF = -1e30

COL_GATE_A = 0
COL_GATE_B = 2048
COL_GDN_V = 4096
COL_GDN_Z = 6144
COL_MOBA_Q = 8192
COL_MOBA_K = 9216
COL_MOBA_V = 10240
COL_GDN_Q = 11264
COL_GDN_K = 12288
N_MAIN = 13312
N_SMALL = 128

V7X_VMEM_BYTES = 64 * 1024 * 1024
LANES = 128


def _vmem(mb):
    return pltpu.CompilerParams(vmem_limit_bytes=mb * 1024 * 1024)


def _silu(x):
    return x * jax.nn.sigmoid(x)


def _layer_norm(r, gain, bias):
    mu = jnp.mean(r, axis=-1, keepdims=True)
    d = r - mu
    var = jnp.mean(d * d, axis=-1, keepdims=True)
    return d * lax.rsqrt(var + LN_EPS) * gain + bias


def _ada_kernel(c_ref, w_ref, b_ref, o_ref):
    sc = _silu(c_ref[...])
    o_ref[...] = jnp.dot(sc, w_ref[...], precision=lax.Precision.HIGHEST,
                         preferred_element_type=F32) + b_ref[...]


def _ada_mod(c, w_ada, b_ada):
    bsz = c.shape[0]
    n = w_ada.shape[1]
    tn = 1024
    return pl.pallas_call(
        _ada_kernel,
        name="ada_mod",
        grid=(n // tn,),
        in_specs=[pl.BlockSpec((bsz, D_MODEL), lambda j: (0, 0)),
                  pl.BlockSpec((D_MODEL, tn), lambda j: (0, j)),
                  pl.BlockSpec((1, tn), lambda j: (0, j))],
        out_specs=pl.BlockSpec((bsz, tn), lambda j: (0, j)),
        out_shape=jax.ShapeDtypeStruct((bsz, n), F32),
        compiler_params=_vmem(40),
    )(c, w_ada, b_ada.reshape(1, n))


def _inproj_kernel(x_ref, sh_ref, sc_ref, w_ref, ws_ref, o_ref, os_ref, h_ref):
    @pl.when(pl.program_id(1) == 0)
    def _():
        h = x_ref[...] * (1.0 + sc_ref[...]) + sh_ref[...]
        h_ref[...] = h.astype(BF16)
        os_ref[...] = jnp.dot(h_ref[...], ws_ref[...], preferred_element_type=F32)

    o_ref[...] = jnp.dot(h_ref[...], w_ref[...], preferred_element_type=F32)


def _in_proj(x2, mod3, w_main, w_small, seq):
    t = x2.shape[0]
    tm, tn = 1024, 1024
    per_b = seq // tm
    return pl.pallas_call(
        _inproj_kernel,
        name="in_proj",
        grid=(t // tm, N_MAIN // tn),
        in_specs=[pl.BlockSpec((tm, D_MODEL), lambda i, j: (i, 0)),
                  pl.BlockSpec((None, 1, D_MODEL), lambda i, j: (i // per_b, 0, 0)),
                  pl.BlockSpec((None, 1, D_MODEL), lambda i, j: (i // per_b, 0, 1)),
                  pl.BlockSpec((D_MODEL, tn), lambda i, j: (0, j)),
                  pl.BlockSpec((D_MODEL, N_SMALL), lambda i, j: (0, 0))],
        out_specs=[pl.BlockSpec((tm, tn), lambda i, j: (i, j)),
                   pl.BlockSpec((tm, N_SMALL), lambda i, j: (i, 0))],
        out_shape=[jax.ShapeDtypeStruct((t, N_MAIN), F32),
                   jax.ShapeDtypeStruct((t, N_SMALL), F32)],
        scratch_shapes=[pltpu.VMEM((tm, D_MODEL), BF16)],
        compiler_params=_vmem(48),
    )(x2, mod3, mod3, w_main, w_small)


def _moba_kernel(rel_ref, bko_ref, bkp_ref, q_ref, k_ref, v_ref, o_ref,
                 bias_own, bias_prev, kb_ref, vb_ref, *, nb):
    h = pl.program_id(0)
    blk = MOBA_BLOCK

    @pl.when(pl.program_id(1) == 0)
    def _():
        bo = bko_ref[...]
        bp = bkp_ref[...]
        acc_o = jnp.zeros((blk, blk), F32)
        acc_p = jnp.zeros((blk, blk), F32)
        for kk in range(REL_BUCKETS):
            val = rel_ref[kk, h]
            acc_o = jnp.where(bo == kk, val, acc_o)
            acc_p = jnp.where(bp == kk, val, acc_p)
        bias_own[...] = acc_o
        bias_prev[...] = acc_p

    bias_far = rel_ref[REL_BUCKETS - 1, h]
    scale = HEAD_DIM ** -0.5
    kf = k_ref[...]
    kmean = jnp.mean(kf.reshape(nb, blk, HEAD_DIM), axis=1)
    kb_ref[...] = kf.astype(BF16)
    vb_ref[...] = v_ref[...].astype(BF16)
    causal = (lax.broadcasted_iota(jnp.int32, (blk, blk), 0)
              >= lax.broadcasted_iota(jnp.int32, (blk, blk), 1))
    nt_dims = (((1,), (1,)), ((), ()))

    for i in range(nb):
        qi = q_ref[i * blk:(i + 1) * blk, :]
        qb = qi.astype(BF16)
        sel = None
        if i > MOBA_TOPK:
            route = lax.dot_general(qi, kmean, nt_dims, precision=lax.Precision.HIGHEST,
                                    preferred_element_type=F32)
            rc = [route[:, n:n + 1] for n in range(i)]
            sel = []
            for n in range(i):
                rank = jnp.zeros((blk, 1), jnp.int32)
                for m in range(i):
                    if m == n:
                        continue
                    beats = (rc[m] >= rc[n]) if m < n else (rc[m] > rc[n])
                    rank = rank + beats.astype(jnp.int32)
                sel.append(rank < MOBA_TOPK)
        s_list = []
        for n in range(i + 1):
            s = lax.dot_general(qb, kb_ref[n * blk:(n + 1) * blk, :], nt_dims,
                                preferred_element_type=F32) * scale
            if n == i:
                s = jnp.where(causal, s + bias_own[...], NEG_INF)
            else:
                s = s + (bias_prev[...] if n == i - 1 else bias_far)
                if sel is not None:
                    s = jnp.where(sel[n], s, NEG_INF)
            s_list.append(s)
        m_run = jnp.max(s_list[0], axis=-1, keepdims=True)
        for s in s_list[1:]:
            m_run = jnp.maximum(m_run, jnp.max(s, axis=-1, keepdims=True))
        l_run = jnp.zeros((blk, 1), F32)
        acc = jnp.zeros((blk, HEAD_DIM), F32)
        for n, s in enumerate(s_list):
            p = jnp.exp(s - m_run)
            l_run = l_run + jnp.sum(p, axis=-1, keepdims=True)
            acc = acc + jnp.dot(p.astype(BF16), vb_ref[n * blk:(n + 1) * blk, :],
                                preferred_element_type=F32)
        o_ref[i * blk:(i + 1) * blk, :] = (acc / l_run).astype(BF16)


def _moba(proj, rel_bias, bko, bkp, bsz, seq):
    nb = seq // MOBA_BLOCK
    qo, ko, vo = COL_MOBA_Q // HEAD_DIM, COL_MOBA_K // HEAD_DIM, COL_MOBA_V // HEAD_DIM
    blk = MOBA_BLOCK
    return pl.pallas_call(
        functools.partial(_moba_kernel, nb=nb),
        name="moba",
        grid=(MOBA_HEADS, bsz),
        in_specs=[pl.BlockSpec(memory_space=pltpu.SMEM),
                  pl.BlockSpec((blk, blk), lambda h, b: (0, 0)),
                  pl.BlockSpec((blk, blk), lambda h, b: (0, 0)),
                  pl.BlockSpec((seq, HEAD_DIM), lambda h, b: (b, qo + h)),
                  pl.BlockSpec((seq, HEAD_DIM), lambda h, b: (b, ko + h)),
                  pl.BlockSpec((seq, HEAD_DIM), lambda h, b: (b, vo + h))],
        out_specs=pl.BlockSpec((seq, HEAD_DIM), lambda h, b: (b, h)),
        out_shape=jax.ShapeDtypeStruct((bsz * seq, MOBA_W), BF16),
        scratch_shapes=[pltpu.VMEM((blk, blk), F32), pltpu.VMEM((blk, blk), F32),
                        pltpu.VMEM((seq, HEAD_DIM), BF16), pltpu.VMEM((seq, HEAD_DIM), BF16)],
        compiler_params=_vmem(48),
    )(rel_bias, bko, bkp, proj, proj, proj)


def _conv_silu(x, w):
    row = lax.broadcasted_iota(jnp.int32, x.shape, 0)
    acc = x * w[GDN_CONV - 1:GDN_CONV, :]
    for s in range(1, GDN_CONV):
        xs = jnp.where(row >= s, pltpu.roll(x, s, axis=0), 0.0)
        acc = acc + xs * w[GDN_CONV - 1 - s:GDN_CONV - s, :]
    return _silu(acc)


def _l2norm(x):
    return x * lax.rsqrt(jnp.sum(x * x, axis=-1, keepdims=True) + RMS_EPS)


def _softplus(x):
    return jnp.maximum(x, 0.0) + jnp.log1p(jnp.exp(-jnp.abs(x)))


def _chunk_cumsum_rows(g):
    pos = lax.broadcasted_iota(jnp.int32, g.shape, 0) % GDN_CHUNK
    sft = 1
    while sft < GDN_CHUNK:
        g = g + jnp.where(pos >= sft, pltpu.roll(g, sft, axis=0), 0.0)
        sft *= 2
    return g


SUBLANES = 8


def _unit_lower_inverse(lmat):
    n = lmat.shape[0]
    nv = n // SUBLANES
    rid = lax.broadcasted_iota(jnp.int32, (SUBLANES, n), 0)
    cid = lax.broadcasted_iota(jnp.int32, (SUBLANES, n), 1)
    t_rows = [(cid == rid + v * SUBLANES).astype(F32) for v in range(nv)]
    l_rows = [lmat[v * SUBLANES:(v + 1) * SUBLANES, :] for v in range(nv)]
    for m in range(n - 1):
        v0, s0 = divmod(m, SUBLANES)
        row = t_rows[v0][s0:s0 + 1, :]
        for v in range(v0 if s0 < SUBLANES - 1 else v0 + 1, nv):
            t_rows[v] = t_rows[v] - l_rows[v][:, m:m + 1] * row
    return jnp.concatenate(t_rows, axis=0)


def _gdn_kernel(alog_ref, dtb_ref, nw_ref, cwq_ref, cwk_ref, cwv_ref,
                q_ref, k_ref, v_ref, z_ref, sm_ref, rw_ref, y_ref,
                kn_s, qn_s, vb_s, kbg_s, gcc_s, gcr_s, beta_s, wq_s, u_s, at_s, st_s, *, nchunk):
    hq = pl.program_id(1)
    c64 = GDN_CHUNK
    seq = q_ref.shape[0]

    qn = _l2norm(_conv_silu(q_ref[...], cwq_ref[...])) * (HEAD_DIM ** -0.5)
    kn = _l2norm(_conv_silu(k_ref[...], cwk_ref[...]))
    qn_s[...] = qn
    kn_s[...] = kn
    vc = _conv_silu(v_ref[...], cwv_ref[...])
    sm = sm_ref[...]
    lane = lax.broadcasted_iota(jnp.int32, sm.shape, 1)
    tri_u = (lax.broadcasted_iota(jnp.int32, (c64, c64), 0)
             <= lax.broadcasted_iota(jnp.int32, (c64, c64), 1)).astype(F32)
    for j in range(2):
        hv = 2 * hq + j
        b_col = jnp.sum(jnp.where(lane == hv, sm, 0.0), axis=-1, keepdims=True)
        a_col = jnp.sum(jnp.where(lane == GDN_V_HEADS + hv, sm, 0.0), axis=-1, keepdims=True)
        a_neg = -jnp.exp(jnp.full((seq, 1), alog_ref[hv], F32))
        beta = jax.nn.sigmoid(b_col)
        g_col = a_neg * _softplus(a_col + dtb_ref[hv])
        gc_col = _chunk_cumsum_rows(g_col)
        eg = jnp.exp(gc_col)
        beta_s[j] = beta
        gcc_s[j] = gc_col
        vb_s[j] = (vc[:, j * HEAD_DIM:(j + 1) * HEAD_DIM] * beta).astype(BF16)
        kbg_s[j] = (kn * (beta * eg)).astype(BF16)
        wq_s[j, :, c64:, :] = (qn * eg).astype(BF16).reshape(nchunk, c64, HEAD_DIM)
        a_row = rw_ref[2 + j]
        a_neg_r = -jnp.exp(jnp.full((nchunk, c64), alog_ref[hv], F32))
        g_row = a_neg_r * _softplus(a_row + dtb_ref[hv])
        gcr_s[j] = jnp.dot(g_row, tri_u, precision=lax.Precision.HIGHEST,
                           preferred_element_type=F32)

    rows = lax.broadcasted_iota(jnp.int32, (c64, c64), 0)
    cols = lax.broadcasted_iota(jnp.int32, (c64, c64), 1)
    tril = rows >= cols
    strict = rows > cols
    nt_dims = (((1,), (1,)), ((), ()))

    def phase_b(c, carry):
        r = pl.ds(pl.multiple_of(c * c64, c64), c64)
        knc = kn_s[r, :].astype(BF16)
        qnc = qn_s[r, :].astype(BF16)
        kk = lax.dot_general(knc, knc, nt_dims, preferred_element_type=F32)
        qk = lax.dot_general(qnc, knc, nt_dims, preferred_element_type=F32)
        for j in range(2):
            gcc = gcc_s[j, r, :]
            gcr = gcr_s[j, pl.ds(c, 1), :]
            dec = jnp.exp(jnp.where(tril, gcc - gcr, NEG_INF))
            lmat = jnp.where(strict, kk * dec, 0.0) * beta_s[j, r, :]
            at_s[j, r, :] = (qk * dec).astype(BF16)
            xb = _unit_lower_inverse(lmat).astype(BF16)
            u_s[j, r, :] = jnp.dot(xb, vb_s[j, r, :], preferred_element_type=F32)
            wq_s[j, c, :c64, :] = jnp.dot(xb, kbg_s[j, r, :],
                                          preferred_element_type=F32).astype(BF16)
        return carry

    lax.fori_loop(0, nchunk, phase_b, 0, unroll=2)

    st_s[...] = jnp.zeros_like(st_s)
    nw = nw_ref[...]

    def phase_c(c, carry):
        r = pl.ds(pl.multiple_of(c * c64, c64), c64)
        knc = kn_s[r, :]
        for j in range(2):
            st = st_s[j]
            stb = st.astype(BF16)
            wqs = jnp.dot(wq_s[j, c], stb, preferred_element_type=F32)
            v_new = u_s[j, r, :] - wqs[:c64]
            vnb = v_new.astype(BF16)
            o = wqs[c64:] + jnp.dot(at_s[j, r, :], vnb, preferred_element_type=F32)
            gcc = gcc_s[j, r, :]
            gl = gcr_s[j, pl.ds(c, 1), c64 - 1:c64]
            kd = (knc * jnp.exp(gl - gcc)).astype(BF16)
            st_s[j] = st * jnp.exp(gl) + lax.dot_general(
                kd, vnb, (((0,), (0,)), ((), ())), preferred_element_type=F32)
            zc = z_ref[r, j * HEAD_DIM:(j + 1) * HEAD_DIM]
            og = (o * lax.rsqrt(jnp.mean(o * o, axis=-1, keepdims=True) + RMS_EPS)
                  * nw * _silu(zc))
            y_ref[r, j * HEAD_DIM:(j + 1) * HEAD_DIM] = og.astype(BF16)
        return carry

    lax.fori_loop(0, nchunk, phase_c, 0)


def _gdn(proj, small, rows, conv_w, a_log, dt_bias, norm_w, bsz, seq):
    nchunk = seq // GDN_CHUNK
    hd = HEAD_DIM
    qo, ko = COL_GDN_Q // hd, COL_GDN_K // hd
    vo, zo = COL_GDN_V // (2 * hd), COL_GDN_Z // (2 * hd)
    cvo = (2 * GDN_QK_W) // (2 * hd)
    smem = pl.BlockSpec(memory_space=pltpu.SMEM)
    return pl.pallas_call(
        functools.partial(_gdn_kernel, nchunk=nchunk),
        name="gdn",
        grid=(bsz, GDN_QK_HEADS),
        in_specs=[smem, smem,
                  pl.BlockSpec((1, hd), lambda b, h: (0, 0)),
                  pl.BlockSpec((GDN_CONV, hd), lambda b, h: (0, h)),
                  pl.BlockSpec((GDN_CONV, hd), lambda b, h: (0, GDN_QK_HEADS + h)),
                  pl.BlockSpec((GDN_CONV, 2 * hd), lambda b, h: (0, cvo + h)),
                  pl.BlockSpec((seq, hd), lambda b, h: (b, qo + h)),
                  pl.BlockSpec((seq, hd), lambda b, h: (b, ko + h)),
                  pl.BlockSpec((seq, 2 * hd), lambda b, h: (b, vo + h)),
                  pl.BlockSpec((seq, 2 * hd), lambda b, h: (b, zo + h)),
                  pl.BlockSpec((seq, N_SMALL), lambda b, h: (b, 0)),
                  pl.BlockSpec((None, None, 4, nchunk, GDN_CHUNK), lambda b, h: (b, h, 0, 0, 0))],
        out_specs=pl.BlockSpec((seq, 2 * hd), lambda b, h: (b, h)),
        out_shape=jax.ShapeDtypeStruct((bsz * seq, GDN_V_W), BF16),
        scratch_shapes=[pltpu.VMEM((seq, hd), F32),
                        pltpu.VMEM((seq, hd), F32),
                        pltpu.VMEM((2, seq, hd), BF16),
                        pltpu.VMEM((2, seq, hd), BF16),
                        pltpu.VMEM((2, seq, 1), F32),
                        pltpu.VMEM((2, nchunk, GDN_CHUNK), F32),
                        pltpu.VMEM((2, seq, 1), F32),
                        pltpu.VMEM((2, nchunk, 2 * GDN_CHUNK, hd), BF16),
                        pltpu.VMEM((2, seq, hd), F32),
                        pltpu.VMEM((2, seq, GDN_CHUNK), BF16),
                        pltpu.VMEM((2, hd, hd), F32)],
        compiler_params=_vmem(56),
    )(a_log, dt_bias, norm_w, conv_w, conv_w, conv_w, proj, proj, proj, proj, small, rows)


def _merge_kernel(ya_ref, yb_ref, ga_ref, gb_ref, x_ref, g1_ref, sh2_ref, sc2_ref,
                  lng_ref, lnb_ref, wpm_ref, wpg_ref, wo_ref, x1_ref, h2_ref):
    pa = jnp.dot(ya_ref[...], wpm_ref[...], preferred_element_type=F32)
    pb = jnp.dot(yb_ref[...], wpg_ref[...], preferred_element_type=F32)
    merged = jax.nn.sigmoid(ga_ref[...]) * pa + jax.nn.sigmoid(gb_ref[...]) * pb
    y = jnp.dot(merged.astype(BF16), wo_ref[...], preferred_element_type=F32)
    x1 = _layer_norm(DEEPNORM_ALPHA * x_ref[...] + g1_ref[...] * y, lng_ref[...], lnb_ref[...])
    x1_ref[...] = x1
    h2_ref[...] = (x1 * (1.0 + sc2_ref[...]) + sh2_ref[...]).astype(BF16)


def _merge(ya, yb, proj, x2, mod3, ln_g, ln_b, wpm, wpg, wo, seq):
    t = x2.shape[0]
    tm = 256
    per_b = seq // tm
    d = D_MODEL

    def modspec(k):
        return pl.BlockSpec((None, 1, d), lambda i: (i // per_b, 0, k))

    def const(shape):
        return pl.BlockSpec(shape, lambda i: (0, 0), pipeline_mode=pl.Buffered(1))

    return pl.pallas_call(
        _merge_kernel,
        name="merge",
        grid=(t // tm,),
        in_specs=[pl.BlockSpec((tm, MOBA_W), lambda i: (i, 0)),
                  pl.BlockSpec((tm, GDN_V_W), lambda i: (i, 0)),
                  pl.BlockSpec((tm, d), lambda i: (i, COL_GATE_A // d)),
                  pl.BlockSpec((tm, d), lambda i: (i, COL_GATE_B // d)),
                  pl.BlockSpec((tm, d), lambda i: (i, 0)),
                  modspec(2), modspec(3), modspec(4),
                  const((1, d)), const((1, d)),
                  const((MOBA_W, d)), const((GDN_V_W, d)), const((d, d))],
        out_specs=[pl.BlockSpec((tm, d), lambda i: (i, 0)),
                   pl.BlockSpec((tm, d), lambda i: (i, 0))],
        out_shape=[jax.ShapeDtypeStruct((t, d), F32), jax.ShapeDtypeStruct((t, d), BF16)],
        compiler_params=_vmem(56),
    )(ya, yb, proj, proj, x2, mod3, mod3, mod3, ln_g, ln_b, wpm, wpg, wo)


def _ffn_kernel(h_ref, x1_ref, g2_ref, lng_ref, lnb_ref, wg_ref, wu_ref, wo_ref, o_ref, acc_ref):
    f = pl.program_id(1)

    @pl.when(f == 0)
    def _():
        acc_ref[...] = jnp.zeros_like(acc_ref)

    h = h_ref[...]
    gate = jnp.dot(h, wg_ref[...], preferred_element_type=F32)
    up = jnp.dot(h, wu_ref[...], preferred_element_type=F32)
    act = (_silu(gate) * up).astype(BF16)
    acc_ref[...] += jnp.dot(act, wo_ref[...], preferred_element_type=F32)

    @pl.when(f == pl.num_programs(1) - 1)
    def _():
        r = DEEPNORM_ALPHA * x1_ref[...] + g2_ref[...] * acc_ref[...]
        o_ref[...] = _layer_norm(r, lng_ref[...], lnb_ref[...])


def _ffn(h2, x1, mod3, ln_g, ln_b, w_in, w_out, seq):
    t = h2.shape[0]
    tm, tf = 512, 512
    per_b = seq // tm
    d = D_MODEL
    nf = D_FF // tf
    return pl.pallas_call(
        _ffn_kernel,
        name="ffn",
        grid=(t // tm, nf),
        in_specs=[pl.BlockSpec((tm, d), lambda i, f: (i, 0)),
                  pl.BlockSpec((tm, d), lambda i, f: (i, 0)),
                  pl.BlockSpec((None, 1, d), lambda i, f: (i // per_b, 0, 5)),
                  pl.BlockSpec((1, d), lambda i, f: (0, 0)),
                  pl.BlockSpec((1, d), lambda i, f: (0, 0)),
                  pl.BlockSpec((d, tf), lambda i, f: (0, f)),
                  pl.BlockSpec((d, tf), lambda i, f: (0, nf + f)),
                  pl.BlockSpec((tf, d), lambda i, f: (f, 0))],
        out_specs=pl.BlockSpec((tm, d), lambda i, f: (i, 0)),
        out_shape=jax.ShapeDtypeStruct((t, d), F32),
        scratch_shapes=[pltpu.VMEM((tm, d), F32)],
        compiler_params=_vmem(48),
    )(h2, x1, mod3, ln_g, ln_b, w_in, w_in, w_out)


def _rel_bucket(dist):
    max_exact = REL_BUCKETS // 2
    n = jnp.maximum(dist, 0)
    nf = jnp.maximum(n, 1).astype(F32)
    large = max_exact + (jnp.log(nf / max_exact) / math.log(REL_MAX_DIST / max_exact)
                         * (REL_BUCKETS - max_exact)).astype(jnp.int32)
    large = jnp.minimum(large, REL_BUCKETS - 1)
    return jnp.where(n < max_exact, n, large)


def _layer(x, c, w_ada, b_ada, w_in, conv_w, a_log, dt_bias, gdn_norm_w, rel_bias,
           w_proj_moba, w_proj_gdn, w_out, ln1_g, ln1_b, w_ffn_in, w_ffn_out, ln2_g, ln2_b):
    bsz, seq, d = x.shape
    t = bsz * seq
    x2 = x.reshape(t, d)

    mod = _ada_mod(c, w_ada, b_ada)
    mod3 = mod.reshape(bsz, 1, 6 * d)

    o1 = 3 * MOBA_W
    o2 = o1 + 2 * GDN_QK_W + GDN_V_W
    o3 = o2 + GDN_V_W
    o5 = o3 + 2 * GDN_V_HEADS
    w_main = jnp.concatenate([
        w_in[:, o5:],
        w_in[:, o1 + 2 * GDN_QK_W:o2],
        w_in[:, o2:o3],
        w_in[:, :o1],
        w_in[:, o1:o1 + 2 * GDN_QK_W],
    ], axis=1).astype(BF16)
    w_small = jnp.pad(w_in[:, o3:o5], ((0, 0), (0, N_SMALL - 2 * GDN_V_HEADS))).astype(BF16)

    proj, small = _in_proj(x2, mod3, w_main, w_small, seq)

    ii = jnp.arange(MOBA_BLOCK, dtype=jnp.int32)
    dist = ii[:, None] - ii[None, :]
    bko = _rel_bucket(dist)
    bkp = _rel_bucket(dist + MOBA_BLOCK)
    ya = _moba(proj, rel_bias, bko, bkp, bsz, seq)

    nchunk = seq // GDN_CHUNK
    sm_t = small[:, :2 * GDN_V_HEADS].reshape(bsz, seq, 2, GDN_QK_HEADS, 2)
    rows = sm_t.transpose(0, 3, 2, 4, 1).reshape(bsz, GDN_QK_HEADS, 4, nchunk, GDN_CHUNK)
    yb = _gdn(proj, small, rows, conv_w, a_log, dt_bias, gdn_norm_w.reshape(1, HEAD_DIM),
              bsz, seq)

    x1, h2 = _merge(ya, yb, proj, x2, mod3, ln1_g.reshape(1, d), ln1_b.reshape(1, d),
                    w_proj_moba.astype(BF16), w_proj_gdn.astype(BF16), w_out.astype(BF16), seq)
    out = _ffn(h2, x1, mod3, ln2_g.reshape(1, d), ln2_b.reshape(1, d),
               w_ffn_in.astype(BF16), w_ffn_out.astype(BF16), seq)
    return out.reshape(bsz, seq, d)


def kernel(x, c, w_ada, b_ada, w_in, conv_w, a_log, dt_bias, gdn_norm_w, rel_bias, w_proj_moba,
           w_proj_gdn, w_out, ln1_g, ln1_b, w_ffn_in, w_ffn_out, ln2_g, ln2_b):
    depth = w_ada.shape[0]
    for l in range(depth):
        x = _layer(x, c, w_ada[l], b_ada[l], w_in[l], conv_w[l], a_log[l], dt_bias[l],
                   gdn_norm_w[l], rel_bias, w_proj_moba[l], w_proj_gdn[l], w_out[l],
                   ln1_g[l], ln1_b[l], w_ffn_in[l], w_ffn_out[l], ln2_g[l], ln2_b[l])
    return x
```

```python
import functools
import math

import jax
import jax.numpy as jnp
from jax import lax
from jax.experimental import pallas as pl
from jax.experimental.pallas import tpu as pltpu

F32 = jnp.float32
BF16 = jnp.bfloat16

D_MODEL = 2048
MOBA_HEADS = 8
HEAD_DIM = 128
MOBA_W = MOBA_HEADS * HEAD_DIM
MOBA_BLOCK = 256
MOBA_TOPK = 3
REL_BUCKETS = 32
REL_MAX_DIST = 128
GDN_QK_HEADS = 8
GDN_V_HEADS = 16
GDN_QK_W = GDN_QK_HEADS * HEAD_DIM
GDN_V_W = GDN_V_HEADS * HEAD_DIM
GDN_CONV = 4
GDN_CHUNK = 64
D_FF = 5632
DEEPNORM_ALPHA = 2.0 ** 0.25
LN_EPS = 1e-5
RMS_EPS = 1e-6
NEG_INF = -1e30

COL_GATE_A = 0
COL_GATE_B = 2048
COL_GDN_V = 4096
COL_GDN_Z = 6144
COL_MOBA_Q = 8192
COL_MOBA_K = 9216
COL_MOBA_V = 10240
COL_GDN_Q = 11264
COL_GDN_K = 12288
N_MAIN = 13312
N_SMALL = 128

V7X_VMEM_BYTES = 64 * 1024 * 1024
LANES = 128


def _vmem(mb):
    return pltpu.CompilerParams(vmem_limit_bytes=mb * 1024 * 1024)


def _silu(x):
    return x * jax.nn.sigmoid(x)


def _layer_norm(r, gain, bias):
    mu = jnp.mean(r, axis=-1, keepdims=True)
    d = r - mu
    var = jnp.mean(d * d, axis=-1, keepdims=True)
    return d * lax.rsqrt(var + LN_EPS) * gain + bias


def _ada_kernel(c_ref, w_ref, b_ref, o_ref):
    sc = _silu(c_ref[...])
    o_ref[...] = jnp.dot(sc, w_ref[...], precision=lax.Precision.HIGHEST,
                         preferred_element_type=F32) + b_ref[...]


def _ada_mod(c, w_ada, b_ada):
    bsz = c.shape[0]
    n = w_ada.shape[1]
    tn = 1024
    return pl.pallas_call(
        _ada_kernel,
        name="ada_mod",
        grid=(n // tn,),
        in_specs=[pl.BlockSpec((bsz, D_MODEL), lambda j: (0, 0)),
                  pl.BlockSpec((D_MODEL, tn), lambda j: (0, j)),
                  pl.BlockSpec((1, tn), lambda j: (0, j))],
        out_specs=pl.BlockSpec((bsz, tn), lambda j: (0, j)),
        out_shape=jax.ShapeDtypeStruct((bsz, n), F32),
        compiler_params=_vmem(40),
    )(c, w_ada, b_ada.reshape(1, n))


def _inproj_kernel(x_ref, sh_ref, sc_ref, w_ref, ws_ref, o_ref, os_ref, h_ref):
    @pl.when(pl.program_id(1) == 0)
    def _():
        h = x_ref[...] * (1.0 + sc_ref[...]) + sh_ref[...]
        h_ref[...] = h.astype(BF16)
        os_ref[...] = jnp.dot(h_ref[...], ws_ref[...], preferred_element_type=F32)

    o_ref[...] = jnp.dot(h_ref[...], w_ref[...], preferred_element_type=F32)


def _in_proj(x2, mod3, w_main, w_small, seq):
    t = x2.shape[0]
    tm, tn = 1024, 1024
    per_b = seq // tm
    return pl.pallas_call(
        _inproj_kernel,
        name="in_proj",
        grid=(t // tm, N_MAIN // tn),
        in_specs=[pl.BlockSpec((tm, D_MODEL), lambda i, j: (i, 0)),
                  pl.BlockSpec((None, 1, D_MODEL), lambda i, j: (i // per_b, 0, 0)),
                  pl.BlockSpec((None, 1, D_MODEL), lambda i, j: (i // per_b, 0, 1)),
                  pl.BlockSpec((D_MODEL, tn), lambda i, j: (0, j)),
                  pl.BlockSpec((D_MODEL, N_SMALL), lambda i, j: (0, 0))],
        out_specs=[pl.BlockSpec((tm, tn), lambda i, j: (i, j)),
                   pl.BlockSpec((tm, N_SMALL), lambda i, j: (i, 0))],
        out_shape=[jax.ShapeDtypeStruct((t, N_MAIN), F32),
                   jax.ShapeDtypeStruct((t, N_SMALL), F32)],
        scratch_shapes=[pltpu.VMEM((tm, D_MODEL), BF16)],
        compiler_params=_vmem(48),
    )(x2, mod3, mod3, w_main, w_small)


def _moba_kernel(rel_ref, bko_ref, bkp_ref, q_ref, k_ref, v_ref, o_ref,
                 bias_own, bias_prev, kb_ref, vb_ref, *, nb):
    h = pl.program_id(0)
    blk = MOBA_BLOCK

    @pl.when(pl.program_id(1) == 0)
    def _():
        bo = bko_ref[...]
        bp = bkp_ref[...]
        acc_o = jnp.zeros((blk, blk), F32)
        acc_p = jnp.zeros((blk, blk), F32)
        for kk in range(REL_BUCKETS):
            val = rel_ref[kk, h]
            acc_o = jnp.where(bo == kk, val, acc_o)
            acc_p = jnp.where(bp == kk, val, acc_p)
        bias_own[...] = acc_o
        bias_prev[...] = acc_p

    bias_far = rel_ref[REL_BUCKETS - 1, h]
    scale = HEAD_DIM ** -0.5
    kf = k_ref[...]
    kmean = jnp.mean(kf.reshape(nb, blk, HEAD_DIM), axis=1)
    kb_ref[...] = kf.astype(BF16)
    vb_ref[...] = v_ref[...].astype(BF16)
    causal = (lax.broadcasted_iota(jnp.int32, (blk, blk), 0)
              >= lax.broadcasted_iota(jnp.int32, (blk, blk), 1))
    nt_dims = (((1,), (1,)), ((), ()))

    for i in range(nb):
        qi = q_ref[i * blk:(i + 1) * blk, :]
        qb = qi.astype(BF16)
        sel = None
        if i > MOBA_TOPK:
            route = lax.dot_general(qi, kmean, nt_dims, precision=lax.Precision.HIGHEST,
                                    preferred_element_type=F32)
            rc = [route[:, n:n + 1] for n in range(i)]
            sel = []
            for n in range(i):
                rank = jnp.zeros((blk, 1), jnp.int32)
                for m in range(i):
                    if m == n:
                        continue
                    beats = (rc[m] >= rc[n]) if m < n else (rc[m] > rc[n])
                    rank = rank + beats.astype(jnp.int32)
                sel.append(rank < MOBA_TOPK)
        s_list = []
        for n in range(i + 1):
            s = lax.dot_general(qb, kb_ref[n * blk:(n + 1) * blk, :], nt_dims,
                                preferred_element_type=F32) * scale
            if n == i:
                s = jnp.where(causal, s + bias_own[...], NEG_INF)
            else:
                s = s + (bias_prev[...] if n == i - 1 else bias_far)
                if sel is not None:
                    s = jnp.where(sel[n], s, NEG_INF)
            s_list.append(s)
        m_run = jnp.max(s_list[0], axis=-1, keepdims=True)
        for s in s_list[1:]:
            m_run = jnp.maximum(m_run, jnp.max(s, axis=-1, keepdims=True))
        l_run = jnp.zeros((blk, 1), F32)
        acc = jnp.zeros((blk, HEAD_DIM), F32)
        for n, s in enumerate(s_list):
            p = jnp.exp(s - m_run)
            l_run = l_run + jnp.sum(p, axis=-1, keepdims=True)
            acc = acc + jnp.dot(p.astype(BF16), vb_ref[n * blk:(n + 1) * blk, :],
                                preferred_element_type=F32)
        o_ref[i * blk:(i + 1) * blk, :] = (acc / l_run).astype(BF16)


def _moba(proj, rel_bias, bko, bkp, bsz, seq):
    nb = seq // MOBA_BLOCK
    qo, ko, vo = COL_MOBA_Q // HEAD_DIM, COL_MOBA_K // HEAD_DIM, COL_MOBA_V // HEAD_DIM
    blk = MOBA_BLOCK
    return pl.pallas_call(
        functools.partial(_moba_kernel, nb=nb),
        name="moba",
        grid=(MOBA_HEADS, bsz),
        in_specs=[pl.BlockSpec(memory_space=pltpu.SMEM),
                  pl.BlockSpec((blk, blk), lambda h, b: (0, 0)),
                  pl.BlockSpec((blk, blk), lambda h, b: (0, 0)),
                  pl.BlockSpec((seq, HEAD_DIM), lambda h, b: (b, qo + h)),
                  pl.BlockSpec((seq, HEAD_DIM), lambda h, b: (b, ko + h)),
                  pl.BlockSpec((seq, HEAD_DIM), lambda h, b: (b, vo + h))],
        out_specs=pl.BlockSpec((seq, HEAD_DIM), lambda h, b: (b, h)),
        out_shape=jax.ShapeDtypeStruct((bsz * seq, MOBA_W), BF16),
        scratch_shapes=[pltpu.VMEM((blk, blk), F32), pltpu.VMEM((blk, blk), F32),
                        pltpu.VMEM((seq, HEAD_DIM), BF16), pltpu.VMEM((seq, HEAD_DIM), BF16)],
        compiler_params=_vmem(48),
    )(rel_bias, bko, bkp, proj, proj, proj)


def _conv_silu(x, w):
    row = lax.broadcasted_iota(jnp.int32, x.shape, 0)
    acc = x * w[GDN_CONV - 1:GDN_CONV, :]
    for s in range(1, GDN_CONV):
        xs = jnp.where(row >= s, pltpu.roll(x, s, axis=0), 0.0)
        acc = acc + xs * w[GDN_CONV - 1 - s:GDN_CONV - s, :]
    return _silu(acc)


def _l2norm(x):
    return x * lax.rsqrt(jnp.sum(x * x, axis=-1, keepdims=True) + RMS_EPS)


def _softplus(x):
    return jnp.maximum(x, 0.0) + jnp.log1p(jnp.exp(-jnp.abs(x)))


def _chunk_cumsum_rows(g):
    pos = lax.broadcasted_iota(jnp.int32, g.shape, 0) % GDN_CHUNK
    sft = 1
    while sft < GDN_CHUNK:
        g = g + jnp.where(pos >= sft, pltpu.roll(g, sft, axis=0), 0.0)
        sft *= 2
    return g


SUBLANES = 8


ELIM_BLOCK = 16
PHASE_B_CHUNKS = 4


def _block_diag_inverse(lmat):
    n = lmat.shape[0]
    nv = n // SUBLANES
    vpb = ELIM_BLOCK // SUBLANES
    rid = lax.broadcasted_iota(jnp.int32, (SUBLANES, n), 0)
    cid = lax.broadcasted_iota(jnp.int32, (SUBLANES, n), 1)
    t_rows = [(cid == rid + v * SUBLANES).astype(F32) for v in range(nv)]
    l_rows = [lmat[v * SUBLANES:(v + 1) * SUBLANES, :] for v in range(nv)]
    for m in range(n - 1):
        v0, s0 = divmod(m, SUBLANES)
        v_end = (v0 // vpb + 1) * vpb
        row = t_rows[v0][s0:s0 + 1, :]
        for v in range(v0 if s0 < SUBLANES - 1 else v0 + 1, v_end):
            t_rows[v] = t_rows[v] - l_rows[v][:, m:m + 1] * row
    return jnp.concatenate(t_rows, axis=0)


def _merge_block_inverses(lmats, ts):
    n = lmats[0].shape[0]
    ri = lax.broadcasted_iota(jnp.int32, (n, n), 0)
    ci = lax.broadcasted_iota(jnp.int32, (n, n), 1)
    k = ELIM_BLOCK
    while k < n:
        off = (ri // (2 * k) == ci // (2 * k)) & (ri // k > ci // k)
        lks = [jnp.where(off, lm, 0.0).astype(BF16) for lm in lmats]
        tbs = [t.astype(BF16) for t in ts]
        lts = [jnp.dot(lk, tb, preferred_element_type=F32).astype(BF16)
               for lk, tb in zip(lks, tbs)]
        ts = [t - jnp.dot(tb, lt, preferred_element_type=F32)
              for t, tb, lt in zip(ts, tbs, lts)]
        k *= 2
    return ts


def _lane_pick(x, lane, idx):
    return jnp.sum(jnp.where(lane == idx, x, 0.0), axis=-1, keepdims=True)


def _gdn_kernel(alog_ref, dtb_ref, gp_ref, nw_ref, cwq_ref, cwk_ref, cwv_ref,
                q_ref, k_ref, v_ref, z_ref, sm_ref, rw_ref, y_ref,
                kn_s, qn_s, rhs_s, gcc_s, gcr_s, beta_s, wq_s, u_s, lhs2_s, st_s, lm_s, t0_s,
                *, nchunk):
    hq = pl.program_id(1)
    c64 = GDN_CHUNK
    hd = HEAD_DIM
    heads = range(2)

    qn = _l2norm(_conv_silu(q_ref[...], cwq_ref[...])) * (hd ** -0.5)
    kn = _l2norm(_conv_silu(k_ref[...], cwk_ref[...]))
    qn_s[...] = qn.astype(BF16)
    kn_s[...] = kn
    vc = _conv_silu(v_ref[...], cwv_ref[...])
    sm = sm_ref[...]
    lane = lax.broadcasted_iota(jnp.int32, sm.shape, 1)
    sig_all = jax.nn.sigmoid(sm)
    g_all = -jnp.exp(gp_ref[0:1, :]) * _softplus(sm + gp_ref[1:2, :])
    gc_all = _chunk_cumsum_rows(g_all)
    tri_u = (lax.broadcasted_iota(jnp.int32, (c64, c64), 0)
             <= lax.broadcasted_iota(jnp.int32, (c64, c64), 1)).astype(F32)
    for j in heads:
        hv = 2 * hq + j
        beta = _lane_pick(sig_all, lane, hv)
        gc_col = _lane_pick(gc_all, lane, GDN_V_HEADS + hv)
        eg = jnp.exp(gc_col)
        beta_s[j] = beta
        gcc_s[j] = gc_col
        rhs_s[j, :, :hd] = (vc[:, j * hd:(j + 1) * hd] * beta).astype(BF16)
        rhs_s[j, :, hd:] = (kn * (beta * eg)).astype(BF16)
        wq_s[j, :, c64:, :] = (qn * eg).astype(BF16).reshape(nchunk, c64, hd)
        a_neg_r = -jnp.exp(jnp.full((nchunk, c64), alog_ref[hv], F32))
        g_row = a_neg_r * _softplus(rw_ref[2 + j] + dtb_ref[hv])
        gcr_s[j] = jnp.dot(g_row, tri_u, precision=lax.Precision.HIGHEST,
                           preferred_element_type=F32)

    rows = lax.broadcasted_iota(jnp.int32, (c64, c64), 0)
    cols = lax.broadcasted_iota(jnp.int32, (c64, c64), 1)
    tril = rows >= cols
    strict = rows > cols
    nt_dims = (((1,), (1,)), ((), ()))

    def group(it):
        cs = [it * PHASE_B_CHUNKS + i for i in range(PHASE_B_CHUNKS)]
        rs = [pl.ds(pl.multiple_of(c * c64, c64), c64) for c in cs]
        return cs, rs, [(j, c, r) for c, r in zip(cs, rs) for j in heads]

    def stage_gram(it):
        _, rs, _ = group(it)
        kns = [kn_s[r, :] for r in rs]
        knbs = [kn.astype(BF16) for kn in kns]
        kks = [lax.dot_general(kb, kb, nt_dims, preferred_element_type=F32) for kb in knbs]
        qks = [lax.dot_general(qn_s[r, :], kb, nt_dims, preferred_element_type=F32)
               for r, kb in zip(rs, knbs)]
        return kns, kks, qks

    def stage_factor(it, gram):
        cs, rs, _ = group(it)
        kns, kks, qks = gram
        for i, (c, r) in enumerate(zip(cs, rs)):
            kds, ats = [], []
            for j in heads:
                gcc = gcc_s[j, r, :]
                gcr = gcr_s[j, pl.ds(c, 1), :]
                dec = jnp.exp(jnp.where(tril, gcc - gcr, NEG_INF))
                lmat = jnp.where(strict, kks[i] * dec, 0.0) * beta_s[j, r, :]
                lm_s[j, r, :] = lmat
                t0_s[j, r, :] = _block_diag_inverse(lmat)
                ats.append(qks[i] * dec)
                kds.append(kns[i] * jnp.exp(gcr[:, c64 - 1:c64] - gcc))
            lhs2_s[c, :2 * c64, :] = jnp.concatenate(kds, axis=0).T.astype(BF16)
            lhs2_s[c, 2 * c64:, :] = jnp.concatenate(ats, axis=1).astype(BF16)

    def stage_solve(it):
        _, _, where_to = group(it)
        ts = _merge_block_inverses([lm_s[j, r, :] for j, c, r in where_to],
                                   [t0_s[j, r, :] for j, c, r in where_to])
        sols = [jnp.dot(t.astype(BF16), rhs_s[j, r, :], preferred_element_type=F32)
                for t, (j, c, r) in zip(ts, where_to)]
        for sol, (j, c, r) in zip(sols, where_to):
            u_s[j, r, :] = sol[:, :hd]
            wq_s[j, c, :c64, :] = sol[:, hd:].astype(BF16)

    ngroup = nchunk // PHASE_B_CHUNKS
    stage_factor(0, stage_gram(0))

    def phase_b(it, carry):
        gram = stage_gram(it)
        stage_solve(it - 1)
        stage_factor(it, gram)
        return carry

    lax.fori_loop(1, ngroup, phase_b, 0)
    stage_solve(ngroup - 1)

    st_s[...] = jnp.zeros_like(st_s)
    zeros = jnp.zeros((c64, hd), BF16)

    def phase_c(c, carry):
        r = pl.ds(pl.multiple_of(c * c64, c64), c64)
        sts = [st_s[j] for j in heads]
        wqs = [jnp.dot(wq_s[j, c], sts[j].astype(BF16), preferred_element_type=F32)
               for j in heads]
        vna, vnb = [(u_s[j, r, :] - wqs[j][:c64]).astype(BF16) for j in heads]
        rhs_bd = jnp.concatenate([jnp.concatenate([vna, zeros], axis=1),
                                  jnp.concatenate([zeros, vnb], axis=1)], axis=0)
        fused = jnp.dot(lhs2_s[c], rhs_bd, preferred_element_type=F32)
        for j in heads:
            gl = gcr_s[j, pl.ds(c, 1), c64 - 1:c64]
            st_s[j] = sts[j] * jnp.exp(gl) + fused[:2 * c64, j * hd:(j + 1) * hd]
            u_s[j, r, :] = wqs[j][c64:] + fused[2 * c64:, j * hd:(j + 1) * hd]
        return carry

    lax.fori_loop(0, nchunk, phase_c, 0)

    nw = nw_ref[...]
    for j in heads:
        o = u_s[j]
        og = (o * lax.rsqrt(jnp.mean(o * o, axis=-1, keepdims=True) + RMS_EPS)
              * nw * _silu(z_ref[:, j * hd:(j + 1) * hd]))
        y_ref[:, j * hd:(j + 1) * hd] = og.astype(BF16)


def _gdn(proj, small, rows, conv_w, a_log, dt_bias, norm_w, bsz, seq):
    nchunk = seq // GDN_CHUNK
    hd = HEAD_DIM
    qo, ko = COL_GDN_Q // hd, COL_GDN_K // hd
    vo, zo = COL_GDN_V // (2 * hd), COL_GDN_Z // (2 * hd)
    cvo = (2 * GDN_QK_W) // (2 * hd)
    smem = pl.BlockSpec(memory_space=pltpu.SMEM)
    pad = (GDN_V_HEADS, N_SMALL - 2 * GDN_V_HEADS)
    gate_params = jnp.stack([jnp.pad(a_log, pad), jnp.pad(dt_bias, pad)])
    return pl.pallas_call(
        functools.partial(_gdn_kernel, nchunk=nchunk),
        name="gdn",
        grid=(bsz, GDN_QK_HEADS),
        in_specs=[smem, smem,
                  pl.BlockSpec((2, N_SMALL), lambda b, h: (0, 0)),
                  pl.BlockSpec((1, hd), lambda b, h: (0, 0)),
                  pl.BlockSpec((GDN_CONV, hd), lambda b, h: (0, h)),
                  pl.BlockSpec((GDN_CONV, hd), lambda b, h: (0, GDN_QK_HEADS + h)),
                  pl.BlockSpec((GDN_CONV, 2 * hd), lambda b, h: (0, cvo + h)),
                  pl.BlockSpec((seq, hd), lambda b, h: (b, qo + h)),
                  pl.BlockSpec((seq, hd), lambda b, h: (b, ko + h)),
                  pl.BlockSpec((seq, 2 * hd), lambda b, h: (b, vo + h)),
                  pl.BlockSpec((seq, 2 * hd), lambda b, h: (b, zo + h)),
                  pl.BlockSpec((seq, N_SMALL), lambda b, h: (b, 0)),
                  pl.BlockSpec((None, None, 4, nchunk, GDN_CHUNK), lambda b, h: (b, h, 0, 0, 0))],
        out_specs=pl.BlockSpec((seq, 2 * hd), lambda b, h: (b, h)),
        out_shape=jax.ShapeDtypeStruct((bsz * seq, GDN_V_W), BF16),
        scratch_shapes=[pltpu.VMEM((seq, hd), F32),
                        pltpu.VMEM((seq, hd), BF16),
                        pltpu.VMEM((2, seq, 2 * hd), BF16),
                        pltpu.VMEM((2, seq, 1), F32),
                        pltpu.VMEM((2, nchunk, GDN_CHUNK), F32),
                        pltpu.VMEM((2, seq, 1), F32),
                        pltpu.VMEM((2, nchunk, 2 * GDN_CHUNK, hd), BF16),
                        pltpu.VMEM((2, seq, hd), F32),
                        pltpu.VMEM((nchunk, 3 * GDN_CHUNK, hd), BF16),
                        pltpu.VMEM((2, hd, hd), F32),
                        pltpu.VMEM((2, seq, GDN_CHUNK), F32),
                        pltpu.VMEM((2, seq, GDN_CHUNK), F32)],
        compiler_params=_vmem(56),
    )(a_log, dt_bias, gate_params, norm_w, conv_w, conv_w, conv_w, proj, proj, proj, proj, small, rows)


def _merge_kernel(ya_ref, yb_ref, ga_ref, gb_ref, x_ref, g1_ref, sh2_ref, sc2_ref,
                  lng_ref, lnb_ref, wpm_ref, wpg_ref, wo_ref, x1_ref, h2_ref):
    pa = jnp.dot(ya_ref[...], wpm_ref[...], preferred_element_type=F32)
    pb = jnp.dot(yb_ref[...], wpg_ref[...], preferred_element_type=F32)
    merged = jax.nn.sigmoid(ga_ref[...]) * pa + jax.nn.sigmoid(gb_ref[...]) * pb
    y = jnp.dot(merged.astype(BF16), wo_ref[...], preferred_element_type=F32)
    x1 = _layer_norm(DEEPNORM_ALPHA * x_ref[...] + g1_ref[...] * y, lng_ref[...], lnb_ref[...])
    x1_ref[...] = x1
    h2_ref[...] = (x1 * (1.0 + sc2_ref[...]) + sh2_ref[...]).astype(BF16)


def _merge(ya, yb, proj, x2, mod3, ln_g, ln_b, wpm, wpg, wo, seq):
    t = x2.shape[0]
    tm = 256
    per_b = seq // tm
    d = D_MODEL

    def modspec(k):
        return pl.BlockSpec((None, 1, d), lambda i: (i // per_b, 0, k))

    def const(shape):
        return pl.BlockSpec(shape, lambda i: (0, 0), pipeline_mode=pl.Buffered(1))

    return pl.pallas_call(
        _merge_kernel,
        name="merge",
        grid=(t // tm,),
        in_specs=[pl.BlockSpec((tm, MOBA_W), lambda i: (i, 0)),
                  pl.BlockSpec((tm, GDN_V_W), lambda i: (i, 0)),
                  pl.BlockSpec((tm, d), lambda i: (i, COL_GATE_A // d)),
                  pl.BlockSpec((tm, d), lambda i: (i, COL_GATE_B // d)),
                  pl.BlockSpec((tm, d), lambda i: (i, 0)),
                  modspec(2), modspec(3), modspec(4),
                  const((1, d)), const((1, d)),
                  const((MOBA_W, d)), const((GDN_V_W, d)), const((d, d))],
        out_specs=[pl.BlockSpec((tm, d), lambda i: (i, 0)),
                   pl.BlockSpec((tm, d), lambda i: (i, 0))],
        out_shape=[jax.ShapeDtypeStruct((t, d), F32), jax.ShapeDtypeStruct((t, d), BF16)],
        compiler_params=_vmem(56),
    )(ya, yb, proj, proj, x2, mod3, mod3, mod3, ln_g, ln_b, wpm, wpg, wo)


def _ffn_kernel(h_ref, x1_ref, g2_ref, lng_ref, lnb_ref, wg_ref, wu_ref, wo_ref, o_ref, acc_ref):
    f = pl.program_id(1)

    @pl.when(f == 0)
    def _():
        acc_ref[...] = jnp.zeros_like(acc_ref)

    h = h_ref[...]
    gate = jnp.dot(h, wg_ref[...], preferred_element_type=F32)
    up = jnp.dot(h, wu_ref[...], preferred_element_type=F32)
    act = (_silu(gate) * up).astype(BF16)
    acc_ref[...] += jnp.dot(act, wo_ref[...], preferred_element_type=F32)

    @pl.when(f == pl.num_programs(1) - 1)
    def _():
        r = DEEPNORM_ALPHA * x1_ref[...] + g2_ref[...] * acc_ref[...]
        o_ref[...] = _layer_norm(r, lng_ref[...], lnb_ref[...])


def _ffn(h2, x1, mod3, ln_g, ln_b, w_in, w_out, seq):
    t = h2.shape[0]
    tm, tf = 512, 512
    per_b = seq // tm
    d = D_MODEL
    nf = D_FF // tf
    return pl.pallas_call(
        _ffn_kernel,
        name="ffn",
        grid=(t // tm, nf),
        in_specs=[pl.BlockSpec((tm, d), lambda i, f: (i, 0)),
                  pl.BlockSpec((tm, d), lambda i, f: (i, 0)),
                  pl.BlockSpec((None, 1, d), lambda i, f: (i // per_b, 0, 5)),
                  pl.BlockSpec((1, d), lambda i, f: (0, 0)),
                  pl.BlockSpec((1, d), lambda i, f: (0, 0)),
                  pl.BlockSpec((d, tf), lambda i, f: (0, f)),
                  pl.BlockSpec((d, tf), lambda i, f: (0, nf + f)),
                  pl.BlockSpec((tf, d), lambda i, f: (f, 0))],
        out_specs=pl.BlockSpec((tm, d), lambda i, f: (i, 0)),
        out_shape=jax.ShapeDtypeStruct((t, d), F32),
        scratch_shapes=[pltpu.VMEM((tm, d), F32)],
        compiler_params=_vmem(48),
    )(h2, x1, mod3, ln_g, ln_b, w_in, w_in, w_out)


def _rel_bucket(dist):
    max_exact = REL_BUCKETS // 2
    n = jnp.maximum(dist, 0)
    nf = jnp.maximum(n, 1).astype(F32)
    large = max_exact + (jnp.log(nf / max_exact) / math.log(REL_MAX_DIST / max_exact)
                         * (REL_BUCKETS - max_exact)).astype(jnp.int32)
    large = jnp.minimum(large, REL_BUCKETS - 1)
    return jnp.where(n < max_exact, n, large)


def _layer(x, c, w_ada, b_ada, w_in, conv_w, a_log, dt_bias, gdn_norm_w, rel_bias,
           w_proj_moba, w_proj_gdn, w_out, ln1_g, ln1_b, w_ffn_in, w_ffn_out, ln2_g, ln2_b):
    bsz, seq, d = x.shape
    t = bsz * seq
    x2 = x.reshape(t, d)

    mod = _ada_mod(c, w_ada, b_ada)
    mod3 = mod.reshape(bsz, 1, 6 * d)

    o1 = 3 * MOBA_W
    o2 = o1 + 2 * GDN_QK_W + GDN_V_W
    o3 = o2 + GDN_V_W
    o5 = o3 + 2 * GDN_V_HEADS
    w_main = jnp.concatenate([
        w_in[:, o5:],
        w_in[:, o1 + 2 * GDN_QK_W:o2],
        w_in[:, o2:o3],
        w_in[:, :o1],
        w_in[:, o1:o1 + 2 * GDN_QK_W],
    ], axis=1).astype(BF16)
    w_small = jnp.pad(w_in[:, o3:o5], ((0, 0), (0, N_SMALL - 2 * GDN_V_HEADS))).astype(BF16)

    proj, small = _in_proj(x2, mod3, w_main, w_small, seq)

    ii = jnp.arange(MOBA_BLOCK, dtype=jnp.int32)
    dist = ii[:, None] - ii[None, :]
    bko = _rel_bucket(dist)
    bkp = _rel_bucket(dist + MOBA_BLOCK)
    ya = _moba(proj, rel_bias, bko, bkp, bsz, seq)

    nchunk = seq // GDN_CHUNK
    sm_t = small[:, :2 * GDN_V_HEADS].reshape(bsz, seq, 2, GDN_QK_HEADS, 2)
    rows = sm_t.transpose(0, 3, 2, 4, 1).reshape(bsz, GDN_QK_HEADS, 4, nchunk, GDN_CHUNK)
    yb = _gdn(proj, small, rows, conv_w, a_log, dt_bias, gdn_norm_w.reshape(1, HEAD_DIM),
              bsz, seq)

    x1, h2 = _merge(ya, yb, proj, x2, mod3, ln1_g.reshape(1, d), ln1_b.reshape(1, d),
                    w_proj_moba.astype(BF16), w_proj_gdn.astype(BF16), w_out.astype(BF16), seq)
    out = _ffn(h2, x1, mod3, ln2_g.reshape(1, d), ln2_b.reshape(1, d),
               w_ffn_in.astype(BF16), w_ffn_out.astype(BF16), seq)
    return out.reshape(bsz, seq, d)


def kernel(x, c, w_ada, b_ada, w_in, conv_w, a_log, dt_bias, gdn_norm_w, rel_bias, w_proj_moba,
           w_proj_gdn, w_out, ln1_g, ln1_b, w_ffn_in, w_ffn_out, ln2_g, ln2_b):
    depth = w_ada.shape[0]
    for l in range(depth):
        x = _layer(x, c, w_ada[l], b_ada[l], w_in[l], conv_w[l], a_log[l], dt_bias[l],
                   gdn_norm_w[l], rel_bias, w_proj_moba[l], w_proj_gdn[l], w_out[l],
                   ln1_g[l], ln1_b[l], w_ffn_in[l], w_ffn_out[l], ln2_g[l], ln2_b[l])
    return x
```

```python
import functools
import math

import jax
import jax.numpy as jnp
from jax import lax
from jax.experimental import pallas as pl
from jax.experimental.pallas import tpu as pltpu

F32 = jnp.float32
BF16 = jnp.bfloat16

D_MODEL = 2048
MOBA_HEADS = 8
HEAD_DIM = 128
MOBA_W = MOBA_HEADS * HEAD_DIM
MOBA_BLOCK = 256
MOBA_TOPK = 3
REL_BUCKETS = 32
REL_MAX_DIST = 128
GDN_QK_HEADS = 8
GDN_V_HEADS = 16
GDN_QK_W = GDN_QK_HEADS * HEAD_DIM
GDN_V_W = GDN_V_HEADS * HEAD_DIM
GDN_CONV = 4
GDN_CHUNK = 64
D_FF = 5632
DEEPNORM_ALPHA = 2.0 ** 0.25
LN_EPS = 1e-5
RMS_EPS = 1e-6
NEG_INF = -1e30

COL_GATE_A = 0
COL_GATE_B = 2048
COL_GDN_V = 4096
COL_GDN_Z = 6144
COL_MOBA_Q = 8192
COL_MOBA_K = 9216
COL_MOBA_V = 10240
COL_GDN_Q = 11264
COL_GDN_K = 12288
N_MAIN = 13312
N_SMALL = 128

V7X_VMEM_BYTES = 64 * 1024 * 1024
LANES = 128


def _vmem(mb):
    return pltpu.CompilerParams(vmem_limit_bytes=mb * 1024 * 1024)


def _silu(x):
    return x * jax.nn.sigmoid(x)


def _layer_norm(r, gain, bias):
    mu = jnp.mean(r, axis=-1, keepdims=True)
    d = r - mu
    var = jnp.mean(d * d, axis=-1, keepdims=True)
    return d * lax.rsqrt(var + LN_EPS) * gain + bias


def _ada_kernel(c_ref, w_ref, b_ref, o_ref):
    sc = _silu(c_ref[...])
    o_ref[...] = jnp.dot(sc, w_ref[...], precision=lax.Precision.HIGHEST,
                         preferred_element_type=F32) + b_ref[...]


def _ada_mod(c, w_ada, b_ada):
    bsz = c.shape[0]
    n = w_ada.shape[1]
    tn = 1024
    return pl.pallas_call(
        _ada_kernel,
        name="ada_mod",
        grid=(n // tn,),
        in_specs=[pl.BlockSpec((bsz, D_MODEL), lambda j: (0, 0)),
                  pl.BlockSpec((D_MODEL, tn), lambda j: (0, j)),
                  pl.BlockSpec((1, tn), lambda j: (0, j))],
        out_specs=pl.BlockSpec((bsz, tn), lambda j: (0, j)),
        out_shape=jax.ShapeDtypeStruct((bsz, n), F32),
        compiler_params=_vmem(40),
    )(c, w_ada, b_ada.reshape(1, n))


def _inproj_kernel(x_ref, sh_ref, sc_ref, w_ref, ws_ref, o_ref, os_ref, h_ref):
    @pl.when(pl.program_id(1) == 0)
    def _():
        h = x_ref[...] * (1.0 + sc_ref[...]) + sh_ref[...]
        h_ref[...] = h.astype(BF16)
        os_ref[...] = jnp.dot(h_ref[...], ws_ref[...], preferred_element_type=F32)

    o_ref[...] = jnp.dot(h_ref[...], w_ref[...], preferred_element_type=F32)


def _in_proj(x2, mod3, w_main, w_small, seq):
    t = x2.shape[0]
    tm, tn = 1024, 1024
    per_b = seq // tm
    return pl.pallas_call(
        _inproj_kernel,
        name="in_proj",
        grid=(t // tm, N_MAIN // tn),
        in_specs=[pl.BlockSpec((tm, D_MODEL), lambda i, j: (i, 0)),
                  pl.BlockSpec((None, 1, D_MODEL), lambda i, j: (i // per_b, 0, 0)),
                  pl.BlockSpec((None, 1, D_MODEL), lambda i, j: (i // per_b, 0, 1)),
                  pl.BlockSpec((D_MODEL, tn), lambda i, j: (0, j)),
                  pl.BlockSpec((D_MODEL, N_SMALL), lambda i, j: (0, 0))],
        out_specs=[pl.BlockSpec((tm, tn), lambda i, j: (i, j)),
                   pl.BlockSpec((tm, N_SMALL), lambda i, j: (i, 0))],
        out_shape=[jax.ShapeDtypeStruct((t, N_MAIN), F32),
                   jax.ShapeDtypeStruct((t, N_SMALL), F32)],
        scratch_shapes=[pltpu.VMEM((tm, D_MODEL), BF16)],
        compiler_params=_vmem(48),
    )(x2, mod3, mod3, w_main, w_small)


def _moba_kernel(rel_ref, bko_ref, bkp_ref, q_ref, k_ref, v_ref, o_ref,
                 bias_own, bias_prev, kb_ref, vb_ref, *, nb):
    h = pl.program_id(0)
    blk = MOBA_BLOCK

    @pl.when(pl.program_id(1) == 0)
    def _():
        bo = bko_ref[...]
        bp = bkp_ref[...]
        acc_o = jnp.zeros((blk, blk), F32)
        acc_p = jnp.zeros((blk, blk), F32)
        for kk in range(REL_BUCKETS):
            val = rel_ref[kk, h]
            acc_o = jnp.where(bo == kk, val, acc_o)
            acc_p = jnp.where(bp == kk, val, acc_p)
        bias_own[...] = acc_o
        bias_prev[...] = acc_p

    bias_far = rel_ref[REL_BUCKETS - 1, h]
    scale = HEAD_DIM ** -0.5
    kf = k_ref[...]
    kmean = jnp.mean(kf.reshape(nb, blk, HEAD_DIM), axis=1)
    kb_ref[...] = kf.astype(BF16)
    vb_ref[...] = v_ref[...].astype(BF16)
    causal = (lax.broadcasted_iota(jnp.int32, (blk, blk), 0)
              >= lax.broadcasted_iota(jnp.int32, (blk, blk), 1))
    nt_dims = (((1,), (1,)), ((), ()))

    for i in range(nb):
        qi = q_ref[i * blk:(i + 1) * blk, :]
        qb = qi.astype(BF16)
        sel = None
        if i > MOBA_TOPK:
            route = lax.dot_general(qi, kmean, nt_dims, precision=lax.Precision.HIGHEST,
                                    preferred_element_type=F32)
            rc = [route[:, n:n + 1] for n in range(i)]
            sel = []
            for n in range(i):
                rank = jnp.zeros((blk, 1), jnp.int32)
                for m in range(i):
                    if m == n:
                        continue
                    beats = (rc[m] >= rc[n]) if m < n else (rc[m] > rc[n])
                    rank = rank + beats.astype(jnp.int32)
                sel.append(rank < MOBA_TOPK)
        s_list = []
        for n in range(i + 1):
            s = lax.dot_general(qb, kb_ref[n * blk:(n + 1) * blk, :], nt_dims,
                                preferred_element_type=F32) * scale
            if n == i:
                s = jnp.where(causal, s + bias_own[...], NEG_INF)
            else:
                s = s + (bias_prev[...] if n == i - 1 else bias_far)
                if sel is not None:
                    s = jnp.where(sel[n], s, NEG_INF)
            s_list.append(s)
        m_run = jnp.max(s_list[0], axis=-1, keepdims=True)
        for s in s_list[1:]:
            m_run = jnp.maximum(m_run, jnp.max(s, axis=-1, keepdims=True))
        l_run = jnp.zeros((blk, 1), F32)
        acc = jnp.zeros((blk, HEAD_DIM), F32)
        for n, s in enumerate(s_list):
            p = jnp.exp(s - m_run)
            l_run = l_run + jnp.sum(p, axis=-1, keepdims=True)
            acc = acc + jnp.dot(p.astype(BF16), vb_ref[n * blk:(n + 1) * blk, :],
                                preferred_element_type=F32)
        o_ref[i * blk:(i + 1) * blk, :] = (acc / l_run).astype(BF16)


def _moba(proj, rel_bias, bko, bkp, bsz, seq):
    nb = seq // MOBA_BLOCK
    qo, ko, vo = COL_MOBA_Q // HEAD_DIM, COL_MOBA_K // HEAD_DIM, COL_MOBA_V // HEAD_DIM
    blk = MOBA_BLOCK
    return pl.pallas_call(
        functools.partial(_moba_kernel, nb=nb),
        name="moba",
        grid=(MOBA_HEADS, bsz),
        in_specs=[pl.BlockSpec(memory_space=pltpu.SMEM),
                  pl.BlockSpec((blk, blk), lambda h, b: (0, 0)),
                  pl.BlockSpec((blk, blk), lambda h, b: (0, 0)),
                  pl.BlockSpec((seq, HEAD_DIM), lambda h, b: (b, qo + h)),
                  pl.BlockSpec((seq, HEAD_DIM), lambda h, b: (b, ko + h)),
                  pl.BlockSpec((seq, HEAD_DIM), lambda h, b: (b, vo + h))],
        out_specs=pl.BlockSpec((seq, HEAD_DIM), lambda h, b: (b, h)),
        out_shape=jax.ShapeDtypeStruct((bsz * seq, MOBA_W), BF16),
        scratch_shapes=[pltpu.VMEM((blk, blk), F32), pltpu.VMEM((blk, blk), F32),
                        pltpu.VMEM((seq, HEAD_DIM), BF16), pltpu.VMEM((seq, HEAD_DIM), BF16)],
        compiler_params=_vmem(48),
    )(rel_bias, bko, bkp, proj, proj, proj)


def _conv_silu(x, w):
    row = lax.broadcasted_iota(jnp.int32, x.shape, 0)
    acc = x * w[GDN_CONV - 1:GDN_CONV, :]
    for s in range(1, GDN_CONV):
        xs = jnp.where(row >= s, pltpu.roll(x, s, axis=0), 0.0)
        acc = acc + xs * w[GDN_CONV - 1 - s:GDN_CONV - s, :]
    return _silu(acc)


def _l2norm(x):
    return x * lax.rsqrt(jnp.sum(x * x, axis=-1, keepdims=True) + RMS_EPS)


def _softplus(x):
    return jnp.maximum(x, 0.0) + jnp.log1p(jnp.exp(-jnp.abs(x)))


def _chunk_cumsum_rows(g):
    pos = lax.broadcasted_iota(jnp.int32, g.shape, 0) % GDN_CHUNK
    sft = 1
    while sft < GDN_CHUNK:
        g = g + jnp.where(pos >= sft, pltpu.roll(g, sft, axis=0), 0.0)
        sft *= 2
    return g


SUBLANES = 8


ELIM_BLOCK = 16
MERGE_LEVELS = (GDN_CHUNK // ELIM_BLOCK).bit_length() - 1


def _block_diag_inverse(lmat):
    n = lmat.shape[0]
    nv = n // SUBLANES
    vpb = ELIM_BLOCK // SUBLANES
    rid = lax.broadcasted_iota(jnp.int32, (SUBLANES, n), 0)
    cid = lax.broadcasted_iota(jnp.int32, (SUBLANES, n), 1)
    t_rows = [(cid == rid + v * SUBLANES).astype(F32) for v in range(nv)]
    l_rows = [lmat[v * SUBLANES:(v + 1) * SUBLANES, :] for v in range(nv)]
    for m in range(n - 1):
        v0, s0 = divmod(m, SUBLANES)
        v_end = (v0 // vpb + 1) * vpb
        row = t_rows[v0][s0:s0 + 1, :]
        for v in range(v0 if s0 < SUBLANES - 1 else v0 + 1, v_end):
            t_rows[v] = t_rows[v] - l_rows[v][:, m:m + 1] * row
    return jnp.concatenate(t_rows, axis=0)


def _merge_block_pairs(lmats, ts, k):
    n = lmats[0].shape[0]
    ri = lax.broadcasted_iota(jnp.int32, (n, n), 0)
    ci = lax.broadcasted_iota(jnp.int32, (n, n), 1)
    off = (ri // (2 * k) == ci // (2 * k)) & (ri // k > ci // k)
    lks = [jnp.where(off, lm, 0.0).astype(BF16) for lm in lmats]
    tbs = [t.astype(BF16) for t in ts]
    lts = [jnp.dot(lk, tb, preferred_element_type=F32) for lk, tb in zip(lks, tbs)]

    def finish():
        return [t - jnp.dot(tb, lt.astype(BF16), preferred_element_type=F32)
                for t, tb, lt in zip(ts, tbs, lts)]

    return finish


def _software_pipeline(stages, n):
    ns = len(stages)

    def run(it, lo, hi):
        conts = [stages[s](it - s) for s in reversed(range(lo, hi))]
        for cont in conts:
            if cont is not None:
                cont()

    for it in range(ns - 1):
        run(it, 0, it + 1)

    def body(it, carry):
        run(it, 0, ns)
        return carry

    lax.fori_loop(ns - 1, n, body, 0)
    for it in range(n, n + ns - 1):
        run(it, it - n + 1, ns)


def _lane_pick(x, lane, idx):
    return jnp.sum(jnp.where(lane == idx, x, 0.0), axis=-1, keepdims=True)


def _gdn_kernel(alog_ref, dtb_ref, gp_ref, nw_ref, cwq_ref, cwk_ref, cwv_ref,
                q_ref, k_ref, v_ref, z_ref, sm_ref, rw_ref, y_ref,
                kb_s, qn_s, kbeta_s, kdp_s, rhs_s, qd_s, gcb_s, gcr_s, gram_s, lm_s, lhs2_s,
                sol_s, mp_s, n_s, r_s, st_s, *tl_s, nchunk):
    hq = pl.program_id(1)
    c64 = GDN_CHUNK
    hd = HEAD_DIM
    heads = range(2)
    seq = q_ref.shape[0]

    qn = _l2norm(_conv_silu(q_ref[...], cwq_ref[...])) * (hd ** -0.5)
    kn = _l2norm(_conv_silu(k_ref[...], cwk_ref[...]))
    qn_s[...] = qn.astype(BF16)
    kb_s[...] = kn.astype(BF16)
    vc = _conv_silu(v_ref[...], cwv_ref[...])
    sm = sm_ref[...]
    lane = lax.broadcasted_iota(jnp.int32, sm.shape, 1)
    sig_all = jax.nn.sigmoid(sm)
    g_all = -jnp.exp(gp_ref[0:1, :]) * _softplus(sm + gp_ref[1:2, :])
    gc_all = _chunk_cumsum_rows(g_all)
    gc_end = jnp.broadcast_to(gc_all.reshape(nchunk, c64, N_SMALL)[:, c64 - 1:, :],
                              (nchunk, c64, N_SMALL)).reshape(seq, N_SMALL)
    rest_all = gc_end - gc_all
    tri_u = (lax.broadcasted_iota(jnp.int32, (c64, c64), 0)
             <= lax.broadcasted_iota(jnp.int32, (c64, c64), 1)).astype(F32)
    for j in heads:
        hv = 2 * hq + j
        beta = _lane_pick(sig_all, lane, hv)
        gc_col = _lane_pick(gc_all, lane, GDN_V_HEADS + hv)
        rest_col = _lane_pick(rest_all, lane, GDN_V_HEADS + hv)
        eg = jnp.exp(gc_col)
        rhs_s[j, :, :hd] = (vc[:, j * hd:(j + 1) * hd] * beta).astype(BF16)
        rhs_s[j, :, hd:] = (kn * (beta * eg)).astype(BF16)
        qd_s[j] = (qn * eg).astype(BF16)
        kbeta_s[j] = (kn * beta).astype(BF16)
        kdp_s[:, j * c64:(j + 1) * c64, :] = (
            (kn * jnp.exp(rest_col)).astype(BF16).reshape(nchunk, c64, hd))
        gcb_s[j] = jnp.broadcast_to(gc_col, (seq, c64))
        a_neg_r = -jnp.exp(jnp.full((nchunk, c64), alog_ref[hv], F32))
        g_row = a_neg_r * _softplus(rw_ref[2 + j] + dtb_ref[hv])
        gcr_s[j] = jnp.dot(g_row, tri_u, precision=lax.Precision.HIGHEST,
                           preferred_element_type=F32)

    rows = lax.broadcasted_iota(jnp.int32, (c64, c64), 0)
    cols = lax.broadcasted_iota(jnp.int32, (c64, c64), 1)
    tril = rows >= cols
    strict = rows > cols
    eye2 = (lax.broadcasted_iota(jnp.int32, (2 * c64, 2 * c64), 0)
            == lax.broadcasted_iota(jnp.int32, (2 * c64, 2 * c64), 1)).astype(BF16)
    nt_dims = (((1,), (1,)), ((), ()))

    def rows_of(c):
        start = c * c64
        return pl.ds(start if isinstance(c, int) else pl.multiple_of(start, c64), c64)

    def stage_gram(c):
        r = rows_of(c)
        kb = kb_s[r, :]
        grams = [lax.dot_general(kbeta_s[j, r, :], kb, nt_dims, preferred_element_type=F32)
                 for j in heads]
        grams.append(lax.dot_general(qn_s[r, :], kb, nt_dims, preferred_element_type=F32))
        kd_t = lax.dot_general(eye2, kdp_s[c], nt_dims, preferred_element_type=F32)

        def finish():
            for i, g in enumerate(grams):
                gram_s[i, r, :] = g
            lhs2_s[c, :2 * c64, :] = kd_t.astype(BF16)

        return finish

    def stage_factor(c):
        r = rows_of(c)
        qk = gram_s[2, r, :]
        ats = []
        for j in heads:
            dec = jnp.exp(jnp.where(tril, gcb_s[j, r, :] - gcr_s[j, pl.ds(c, 1), :], NEG_INF))
            lmat = jnp.where(strict, gram_s[j, r, :] * dec, 0.0)
            lm_s[j, r, :] = lmat
            tl_s[0][j, r, :] = _block_diag_inverse(lmat)
            ats.append(qk * dec)
        lhs2_s[c, 2 * c64:, :] = jnp.concatenate(ats, axis=1).astype(BF16)

    def stage_merge(level):
        def stage(c):
            r = rows_of(c)
            merged = _merge_block_pairs([lm_s[j, r, :] for j in heads],
                                        [tl_s[level][j, r, :] for j in heads],
                                        ELIM_BLOCK << level)

            def finish():
                ts = merged()
                for j in heads:
                    tl_s[level + 1][j, r, :] = ts[j]

            return finish
        return stage

    def stage_solve(c):
        r = rows_of(c)
        sols = [jnp.dot(tl_s[MERGE_LEVELS][j, r, :].astype(BF16), rhs_s[j, r, :],
                        preferred_element_type=F32) for j in heads]

        def finish():
            for j in heads:
                sol_s[j, r, :] = sols[j].astype(BF16)

        return finish

    def stage_fold(c):
        r = rows_of(c)
        zero = jnp.zeros((c64, 2 * hd), BF16)
        rhs_bd = jnp.concatenate([jnp.concatenate([sol_s[0, r, :], zero], axis=1),
                                  jnp.concatenate([zero, sol_s[1, r, :]], axis=1)], axis=0)
        out = jnp.dot(lhs2_s[c], rhs_bd, preferred_element_type=F32)

        def finish():
            for j in heads:
                u_col, w_col = 2 * j * hd, (2 * j + 1) * hd
                n_s[j, c] = out[:2 * c64, u_col:u_col + hd]
                mp_s[j, c, :2 * c64, :] = out[:2 * c64, w_col:w_col + hd].astype(BF16)
                mp_s[j, c, 2 * c64:, :] = (qd_s[j, r, :].astype(F32)
                                           - out[2 * c64:, w_col:w_col + hd]).astype(BF16)
                r_s[j, r, :] = out[2 * c64:, u_col:u_col + hd]

        return finish

    _software_pipeline([stage_gram, stage_factor]
                       + [stage_merge(lv) for lv in range(MERGE_LEVELS)]
                       + [stage_solve, stage_fold], nchunk)

    st_s[...] = jnp.zeros_like(st_s)

    def phase_c(c, carry):
        r = pl.ds(pl.multiple_of(c * c64, c64), c64)
        sts = [st_s[j] for j in heads]
        outs = [jnp.dot(mp_s[j, c], sts[j].astype(BF16), preferred_element_type=F32)
                for j in heads]
        for j in heads:
            gl = gcr_s[j, pl.ds(c, 1), c64 - 1:c64]
            st_s[j] = sts[j] * jnp.exp(gl) - outs[j][:2 * c64] + n_s[j, c]
            r_s[j, r, :] = r_s[j, r, :] + outs[j][2 * c64:]
        return carry

    lax.fori_loop(0, nchunk, phase_c, 0)

    nw = nw_ref[...]
    for j in heads:
        o = r_s[j]
        og = (o * lax.rsqrt(jnp.mean(o * o, axis=-1, keepdims=True) + RMS_EPS)
              * nw * _silu(z_ref[:, j * hd:(j + 1) * hd]))
        y_ref[:, j * hd:(j + 1) * hd] = og.astype(BF16)


def _gdn(proj, small, rows, conv_w, a_log, dt_bias, norm_w, bsz, seq):
    nchunk = seq // GDN_CHUNK
    hd = HEAD_DIM
    c64 = GDN_CHUNK
    qo, ko = COL_GDN_Q // hd, COL_GDN_K // hd
    vo, zo = COL_GDN_V // (2 * hd), COL_GDN_Z // (2 * hd)
    cvo = (2 * GDN_QK_W) // (2 * hd)
    smem = pl.BlockSpec(memory_space=pltpu.SMEM)
    pad = (GDN_V_HEADS, N_SMALL - 2 * GDN_V_HEADS)
    gate_params = jnp.stack([jnp.pad(a_log, pad), jnp.pad(dt_bias, pad)])
    return pl.pallas_call(
        functools.partial(_gdn_kernel, nchunk=nchunk),
        name="gdn",
        grid=(bsz, GDN_QK_HEADS),
        in_specs=[smem, smem,
                  pl.BlockSpec((2, N_SMALL), lambda b, h: (0, 0)),
                  pl.BlockSpec((1, hd), lambda b, h: (0, 0)),
                  pl.BlockSpec((GDN_CONV, hd), lambda b, h: (0, h)),
                  pl.BlockSpec((GDN_CONV, hd), lambda b, h: (0, GDN_QK_HEADS + h)),
                  pl.BlockSpec((GDN_CONV, 2 * hd), lambda b, h: (0, cvo + h)),
                  pl.BlockSpec((seq, hd), lambda b, h: (b, qo + h)),
                  pl.BlockSpec((seq, hd), lambda b, h: (b, ko + h)),
                  pl.BlockSpec((seq, 2 * hd), lambda b, h: (b, vo + h)),
                  pl.BlockSpec((seq, 2 * hd), lambda b, h: (b, zo + h)),
                  pl.BlockSpec((seq, N_SMALL), lambda b, h: (b, 0)),
                  pl.BlockSpec((None, None, 4, nchunk, GDN_CHUNK), lambda b, h: (b, h, 0, 0, 0))],
        out_specs=pl.BlockSpec((seq, 2 * hd), lambda b, h: (b, h)),
        out_shape=jax.ShapeDtypeStruct((bsz * seq, GDN_V_W), BF16),
        scratch_shapes=[pltpu.VMEM((seq, hd), BF16),
                        pltpu.VMEM((seq, hd), BF16),
                        pltpu.VMEM((2, seq, hd), BF16),
                        pltpu.VMEM((nchunk, 2 * c64, hd), BF16),
                        pltpu.VMEM((2, seq, 2 * hd), BF16),
                        pltpu.VMEM((2, seq, hd), BF16),
                        pltpu.VMEM((2, seq, c64), F32),
                        pltpu.VMEM((2, nchunk, c64), F32),
                        pltpu.VMEM((3, seq, c64), F32),
                        pltpu.VMEM((2, seq, c64), F32),
                        pltpu.VMEM((nchunk, 3 * c64, hd), BF16),
                        pltpu.VMEM((2, seq, 2 * hd), BF16),
                        pltpu.VMEM((2, nchunk, 3 * c64, hd), BF16),
                        pltpu.VMEM((2, nchunk, hd, hd), F32),
                        pltpu.VMEM((2, seq, hd), F32),
                        pltpu.VMEM((2, hd, hd), F32)]
                       + [pltpu.VMEM((2, seq, c64), F32)] * (MERGE_LEVELS + 1),
        compiler_params=_vmem(58),
    )(a_log, dt_bias, gate_params, norm_w, conv_w, conv_w, conv_w, proj, proj, proj, proj, small, rows)


def _merge_kernel(ya_ref, yb_ref, ga_ref, gb_ref, x_ref, g1_ref, sh2_ref, sc2_ref,
                  lng_ref, lnb_ref, wpm_ref, wpg_ref, wo_ref, x1_ref, h2_ref):
    pa = jnp.dot(ya_ref[...], wpm_ref[...], preferred_element_type=F32)
    pb = jnp.dot(yb_ref[...], wpg_ref[...], preferred_element_type=F32)
    merged = jax.nn.sigmoid(ga_ref[...]) * pa + jax.nn.sigmoid(gb_ref[...]) * pb
    y = jnp.dot(merged.astype(BF16), wo_ref[...], preferred_element_type=F32)
    x1 = _layer_norm(DEEPNORM_ALPHA * x_ref[...] + g1_ref[...] * y, lng_ref[...], lnb_ref[...])
    x1_ref[...] = x1
    h2_ref[...] = (x1 * (1.0 + sc2_ref[...]) + sh2_ref[...]).astype(BF16)


def _merge(ya, yb, proj, x2, mod3, ln_g, ln_b, wpm, wpg, wo, seq):
    t = x2.shape[0]
    tm = 256
    per_b = seq // tm
    d = D_MODEL

    def modspec(k):
        return pl.BlockSpec((None, 1, d), lambda i: (i // per_b, 0, k))

    def const(shape):
        return pl.BlockSpec(shape, lambda i: (0, 0), pipeline_mode=pl.Buffered(1))

    return pl.pallas_call(
        _merge_kernel,
        name="merge",
        grid=(t // tm,),
        in_specs=[pl.BlockSpec((tm, MOBA_W), lambda i: (i, 0)),
                  pl.BlockSpec((tm, GDN_V_W), lambda i: (i, 0)),
                  pl.BlockSpec((tm, d), lambda i: (i, COL_GATE_A // d)),
                  pl.BlockSpec((tm, d), lambda i: (i, COL_GATE_B // d)),
                  pl.BlockSpec((tm, d), lambda i: (i, 0)),
                  modspec(2), modspec(3), modspec(4),
                  const((1, d)), const((1, d)),
                  const((MOBA_W, d)), const((GDN_V_W, d)), const((d, d))],
        out_specs=[pl.BlockSpec((tm, d), lambda i: (i, 0)),
                   pl.BlockSpec((tm, d), lambda i: (i, 0))],
        out_shape=[jax.ShapeDtypeStruct((t, d), F32), jax.ShapeDtypeStruct((t, d), BF16)],
        compiler_params=_vmem(56),
    )(ya, yb, proj, proj, x2, mod3, mod3, mod3, ln_g, ln_b, wpm, wpg, wo)


def _ffn_kernel(h_ref, x1_ref, g2_ref, lng_ref, lnb_ref, wg_ref, wu_ref, wo_ref, o_ref, acc_ref):
    f = pl.program_id(1)

    @pl.when(f == 0)
    def _():
        acc_ref[...] = jnp.zeros_like(acc_ref)

    h = h_ref[...]
    gate = jnp.dot(h, wg_ref[...], preferred_element_type=F32)
    up = jnp.dot(h, wu_ref[...], preferred_element_type=F32)
    act = (_silu(gate) * up).astype(BF16)
    acc_ref[...] += jnp.dot(act, wo_ref[...], preferred_element_type=F32)

    @pl.when(f == pl.num_programs(1) - 1)
    def _():
        r = DEEPNORM_ALPHA * x1_ref[...] + g2_ref[...] * acc_ref[...]
        o_ref[...] = _layer_norm(r, lng_ref[...], lnb_ref[...])


def _ffn(h2, x1, mod3, ln_g, ln_b, w_in, w_out, seq):
    t = h2.shape[0]
    tm, tf = 512, 512
    per_b = seq // tm
    d = D_MODEL
    nf = D_FF // tf
    return pl.pallas_call(
        _ffn_kernel,
        name="ffn",
        grid=(t // tm, nf),
        in_specs=[pl.BlockSpec((tm, d), lambda i, f: (i, 0)),
                  pl.BlockSpec((tm, d), lambda i, f: (i, 0)),
                  pl.BlockSpec((None, 1, d), lambda i, f: (i // per_b, 0, 5)),
                  pl.BlockSpec((1, d), lambda i, f: (0, 0)),
                  pl.BlockSpec((1, d), lambda i, f: (0, 0)),
                  pl.BlockSpec((d, tf), lambda i, f: (0, f)),
                  pl.BlockSpec((d, tf), lambda i, f: (0, nf + f)),
                  pl.BlockSpec((tf, d), lambda i, f: (f, 0))],
        out_specs=pl.BlockSpec((tm, d), lambda i, f: (i, 0)),
        out_shape=jax.ShapeDtypeStruct((t, d), F32),
        scratch_shapes=[pltpu.VMEM((tm, d), F32)],
        compiler_params=_vmem(48),
    )(h2, x1, mod3, ln_g, ln_b, w_in, w_in, w_out)


def _rel_bucket(dist):
    max_exact = REL_BUCKETS // 2
    n = jnp.maximum(dist, 0)
    nf = jnp.maximum(n, 1).astype(F32)
    large = max_exact + (jnp.log(nf / max_exact) / math.log(REL_MAX_DIST / max_exact)
                         * (REL_BUCKETS - max_exact)).astype(jnp.int32)
    large = jnp.minimum(large, REL_BUCKETS - 1)
    return jnp.where(n < max_exact, n, large)


def _layer(x, c, w_ada, b_ada, w_in, conv_w, a_log, dt_bias, gdn_norm_w, rel_bias,
           w_proj_moba, w_proj_gdn, w_out, ln1_g, ln1_b, w_ffn_in, w_ffn_out, ln2_g, ln2_b):
    bsz, seq, d = x.shape
    t = bsz * seq
    x2 = x.reshape(t, d)

    mod = _ada_mod(c, w_ada, b_ada)
    mod3 = mod.reshape(bsz, 1, 6 * d)

    o1 = 3 * MOBA_W
    o2 = o1 + 2 * GDN_QK_W + GDN_V_W
    o3 = o2 + GDN_V_W
    o5 = o3 + 2 * GDN_V_HEADS
    w_main = jnp.concatenate([
        w_in[:, o5:],
        w_in[:, o1 + 2 * GDN_QK_W:o2],
        w_in[:, o2:o3],
        w_in[:, :o1],
        w_in[:, o1:o1 + 2 * GDN_QK_W],
    ], axis=1).astype(BF16)
    w_small = jnp.pad(w_in[:, o3:o5], ((0, 0), (0, N_SMALL - 2 * GDN_V_HEADS))).astype(BF16)

    proj, small = _in_proj(x2, mod3, w_main, w_small, seq)

    ii = jnp.arange(MOBA_BLOCK, dtype=jnp.int32)
    dist = ii[:, None] - ii[None, :]
    bko = _rel_bucket(dist)
    bkp = _rel_bucket(dist + MOBA_BLOCK)
    ya = _moba(proj, rel_bias, bko, bkp, bsz, seq)

    nchunk = seq // GDN_CHUNK
    sm_t = small[:, :2 * GDN_V_HEADS].reshape(bsz, seq, 2, GDN_QK_HEADS, 2)
    rows = sm_t.transpose(0, 3, 2, 4, 1).reshape(bsz, GDN_QK_HEADS, 4, nchunk, GDN_CHUNK)
    yb = _gdn(proj, small, rows, conv_w, a_log, dt_bias, gdn_norm_w.reshape(1, HEAD_DIM),
              bsz, seq)

    x1, h2 = _merge(ya, yb, proj, x2, mod3, ln1_g.reshape(1, d), ln1_b.reshape(1, d),
                    w_proj_moba.astype(BF16), w_proj_gdn.astype(BF16), w_out.astype(BF16), seq)
    out = _ffn(h2, x1, mod3, ln2_g.reshape(1, d), ln2_b.reshape(1, d),
               w_ffn_in.astype(BF16), w_ffn_out.astype(BF16), seq)
    return out.reshape(bsz, seq, d)


def kernel(x, c, w_ada, b_ada, w_in, conv_w, a_log, dt_bias, gdn_norm_w, rel_bias, w_proj_moba,
           w_proj_gdn, w_out, ln1_g, ln1_b, w_ffn_in, w_ffn_out, ln2_g, ln2_b):
    depth = w_ada.shape[0]
    for l in range(depth):
        x = _layer(x, c, w_ada[l], b_ada[l], w_in[l], conv_w[l], a_log[l], dt_bias[l],
                   gdn_norm_w[l], rel_bias, w_proj_moba[l], w_proj_gdn[l], w_out[l],
                   ln1_g[l], ln1_b[l], w_ffn_in[l], w_ffn_out[l], ln2_g[l], ln2_b[l])
    return x
```

```python
import functools
import math

import jax
import jax.numpy as jnp
from jax import lax
from jax.experimental import pallas as pl
from jax.experimental.pallas import tpu as pltpu

F32 = jnp.float32
BF16 = jnp.bfloat16

D_MODEL = 2048
MOBA_HEADS = 8
HEAD_DIM = 128
MOBA_W = MOBA_HEADS * HEAD_DIM
MOBA_BLOCK = 256
MOBA_TOPK = 3
REL_BUCKETS = 32
REL_MAX_DIST = 128
GDN_QK_HEADS = 8
GDN_V_HEADS = 16
GDN_QK_W = GDN_QK_HEADS * HEAD_DIM
GDN_V_W = GDN_V_HEADS * HEAD_DIM
GDN_CONV = 4
GDN_CHUNK = 64
D_FF = 5632
DEEPNORM_ALPHA = 2.0 ** 0.25
LN_EPS = 1e-5
RMS_EPS = 1e-6
NEG_INF = -1e30

COL_GATE_A = 0
COL_GATE_B = 2048
COL_GDN_V = 4096
COL_GDN_Z = 6144
COL_MOBA_Q = 8192
COL_MOBA_K = 9216
COL_MOBA_V = 10240
COL_GDN_Q = 11264
COL_GDN_K = 12288
N_MAIN = 13312
N_SMALL = 128

V7X_VMEM_BYTES = 64 * 1024 * 1024
LANES = 128


def _vmem(mb):
    return pltpu.CompilerParams(vmem_limit_bytes=mb * 1024 * 1024)


def _silu(x):
    return x * jax.nn.sigmoid(x)


def _layer_norm(r, gain, bias):
    mu = jnp.mean(r, axis=-1, keepdims=True)
    d = r - mu
    var = jnp.mean(d * d, axis=-1, keepdims=True)
    return d * lax.rsqrt(var + LN_EPS) * gain + bias


def _ada_kernel(c_ref, w_ref, b_ref, o_ref):
    sc = _silu(c_ref[...])
    o_ref[...] = jnp.dot(sc, w_ref[...], precision=lax.Precision.HIGHEST,
                         preferred_element_type=F32) + b_ref[...]


def _ada_mod(c, w_ada, b_ada):
    bsz = c.shape[0]
    n = w_ada.shape[1]
    tn = 1024
    return pl.pallas_call(
        _ada_kernel,
        name="ada_mod",
        grid=(n // tn,),
        in_specs=[pl.BlockSpec((bsz, D_MODEL), lambda j: (0, 0)),
                  pl.BlockSpec((D_MODEL, tn), lambda j: (0, j)),
                  pl.BlockSpec((1, tn), lambda j: (0, j))],
        out_specs=pl.BlockSpec((bsz, tn), lambda j: (0, j)),
        out_shape=jax.ShapeDtypeStruct((bsz, n), F32),
        compiler_params=_vmem(40),
    )(c, w_ada, b_ada.reshape(1, n))


def _inproj_kernel(x_ref, sh_ref, sc_ref, w_ref, ws_ref, o_ref, os_ref, h_ref):
    @pl.when(pl.program_id(1) == 0)
    def _():
        h = x_ref[...] * (1.0 + sc_ref[...]) + sh_ref[...]
        h_ref[...] = h.astype(BF16)
        os_ref[...] = jnp.dot(h_ref[...], ws_ref[...], preferred_element_type=F32)

    o_ref[...] = jnp.dot(h_ref[...], w_ref[...], preferred_element_type=F32)


def _in_proj(x2, mod3, w_main, w_small, seq):
    t = x2.shape[0]
    tm, tn = 1024, 1024
    per_b = seq // tm
    return pl.pallas_call(
        _inproj_kernel,
        name="in_proj",
        grid=(t // tm, N_MAIN // tn),
        in_specs=[pl.BlockSpec((tm, D_MODEL), lambda i, j: (i, 0)),
                  pl.BlockSpec((None, 1, D_MODEL), lambda i, j: (i // per_b, 0, 0)),
                  pl.BlockSpec((None, 1, D_MODEL), lambda i, j: (i // per_b, 0, 1)),
                  pl.BlockSpec((D_MODEL, tn), lambda i, j: (0, j)),
                  pl.BlockSpec((D_MODEL, N_SMALL), lambda i, j: (0, 0))],
        out_specs=[pl.BlockSpec((tm, tn), lambda i, j: (i, j)),
                   pl.BlockSpec((tm, N_SMALL), lambda i, j: (i, 0))],
        out_shape=[jax.ShapeDtypeStruct((t, N_MAIN), F32),
                   jax.ShapeDtypeStruct((t, N_SMALL), F32)],
        scratch_shapes=[pltpu.VMEM((tm, D_MODEL), BF16)],
        compiler_params=_vmem(48),
    )(x2, mod3, mod3, w_main, w_small)


def _moba_kernel(rel_ref, bko_ref, bkp_ref, q_ref, k_ref, v_ref, o_ref,
                 bias_own, bias_prev, kb_ref, vt_ref, *, nb):
    h = pl.program_id(0)
    blk = MOBA_BLOCK

    @pl.when(pl.program_id(1) == 0)
    def _():
        bo = bko_ref[...]
        bp = bkp_ref[...]
        acc_o = jnp.zeros((blk, blk), F32)
        acc_p = jnp.zeros((blk, blk), F32)
        for kk in range(REL_BUCKETS):
            val = rel_ref[kk, h]
            acc_o = jnp.where(bo == kk, val, acc_o)
            acc_p = jnp.where(bp == kk, val, acc_p)
        bias_own[...] = acc_o
        bias_prev[...] = acc_p

    bias_far = rel_ref[REL_BUCKETS - 1, h]
    scale = HEAD_DIM ** -0.5
    kf = k_ref[...]
    kmean = jnp.mean(kf.reshape(nb, blk, HEAD_DIM), axis=1)
    kb_ref[...] = kf.astype(BF16)
    nt_dims = (((1,), (1,)), ((), ()))
    eye = (lax.broadcasted_iota(jnp.int32, (HEAD_DIM, HEAD_DIM), 0)
           == lax.broadcasted_iota(jnp.int32, (HEAD_DIM, HEAD_DIM), 1)).astype(BF16)
    vt_ref[...] = lax.dot_general(eye, v_ref[...].astype(BF16), nt_dims,
                                  preferred_element_type=F32).astype(BF16)
    causal = (lax.broadcasted_iota(jnp.int32, (blk, blk), 0)
              <= lax.broadcasted_iota(jnp.int32, (blk, blk), 1))

    for i in range(nb):
        qi = q_ref[i * blk:(i + 1) * blk, :]
        qb = qi.astype(BF16)
        sel = None
        if i > MOBA_TOPK:
            route = lax.dot_general(kmean, qi, nt_dims, precision=lax.Precision.HIGHEST,
                                    preferred_element_type=F32)
            rc = [route[n:n + 1, :] for n in range(i)]
            sel = []
            for n in range(i):
                rank = jnp.zeros((1, blk), jnp.int32)
                for m in range(i):
                    if m == n:
                        continue
                    beats = (rc[m] >= rc[n]) if m < n else (rc[m] > rc[n])
                    rank = rank + beats.astype(jnp.int32)
                sel.append(rank < MOBA_TOPK)
        s_list = []
        for n in range(i + 1):
            s = lax.dot_general(kb_ref[n * blk:(n + 1) * blk, :], qb, nt_dims,
                                preferred_element_type=F32) * scale
            if n == i:
                s = jnp.where(causal, s + bias_own[...], NEG_INF)
            else:
                s = s + (bias_prev[...] if n == i - 1 else bias_far)
                if sel is not None:
                    s = jnp.where(sel[n], s, NEG_INF)
            s_list.append(s)
        m_run = jnp.max(s_list[0], axis=0, keepdims=True)
        for s in s_list[1:]:
            m_run = jnp.maximum(m_run, jnp.max(s, axis=0, keepdims=True))
        l_run = jnp.zeros((1, blk), F32)
        acc = jnp.zeros((HEAD_DIM, blk), F32)
        for n, s in enumerate(s_list):
            p = jnp.exp(s - m_run)
            l_run = l_run + jnp.sum(p, axis=0, keepdims=True)
            acc = acc + jnp.dot(vt_ref[:, n * blk:(n + 1) * blk], p.astype(BF16),
                                preferred_element_type=F32)
        o_ref[i * blk:(i + 1) * blk, :] = (acc / l_run).T.astype(BF16)


def _moba(proj, rel_bias, bko, bkp, bsz, seq):
    nb = seq // MOBA_BLOCK
    qo, ko, vo = COL_MOBA_Q // HEAD_DIM, COL_MOBA_K // HEAD_DIM, COL_MOBA_V // HEAD_DIM
    blk = MOBA_BLOCK
    return pl.pallas_call(
        functools.partial(_moba_kernel, nb=nb),
        name="moba",
        grid=(MOBA_HEADS, bsz),
        in_specs=[pl.BlockSpec(memory_space=pltpu.SMEM),
                  pl.BlockSpec((blk, blk), lambda h, b: (0, 0)),
                  pl.BlockSpec((blk, blk), lambda h, b: (0, 0)),
                  pl.BlockSpec((seq, HEAD_DIM), lambda h, b: (b, qo + h)),
                  pl.BlockSpec((seq, HEAD_DIM), lambda h, b: (b, ko + h)),
                  pl.BlockSpec((seq, HEAD_DIM), lambda h, b: (b, vo + h))],
        out_specs=pl.BlockSpec((seq, HEAD_DIM), lambda h, b: (b, h)),
        out_shape=jax.ShapeDtypeStruct((bsz * seq, MOBA_W), BF16),
        scratch_shapes=[pltpu.VMEM((blk, blk), F32), pltpu.VMEM((blk, blk), F32),
                        pltpu.VMEM((seq, HEAD_DIM), BF16), pltpu.VMEM((HEAD_DIM, seq), BF16)],
        compiler_params=_vmem(48),
    )(rel_bias, bko, bkp, proj, proj, proj)


def _conv_silu(x, w):
    row = lax.broadcasted_iota(jnp.int32, x.shape, 0)
    acc = x * w[GDN_CONV - 1:GDN_CONV, :]
    for s in range(1, GDN_CONV):
        xs = jnp.where(row >= s, pltpu.roll(x, s, axis=0), 0.0)
        acc = acc + xs * w[GDN_CONV - 1 - s:GDN_CONV - s, :]
    return _silu(acc)


def _l2norm(x):
    return x * lax.rsqrt(jnp.sum(x * x, axis=-1, keepdims=True) + RMS_EPS)


def _softplus(x):
    return jnp.maximum(x, 0.0) + jnp.log1p(jnp.exp(-jnp.abs(x)))


def _chunk_cumsum_rows(g):
    pos = lax.broadcasted_iota(jnp.int32, g.shape, 0) % GDN_CHUNK
    sft = 1
    while sft < GDN_CHUNK:
        g = g + jnp.where(pos >= sft, pltpu.roll(g, sft, axis=0), 0.0)
        sft *= 2
    return g


SUBLANES = 8


ELIM_BLOCK = 16
MERGE_LEVELS = (GDN_CHUNK // ELIM_BLOCK).bit_length() - 1


def _block_diag_inverse(lmat):
    n = lmat.shape[0]
    nv = n // SUBLANES
    vpb = ELIM_BLOCK // SUBLANES
    rid = lax.broadcasted_iota(jnp.int32, (SUBLANES, n), 0)
    cid = lax.broadcasted_iota(jnp.int32, (SUBLANES, n), 1)
    t_rows = [(cid == rid + v * SUBLANES).astype(F32) for v in range(nv)]
    l_rows = [lmat[v * SUBLANES:(v + 1) * SUBLANES, :] for v in range(nv)]
    for m in range(n - 1):
        v0, s0 = divmod(m, SUBLANES)
        v_end = (v0 // vpb + 1) * vpb
        row = t_rows[v0][s0:s0 + 1, :]
        for v in range(v0 if s0 < SUBLANES - 1 else v0 + 1, v_end):
            t_rows[v] = t_rows[v] - l_rows[v][:, m:m + 1] * row
    return jnp.concatenate(t_rows, axis=0)


def _merge_block_pairs(lmats, ts, k):
    n = lmats[0].shape[0]
    ri = lax.broadcasted_iota(jnp.int32, (n, n), 0)
    ci = lax.broadcasted_iota(jnp.int32, (n, n), 1)
    off = (ri // (2 * k) == ci // (2 * k)) & (ri // k > ci // k)
    lks = [jnp.where(off, lm, 0.0).astype(BF16) for lm in lmats]
    tbs = [t.astype(BF16) for t in ts]
    lts = [jnp.dot(lk, tb, preferred_element_type=F32) for lk, tb in zip(lks, tbs)]

    def finish():
        return [t - jnp.dot(tb, lt.astype(BF16), preferred_element_type=F32)
                for t, tb, lt in zip(ts, tbs, lts)]

    return finish


def _software_pipeline(stages, n):
    ns = len(stages)

    def run(it, lo, hi):
        conts = [stages[s](it - s) for s in reversed(range(lo, hi))]
        for cont in conts:
            if cont is not None:
                cont()

    for it in range(ns - 1):
        run(it, 0, it + 1)

    def body(it, carry):
        run(it, 0, ns)
        return carry

    lax.fori_loop(ns - 1, n, body, 0)
    for it in range(n, n + ns - 1):
        run(it, it - n + 1, ns)


def _lane_pick(x, lane, idx):
    return jnp.sum(jnp.where(lane == idx, x, 0.0), axis=-1, keepdims=True)


def _gdn_kernel(alog_ref, dtb_ref, gp_ref, nw_ref, cwq_ref, cwk_ref, cwv_ref,
                q_ref, k_ref, v_ref, z_ref, sm_ref, rw_ref, y_ref,
                kb_s, qn_s, kbeta_s, kdp_s, rhs_s, qd_s, gcb_s, gcr_s, gram_s, lm_s, lhs2_s,
                sol_s, mp_s, n_s, r_s, st_s, *tl_s, nchunk):
    hq = pl.program_id(1)
    c64 = GDN_CHUNK
    hd = HEAD_DIM
    heads = range(2)
    seq = q_ref.shape[0]

    qn = _l2norm(_conv_silu(q_ref[...], cwq_ref[...])) * (hd ** -0.5)
    kn = _l2norm(_conv_silu(k_ref[...], cwk_ref[...]))
    qn_s[...] = qn.astype(BF16)
    kb_s[...] = kn.astype(BF16)
    vc = _conv_silu(v_ref[...], cwv_ref[...])
    sm = sm_ref[...]
    lane = lax.broadcasted_iota(jnp.int32, sm.shape, 1)
    sig_all = jax.nn.sigmoid(sm)
    g_all = -jnp.exp(gp_ref[0:1, :]) * _softplus(sm + gp_ref[1:2, :])
    gc_all = _chunk_cumsum_rows(g_all)
    gc_end = jnp.broadcast_to(gc_all.reshape(nchunk, c64, N_SMALL)[:, c64 - 1:, :],
                              (nchunk, c64, N_SMALL)).reshape(seq, N_SMALL)
    rest_all = gc_end - gc_all
    tri_u = (lax.broadcasted_iota(jnp.int32, (c64, c64), 0)
             <= lax.broadcasted_iota(jnp.int32, (c64, c64), 1)).astype(F32)
    for j in heads:
        hv = 2 * hq + j
        beta = _lane_pick(sig_all, lane, hv)
        gc_col = _lane_pick(gc_all, lane, GDN_V_HEADS + hv)
        rest_col = _lane_pick(rest_all, lane, GDN_V_HEADS + hv)
        eg = jnp.exp(gc_col)
        rhs_s[j, :, :hd] = (vc[:, j * hd:(j + 1) * hd] * beta).astype(BF16)
        rhs_s[j, :, hd:] = (kn * (beta * eg)).astype(BF16)
        qd_s[j] = (qn * eg).astype(BF16)
        kbeta_s[j] = (kn * beta).astype(BF16)
        kdp_s[:, j * c64:(j + 1) * c64, :] = (
            (kn * jnp.exp(rest_col)).astype(BF16).reshape(nchunk, c64, hd))
        gcb_s[j] = jnp.broadcast_to(gc_col, (seq, c64))
        a_neg_r = -jnp.exp(jnp.full((nchunk, c64), alog_ref[hv], F32))
        g_row = a_neg_r * _softplus(rw_ref[2 + j] + dtb_ref[hv])
        gcr_s[j] = jnp.dot(g_row, tri_u, precision=lax.Precision.HIGHEST,
                           preferred_element_type=F32)

    rows = lax.broadcasted_iota(jnp.int32, (c64, c64), 0)
    cols = lax.broadcasted_iota(jnp.int32, (c64, c64), 1)
    tril = rows >= cols
    strict = rows > cols
    eye2 = (lax.broadcasted_iota(jnp.int32, (2 * c64, 2 * c64), 0)
            == lax.broadcasted_iota(jnp.int32, (2 * c64, 2 * c64), 1)).astype(BF16)
    nt_dims = (((1,), (1,)), ((), ()))

    def rows_of(c):
        start = c * c64
        return pl.ds(start if isinstance(c, int) else pl.multiple_of(start, c64), c64)

    def stage_gram(c):
        r = rows_of(c)
        kb = kb_s[r, :]
        grams = [lax.dot_general(kbeta_s[j, r, :], kb, nt_dims, preferred_element_type=F32)
                 for j in heads]
        grams.append(lax.dot_general(qn_s[r, :], kb, nt_dims, preferred_element_type=F32))
        kd_t = lax.dot_general(eye2, kdp_s[c], nt_dims, preferred_element_type=F32)

        def finish():
            for i, g in enumerate(grams):
                gram_s[i, r, :] = g
            lhs2_s[c, :2 * c64, :] = kd_t.astype(BF16)

        return finish

    def stage_factor(c):
        r = rows_of(c)
        qk = gram_s[2, r, :]
        ats = []
        for j in heads:
            dec = jnp.exp(jnp.where(tril, gcb_s[j, r, :] - gcr_s[j, pl.ds(c, 1), :], NEG_INF))
            lmat = jnp.where(strict, gram_s[j, r, :] * dec, 0.0)
            lm_s[j, r, :] = lmat
            tl_s[0][j, r, :] = _block_diag_inverse(lmat)
            ats.append(qk * dec)
        lhs2_s[c, 2 * c64:, :] = jnp.concatenate(ats, axis=1).astype(BF16)

    def stage_merge(level):
        def stage(c):
            r = rows_of(c)
            merged = _merge_block_pairs([lm_s[j, r, :] for j in heads],
                                        [tl_s[level][j, r, :] for j in heads],
                                        ELIM_BLOCK << level)

            def finish():
                ts = merged()
                for j in heads:
                    tl_s[level + 1][j, r, :] = ts[j]

            return finish
        return stage

    def stage_solve(c):
        r = rows_of(c)
        sols = [jnp.dot(tl_s[MERGE_LEVELS][j, r, :].astype(BF16), rhs_s[j, r, :],
                        preferred_element_type=F32) for j in heads]

        def finish():
            for j in heads:
                sol_s[j, r, :] = sols[j].astype(BF16)

        return finish

    def stage_fold(c):
        r = rows_of(c)
        zero = jnp.zeros((c64, 2 * hd), BF16)
        rhs_bd = jnp.concatenate([jnp.concatenate([sol_s[0, r, :], zero], axis=1),
                                  jnp.concatenate([zero, sol_s[1, r, :]], axis=1)], axis=0)
        out = jnp.dot(lhs2_s[c], rhs_bd, preferred_element_type=F32)

        def finish():
            for j in heads:
                u_col, w_col = 2 * j * hd, (2 * j + 1) * hd
                n_s[j, c] = out[:2 * c64, u_col:u_col + hd]
                mp_s[j, c, :2 * c64, :] = out[:2 * c64, w_col:w_col + hd].astype(BF16)
                mp_s[j, c, 2 * c64:, :] = (qd_s[j, r, :].astype(F32)
                                           - out[2 * c64:, w_col:w_col + hd]).astype(BF16)
                r_s[j, r, :] = out[2 * c64:, u_col:u_col + hd]

        return finish

    _software_pipeline([stage_gram, stage_factor]
                       + [stage_merge(lv) for lv in range(MERGE_LEVELS)]
                       + [stage_solve, stage_fold], nchunk)

    st_s[...] = jnp.zeros_like(st_s)

    def phase_c(c, carry):
        r = pl.ds(pl.multiple_of(c * c64, c64), c64)
        sts = [st_s[j] for j in heads]
        outs = [jnp.dot(mp_s[j, c], sts[j].astype(BF16), preferred_element_type=F32)
                for j in heads]
        for j in heads:
            gl = gcr_s[j, pl.ds(c, 1), c64 - 1:c64]
            st_s[j] = sts[j] * jnp.exp(gl) - outs[j][:2 * c64] + n_s[j, c]
            r_s[j, r, :] = r_s[j, r, :] + outs[j][2 * c64:]
        return carry

    lax.fori_loop(0, nchunk, phase_c, 0)

    nw = nw_ref[...]
    for j in heads:
        o = r_s[j]
        og = (o * lax.rsqrt(jnp.mean(o * o, axis=-1, keepdims=True) + RMS_EPS)
              * nw * _silu(z_ref[:, j * hd:(j + 1) * hd]))
        y_ref[:, j * hd:(j + 1) * hd] = og.astype(BF16)


def _gdn(proj, small, rows, conv_w, a_log, dt_bias, norm_w, bsz, seq):
    nchunk = seq // GDN_CHUNK
    hd = HEAD_DIM
    c64 = GDN_CHUNK
    qo, ko = COL_GDN_Q // hd, COL_GDN_K // hd
    vo, zo = COL_GDN_V // (2 * hd), COL_GDN_Z // (2 * hd)
    cvo = (2 * GDN_QK_W) // (2 * hd)
    smem = pl.BlockSpec(memory_space=pltpu.SMEM)
    pad = (GDN_V_HEADS, N_SMALL - 2 * GDN_V_HEADS)
    gate_params = jnp.stack([jnp.pad(a_log, pad), jnp.pad(dt_bias, pad)])
    return pl.pallas_call(
        functools.partial(_gdn_kernel, nchunk=nchunk),
        name="gdn",
        grid=(bsz, GDN_QK_HEADS),
        in_specs=[smem, smem,
                  pl.BlockSpec((2, N_SMALL), lambda b, h: (0, 0)),
                  pl.BlockSpec((1, hd), lambda b, h: (0, 0)),
                  pl.BlockSpec((GDN_CONV, hd), lambda b, h: (0, h)),
                  pl.BlockSpec((GDN_CONV, hd), lambda b, h: (0, GDN_QK_HEADS + h)),
                  pl.BlockSpec((GDN_CONV, 2 * hd), lambda b, h: (0, cvo + h)),
                  pl.BlockSpec((seq, hd), lambda b, h: (b, qo + h)),
                  pl.BlockSpec((seq, hd), lambda b, h: (b, ko + h)),
                  pl.BlockSpec((seq, 2 * hd), lambda b, h: (b, vo + h)),
                  pl.BlockSpec((seq, 2 * hd), lambda b, h: (b, zo + h)),
                  pl.BlockSpec((seq, N_SMALL), lambda b, h: (b, 0)),
                  pl.BlockSpec((None, None, 4, nchunk, GDN_CHUNK), lambda b, h: (b, h, 0, 0, 0))],
        out_specs=pl.BlockSpec((seq, 2 * hd), lambda b, h: (b, h)),
        out_shape=jax.ShapeDtypeStruct((bsz * seq, GDN_V_W), BF16),
        scratch_shapes=[pltpu.VMEM((seq, hd), BF16),
                        pltpu.VMEM((seq, hd), BF16),
                        pltpu.VMEM((2, seq, hd), BF16),
                        pltpu.VMEM((nchunk, 2 * c64, hd), BF16),
                        pltpu.VMEM((2, seq, 2 * hd), BF16),
                        pltpu.VMEM((2, seq, hd), BF16),
                        pltpu.VMEM((2, seq, c64), F32),
                        pltpu.VMEM((2, nchunk, c64), F32),
                        pltpu.VMEM((3, seq, c64), F32),
                        pltpu.VMEM((2, seq, c64), F32),
                        pltpu.VMEM((nchunk, 3 * c64, hd), BF16),
                        pltpu.VMEM((2, seq, 2 * hd), BF16),
                        pltpu.VMEM((2, nchunk, 3 * c64, hd), BF16),
                        pltpu.VMEM((2, nchunk, hd, hd), F32),
                        pltpu.VMEM((2, seq, hd), F32),
                        pltpu.VMEM((2, hd, hd), F32)]
                       + [pltpu.VMEM((2, seq, c64), F32)] * (MERGE_LEVELS + 1),
        compiler_params=_vmem(58),
    )(a_log, dt_bias, gate_params, norm_w, conv_w, conv_w, conv_w, proj, proj, proj, proj, small, rows)


def _merge_kernel(ya_ref, yb_ref, ga_ref, gb_ref, x_ref, g1_ref, sh2_ref, sc2_ref,
                  lng_ref, lnb_ref, wpm_ref, wpg_ref, wo_ref, x1_ref, h2_ref):
    pa = jnp.dot(ya_ref[...], wpm_ref[...], preferred_element_type=F32)
    pb = jnp.dot(yb_ref[...], wpg_ref[...], preferred_element_type=F32)
    merged = jax.nn.sigmoid(ga_ref[...]) * pa + jax.nn.sigmoid(gb_ref[...]) * pb
    y = jnp.dot(merged.astype(BF16), wo_ref[...], preferred_element_type=F32)
    x1 = _layer_norm(DEEPNORM_ALPHA * x_ref[...] + g1_ref[...] * y, lng_ref[...], lnb_ref[...])
    x1_ref[...] = x1
    h2_ref[...] = (x1 * (1.0 + sc2_ref[...]) + sh2_ref[...]).astype(BF16)


def _merge(ya, yb, proj, x2, mod3, ln_g, ln_b, wpm, wpg, wo, seq):
    t = x2.shape[0]
    tm = 256
    per_b = seq // tm
    d = D_MODEL

    def modspec(k):
        return pl.BlockSpec((None, 1, d), lambda i: (i // per_b, 0, k))

    def const(shape):
        return pl.BlockSpec(shape, lambda i: (0, 0), pipeline_mode=pl.Buffered(1))

    return pl.pallas_call(
        _merge_kernel,
        name="merge",
        grid=(t // tm,),
        in_specs=[pl.BlockSpec((tm, MOBA_W), lambda i: (i, 0)),
                  pl.BlockSpec((tm, GDN_V_W), lambda i: (i, 0)),
                  pl.BlockSpec((tm, d), lambda i: (i, COL_GATE_A // d)),
                  pl.BlockSpec((tm, d), lambda i: (i, COL_GATE_B // d)),
                  pl.BlockSpec((tm, d), lambda i: (i, 0)),
                  modspec(2), modspec(3), modspec(4),
                  const((1, d)), const((1, d)),
                  const((MOBA_W, d)), const((GDN_V_W, d)), const((d, d))],
        out_specs=[pl.BlockSpec((tm, d), lambda i: (i, 0)),
                   pl.BlockSpec((tm, d), lambda i: (i, 0))],
        out_shape=[jax.ShapeDtypeStruct((t, d), F32), jax.ShapeDtypeStruct((t, d), BF16)],
        compiler_params=_vmem(56),
    )(ya, yb, proj, proj, x2, mod3, mod3, mod3, ln_g, ln_b, wpm, wpg, wo)


def _ffn_kernel(h_ref, x1_ref, g2_ref, lng_ref, lnb_ref, wg_ref, wu_ref, wo_ref, o_ref, acc_ref):
    f = pl.program_id(1)

    @pl.when(f == 0)
    def _():
        acc_ref[...] = jnp.zeros_like(acc_ref)

    h = h_ref[...]
    gate = jnp.dot(h, wg_ref[...], preferred_element_type=F32)
    up = jnp.dot(h, wu_ref[...], preferred_element_type=F32)
    act = (_silu(gate) * up).astype(BF16)
    acc_ref[...] += jnp.dot(act, wo_ref[...], preferred_element_type=F32)

    @pl.when(f == pl.num_programs(1) - 1)
    def _():
        r = DEEPNORM_ALPHA * x1_ref[...] + g2_ref[...] * acc_ref[...]
        o_ref[...] = _layer_norm(r, lng_ref[...], lnb_ref[...])


def _ffn(h2, x1, mod3, ln_g, ln_b, w_in, w_out, seq):
    t = h2.shape[0]
    tm, tf = 512, 512
    per_b = seq // tm
    d = D_MODEL
    nf = D_FF // tf
    return pl.pallas_call(
        _ffn_kernel,
        name="ffn",
        grid=(t // tm, nf),
        in_specs=[pl.BlockSpec((tm, d), lambda i, f: (i, 0)),
                  pl.BlockSpec((tm, d), lambda i, f: (i, 0)),
                  pl.BlockSpec((None, 1, d), lambda i, f: (i // per_b, 0, 5)),
                  pl.BlockSpec((1, d), lambda i, f: (0, 0)),
                  pl.BlockSpec((1, d), lambda i, f: (0, 0)),
                  pl.BlockSpec((d, tf), lambda i, f: (0, f)),
                  pl.BlockSpec((d, tf), lambda i, f: (0, nf + f)),
                  pl.BlockSpec((tf, d), lambda i, f: (f, 0))],
        out_specs=pl.BlockSpec((tm, d), lambda i, f: (i, 0)),
        out_shape=jax.ShapeDtypeStruct((t, d), F32),
        scratch_shapes=[pltpu.VMEM((tm, d), F32)],
        compiler_params=_vmem(48),
    )(h2, x1, mod3, ln_g, ln_b, w_in, w_in, w_out)


def _rel_bucket(dist):
    max_exact = REL_BUCKETS // 2
    n = jnp.maximum(dist, 0)
    nf = jnp.maximum(n, 1).astype(F32)
    large = max_exact + (jnp.log(nf / max_exact) / math.log(REL_MAX_DIST / max_exact)
                         * (REL_BUCKETS - max_exact)).astype(jnp.int32)
    large = jnp.minimum(large, REL_BUCKETS - 1)
    return jnp.where(n < max_exact, n, large)


def _layer(x, c, w_ada, b_ada, w_in, conv_w, a_log, dt_bias, gdn_norm_w, rel_bias,
           w_proj_moba, w_proj_gdn, w_out, ln1_g, ln1_b, w_ffn_in, w_ffn_out, ln2_g, ln2_b):
    bsz, seq, d = x.shape
    t = bsz * seq
    x2 = x.reshape(t, d)

    mod = _ada_mod(c, w_ada, b_ada)
    mod3 = mod.reshape(bsz, 1, 6 * d)

    o1 = 3 * MOBA_W
    o2 = o1 + 2 * GDN_QK_W + GDN_V_W
    o3 = o2 + GDN_V_W
    o5 = o3 + 2 * GDN_V_HEADS
    w_main = jnp.concatenate([
        w_in[:, o5:],
        w_in[:, o1 + 2 * GDN_QK_W:o2],
        w_in[:, o2:o3],
        w_in[:, :o1],
        w_in[:, o1:o1 + 2 * GDN_QK_W],
    ], axis=1).astype(BF16)
    w_small = jnp.pad(w_in[:, o3:o5], ((0, 0), (0, N_SMALL - 2 * GDN_V_HEADS))).astype(BF16)

    proj, small = _in_proj(x2, mod3, w_main, w_small, seq)

    ii = jnp.arange(MOBA_BLOCK, dtype=jnp.int32)
    dist = ii[None, :] - ii[:, None]
    bko = _rel_bucket(dist)
    bkp = _rel_bucket(dist + MOBA_BLOCK)
    ya = _moba(proj, rel_bias, bko, bkp, bsz, seq)

    nchunk = seq // GDN_CHUNK
    sm_t = small[:, :2 * GDN_V_HEADS].reshape(bsz, seq, 2, GDN_QK_HEADS, 2)
    rows = sm_t.transpose(0, 3, 2, 4, 1).reshape(bsz, GDN_QK_HEADS, 4, nchunk, GDN_CHUNK)
    yb = _gdn(proj, small, rows, conv_w, a_log, dt_bias, gdn_norm_w.reshape(1, HEAD_DIM),
              bsz, seq)

    x1, h2 = _merge(ya, yb, proj, x2, mod3, ln1_g.reshape(1, d), ln1_b.reshape(1, d),
                    w_proj_moba.astype(BF16), w_proj_gdn.astype(BF16), w_out.astype(BF16), seq)
    out = _ffn(h2, x1, mod3, ln2_g.reshape(1, d), ln2_b.reshape(1, d),
               w_ffn_in.astype(BF16), w_ffn_out.astype(BF16), seq)
    return out.reshape(bsz, seq, d)


def kernel(x, c, w_ada, b_ada, w_in, conv_w, a_log, dt_bias, gdn_norm_w, rel_bias, w_proj_moba,
           w_proj_gdn, w_out, ln1_g, ln1_b, w_ffn_in, w_ffn_out, ln2_g, ln2_b):
    depth = w_ada.shape[0]
    for l in range(depth):
        x = _layer(x, c, w_ada[l], b_ada[l], w_in[l], conv_w[l], a_log[l], dt_bias[l],
                   gdn_norm_w[l], rel_bias, w_proj_moba[l], w_proj_gdn[l], w_out[l],
                   ln1_g[l], ln1_b[l], w_ffn_in[l], w_ffn_out[l], ln2_g[l], ln2_b[l])
    return x
```

```python
import functools
import math

import jax
import jax.numpy as jnp
from jax import lax
from jax.experimental import pallas as pl
from jax.experimental.pallas import tpu as pltpu

F32 = jnp.float32
BF16 = jnp.bfloat16

D_MODEL = 2048
MOBA_HEADS = 8
HEAD_DIM = 128
MOBA_W = MOBA_HEADS * HEAD_DIM
MOBA_BLOCK = 256
MOBA_TOPK = 3
REL_BUCKETS = 32
REL_MAX_DIST = 128
GDN_QK_HEADS = 8
GDN_V_HEADS = 16
GDN_QK_W = GDN_QK_HEADS * HEAD_DIM
GDN_V_W = GDN_V_HEADS * HEAD_DIM
GDN_CONV = 4
GDN_CHUNK = 64
D_FF = 5632
DEEPNORM_ALPHA = 2.0 ** 0.25
LN_EPS = 1e-5
RMS_EPS = 1e-6
NEG_INF = -1e30

COL_GATE_A = 0
COL_GATE_B = 2048
COL_MOBA_Q = 4096
COL_MOBA_K = 5120
COL_MOBA_V = 6144
COL_GDN_Q = 7168
COL_GDN_K = 8192
COL_GDN_V = 9216
COL_GDN_Z = 11264
N_MAIN = 13312
N_GATE = 4096
N_HEAD = N_MAIN - N_GATE
N_SMALL = 128

V7X_VMEM_BYTES = 64 * 1024 * 1024
LANES = 128


def _vmem(mb):
    return pltpu.CompilerParams(vmem_limit_bytes=mb * 1024 * 1024)


def _silu(x):
    return x * jax.nn.sigmoid(x)


def _layer_norm(r, gain, bias):
    mu = jnp.mean(r, axis=-1, keepdims=True)
    d = r - mu
    var = jnp.mean(d * d, axis=-1, keepdims=True)
    return d * lax.rsqrt(var + LN_EPS) * gain + bias


def _ada_kernel(c_ref, w_ref, b_ref, o_ref):
    sc = _silu(c_ref[...])
    o_ref[...] = jnp.dot(sc, w_ref[...], precision=lax.Precision.HIGHEST,
                         preferred_element_type=F32) + b_ref[...]


def _ada_mod(c, w_ada, b_ada):
    bsz = c.shape[0]
    n = w_ada.shape[1]
    tn = 1024
    return pl.pallas_call(
        _ada_kernel,
        name="ada_mod",
        grid=(n // tn,),
        in_specs=[pl.BlockSpec((bsz, D_MODEL), lambda j: (0, 0)),
                  pl.BlockSpec((D_MODEL, tn), lambda j: (0, j)),
                  pl.BlockSpec((1, tn), lambda j: (0, j))],
        out_specs=pl.BlockSpec((bsz, tn), lambda j: (0, j)),
        out_shape=jax.ShapeDtypeStruct((bsz, n), F32),
        compiler_params=_vmem(40),
    )(c, w_ada, b_ada.reshape(1, n))


def _inproj_kernel(x_ref, sh_ref, sc_ref, w_ref, wg_ref, ws_ref, o_ref, os_ref, h_ref, *, n_head):
    j = pl.program_id(1)

    @pl.when(j == 0)
    def _():
        h = x_ref[...] * (1.0 + sc_ref[...]) + sh_ref[...]
        h_ref[...] = h.astype(BF16)
        os_ref[...] = jnp.dot(h_ref[...], ws_ref[...], preferred_element_type=F32)

    @pl.when(j < n_head)
    def _():
        o_ref[...] = jnp.dot(h_ref[...], w_ref[...], preferred_element_type=F32)

    @pl.when(j >= n_head)
    def _():
        o_ref[...] = jnp.dot(h_ref[...], wg_ref[...], preferred_element_type=F32)


def _in_proj(x2, mod3, w_all, w_gate, w_small, seq):
    t = x2.shape[0]
    tm, tn = 1024, 1024
    per_b = seq // tm
    n_head, n_gate, n_tiles = N_HEAD // tn, N_GATE // tn, N_MAIN // tn
    return pl.pallas_call(
        functools.partial(_inproj_kernel, n_head=n_head),
        name="in_proj",
        grid=(t // tm, n_tiles),
        in_specs=[pl.BlockSpec((tm, D_MODEL), lambda i, j: (i, 0)),
                  pl.BlockSpec((None, 1, D_MODEL), lambda i, j: (i // per_b, 0, 0)),
                  pl.BlockSpec((None, 1, D_MODEL), lambda i, j: (i // per_b, 0, 1)),
                  pl.BlockSpec((D_MODEL, tn), lambda i, j: (0, jnp.minimum(j, n_head - 1))),
                  pl.BlockSpec((D_MODEL, tn), lambda i, j: (0, jnp.maximum(j - n_head, 0))),
                  pl.BlockSpec((D_MODEL, N_SMALL), lambda i, j: (0, 0))],
        out_specs=[pl.BlockSpec((tm, tn), lambda i, j: (i, (j + n_gate) % n_tiles)),
                   pl.BlockSpec((tm, N_SMALL), lambda i, j: (i, 0))],
        out_shape=[jax.ShapeDtypeStruct((t, N_MAIN), F32),
                   jax.ShapeDtypeStruct((t, N_SMALL), F32)],
        scratch_shapes=[pltpu.VMEM((tm, D_MODEL), BF16)],
        compiler_params=_vmem(56),
    )(x2, mod3, mod3, w_all, w_gate, w_small)


def _moba_kernel(rel_ref, bko_ref, bkp_ref, q_ref, k_ref, v_ref, o_ref,
                 bias_own, bias_prev, kb_ref, vt_ref, *, nb):
    h = pl.program_id(0)
    blk = MOBA_BLOCK

    @pl.when(pl.program_id(1) == 0)
    def _():
        bo = bko_ref[...]
        bp = bkp_ref[...]
        acc_o = jnp.zeros((blk, blk), F32)
        acc_p = jnp.zeros((blk, blk), F32)
        for kk in range(REL_BUCKETS):
            val = rel_ref[kk, h]
            acc_o = jnp.where(bo == kk, val, acc_o)
            acc_p = jnp.where(bp == kk, val, acc_p)
        bias_own[...] = acc_o
        bias_prev[...] = acc_p

    bias_far = rel_ref[REL_BUCKETS - 1, h]
    scale = HEAD_DIM ** -0.5
    kf = k_ref[...]
    kmean = jnp.mean(kf.reshape(nb, blk, HEAD_DIM), axis=1)
    kb_ref[...] = kf.astype(BF16)
    nt_dims = (((1,), (1,)), ((), ()))
    eye = (lax.broadcasted_iota(jnp.int32, (HEAD_DIM, HEAD_DIM), 0)
           == lax.broadcasted_iota(jnp.int32, (HEAD_DIM, HEAD_DIM), 1)).astype(BF16)
    vt_ref[...] = lax.dot_general(eye, v_ref[...].astype(BF16), nt_dims,
                                  preferred_element_type=F32).astype(BF16)
    causal = (lax.broadcasted_iota(jnp.int32, (blk, blk), 0)
              <= lax.broadcasted_iota(jnp.int32, (blk, blk), 1))

    for i in range(nb):
        qi = q_ref[i * blk:(i + 1) * blk, :]
        qb = qi.astype(BF16)
        sel = None
        if i > MOBA_TOPK:
            route = lax.dot_general(kmean, qi, nt_dims, precision=lax.Precision.HIGHEST,
                                    preferred_element_type=F32)
            rc = [route[n:n + 1, :] for n in range(i)]
            sel = []
            for n in range(i):
                rank = jnp.zeros((1, blk), jnp.int32)
                for m in range(i):
                    if m == n:
                        continue
                    beats = (rc[m] >= rc[n]) if m < n else (rc[m] > rc[n])
                    rank = rank + beats.astype(jnp.int32)
                sel.append(rank < MOBA_TOPK)
        s_list = []
        for n in range(i + 1):
            s = lax.dot_general(kb_ref[n * blk:(n + 1) * blk, :], qb, nt_dims,
                                preferred_element_type=F32) * scale
            if n == i:
                s = jnp.where(causal, s + bias_own[...], NEG_INF)
            else:
                s = s + (bias_prev[...] if n == i - 1 else bias_far)
                if sel is not None:
                    s = jnp.where(sel[n], s, NEG_INF)
            s_list.append(s)
        m_run = jnp.max(s_list[0], axis=0, keepdims=True)
        for s in s_list[1:]:
            m_run = jnp.maximum(m_run, jnp.max(s, axis=0, keepdims=True))
        l_run = jnp.zeros((1, blk), F32)
        acc = jnp.zeros((HEAD_DIM, blk), F32)
        for n, s in enumerate(s_list):
            p = jnp.exp(s - m_run)
            l_run = l_run + jnp.sum(p, axis=0, keepdims=True)
            acc = acc + jnp.dot(vt_ref[:, n * blk:(n + 1) * blk], p.astype(BF16),
                                preferred_element_type=F32)
        o_ref[i * blk:(i + 1) * blk, :] = (acc / l_run).T.astype(BF16)


def _moba(proj, rel_bias, bko, bkp, bsz, seq):
    nb = seq // MOBA_BLOCK
    qo, ko, vo = COL_MOBA_Q // HEAD_DIM, COL_MOBA_K // HEAD_DIM, COL_MOBA_V // HEAD_DIM
    blk = MOBA_BLOCK
    return pl.pallas_call(
        functools.partial(_moba_kernel, nb=nb),
        name="moba",
        grid=(MOBA_HEADS, bsz),
        in_specs=[pl.BlockSpec(memory_space=pltpu.SMEM),
                  pl.BlockSpec((blk, blk), lambda h, b: (0, 0)),
                  pl.BlockSpec((blk, blk), lambda h, b: (0, 0)),
                  pl.BlockSpec((seq, HEAD_DIM), lambda h, b: (b, qo + h)),
                  pl.BlockSpec((seq, HEAD_DIM), lambda h, b: (b, ko + h)),
                  pl.BlockSpec((seq, HEAD_DIM), lambda h, b: (b, vo + h))],
        out_specs=pl.BlockSpec((seq, HEAD_DIM), lambda h, b: (b, h)),
        out_shape=jax.ShapeDtypeStruct((bsz * seq, MOBA_W), BF16),
        scratch_shapes=[pltpu.VMEM((blk, blk), F32), pltpu.VMEM((blk, blk), F32),
                        pltpu.VMEM((seq, HEAD_DIM), BF16), pltpu.VMEM((HEAD_DIM, seq), BF16)],
        compiler_params=_vmem(48),
    )(rel_bias, bko, bkp, proj, proj, proj)


def _conv_silu(x, w):
    row = lax.broadcasted_iota(jnp.int32, x.shape, 0)
    acc = x * w[GDN_CONV - 1:GDN_CONV, :]
    for s in range(1, GDN_CONV):
        xs = jnp.where(row >= s, pltpu.roll(x, s, axis=0), 0.0)
        acc = acc + xs * w[GDN_CONV - 1 - s:GDN_CONV - s, :]
    return _silu(acc)


def _l2norm(x):
    return x * lax.rsqrt(jnp.sum(x * x, axis=-1, keepdims=True) + RMS_EPS)


def _softplus(x):
    return jnp.maximum(x, 0.0) + jnp.log1p(jnp.exp(-jnp.abs(x)))


def _chunk_cumsum_rows(g):
    pos = lax.broadcasted_iota(jnp.int32, g.shape, 0) % GDN_CHUNK
    sft = 1
    while sft < GDN_CHUNK:
        g = g + jnp.where(pos >= sft, pltpu.roll(g, sft, axis=0), 0.0)
        sft *= 2
    return g


SUBLANES = 8


ELIM_BLOCK = 8
MERGE_LEVELS = (GDN_CHUNK // ELIM_BLOCK).bit_length() - 1


def _block_diag_inverse(lpair):
    n = lpair.shape[0]
    nv = n // SUBLANES
    vpb = ELIM_BLOCK // SUBLANES
    rid = lax.broadcasted_iota(jnp.int32, (SUBLANES, 2 * n), 0)
    lid = lax.broadcasted_iota(jnp.int32, (SUBLANES, 2 * n), 1)
    t_rows = [(lid % n == rid + v * SUBLANES).astype(F32) for v in range(nv)]
    l_rows = [lpair[v * SUBLANES:(v + 1) * SUBLANES, :] for v in range(nv)]
    for m in range(n - 1):
        v0, s0 = divmod(m, SUBLANES)
        v_end = (v0 // vpb + 1) * vpb
        row = t_rows[v0][s0:s0 + 1, :]
        col = (lid // n) * n + m
        for v in range(v0 if s0 < SUBLANES - 1 else v0 + 1, v_end):
            t_rows[v] = t_rows[v] - jnp.take_along_axis(l_rows[v], col, axis=1) * row
    return jnp.concatenate(t_rows, axis=0)


def _lane_block_diag(pair):
    first = lax.broadcasted_iota(jnp.int32, pair.shape, 1) < pair.shape[1] // 2
    zero = jnp.zeros_like(pair)
    return jnp.concatenate([jnp.where(first, pair, zero), jnp.where(first, zero, pair)], axis=0)


def _merge_lower_products(lpair, tpair, k):
    n = lpair.shape[0]
    ri = lax.broadcasted_iota(jnp.int32, lpair.shape, 0)
    ci = lax.broadcasted_iota(jnp.int32, lpair.shape, 1) % n
    off = (ri // (2 * k) == ci // (2 * k)) & (ri // k > ci // k)
    lk = jnp.where(off, lpair, 0.0).astype(BF16)
    return jnp.dot(lk, _lane_block_diag(tpair.astype(BF16)), preferred_element_type=F32)


def _merge_apply(tpair, lt):
    return tpair - jnp.dot(tpair.astype(BF16), _lane_block_diag(lt), preferred_element_type=F32)


def _software_pipeline(stages, n):
    ns = len(stages)

    def run(it, lo, hi):
        conts = [stages[s](it - s) for s in reversed(range(lo, hi))]
        for cont in conts:
            if cont is not None:
                cont()

    for it in range(ns - 1):
        run(it, 0, it + 1)

    def body(it, carry):
        run(it, 0, ns)
        return carry

    lax.fori_loop(ns - 1, n, body, 0)
    for it in range(n, n + ns - 1):
        run(it, it - n + 1, ns)


def _lane_pick(x, lane, idx):
    return jnp.sum(jnp.where(lane == idx, x, 0.0), axis=-1, keepdims=True)


def _gdn_kernel(alog_ref, dtb_ref, gp_ref, nw_ref, cwq_ref, cwk_ref, cwv_ref,
                q_ref, k_ref, v_ref, z_ref, sm_ref, rw_ref, y_ref,
                kb_s, qn_s, kdp_s, rhs_s, qd_s, gcb_s, betab_s, gcr_s, gram_s, lm_s, lhs2_s,
                sol_s, mp_s, n_s, r_s, st_s, *merge_s, nchunk):
    tl_s, lt_s = merge_s[:MERGE_LEVELS + 1], merge_s[MERGE_LEVELS + 1:]
    hq = pl.program_id(1)
    c64 = GDN_CHUNK
    hd = HEAD_DIM
    heads = range(2)
    seq = q_ref.shape[0]

    qn = _l2norm(_conv_silu(q_ref[...], cwq_ref[...])) * (hd ** -0.5)
    kn = _l2norm(_conv_silu(k_ref[...], cwk_ref[...]))
    qn_s[...] = qn.astype(BF16)
    kb_s[...] = kn.astype(BF16)
    vc = _conv_silu(v_ref[...], cwv_ref[...])
    sm = sm_ref[...]
    lane = lax.broadcasted_iota(jnp.int32, sm.shape, 1)
    sig_all = jax.nn.sigmoid(sm)
    g_all = -jnp.exp(gp_ref[0:1, :]) * _softplus(sm + gp_ref[1:2, :])
    gc_all = _chunk_cumsum_rows(g_all)
    gc_end = jnp.broadcast_to(gc_all.reshape(nchunk, c64, N_SMALL)[:, c64 - 1:, :],
                              (nchunk, c64, N_SMALL)).reshape(seq, N_SMALL)
    rest_all = gc_end - gc_all
    tri_u = (lax.broadcasted_iota(jnp.int32, (c64, c64), 0)
             <= lax.broadcasted_iota(jnp.int32, (c64, c64), 1)).astype(F32)
    betas, gc_cols, gc_rows = [], [], []
    for j in heads:
        hv = 2 * hq + j
        beta = _lane_pick(sig_all, lane, hv)
        gc_col = _lane_pick(gc_all, lane, GDN_V_HEADS + hv)
        rest_col = _lane_pick(rest_all, lane, GDN_V_HEADS + hv)
        eg = jnp.exp(gc_col)
        rhs_s[j, :, :hd] = (vc[:, j * hd:(j + 1) * hd] * beta).astype(BF16)
        rhs_s[j, :, hd:] = (kn * (beta * eg)).astype(BF16)
        qd_s[j] = (qn * eg).astype(BF16)
        kdp_s[:, j * c64:(j + 1) * c64, :] = (
            (kn * jnp.exp(rest_col)).astype(BF16).reshape(nchunk, c64, hd))
        betas.append(beta)
        gc_cols.append(gc_col)
        a_neg_r = -jnp.exp(jnp.full((nchunk, c64), alog_ref[hv], F32))
        g_row = a_neg_r * _softplus(rw_ref[2 + j] + dtb_ref[hv])
        gc_rows.append(jnp.dot(g_row, tri_u, precision=lax.Precision.HIGHEST,
                               preferred_element_type=F32))
    first = lane < c64
    gcb_s[...] = jnp.where(first, gc_cols[0], gc_cols[1])
    betab_s[...] = jnp.where(first, betas[0], betas[1])
    gcr_s[...] = jnp.concatenate(gc_rows, axis=1)

    rows = lax.broadcasted_iota(jnp.int32, (c64, 2 * c64), 0)
    cols = lax.broadcasted_iota(jnp.int32, (c64, 2 * c64), 1) % c64
    tril = rows >= cols
    strict = rows > cols
    eye2 = (lax.broadcasted_iota(jnp.int32, (2 * c64, 2 * c64), 0)
            == lax.broadcasted_iota(jnp.int32, (2 * c64, 2 * c64), 1)).astype(BF16)
    nt_dims = (((1,), (1,)), ((), ()))

    def rows_of(c):
        start = c * c64
        return pl.ds(start if isinstance(c, int) else pl.multiple_of(start, c64), c64)

    def stage_gram(c):
        r = rows_of(c)
        kb = kb_s[r, :]
        kq = jnp.concatenate([kb, qn_s[r, :]], axis=0)
        gram = lax.dot_general(kq, jnp.concatenate([kb, kb], axis=0), nt_dims,
                               preferred_element_type=F32)
        kd_t = lax.dot_general(eye2, kdp_s[c], nt_dims, preferred_element_type=F32)

        def finish():
            gram_s[c] = gram
            lhs2_s[c, :2 * c64, :] = kd_t.astype(BF16)

        return finish

    def stage_factor(c):
        r = rows_of(c)
        dec = jnp.exp(jnp.where(tril, gcb_s[r, :] - gcr_s[pl.ds(c, 1), :], NEG_INF))
        lpair = jnp.where(strict, gram_s[c, :c64, :] * dec, 0.0) * betab_s[r, :]
        lm_s[r, :] = lpair
        lhs2_s[c, 2 * c64:, :] = (gram_s[c, c64:, :] * dec).astype(BF16)
        tl_s[0][r, :] = _block_diag_inverse(lpair)

    def stage_merge_products(level):
        def stage(c):
            r = rows_of(c)
            lt = _merge_lower_products(lm_s[r, :], tl_s[level][r, :], ELIM_BLOCK << level)

            def finish():
                lt_s[level][r, :] = lt.astype(BF16)

            return finish
        return stage

    def stage_merge_apply(level):
        def stage(c):
            r = rows_of(c)
            merged = _merge_apply(tl_s[level][r, :], lt_s[level][r, :])

            def finish():
                tl_s[level + 1][r, :] = merged

            return finish
        return stage

    def stage_solve(c):
        r = rows_of(c)
        zero = jnp.zeros((c64, 2 * hd), BF16)
        rhs_bd = jnp.concatenate([jnp.concatenate([rhs_s[0, r, :], zero], axis=1),
                                  jnp.concatenate([zero, rhs_s[1, r, :]], axis=1)], axis=0)
        sol = jnp.dot(tl_s[MERGE_LEVELS][r, :].astype(BF16), rhs_bd,
                      preferred_element_type=F32)

        def finish():
            sol_s[r, :] = sol.astype(BF16)

        return finish

    def stage_fold(c):
        r = rows_of(c)
        sol = sol_s[r, :]
        out = jnp.dot(lhs2_s[c], _lane_block_diag(sol), preferred_element_type=F32)

        def finish():
            for j in heads:
                u_col, w_col = 2 * j * hd, (2 * j + 1) * hd
                n_s[j, c] = out[:2 * c64, u_col:u_col + hd]
                mp_s[j, c, :2 * c64, :] = out[:2 * c64, w_col:w_col + hd].astype(BF16)
                mp_s[j, c, 2 * c64:, :] = (qd_s[j, r, :].astype(F32)
                                           - out[2 * c64:, w_col:w_col + hd]).astype(BF16)
                r_s[j, r, :] = out[2 * c64:, u_col:u_col + hd]

        return finish

    merges = [stage(lv) for lv in range(MERGE_LEVELS)
              for stage in (stage_merge_products, stage_merge_apply)]
    _software_pipeline([stage_gram, stage_factor] + merges + [stage_solve, stage_fold], nchunk)

    st_s[...] = jnp.zeros_like(st_s)

    def phase_c(c, carry):
        r = pl.ds(pl.multiple_of(c * c64, c64), c64)
        sts = [st_s[j] for j in heads]
        outs = [jnp.dot(mp_s[j, c], sts[j].astype(BF16), preferred_element_type=F32)
                for j in heads]
        for j in heads:
            end = (j + 1) * c64
            gl = gcr_s[pl.ds(c, 1), end - 1:end]
            st_s[j] = sts[j] * jnp.exp(gl) - outs[j][:2 * c64] + n_s[j, c]
            r_s[j, r, :] = r_s[j, r, :] + outs[j][2 * c64:]
        return carry

    lax.fori_loop(0, nchunk, phase_c, 0)

    nw = nw_ref[...]
    for j in heads:
        o = r_s[j]
        og = (o * lax.rsqrt(jnp.mean(o * o, axis=-1, keepdims=True) + RMS_EPS)
              * nw * _silu(z_ref[:, j * hd:(j + 1) * hd]))
        y_ref[:, j * hd:(j + 1) * hd] = og.astype(BF16)


def _gdn(proj, small, rows, conv_w, a_log, dt_bias, norm_w, bsz, seq):
    nchunk = seq // GDN_CHUNK
    hd = HEAD_DIM
    c64 = GDN_CHUNK
    qo, ko = COL_GDN_Q // hd, COL_GDN_K // hd
    vo, zo = COL_GDN_V // (2 * hd), COL_GDN_Z // (2 * hd)
    cvo = (2 * GDN_QK_W) // (2 * hd)
    smem = pl.BlockSpec(memory_space=pltpu.SMEM)
    pad = (GDN_V_HEADS, N_SMALL - 2 * GDN_V_HEADS)
    gate_params = jnp.stack([jnp.pad(a_log, pad), jnp.pad(dt_bias, pad)])
    return pl.pallas_call(
        functools.partial(_gdn_kernel, nchunk=nchunk),
        name="gdn",
        grid=(bsz, GDN_QK_HEADS),
        in_specs=[smem, smem,
                  pl.BlockSpec((2, N_SMALL), lambda b, h: (0, 0)),
                  pl.BlockSpec((1, hd), lambda b, h: (0, 0)),
                  pl.BlockSpec((GDN_CONV, hd), lambda b, h: (0, h)),
                  pl.BlockSpec((GDN_CONV, hd), lambda b, h: (0, GDN_QK_HEADS + h)),
                  pl.BlockSpec((GDN_CONV, 2 * hd), lambda b, h: (0, cvo + h)),
                  pl.BlockSpec((seq, hd), lambda b, h: (b, qo + h)),
                  pl.BlockSpec((seq, hd), lambda b, h: (b, ko + h)),
                  pl.BlockSpec((seq, 2 * hd), lambda b, h: (b, vo + h)),
                  pl.BlockSpec((seq, 2 * hd), lambda b, h: (b, zo + h)),
                  pl.BlockSpec((seq, N_SMALL), lambda b, h: (b, 0)),
                  pl.BlockSpec((None, None, 4, nchunk, GDN_CHUNK), lambda b, h: (b, h, 0, 0, 0))],
        out_specs=pl.BlockSpec((seq, 2 * hd), lambda b, h: (b, h)),
        out_shape=jax.ShapeDtypeStruct((bsz * seq, GDN_V_W), BF16),
        scratch_shapes=[pltpu.VMEM((seq, hd), BF16),
                        pltpu.VMEM((seq, hd), BF16),
                        pltpu.VMEM((nchunk, 2 * c64, hd), BF16),
                        pltpu.VMEM((2, seq, 2 * hd), BF16),
                        pltpu.VMEM((2, seq, hd), BF16),
                        pltpu.VMEM((seq, 2 * c64), F32),
                        pltpu.VMEM((seq, 2 * c64), F32),
                        pltpu.VMEM((nchunk, 2 * c64), F32),
                        pltpu.VMEM((nchunk, 2 * c64, 2 * c64), F32),
                        pltpu.VMEM((seq, 2 * c64), F32),
                        pltpu.VMEM((nchunk, 3 * c64, hd), BF16),
                        pltpu.VMEM((seq, 4 * hd), BF16),
                        pltpu.VMEM((2, nchunk, 3 * c64, hd), BF16),
                        pltpu.VMEM((2, nchunk, hd, hd), F32),
                        pltpu.VMEM((2, seq, hd), F32),
                        pltpu.VMEM((2, hd, hd), F32)]
                       + [pltpu.VMEM((seq, 2 * c64), F32)] * (MERGE_LEVELS + 1)
                       + [pltpu.VMEM((seq, 2 * c64), BF16)] * MERGE_LEVELS,
        compiler_params=_vmem(58),
    )(a_log, dt_bias, gate_params, norm_w, conv_w, conv_w, conv_w, proj, proj, proj, proj, small, rows)


def _merge_kernel(ya_ref, yb_ref, ga_ref, gb_ref, x_ref, g1_ref, sh2_ref, sc2_ref,
                  lng_ref, lnb_ref, wpm_ref, wpg_ref, wo_ref, x1_ref, h2_ref):
    pa = jnp.dot(ya_ref[...], wpm_ref[...], preferred_element_type=F32)
    pb = jnp.dot(yb_ref[...], wpg_ref[...], preferred_element_type=F32)
    merged = jax.nn.sigmoid(ga_ref[...]) * pa + jax.nn.sigmoid(gb_ref[...]) * pb
    y = jnp.dot(merged.astype(BF16), wo_ref[...], preferred_element_type=F32)
    x1 = _layer_norm(DEEPNORM_ALPHA * x_ref[...] + g1_ref[...] * y, lng_ref[...], lnb_ref[...])
    x1_ref[...] = x1
    h2_ref[...] = (x1 * (1.0 + sc2_ref[...]) + sh2_ref[...]).astype(BF16)


def _merge(ya, yb, proj, x2, mod3, ln_g, ln_b, wpm, wpg, wo, seq):
    t = x2.shape[0]
    tm = 256
    per_b = seq // tm
    d = D_MODEL

    def modspec(k):
        return pl.BlockSpec((None, 1, d), lambda i: (i // per_b, 0, k))

    def const(shape):
        return pl.BlockSpec(shape, lambda i: (0, 0), pipeline_mode=pl.Buffered(1))

    return pl.pallas_call(
        _merge_kernel,
        name="merge",
        grid=(t // tm,),
        in_specs=[pl.BlockSpec((tm, MOBA_W), lambda i: (i, 0)),
                  pl.BlockSpec((tm, GDN_V_W), lambda i: (i, 0)),
                  pl.BlockSpec((tm, d), lambda i: (i, COL_GATE_A // d)),
                  pl.BlockSpec((tm, d), lambda i: (i, COL_GATE_B // d)),
                  pl.BlockSpec((tm, d), lambda i: (i, 0)),
                  modspec(2), modspec(3), modspec(4),
                  const((1, d)), const((1, d)),
                  const((MOBA_W, d)), const((GDN_V_W, d)), const((d, d))],
        out_specs=[pl.BlockSpec((tm, d), lambda i: (i, 0)),
                   pl.BlockSpec((tm, d), lambda i: (i, 0))],
        out_shape=[jax.ShapeDtypeStruct((t, d), F32), jax.ShapeDtypeStruct((t, d), BF16)],
        compiler_params=_vmem(56),
    )(ya, yb, proj, proj, x2, mod3, mod3, mod3, ln_g, ln_b, wpm, wpg, wo)


def _ffn_kernel(h_ref, x1_ref, g2_ref, lng_ref, lnb_ref, wg_ref, wu_ref, wo_ref, o_ref, acc_ref):
    f = pl.program_id(1)

    @pl.when(f == 0)
    def _():
        acc_ref[...] = jnp.zeros_like(acc_ref)

    h = h_ref[...]
    gate = jnp.dot(h, wg_ref[...], preferred_element_type=F32)
    up = jnp.dot(h, wu_ref[...], preferred_element_type=F32)
    act = (_silu(gate) * up).astype(BF16)
    acc_ref[...] += jnp.dot(act, wo_ref[...], preferred_element_type=F32)

    @pl.when(f == pl.num_programs(1) - 1)
    def _():
        r = DEEPNORM_ALPHA * x1_ref[...] + g2_ref[...] * acc_ref[...]
        o_ref[...] = _layer_norm(r, lng_ref[...], lnb_ref[...])


def _ffn(h2, x1, mod3, ln_g, ln_b, w_in, w_out, seq):
    t = h2.shape[0]
    tm, tf = 512, 512
    per_b = seq // tm
    d = D_MODEL
    nf = D_FF // tf
    return pl.pallas_call(
        _ffn_kernel,
        name="ffn",
        grid=(t // tm, nf),
        in_specs=[pl.BlockSpec((tm, d), lambda i, f: (i, 0)),
                  pl.BlockSpec((tm, d), lambda i, f: (i, 0)),
                  pl.BlockSpec((None, 1, d), lambda i, f: (i // per_b, 0, 5)),
                  pl.BlockSpec((1, d), lambda i, f: (0, 0)),
                  pl.BlockSpec((1, d), lambda i, f: (0, 0)),
                  pl.BlockSpec((d, tf), lambda i, f: (0, f)),
                  pl.BlockSpec((d, tf), lambda i, f: (0, nf + f)),
                  pl.BlockSpec((tf, d), lambda i, f: (f, 0))],
        out_specs=pl.BlockSpec((tm, d), lambda i, f: (i, 0)),
        out_shape=jax.ShapeDtypeStruct((t, d), F32),
        scratch_shapes=[pltpu.VMEM((tm, d), F32)],
        compiler_params=_vmem(48),
    )(h2, x1, mod3, ln_g, ln_b, w_in, w_in, w_out)


def _rel_bucket(dist):
    max_exact = REL_BUCKETS // 2
    n = jnp.maximum(dist, 0)
    nf = jnp.maximum(n, 1).astype(F32)
    large = max_exact + (jnp.log(nf / max_exact) / math.log(REL_MAX_DIST / max_exact)
                         * (REL_BUCKETS - max_exact)).astype(jnp.int32)
    large = jnp.minimum(large, REL_BUCKETS - 1)
    return jnp.where(n < max_exact, n, large)


def _layer(x, c, w_ada, b_ada, w_in, conv_w, a_log, dt_bias, gdn_norm_w, rel_bias,
           w_proj_moba, w_proj_gdn, w_out, ln1_g, ln1_b, w_ffn_in, w_ffn_out, ln2_g, ln2_b):
    bsz, seq, d = x.shape
    t = bsz * seq
    x2 = x.reshape(t, d)

    mod = _ada_mod(c, w_ada, b_ada)
    mod3 = mod.reshape(bsz, 1, 6 * d)

    w_all = w_in.astype(BF16)
    n_gates = 2 * GDN_V_HEADS
    w_gate = w_all[:, N_HEAD + n_gates:]
    w_small = jnp.pad(w_all[:, N_HEAD:N_HEAD + n_gates], ((0, 0), (0, N_SMALL - n_gates)))

    proj, small = _in_proj(x2, mod3, w_all, w_gate, w_small, seq)

    ii = jnp.arange(MOBA_BLOCK, dtype=jnp.int32)
    dist = ii[None, :] - ii[:, None]
    bko = _rel_bucket(dist)
    bkp = _rel_bucket(dist + MOBA_BLOCK)
    ya = _moba(proj, rel_bias, bko, bkp, bsz, seq)

    nchunk = seq // GDN_CHUNK
    sm_t = small[:, :2 * GDN_V_HEADS].reshape(bsz, seq, 2, GDN_QK_HEADS, 2)
    rows = sm_t.transpose(0, 3, 2, 4, 1).reshape(bsz, GDN_QK_HEADS, 4, nchunk, GDN_CHUNK)
    yb = _gdn(proj, small, rows, conv_w, a_log, dt_bias, gdn_norm_w.reshape(1, HEAD_DIM),
              bsz, seq)

    x1, h2 = _merge(ya, yb, proj, x2, mod3, ln1_g.reshape(1, d), ln1_b.reshape(1, d),
                    w_proj_moba.astype(BF16), w_proj_gdn.astype(BF16), w_out.astype(BF16), seq)
    out = _ffn(h2, x1, mod3, ln2_g.reshape(1, d), ln2_b.reshape(1, d),
               w_ffn_in.astype(BF16), w_ffn_out.astype(BF16), seq)
    return out.reshape(bsz, seq, d)


def kernel(x, c, w_ada, b_ada, w_in, conv_w, a_log, dt_bias, gdn_norm_w, rel_bias, w_proj_moba,
           w_proj_gdn, w_out, ln1_g, ln1_b, w_ffn_in, w_ffn_out, ln2_g, ln2_b):
    depth = w_ada.shape[0]
    for l in range(depth):
        x = _layer(x, c, w_ada[l], b_ada[l], w_in[l], conv_w[l], a_log[l], dt_bias[l],
                   gdn_norm_w[l], rel_bias, w_proj_moba[l], w_proj_gdn[l], w_out[l],
                   ln1_g[l], ln1_b[l], w_ffn_in[l], w_ffn_out[l], ln2_g[l], ln2_b[l])
    return x
```

```python
import functools
import math

import jax
import jax.numpy as jnp
from jax import lax
from jax.experimental import pallas as pl
from jax.experimental.pallas import tpu as pltpu

F32 = jnp.float32
BF16 = jnp.bfloat16

D_MODEL = 2048
MOBA_HEADS = 8
HEAD_DIM = 128
MOBA_W = MOBA_HEADS * HEAD_DIM
MOBA_BLOCK = 256
MOBA_TOPK = 3
REL_BUCKETS = 32
REL_MAX_DIST = 128
GDN_QK_HEADS = 8
GDN_V_HEADS = 16
GDN_QK_W = GDN_QK_HEADS * HEAD_DIM
GDN_V_W = GDN_V_HEADS * HEAD_DIM
GDN_CONV = 4
GDN_CHUNK = 64
D_FF = 5632
DEEPNORM_ALPHA = 2.0 ** 0.25
LN_EPS = 1e-5
RMS_EPS = 1e-6
NEG_INF = -1e30

COL_GATE_A = 0
COL_GATE_B = 2048
COL_MOBA_Q = 4096
COL_MOBA_K = 5120
COL_MOBA_V = 6144
COL_GDN_Q = 7168
COL_GDN_K = 8192
COL_GDN_V = 9216
COL_GDN_Z = 11264
N_MAIN = 13312
N_GATE = 4096
N_HEAD = N_MAIN - N_GATE
N_SMALL = 128

V7X_VMEM_BYTES = 64 * 1024 * 1024
LANES = 128


def _vmem(mb):
    return pltpu.CompilerParams(vmem_limit_bytes=mb * 1024 * 1024)


def _silu(x):
    return x * jax.nn.sigmoid(x)


def _layer_norm(r, gain, bias):
    mu = jnp.mean(r, axis=-1, keepdims=True)
    d = r - mu
    var = jnp.mean(d * d, axis=-1, keepdims=True)
    return d * lax.rsqrt(var + LN_EPS) * gain + bias


def _ada_kernel(c_ref, w_ref, b_ref, o_ref):
    sc = _silu(c_ref[...])
    o_ref[...] = jnp.dot(sc, w_ref[...], precision=lax.Precision.HIGHEST,
                         preferred_element_type=F32) + b_ref[...]


def _ada_mod(c, w_ada, b_ada):
    bsz = c.shape[0]
    n = w_ada.shape[1]
    tn = 1024
    return pl.pallas_call(
        _ada_kernel,
        name="ada_mod",
        grid=(n // tn,),
        in_specs=[pl.BlockSpec((bsz, D_MODEL), lambda j: (0, 0)),
                  pl.BlockSpec((D_MODEL, tn), lambda j: (0, j)),
                  pl.BlockSpec((1, tn), lambda j: (0, j))],
        out_specs=pl.BlockSpec((bsz, tn), lambda j: (0, j)),
        out_shape=jax.ShapeDtypeStruct((bsz, n), F32),
        compiler_params=_vmem(40),
    )(c, w_ada, b_ada.reshape(1, n))


def _inproj_kernel(x_ref, sh_ref, sc_ref, w_ref, wg_ref, ws_ref, o_ref, os_ref, h_ref, *, n_head):
    j = pl.program_id(1)

    @pl.when(j == 0)
    def _():
        h = x_ref[...] * (1.0 + sc_ref[...]) + sh_ref[...]
        h_ref[...] = h.astype(BF16)
        os_ref[...] = jnp.dot(h_ref[...], ws_ref[...], preferred_element_type=F32)

    @pl.when(j < n_head)
    def _():
        o_ref[...] = jnp.dot(h_ref[...], w_ref[...], preferred_element_type=F32)

    @pl.when(j >= n_head)
    def _():
        o_ref[...] = jnp.dot(h_ref[...], wg_ref[...], preferred_element_type=F32)


def _in_proj(x2, mod3, w_all, w_gate, w_small, seq):
    t = x2.shape[0]
    tm, tn = 1024, 1024
    per_b = seq // tm
    n_head, n_gate, n_tiles = N_HEAD // tn, N_GATE // tn, N_MAIN // tn
    return pl.pallas_call(
        functools.partial(_inproj_kernel, n_head=n_head),
        name="in_proj",
        grid=(t // tm, n_tiles),
        in_specs=[pl.BlockSpec((tm, D_MODEL), lambda i, j: (i, 0)),
                  pl.BlockSpec((None, 1, D_MODEL), lambda i, j: (i // per_b, 0, 0)),
                  pl.BlockSpec((None, 1, D_MODEL), lambda i, j: (i // per_b, 0, 1)),
                  pl.BlockSpec((D_MODEL, tn), lambda i, j: (0, jnp.minimum(j, n_head - 1))),
                  pl.BlockSpec((D_MODEL, tn), lambda i, j: (0, jnp.maximum(j - n_head, 0))),
                  pl.BlockSpec((D_MODEL, N_SMALL), lambda i, j: (0, 0))],
        out_specs=[pl.BlockSpec((tm, tn), lambda i, j: (i, (j + n_gate) % n_tiles)),
                   pl.BlockSpec((tm, N_SMALL), lambda i, j: (i, 0))],
        out_shape=[jax.ShapeDtypeStruct((t, N_MAIN), F32),
                   jax.ShapeDtypeStruct((t, N_SMALL), F32)],
        scratch_shapes=[pltpu.VMEM((tm, D_MODEL), BF16)],
        compiler_params=_vmem(56),
    )(x2, mod3, mod3, w_all, w_gate, w_small)


def _moba_kernel(rel_ref, bko_ref, bkp_ref, q_ref, k_ref, v_ref, o_ref,
                 bias_own, bias_prev, kb_ref, vt_ref, *, nb):
    h = pl.program_id(0)
    blk = MOBA_BLOCK

    @pl.when(pl.program_id(1) == 0)
    def _():
        bo = bko_ref[...]
        bp = bkp_ref[...]
        acc_o = jnp.zeros((blk, blk), F32)
        acc_p = jnp.zeros((blk, blk), F32)
        for kk in range(REL_BUCKETS):
            val = rel_ref[kk, h]
            acc_o = jnp.where(bo == kk, val, acc_o)
            acc_p = jnp.where(bp == kk, val, acc_p)
        bias_own[...] = acc_o
        bias_prev[...] = acc_p

    bias_far = rel_ref[REL_BUCKETS - 1, h]
    scale = HEAD_DIM ** -0.5
    kf = k_ref[...]
    kmean = jnp.mean(kf.reshape(nb, blk, HEAD_DIM), axis=1)
    kb_ref[...] = kf.astype(BF16)
    nt_dims = (((1,), (1,)), ((), ()))
    eye = (lax.broadcasted_iota(jnp.int32, (HEAD_DIM, HEAD_DIM), 0)
           == lax.broadcasted_iota(jnp.int32, (HEAD_DIM, HEAD_DIM), 1)).astype(BF16)
    vt_ref[...] = lax.dot_general(eye, v_ref[...].astype(BF16), nt_dims,
                                  preferred_element_type=F32).astype(BF16)
    causal = (lax.broadcasted_iota(jnp.int32, (blk, blk), 0)
              <= lax.broadcasted_iota(jnp.int32, (blk, blk), 1))

    for i in range(nb):
        qi = q_ref[i * blk:(i + 1) * blk, :]
        qb = qi.astype(BF16)
        sel = None
        if i > MOBA_TOPK:
            route = lax.dot_general(kmean, qi, nt_dims, precision=lax.Precision.HIGHEST,
                                    preferred_element_type=F32)
            rc = [route[n:n + 1, :] for n in range(i)]
            sel = []
            for n in range(i):
                rank = jnp.zeros((1, blk), jnp.int32)
                for m in range(i):
                    if m == n:
                        continue
                    beats = (rc[m] >= rc[n]) if m < n else (rc[m] > rc[n])
                    rank = rank + beats.astype(jnp.int32)
                sel.append(rank < MOBA_TOPK)
        s_list = []
        for n in range(i + 1):
            s = lax.dot_general(kb_ref[n * blk:(n + 1) * blk, :], qb, nt_dims,
                                preferred_element_type=F32) * scale
            if n == i:
                s = jnp.where(causal, s + bias_own[...], NEG_INF)
            else:
                s = s + (bias_prev[...] if n == i - 1 else bias_far)
                if sel is not None:
                    s = jnp.where(sel[n], s, NEG_INF)
            s_list.append(s)
        m_run = jnp.max(s_list[0], axis=0, keepdims=True)
        for s in s_list[1:]:
            m_run = jnp.maximum(m_run, jnp.max(s, axis=0, keepdims=True))
        l_run = jnp.zeros((1, blk), F32)
        acc = jnp.zeros((HEAD_DIM, blk), F32)
        for n, s in enumerate(s_list):
            p = jnp.exp(s - m_run)
            l_run = l_run + jnp.sum(p, axis=0, keepdims=True)
            acc = acc + jnp.dot(vt_ref[:, n * blk:(n + 1) * blk], p.astype(BF16),
                                preferred_element_type=F32)
        o_ref[i * blk:(i + 1) * blk, :] = (acc / l_run).T.astype(BF16)


def _moba(proj, rel_bias, bko, bkp, bsz, seq):
    nb = seq // MOBA_BLOCK
    qo, ko, vo = COL_MOBA_Q // HEAD_DIM, COL_MOBA_K // HEAD_DIM, COL_MOBA_V // HEAD_DIM
    blk = MOBA_BLOCK
    return pl.pallas_call(
        functools.partial(_moba_kernel, nb=nb),
        name="moba",
        grid=(MOBA_HEADS, bsz),
        in_specs=[pl.BlockSpec(memory_space=pltpu.SMEM),
                  pl.BlockSpec((blk, blk), lambda h, b: (0, 0)),
                  pl.BlockSpec((blk, blk), lambda h, b: (0, 0)),
                  pl.BlockSpec((seq, HEAD_DIM), lambda h, b: (b, qo + h)),
                  pl.BlockSpec((seq, HEAD_DIM), lambda h, b: (b, ko + h)),
                  pl.BlockSpec((seq, HEAD_DIM), lambda h, b: (b, vo + h))],
        out_specs=pl.BlockSpec((seq, HEAD_DIM), lambda h, b: (b, h)),
        out_shape=jax.ShapeDtypeStruct((bsz * seq, MOBA_W), BF16),
        scratch_shapes=[pltpu.VMEM((blk, blk), F32), pltpu.VMEM((blk, blk), F32),
                        pltpu.VMEM((seq, HEAD_DIM), BF16), pltpu.VMEM((HEAD_DIM, seq), BF16)],
        compiler_params=_vmem(48),
    )(rel_bias, bko, bkp, proj, proj, proj)


SUBLANES = 8


def _conv_silu(xh, w):
    acc = xh[SUBLANES:, :] * w[GDN_CONV - 1:GDN_CONV, :]
    for s in range(1, GDN_CONV):
        acc = acc + pltpu.roll(xh, s, axis=0)[SUBLANES:, :] * w[GDN_CONV - 1 - s:GDN_CONV - s, :]
    return _silu(acc)


def _l2norm(x):
    return x * lax.rsqrt(jnp.sum(x * x, axis=-1, keepdims=True) + RMS_EPS)


def _softplus(x):
    return jnp.maximum(x, 0.0) + jnp.log1p(jnp.exp(-jnp.abs(x)))


ELIM_BLOCK = 8
MERGE_LEVELS = (GDN_CHUNK // ELIM_BLOCK).bit_length() - 1


def _block_diag_inverse(lpair):
    n = lpair.shape[0]
    nv = n // SUBLANES
    vpb = ELIM_BLOCK // SUBLANES
    rid = lax.broadcasted_iota(jnp.int32, (SUBLANES, 2 * n), 0)
    lid = lax.broadcasted_iota(jnp.int32, (SUBLANES, 2 * n), 1)
    t_rows = [(lid % n == rid + v * SUBLANES).astype(F32) for v in range(nv)]
    l_rows = [lpair[v * SUBLANES:(v + 1) * SUBLANES, :] for v in range(nv)]
    for m in range(n - 1):
        v0, s0 = divmod(m, SUBLANES)
        v_end = (v0 // vpb + 1) * vpb
        row = t_rows[v0][s0:s0 + 1, :]
        col = (lid // n) * n + m
        for v in range(v0 if s0 < SUBLANES - 1 else v0 + 1, v_end):
            t_rows[v] = t_rows[v] - jnp.take_along_axis(l_rows[v], col, axis=1) * row
    return jnp.concatenate(t_rows, axis=0)


def _lane_block_diag(pair):
    first = lax.broadcasted_iota(jnp.int32, pair.shape, 1) < pair.shape[1] // 2
    zero = jnp.zeros_like(pair)
    return jnp.concatenate([jnp.where(first, pair, zero), jnp.where(first, zero, pair)], axis=0)


def _merge_lower_products(lpair, tpair, k):
    n = lpair.shape[0]
    ri = lax.broadcasted_iota(jnp.int32, lpair.shape, 0)
    ci = lax.broadcasted_iota(jnp.int32, lpair.shape, 1) % n
    off = (ri // (2 * k) == ci // (2 * k)) & (ri // k > ci // k)
    lk = jnp.where(off, lpair, 0.0).astype(BF16)
    return jnp.dot(lk, _lane_block_diag(tpair.astype(BF16)), preferred_element_type=F32)


def _merge_apply(tpair, lt):
    return tpair - jnp.dot(tpair.astype(BF16), _lane_block_diag(lt), preferred_element_type=F32)


def _software_pipeline(stages, n):
    ns = len(stages)

    def run(it, lo, hi):
        conts = [stages[s](it - s) for s in reversed(range(lo, hi))]
        for cont in conts:
            if cont is not None:
                cont()

    for it in range(ns - 1):
        run(it, 0, it + 1)

    def body(it, carry):
        run(it, 0, ns)
        return carry

    lax.fori_loop(ns - 1, n, body, 0)
    for it in range(n, n + ns - 1):
        run(it, it - n + 1, ns)


def _lane_pick(x, lane, idx):
    return jnp.sum(jnp.where(lane == idx, x, 0.0), axis=-1, keepdims=True)


def _gdn_kernel(alog_ref, dtb_ref, gp_ref, nw_ref, cwq_ref, cwk_ref, cwv_ref,
                q_ref, k_ref, v_ref, z_ref, sm_ref, rw_ref, y_ref,
                kb_s, qn_s, kdp_s, rhs_s, qd_s, gcb_s, betab_s, gcr_s, gram_s, lm_s, lhs2_s,
                sol_s, mp_s, n_s, r_s, st_s, *merge_s, nchunk):
    tl_s, lt_s = merge_s[:MERGE_LEVELS + 1], merge_s[MERGE_LEVELS + 1:]
    hq = pl.program_id(1)
    c64 = GDN_CHUNK
    hd = HEAD_DIM
    heads = range(2)

    tri_u = (lax.broadcasted_iota(jnp.int32, (c64, c64), 0)
             <= lax.broadcasted_iota(jnp.int32, (c64, c64), 1)).astype(F32)
    gc_rows = []
    for j in heads:
        hv = 2 * hq + j
        a_neg_r = -jnp.exp(jnp.full((nchunk, c64), alog_ref[hv], F32))
        g_row = a_neg_r * _softplus(rw_ref[2 + j] + dtb_ref[hv])
        gc_rows.append(jnp.dot(g_row, tri_u, precision=lax.Precision.HIGHEST,
                               preferred_element_type=F32))
    gcr_s[...] = jnp.concatenate(gc_rows, axis=1)

    rows = lax.broadcasted_iota(jnp.int32, (c64, 2 * c64), 0)
    cols = lax.broadcasted_iota(jnp.int32, (c64, 2 * c64), 1) % c64
    tril = rows >= cols
    strict = rows > cols
    eye2 = (lax.broadcasted_iota(jnp.int32, (2 * c64, 2 * c64), 0)
            == lax.broadcasted_iota(jnp.int32, (2 * c64, 2 * c64), 1)).astype(BF16)
    nt_dims = (((1,), (1,)), ((), ()))

    def rows_of(c):
        start = c * c64
        return pl.ds(start if isinstance(c, int) else pl.multiple_of(start, c64), c64)

    def with_halo(x_ref, c):
        if isinstance(c, int) and c == 0:
            return jnp.concatenate([jnp.zeros((SUBLANES, x_ref.shape[1]), F32), x_ref[:c64, :]],
                                   axis=0)
        start = c * c64 - SUBLANES
        if not isinstance(c, int):
            start = pl.multiple_of(start, SUBLANES)
        return x_ref[pl.ds(start, c64 + SUBLANES), :]

    lane = lax.broadcasted_iota(jnp.int32, (c64, N_SMALL), 1)
    first = lane < c64
    pos = lax.broadcasted_iota(jnp.int32, (c64, N_SMALL), 0)

    def stage_prep(c):
        r = rows_of(c)
        qn = _l2norm(_conv_silu(with_halo(q_ref, c), cwq_ref[...])) * (hd ** -0.5)
        kn = _l2norm(_conv_silu(with_halo(k_ref, c), cwk_ref[...]))
        vc = _conv_silu(with_halo(v_ref, c), cwv_ref[...])
        qn_s[r, :] = qn.astype(BF16)
        kb_s[r, :] = kn.astype(BF16)
        sm = sm_ref[r, :]
        sig_all = jax.nn.sigmoid(sm)
        gc_all = -jnp.exp(gp_ref[0:1, :]) * _softplus(sm + gp_ref[1:2, :])
        sft = 1
        while sft < c64:
            gc_all = gc_all + jnp.where(pos >= sft, pltpu.roll(gc_all, sft, axis=0), 0.0)
            sft *= 2
        rest_all = gc_all[c64 - 1:, :] - gc_all
        betas, gc_cols = [], []
        for j in heads:
            hv = 2 * hq + j
            beta = _lane_pick(sig_all, lane, hv)
            gc_col = _lane_pick(gc_all, lane, GDN_V_HEADS + hv)
            rest_col = _lane_pick(rest_all, lane, GDN_V_HEADS + hv)
            eg = jnp.exp(gc_col)
            rhs_s[j, r, :hd] = (vc[:, j * hd:(j + 1) * hd] * beta).astype(BF16)
            rhs_s[j, r, hd:] = (kn * (beta * eg)).astype(BF16)
            qd_s[j, r, :] = (qn * eg).astype(BF16)
            kdp_s[c, j * c64:(j + 1) * c64, :] = (kn * jnp.exp(rest_col)).astype(BF16)
            betas.append(beta)
            gc_cols.append(gc_col)
        gcb_s[r, :] = jnp.where(first, gc_cols[0], gc_cols[1])
        betab_s[r, :] = jnp.where(first, betas[0], betas[1])

    def stage_gram(c):
        r = rows_of(c)
        kb = kb_s[r, :]
        kq = jnp.concatenate([kb, qn_s[r, :]], axis=0)
        gram = lax.dot_general(kq, jnp.concatenate([kb, kb], axis=0), nt_dims,
                               preferred_element_type=F32)
        kd_t = lax.dot_general(eye2, kdp_s[c], nt_dims, preferred_element_type=F32)

        def finish():
            gram_s[c] = gram
            lhs2_s[c, :2 * c64, :] = kd_t.astype(BF16)

        return finish

    def stage_factor(c):
        r = rows_of(c)
        dec = jnp.exp(jnp.where(tril, gcb_s[r, :] - gcr_s[pl.ds(c, 1), :], NEG_INF))
        lpair = jnp.where(strict, gram_s[c, :c64, :] * dec, 0.0) * betab_s[r, :]
        lm_s[r, :] = lpair
        lhs2_s[c, 2 * c64:, :] = (gram_s[c, c64:, :] * dec).astype(BF16)
        tl_s[0][r, :] = _block_diag_inverse(lpair)

    def stage_merge_products(level):
        def stage(c):
            r = rows_of(c)
            lt = _merge_lower_products(lm_s[r, :], tl_s[level][r, :], ELIM_BLOCK << level)

            def finish():
                lt_s[level][r, :] = lt.astype(BF16)

            return finish
        return stage

    def stage_merge_apply(level):
        def stage(c):
            r = rows_of(c)
            merged = _merge_apply(tl_s[level][r, :], lt_s[level][r, :])

            def finish():
                tl_s[level + 1][r, :] = merged

            return finish
        return stage

    def stage_solve(c):
        r = rows_of(c)
        zero = jnp.zeros((c64, 2 * hd), BF16)
        rhs_bd = jnp.concatenate([jnp.concatenate([rhs_s[0, r, :], zero], axis=1),
                                  jnp.concatenate([zero, rhs_s[1, r, :]], axis=1)], axis=0)
        sol = jnp.dot(tl_s[MERGE_LEVELS][r, :].astype(BF16), rhs_bd,
                      preferred_element_type=F32)

        def finish():
            sol_s[r, :] = sol.astype(BF16)

        return finish

    def stage_fold(c):
        r = rows_of(c)
        sol = sol_s[r, :]
        out = jnp.dot(lhs2_s[c], _lane_block_diag(sol), preferred_element_type=F32)

        def finish():
            for j in heads:
                u_col, w_col = 2 * j * hd, (2 * j + 1) * hd
                n_s[j, c] = out[:2 * c64, u_col:u_col + hd]
                mp_s[j, c, :2 * c64, :] = out[:2 * c64, w_col:w_col + hd].astype(BF16)
                mp_s[j, c, 2 * c64:, :] = (qd_s[j, r, :].astype(F32)
                                           - out[2 * c64:, w_col:w_col + hd]).astype(BF16)
                r_s[j, r, :] = out[2 * c64:, u_col:u_col + hd]

        return finish

    merges = [stage(lv) for lv in range(MERGE_LEVELS)
              for stage in (stage_merge_products, stage_merge_apply)]
    _software_pipeline([stage_prep, stage_gram, stage_factor] + merges
                       + [stage_solve, stage_fold], nchunk)

    st_s[...] = jnp.zeros_like(st_s)
    nw = nw_ref[...]

    def state_step(c):
        r = rows_of(c)
        sts = [st_s[j] for j in heads]
        outs = [jnp.dot(mp_s[j, c], sts[j].astype(BF16), preferred_element_type=F32)
                for j in heads]
        for j in heads:
            end = (j + 1) * c64
            gl = gcr_s[pl.ds(c, 1), end - 1:end]
            st_s[j] = sts[j] * jnp.exp(gl) - outs[j][:2 * c64] + n_s[j, c]
            r_s[j, r, :] = r_s[j, r, :] + outs[j][2 * c64:]

    def gated_norm(c):
        r = rows_of(c)
        for j in heads:
            o = r_s[j, r, :]
            og = (o * lax.rsqrt(jnp.mean(o * o, axis=-1, keepdims=True) + RMS_EPS)
                  * nw * _silu(z_ref[r, j * hd:(j + 1) * hd]))
            y_ref[r, j * hd:(j + 1) * hd] = og.astype(BF16)

    state_step(0)

    def phase_c(c, carry):
        gated_norm(c - 1)
        state_step(c)
        return carry

    lax.fori_loop(1, nchunk, phase_c, 0)
    gated_norm(nchunk - 1)


def _gdn(proj, small, rows, conv_w, a_log, dt_bias, norm_w, bsz, seq):
    nchunk = seq // GDN_CHUNK
    hd = HEAD_DIM
    c64 = GDN_CHUNK
    qo, ko = COL_GDN_Q // hd, COL_GDN_K // hd
    vo, zo = COL_GDN_V // (2 * hd), COL_GDN_Z // (2 * hd)
    cvo = (2 * GDN_QK_W) // (2 * hd)
    smem = pl.BlockSpec(memory_space=pltpu.SMEM)
    pad = (GDN_V_HEADS, N_SMALL - 2 * GDN_V_HEADS)
    gate_params = jnp.stack([jnp.pad(a_log, pad), jnp.pad(dt_bias, pad)])
    return pl.pallas_call(
        functools.partial(_gdn_kernel, nchunk=nchunk),
        name="gdn",
        grid=(bsz, GDN_QK_HEADS),
        in_specs=[smem, smem,
                  pl.BlockSpec((2, N_SMALL), lambda b, h: (0, 0)),
                  pl.BlockSpec((1, hd), lambda b, h: (0, 0)),
                  pl.BlockSpec((GDN_CONV, hd), lambda b, h: (0, h)),
                  pl.BlockSpec((GDN_CONV, hd), lambda b, h: (0, GDN_QK_HEADS + h)),
                  pl.BlockSpec((GDN_CONV, 2 * hd), lambda b, h: (0, cvo + h)),
                  pl.BlockSpec((seq, hd), lambda b, h: (b, qo + h)),
                  pl.BlockSpec((seq, hd), lambda b, h: (b, ko + h)),
                  pl.BlockSpec((seq, 2 * hd), lambda b, h: (b, vo + h)),
                  pl.BlockSpec((seq, 2 * hd), lambda b, h: (b, zo + h)),
                  pl.BlockSpec((seq, N_SMALL), lambda b, h: (b, 0)),
                  pl.BlockSpec((None, None, 4, nchunk, GDN_CHUNK), lambda b, h: (b, h, 0, 0, 0))],
        out_specs=pl.BlockSpec((seq, 2 * hd), lambda b, h: (b, h)),
        out_shape=jax.ShapeDtypeStruct((bsz * seq, GDN_V_W), BF16),
        scratch_shapes=[pltpu.VMEM((seq, hd), BF16),
                        pltpu.VMEM((seq, hd), BF16),
                        pltpu.VMEM((nchunk, 2 * c64, hd), BF16),
                        pltpu.VMEM((2, seq, 2 * hd), BF16),
                        pltpu.VMEM((2, seq, hd), BF16),
                        pltpu.VMEM((seq, 2 * c64), F32),
                        pltpu.VMEM((seq, 2 * c64), F32),
                        pltpu.VMEM((nchunk, 2 * c64), F32),
                        pltpu.VMEM((nchunk, 2 * c64, 2 * c64), F32),
                        pltpu.VMEM((seq, 2 * c64), F32),
                        pltpu.VMEM((nchunk, 3 * c64, hd), BF16),
                        pltpu.VMEM((seq, 4 * hd), BF16),
                        pltpu.VMEM((2, nchunk, 3 * c64, hd), BF16),
                        pltpu.VMEM((2, nchunk, hd, hd), F32),
                        pltpu.VMEM((2, seq, hd), F32),
                        pltpu.VMEM((2, hd, hd), F32)]
                       + [pltpu.VMEM((seq, 2 * c64), F32)] * (MERGE_LEVELS + 1)
                       + [pltpu.VMEM((seq, 2 * c64), BF16)] * MERGE_LEVELS,
        compiler_params=_vmem(58),
    )(a_log, dt_bias, gate_params, norm_w, conv_w, conv_w, conv_w, proj, proj, proj, proj, small, rows)


def _merge_kernel(ya_ref, yb_ref, ga_ref, gb_ref, x_ref, g1_ref, sh2_ref, sc2_ref,
                  lng_ref, lnb_ref, wpm_ref, wpg_ref, wo_ref, x1_ref, h2_ref):
    pa = jnp.dot(ya_ref[...], wpm_ref[...], preferred_element_type=F32)
    pb = jnp.dot(yb_ref[...], wpg_ref[...], preferred_element_type=F32)
    merged = jax.nn.sigmoid(ga_ref[...]) * pa + jax.nn.sigmoid(gb_ref[...]) * pb
    y = jnp.dot(merged.astype(BF16), wo_ref[...], preferred_element_type=F32)
    x1 = _layer_norm(DEEPNORM_ALPHA * x_ref[...] + g1_ref[...] * y, lng_ref[...], lnb_ref[...])
    x1_ref[...] = x1
    h2_ref[...] = (x1 * (1.0 + sc2_ref[...]) + sh2_ref[...]).astype(BF16)


def _merge(ya, yb, proj, x2, mod3, ln_g, ln_b, wpm, wpg, wo, seq):
    t = x2.shape[0]
    tm = 256
    per_b = seq // tm
    d = D_MODEL

    def modspec(k):
        return pl.BlockSpec((None, 1, d), lambda i: (i // per_b, 0, k))

    def const(shape):
        return pl.BlockSpec(shape, lambda i: (0, 0), pipeline_mode=pl.Buffered(1))

    return pl.pallas_call(
        _merge_kernel,
        name="merge",
        grid=(t // tm,),
        in_specs=[pl.BlockSpec((tm, MOBA_W), lambda i: (i, 0)),
                  pl.BlockSpec((tm, GDN_V_W), lambda i: (i, 0)),
                  pl.BlockSpec((tm, d), lambda i: (i, COL_GATE_A // d)),
                  pl.BlockSpec((tm, d), lambda i: (i, COL_GATE_B // d)),
                  pl.BlockSpec((tm, d), lambda i: (i, 0)),
                  modspec(2), modspec(3), modspec(4),
                  const((1, d)), const((1, d)),
                  const((MOBA_W, d)), const((GDN_V_W, d)), const((d, d))],
        out_specs=[pl.BlockSpec((tm, d), lambda i: (i, 0)),
                   pl.BlockSpec((tm, d), lambda i: (i, 0))],
        out_shape=[jax.ShapeDtypeStruct((t, d), F32), jax.ShapeDtypeStruct((t, d), BF16)],
        compiler_params=_vmem(56),
    )(ya, yb, proj, proj, x2, mod3, mod3, mod3, ln_g, ln_b, wpm, wpg, wo)


def _ffn_kernel(h_ref, x1_ref, g2_ref, lng_ref, lnb_ref, wg_ref, wu_ref, wo_ref, o_ref, acc_ref):
    f = pl.program_id(1)

    @pl.when(f == 0)
    def _():
        acc_ref[...] = jnp.zeros_like(acc_ref)

    h = h_ref[...]
    gate = jnp.dot(h, wg_ref[...], preferred_element_type=F32)
    up = jnp.dot(h, wu_ref[...], preferred_element_type=F32)
    act = (_silu(gate) * up).astype(BF16)
    acc_ref[...] += jnp.dot(act, wo_ref[...], preferred_element_type=F32)

    @pl.when(f == pl.num_programs(1) - 1)
    def _():
        r = DEEPNORM_ALPHA * x1_ref[...] + g2_ref[...] * acc_ref[...]
        o_ref[...] = _layer_norm(r, lng_ref[...], lnb_ref[...])


def _ffn(h2, x1, mod3, ln_g, ln_b, w_in, w_out, seq):
    t = h2.shape[0]
    tm, tf = 512, 512
    per_b = seq // tm
    d = D_MODEL
    nf = D_FF // tf
    return pl.pallas_call(
        _ffn_kernel,
        name="ffn",
        grid=(t // tm, nf),
        in_specs=[pl.BlockSpec((tm, d), lambda i, f: (i, 0)),
                  pl.BlockSpec((tm, d), lambda i, f: (i, 0)),
                  pl.BlockSpec((None, 1, d), lambda i, f: (i // per_b, 0, 5)),
                  pl.BlockSpec((1, d), lambda i, f: (0, 0)),
                  pl.BlockSpec((1, d), lambda i, f: (0, 0)),
                  pl.BlockSpec((d, tf), lambda i, f: (0, f)),
                  pl.BlockSpec((d, tf), lambda i, f: (0, nf + f)),
                  pl.BlockSpec((tf, d), lambda i, f: (f, 0))],
        out_specs=pl.BlockSpec((tm, d), lambda i, f: (i, 0)),
        out_shape=jax.ShapeDtypeStruct((t, d), F32),
        scratch_shapes=[pltpu.VMEM((tm, d), F32)],
        compiler_params=_vmem(48),
    )(h2, x1, mod3, ln_g, ln_b, w_in, w_in, w_out)


def _rel_bucket(dist):
    max_exact = REL_BUCKETS // 2
    n = jnp.maximum(dist, 0)
    nf = jnp.maximum(n, 1).astype(F32)
    large = max_exact + (jnp.log(nf / max_exact) / math.log(REL_MAX_DIST / max_exact)
                         * (REL_BUCKETS - max_exact)).astype(jnp.int32)
    large = jnp.minimum(large, REL_BUCKETS - 1)
    return jnp.where(n < max_exact, n, large)


def _layer(x, c, w_ada, b_ada, w_in, conv_w, a_log, dt_bias, gdn_norm_w, rel_bias,
           w_proj_moba, w_proj_gdn, w_out, ln1_g, ln1_b, w_ffn_in, w_ffn_out, ln2_g, ln2_b):
    bsz, seq, d = x.shape
    t = bsz * seq
    x2 = x.reshape(t, d)

    mod = _ada_mod(c, w_ada, b_ada)
    mod3 = mod.reshape(bsz, 1, 6 * d)

    w_all = w_in.astype(BF16)
    n_gates = 2 * GDN_V_HEADS
    w_gate = w_all[:, N_HEAD + n_gates:]
    w_small = jnp.pad(w_all[:, N_HEAD:N_HEAD + n_gates], ((0, 0), (0, N_SMALL - n_gates)))

    proj, small = _in_proj(x2, mod3, w_all, w_gate, w_small, seq)

    ii = jnp.arange(MOBA_BLOCK, dtype=jnp.int32)
    dist = ii[None, :] - ii[:, None]
    bko = _rel_bucket(dist)
    bkp = _rel_bucket(dist + MOBA_BLOCK)
    ya = _moba(proj, rel_bias, bko, bkp, bsz, seq)

    nchunk = seq // GDN_CHUNK
    sm_t = small[:, :2 * GDN_V_HEADS].reshape(bsz, seq, 2, GDN_QK_HEADS, 2)
    rows = sm_t.transpose(0, 3, 2, 4, 1).reshape(bsz, GDN_QK_HEADS, 4, nchunk, GDN_CHUNK)
    yb = _gdn(proj, small, rows, conv_w, a_log, dt_bias, gdn_norm_w.reshape(1, HEAD_DIM),
              bsz, seq)

    x1, h2 = _merge(ya, yb, proj, x2, mod3, ln1_g.reshape(1, d), ln1_b.reshape(1, d),
                    w_proj_moba.astype(BF16), w_proj_gdn.astype(BF16), w_out.astype(BF16), seq)
    out = _ffn(h2, x1, mod3, ln2_g.reshape(1, d), ln2_b.reshape(1, d),
               w_ffn_in.astype(BF16), w_ffn_out.astype(BF16), seq)
    return out.reshape(bsz, seq, d)


def kernel(x, c, w_ada, b_ada, w_in, conv_w, a_log, dt_bias, gdn_norm_w, rel_bias, w_proj_moba,
           w_proj_gdn, w_out, ln1_g, ln1_b, w_ffn_in, w_ffn_out, ln2_g, ln2_b):
    depth = w_ada.shape[0]
    for l in range(depth):
        x = _layer(x, c, w_ada[l], b_ada[l], w_in[l], conv_w[l], a_log[l], dt_bias[l],
                   gdn_norm_w[l], rel_bias, w_proj_moba[l], w_proj_gdn[l], w_out[l],
                   ln1_g[l], ln1_b[l], w_ffn_in[l], w_ffn_out[l], ln2_g[l], ln2_b[l])
    return x
```

```python
import functools
import math

import jax
import jax.numpy as jnp
from jax import lax
from jax.experimental import pallas as pl
from jax.experimental.pallas import tpu as pltpu

F32 = jnp.float32
BF16 = jnp.bfloat16

D_MODEL = 2048
MOBA_HEADS = 8
HEAD_DIM = 128
MOBA_W = MOBA_HEADS * HEAD_DIM
MOBA_BLOCK = 256
MOBA_TOPK = 3
REL_BUCKETS = 32
REL_MAX_DIST = 128
GDN_QK_HEADS = 8
GDN_V_HEADS = 16
GDN_QK_W = GDN_QK_HEADS * HEAD_DIM
GDN_V_W = GDN_V_HEADS * HEAD_DIM
GDN_CONV = 4
GDN_CHUNK = 64
D_FF = 5632
DEEPNORM_ALPHA = 2.0 ** 0.25
LN_EPS = 1e-5
RMS_EPS = 1e-6
NEG_INF = -1e30

COL_GATE_A = 0
COL_GATE_B = 2048
COL_MOBA_Q = 4096
COL_MOBA_K = 5120
COL_MOBA_V = 6144
COL_GDN_Q = 7168
COL_GDN_K = 8192
COL_GDN_V = 9216
COL_GDN_Z = 11264
N_MAIN = 13312
N_GATE = 4096
N_HEAD = N_MAIN - N_GATE
N_SMALL = 128

V7X_VMEM_BYTES = 64 * 1024 * 1024
LANES = 128


def _vmem(mb):
    return pltpu.CompilerParams(vmem_limit_bytes=mb * 1024 * 1024)


def _silu(x):
    return x * jax.nn.sigmoid(x)


def _layer_norm(r, gain, bias):
    mu = jnp.mean(r, axis=-1, keepdims=True)
    d = r - mu
    var = jnp.mean(d * d, axis=-1, keepdims=True)
    return d * lax.rsqrt(var + LN_EPS) * gain + bias


def _ada_kernel(c_ref, w_ref, b_ref, o_ref):
    sc = _silu(c_ref[...])
    o_ref[...] = jnp.dot(sc, w_ref[...], precision=lax.Precision.HIGHEST,
                         preferred_element_type=F32) + b_ref[...]


def _ada_mod(c, w_ada, b_ada):
    bsz = c.shape[0]
    n = w_ada.shape[1]
    tn = 1024
    return pl.pallas_call(
        _ada_kernel,
        name="ada_mod",
        grid=(n // tn,),
        in_specs=[pl.BlockSpec((bsz, D_MODEL), lambda j: (0, 0)),
                  pl.BlockSpec((D_MODEL, tn), lambda j: (0, j)),
                  pl.BlockSpec((1, tn), lambda j: (0, j))],
        out_specs=pl.BlockSpec((bsz, tn), lambda j: (0, j)),
        out_shape=jax.ShapeDtypeStruct((bsz, n), F32),
        compiler_params=_vmem(40),
    )(c, w_ada, b_ada.reshape(1, n))


def _inproj_kernel(x_ref, sh_ref, sc_ref, w_ref, wg_ref, ws_ref, o_ref, os_ref, h_ref, *, n_head):
    j = pl.program_id(1)

    @pl.when(j == 0)
    def _():
        h = x_ref[...] * (1.0 + sc_ref[...]) + sh_ref[...]
        h_ref[...] = h.astype(BF16)
        os_ref[...] = jnp.dot(h_ref[...], ws_ref[...], preferred_element_type=F32)

    @pl.when(j < n_head)
    def _():
        o_ref[...] = jnp.dot(h_ref[...], w_ref[...], preferred_element_type=F32)

    @pl.when(j >= n_head)
    def _():
        o_ref[...] = jnp.dot(h_ref[...], wg_ref[...], preferred_element_type=F32)


def _in_proj(x2, mod3, w_all, w_gate, w_small, seq):
    t = x2.shape[0]
    tm, tn = 1024, 1024
    per_b = seq // tm
    n_head, n_gate, n_tiles = N_HEAD // tn, N_GATE // tn, N_MAIN // tn
    return pl.pallas_call(
        functools.partial(_inproj_kernel, n_head=n_head),
        name="in_proj",
        grid=(t // tm, n_tiles),
        in_specs=[pl.BlockSpec((tm, D_MODEL), lambda i, j: (i, 0)),
                  pl.BlockSpec((None, 1, D_MODEL), lambda i, j: (i // per_b, 0, 0)),
                  pl.BlockSpec((None, 1, D_MODEL), lambda i, j: (i // per_b, 0, 1)),
                  pl.BlockSpec((D_MODEL, tn), lambda i, j: (0, jnp.minimum(j, n_head - 1))),
                  pl.BlockSpec((D_MODEL, tn), lambda i, j: (0, jnp.maximum(j - n_head, 0))),
                  pl.BlockSpec((D_MODEL, N_SMALL), lambda i, j: (0, 0))],
        out_specs=[pl.BlockSpec((tm, tn), lambda i, j: (i, (j + n_gate) % n_tiles)),
                   pl.BlockSpec((tm, N_SMALL), lambda i, j: (i, 0))],
        out_shape=[jax.ShapeDtypeStruct((t, N_MAIN), F32),
                   jax.ShapeDtypeStruct((t, N_SMALL), F32)],
        scratch_shapes=[pltpu.VMEM((tm, D_MODEL), BF16)],
        compiler_params=_vmem(56),
    )(x2, mod3, mod3, w_all, w_gate, w_small)


def _moba_kernel(rel_ref, bko_ref, bkp_ref, q_ref, k_ref, v_ref, o_ref,
                 bias_own, bias_prev, kb_ref, vt_ref, *, nb):
    h = pl.program_id(0)
    blk = MOBA_BLOCK

    @pl.when(pl.program_id(1) == 0)
    def _():
        bo = bko_ref[...]
        bp = bkp_ref[...]
        acc_o = jnp.zeros((blk, blk), F32)
        acc_p = jnp.zeros((blk, blk), F32)
        for kk in range(REL_BUCKETS):
            val = rel_ref[kk, h]
            acc_o = jnp.where(bo == kk, val, acc_o)
            acc_p = jnp.where(bp == kk, val, acc_p)
        bias_own[...] = acc_o
        bias_prev[...] = acc_p

    bias_far = rel_ref[REL_BUCKETS - 1, h]
    scale = HEAD_DIM ** -0.5
    kf = k_ref[...]
    kmean = jnp.mean(kf.reshape(nb, blk, HEAD_DIM), axis=1)
    kb_ref[...] = kf.astype(BF16)
    nt_dims = (((1,), (1,)), ((), ()))
    eye = (lax.broadcasted_iota(jnp.int32, (HEAD_DIM, HEAD_DIM), 0)
           == lax.broadcasted_iota(jnp.int32, (HEAD_DIM, HEAD_DIM), 1)).astype(BF16)
    vt_ref[...] = lax.dot_general(eye, v_ref[...].astype(BF16), nt_dims,
                                  preferred_element_type=F32).astype(BF16)
    causal = (lax.broadcasted_iota(jnp.int32, (blk, blk), 0)
              <= lax.broadcasted_iota(jnp.int32, (blk, blk), 1))

    for i in range(nb):
        qi = q_ref[i * blk:(i + 1) * blk, :]
        qb = qi.astype(BF16)
        sel = None
        if i > MOBA_TOPK:
            route = lax.dot_general(kmean, qi, nt_dims, precision=lax.Precision.HIGHEST,
                                    preferred_element_type=F32)
            rc = [route[n:n + 1, :] for n in range(i)]
            sel = []
            for n in range(i):
                rank = jnp.zeros((1, blk), jnp.int32)
                for m in range(i):
                    if m == n:
                        continue
                    beats = (rc[m] >= rc[n]) if m < n else (rc[m] > rc[n])
                    rank = rank + beats.astype(jnp.int32)
                sel.append(rank < MOBA_TOPK)
        s_list = []
        for n in range(i + 1):
            s = lax.dot_general(kb_ref[n * blk:(n + 1) * blk, :], qb, nt_dims,
                                preferred_element_type=F32) * scale
            if n == i:
                s = jnp.where(causal, s + bias_own[...], NEG_INF)
            else:
                s = s + (bias_prev[...] if n == i - 1 else bias_far)
                if sel is not None:
                    s = jnp.where(sel[n], s, NEG_INF)
            s_list.append(s)
        m_run = jnp.max(s_list[0], axis=0, keepdims=True)
        for s in s_list[1:]:
            m_run = jnp.maximum(m_run, jnp.max(s, axis=0, keepdims=True))
        l_run = jnp.zeros((1, blk), F32)
        acc = jnp.zeros((HEAD_DIM, blk), F32)
        for n, s in enumerate(s_list):
            p = jnp.exp(s - m_run)
            l_run = l_run + jnp.sum(p, axis=0, keepdims=True)
            acc = acc + jnp.dot(vt_ref[:, n * blk:(n + 1) * blk], p.astype(BF16),
                                preferred_element_type=F32)
        o_ref[i * blk:(i + 1) * blk, :] = (acc / l_run).T.astype(BF16)


def _moba(proj, rel_bias, bko, bkp, bsz, seq):
    nb = seq // MOBA_BLOCK
    qo, ko, vo = COL_MOBA_Q // HEAD_DIM, COL_MOBA_K // HEAD_DIM, COL_MOBA_V // HEAD_DIM
    blk = MOBA_BLOCK
    return pl.pallas_call(
        functools.partial(_moba_kernel, nb=nb),
        name="moba",
        grid=(MOBA_HEADS, bsz),
        in_specs=[pl.BlockSpec(memory_space=pltpu.SMEM),
                  pl.BlockSpec((blk, blk), lambda h, b: (0, 0)),
                  pl.BlockSpec((blk, blk), lambda h, b: (0, 0)),
                  pl.BlockSpec((seq, HEAD_DIM), lambda h, b: (b, qo + h)),
                  pl.BlockSpec((seq, HEAD_DIM), lambda h, b: (b, ko + h)),
                  pl.BlockSpec((seq, HEAD_DIM), lambda h, b: (b, vo + h))],
        out_specs=pl.BlockSpec((seq, HEAD_DIM), lambda h, b: (b, h)),
        out_shape=jax.ShapeDtypeStruct((bsz * seq, MOBA_W), BF16),
        scratch_shapes=[pltpu.VMEM((blk, blk), F32), pltpu.VMEM((blk, blk), F32),
                        pltpu.VMEM((seq, HEAD_DIM), BF16), pltpu.VMEM((HEAD_DIM, seq), BF16)],
        compiler_params=_vmem(48),
    )(rel_bias, bko, bkp, proj, proj, proj)


SUBLANES = 8


def _conv_silu(xh, w):
    acc = xh[SUBLANES:, :] * w[GDN_CONV - 1:GDN_CONV, :]
    for s in range(1, GDN_CONV):
        acc = acc + pltpu.roll(xh, s, axis=0)[SUBLANES:, :] * w[GDN_CONV - 1 - s:GDN_CONV - s, :]
    return _silu(acc)


def _l2norm(x):
    return x * lax.rsqrt(jnp.sum(x * x, axis=-1, keepdims=True) + RMS_EPS)


def _softplus(x):
    return jnp.maximum(x, 0.0) + jnp.log1p(jnp.exp(-jnp.abs(x)))


ELIM_BLOCK = 8
MERGE_LEVELS = (GDN_CHUNK // ELIM_BLOCK).bit_length() - 1


def _block_diag_inverse(lpair):
    n = lpair.shape[0]
    nv = n // SUBLANES
    vpb = ELIM_BLOCK // SUBLANES
    rid = lax.broadcasted_iota(jnp.int32, (SUBLANES, 2 * n), 0)
    lid = lax.broadcasted_iota(jnp.int32, (SUBLANES, 2 * n), 1)
    t_rows = [(lid % n == rid + v * SUBLANES).astype(F32) for v in range(nv)]
    l_rows = [lpair[v * SUBLANES:(v + 1) * SUBLANES, :] for v in range(nv)]
    for m in range(n - 1):
        v0, s0 = divmod(m, SUBLANES)
        v_end = (v0 // vpb + 1) * vpb
        row = t_rows[v0][s0:s0 + 1, :]
        col = (lid // n) * n + m
        for v in range(v0 if s0 < SUBLANES - 1 else v0 + 1, v_end):
            t_rows[v] = t_rows[v] - jnp.take_along_axis(l_rows[v], col, axis=1) * row
    return jnp.concatenate(t_rows, axis=0)


def _lane_block_diag(pair):
    first = lax.broadcasted_iota(jnp.int32, pair.shape, 1) < pair.shape[1] // 2
    zero = jnp.zeros_like(pair)
    return jnp.concatenate([jnp.where(first, pair, zero), jnp.where(first, zero, pair)], axis=0)


def _merge_lower_products(lpair, tpair, k):
    n = lpair.shape[0]
    ri = lax.broadcasted_iota(jnp.int32, lpair.shape, 0)
    ci = lax.broadcasted_iota(jnp.int32, lpair.shape, 1) % n
    off = (ri // (2 * k) == ci // (2 * k)) & (ri // k > ci // k)
    lk = jnp.where(off, lpair, 0.0).astype(BF16)
    return jnp.dot(lk, _lane_block_diag(tpair.astype(BF16)), preferred_element_type=F32)


def _merge_apply(tpair, lt):
    return tpair - jnp.dot(tpair.astype(BF16), _lane_block_diag(lt), preferred_element_type=F32)


def _software_pipeline(stages, n):
    ns = len(stages)

    def run(it, lo, hi):
        conts = [stages[s](it - s) for s in reversed(range(lo, hi))]
        for cont in conts:
            if cont is not None:
                cont()

    for it in range(ns - 1):
        run(it, 0, it + 1)

    def body(it, carry):
        run(it, 0, ns)
        return carry

    lax.fori_loop(ns - 1, n, body, 0)
    for it in range(n, n + ns - 1):
        run(it, it - n + 1, ns)


def _lane_pick(x, lane, idx):
    return jnp.sum(jnp.where(lane == idx, x, 0.0), axis=-1, keepdims=True)


def _gdn_kernel(alog_ref, dtb_ref, gp_ref, nw_ref, cwq_ref, cwk_ref, cwv_ref,
                q_ref, k_ref, v_ref, z_ref, sm_ref, rw_ref, y_ref,
                kb_s, qn_s, kdp_s, rhs_s, qd_s, gcb_s, betab_s, gcr_s, gram_s, lm_s, lhs2_s,
                sol_s, mp_s, n_s, r_s, st_s, *merge_s, nchunk):
    tl_s, lt_s = merge_s[:MERGE_LEVELS + 1], merge_s[MERGE_LEVELS + 1:]
    hq = pl.program_id(1)
    c64 = GDN_CHUNK
    hd = HEAD_DIM
    heads = range(2)

    tri_u = (lax.broadcasted_iota(jnp.int32, (c64, c64), 0)
             <= lax.broadcasted_iota(jnp.int32, (c64, c64), 1)).astype(F32)
    gc_rows = []
    for j in heads:
        hv = 2 * hq + j
        a_neg_r = -jnp.exp(jnp.full((nchunk, c64), alog_ref[hv], F32))
        g_row = a_neg_r * _softplus(rw_ref[2 + j] + dtb_ref[hv])
        gc_rows.append(jnp.dot(g_row, tri_u, precision=lax.Precision.HIGHEST,
                               preferred_element_type=F32))
    gcr_s[...] = jnp.concatenate(gc_rows, axis=1)

    rows = lax.broadcasted_iota(jnp.int32, (c64, 2 * c64), 0)
    cols = lax.broadcasted_iota(jnp.int32, (c64, 2 * c64), 1) % c64
    tril = rows >= cols
    strict = rows > cols
    eye2 = (lax.broadcasted_iota(jnp.int32, (2 * c64, 2 * c64), 0)
            == lax.broadcasted_iota(jnp.int32, (2 * c64, 2 * c64), 1)).astype(BF16)
    nt_dims = (((1,), (1,)), ((), ()))

    def rows_of(c):
        start = c * c64
        return pl.ds(start if isinstance(c, int) else pl.multiple_of(start, c64), c64)

    def with_halo(x_ref, c):
        if isinstance(c, int) and c == 0:
            return jnp.concatenate([jnp.zeros((SUBLANES, x_ref.shape[1]), F32), x_ref[:c64, :]],
                                   axis=0)
        start = c * c64 - SUBLANES
        if not isinstance(c, int):
            start = pl.multiple_of(start, SUBLANES)
        return x_ref[pl.ds(start, c64 + SUBLANES), :]

    lane = lax.broadcasted_iota(jnp.int32, (c64, N_SMALL), 1)
    first = lane < c64
    pos = lax.broadcasted_iota(jnp.int32, (c64, N_SMALL), 0)

    def stage_prep(c):
        r = rows_of(c)
        qn = _l2norm(_conv_silu(with_halo(q_ref, c), cwq_ref[...])) * (hd ** -0.5)
        kn = _l2norm(_conv_silu(with_halo(k_ref, c), cwk_ref[...]))
        vc = _conv_silu(with_halo(v_ref, c), cwv_ref[...])
        qn_s[r, :] = qn.astype(BF16)
        kb_s[r, :] = kn.astype(BF16)
        sm = sm_ref[r, :]
        sig_all = jax.nn.sigmoid(sm)
        gc_all = -jnp.exp(gp_ref[0:1, :]) * _softplus(sm + gp_ref[1:2, :])
        sft = 1
        while sft < c64:
            gc_all = gc_all + jnp.where(pos >= sft, pltpu.roll(gc_all, sft, axis=0), 0.0)
            sft *= 2
        rest_all = gc_all[c64 - 1:, :] - gc_all
        betas, gc_cols = [], []
        for j in heads:
            hv = 2 * hq + j
            beta = _lane_pick(sig_all, lane, hv)
            gc_col = _lane_pick(gc_all, lane, GDN_V_HEADS + hv)
            rest_col = _lane_pick(rest_all, lane, GDN_V_HEADS + hv)
            eg = jnp.exp(gc_col)
            rhs_s[j, r, :hd] = (vc[:, j * hd:(j + 1) * hd] * beta).astype(BF16)
            rhs_s[j, r, hd:] = (kn * (beta * eg)).astype(BF16)
            qd_s[j, r, :] = (qn * eg).astype(BF16)
            kdp_s[c, j * c64:(j + 1) * c64, :] = (kn * jnp.exp(rest_col)).astype(BF16)
            betas.append(beta)
            gc_cols.append(gc_col)
        gcb_s[r, :] = jnp.where(first, gc_cols[0], gc_cols[1])
        betab_s[r, :] = jnp.where(first, betas[0], betas[1])

    def stage_gram(c):
        r = rows_of(c)
        kb = kb_s[r, :]
        kq = jnp.concatenate([kb, qn_s[r, :]], axis=0)
        gram = lax.dot_general(kq, jnp.concatenate([kb, kb], axis=0), nt_dims,
                               preferred_element_type=F32)
        kd_t = lax.dot_general(eye2, kdp_s[c], nt_dims, preferred_element_type=F32)

        def finish():
            gram_s[c] = gram
            lhs2_s[c, :2 * c64, :] = kd_t.astype(BF16)

        return finish

    def stage_factor(c):
        r = rows_of(c)
        dec = jnp.exp(jnp.where(tril, gcb_s[r, :] - gcr_s[pl.ds(c, 1), :], NEG_INF))
        lpair = jnp.where(strict, gram_s[c, :c64, :] * dec, 0.0) * betab_s[r, :]
        lm_s[r, :] = lpair
        lhs2_s[c, 2 * c64:, :] = (gram_s[c, c64:, :] * dec).astype(BF16)
        tl_s[0][r, :] = _block_diag_inverse(lpair)

    def stage_merge_products(level):
        def stage(c):
            r = rows_of(c)
            lt = _merge_lower_products(lm_s[r, :], tl_s[level][r, :], ELIM_BLOCK << level)

            def finish():
                lt_s[level][r, :] = lt.astype(BF16)

            return finish
        return stage

    def stage_merge_apply(level):
        def stage(c):
            r = rows_of(c)
            merged = _merge_apply(tl_s[level][r, :], lt_s[level][r, :])

            def finish():
                tl_s[level + 1][r, :] = merged

            return finish
        return stage

    def stage_solve(c):
        r = rows_of(c)
        zero = jnp.zeros((c64, 2 * hd), BF16)
        rhs_bd = jnp.concatenate([jnp.concatenate([rhs_s[0, r, :], zero], axis=1),
                                  jnp.concatenate([zero, rhs_s[1, r, :]], axis=1)], axis=0)
        sol = jnp.dot(tl_s[MERGE_LEVELS][r, :].astype(BF16), rhs_bd,
                      preferred_element_type=F32)

        def finish():
            sol_s[r, :] = sol.astype(BF16)

        return finish

    def stage_fold(c):
        r = rows_of(c)
        sol = sol_s[r, :]
        out = jnp.dot(lhs2_s[c], _lane_block_diag(sol), preferred_element_type=F32)

        def finish():
            for j in heads:
                u_col, w_col = 2 * j * hd, (2 * j + 1) * hd
                n_s[j, c] = out[:2 * c64, u_col:u_col + hd]
                mp_s[j, c, :2 * c64, :] = out[:2 * c64, w_col:w_col + hd].astype(BF16)
                mp_s[j, c, 2 * c64:, :] = (qd_s[j, r, :].astype(F32)
                                           - out[2 * c64:, w_col:w_col + hd]).astype(BF16)
                r_s[j, r, :] = out[2 * c64:, u_col:u_col + hd]

        return finish

    st_s[...] = jnp.zeros_like(st_s)
    nw = nw_ref[...]

    def stage_state(c):
        r = rows_of(c)
        sts = [st_s[j] for j in heads]
        outs = [jnp.dot(mp_s[j, c], sts[j].astype(BF16), preferred_element_type=F32)
                for j in heads]

        def finish():
            for j in heads:
                end = (j + 1) * c64
                gl = gcr_s[pl.ds(c, 1), end - 1:end]
                st_s[j] = sts[j] * jnp.exp(gl) - outs[j][:2 * c64] + n_s[j, c]
                r_s[j, r, :] = r_s[j, r, :] + outs[j][2 * c64:]

        return finish

    def stage_norm(c):
        r = rows_of(c)
        for j in heads:
            o = r_s[j, r, :]
            og = (o * lax.rsqrt(jnp.mean(o * o, axis=-1, keepdims=True) + RMS_EPS)
                  * nw * _silu(z_ref[r, j * hd:(j + 1) * hd]))
            y_ref[r, j * hd:(j + 1) * hd] = og.astype(BF16)

    merges = [stage(lv) for lv in range(MERGE_LEVELS)
              for stage in (stage_merge_products, stage_merge_apply)]
    _software_pipeline([stage_prep, stage_gram, stage_factor] + merges
                       + [stage_solve, stage_fold, stage_state, stage_norm], nchunk)


def _gdn(proj, small, rows, conv_w, a_log, dt_bias, norm_w, bsz, seq):
    nchunk = seq // GDN_CHUNK
    hd = HEAD_DIM
    c64 = GDN_CHUNK
    qo, ko = COL_GDN_Q // hd, COL_GDN_K // hd
    vo, zo = COL_GDN_V // (2 * hd), COL_GDN_Z // (2 * hd)
    cvo = (2 * GDN_QK_W) // (2 * hd)
    smem = pl.BlockSpec(memory_space=pltpu.SMEM)
    pad = (GDN_V_HEADS, N_SMALL - 2 * GDN_V_HEADS)
    gate_params = jnp.stack([jnp.pad(a_log, pad), jnp.pad(dt_bias, pad)])
    return pl.pallas_call(
        functools.partial(_gdn_kernel, nchunk=nchunk),
        name="gdn",
        grid=(bsz, GDN_QK_HEADS),
        in_specs=[smem, smem,
                  pl.BlockSpec((2, N_SMALL), lambda b, h: (0, 0)),
                  pl.BlockSpec((1, hd), lambda b, h: (0, 0)),
                  pl.BlockSpec((GDN_CONV, hd), lambda b, h: (0, h)),
                  pl.BlockSpec((GDN_CONV, hd), lambda b, h: (0, GDN_QK_HEADS + h)),
                  pl.BlockSpec((GDN_CONV, 2 * hd), lambda b, h: (0, cvo + h)),
                  pl.BlockSpec((seq, hd), lambda b, h: (b, qo + h)),
                  pl.BlockSpec((seq, hd), lambda b, h: (b, ko + h)),
                  pl.BlockSpec((seq, 2 * hd), lambda b, h: (b, vo + h)),
                  pl.BlockSpec((seq, 2 * hd), lambda b, h: (b, zo + h)),
                  pl.BlockSpec((seq, N_SMALL), lambda b, h: (b, 0)),
                  pl.BlockSpec((None, None, 4, nchunk, GDN_CHUNK), lambda b, h: (b, h, 0, 0, 0))],
        out_specs=pl.BlockSpec((seq, 2 * hd), lambda b, h: (b, h)),
        out_shape=jax.ShapeDtypeStruct((bsz * seq, GDN_V_W), BF16),
        scratch_shapes=[pltpu.VMEM((seq, hd), BF16),
                        pltpu.VMEM((seq, hd), BF16),
                        pltpu.VMEM((nchunk, 2 * c64, hd), BF16),
                        pltpu.VMEM((2, seq, 2 * hd), BF16),
                        pltpu.VMEM((2, seq, hd), BF16),
                        pltpu.VMEM((seq, 2 * c64), F32),
                        pltpu.VMEM((seq, 2 * c64), F32),
                        pltpu.VMEM((nchunk, 2 * c64), F32),
                        pltpu.VMEM((nchunk, 2 * c64, 2 * c64), F32),
                        pltpu.VMEM((seq, 2 * c64), F32),
                        pltpu.VMEM((nchunk, 3 * c64, hd), BF16),
                        pltpu.VMEM((seq, 4 * hd), BF16),
                        pltpu.VMEM((2, nchunk, 3 * c64, hd), BF16),
                        pltpu.VMEM((2, nchunk, hd, hd), F32),
                        pltpu.VMEM((2, seq, hd), F32),
                        pltpu.VMEM((2, hd, hd), F32)]
                       + [pltpu.VMEM((seq, 2 * c64), F32)] * (MERGE_LEVELS + 1)
                       + [pltpu.VMEM((seq, 2 * c64), BF16)] * MERGE_LEVELS,
        compiler_params=_vmem(58),
    )(a_log, dt_bias, gate_params, norm_w, conv_w, conv_w, conv_w, proj, proj, proj, proj, small, rows)


def _merge_kernel(ya_ref, yb_ref, ga_ref, gb_ref, x_ref, g1_ref, sh2_ref, sc2_ref,
                  lng_ref, lnb_ref, wpm_ref, wpg_ref, wo_ref, x1_ref, h2_ref):
    pa = jnp.dot(ya_ref[...], wpm_ref[...], preferred_element_type=F32)
    pb = jnp.dot(yb_ref[...], wpg_ref[...], preferred_element_type=F32)
    merged = jax.nn.sigmoid(ga_ref[...]) * pa + jax.nn.sigmoid(gb_ref[...]) * pb
    y = jnp.dot(merged.astype(BF16), wo_ref[...], preferred_element_type=F32)
    x1 = _layer_norm(DEEPNORM_ALPHA * x_ref[...] + g1_ref[...] * y, lng_ref[...], lnb_ref[...])
    x1_ref[...] = x1
    h2_ref[...] = (x1 * (1.0 + sc2_ref[...]) + sh2_ref[...]).astype(BF16)


def _merge(ya, yb, proj, x2, mod3, ln_g, ln_b, wpm, wpg, wo, seq):
    t = x2.shape[0]
    tm = 256
    per_b = seq // tm
    d = D_MODEL

    def modspec(k):
        return pl.BlockSpec((None, 1, d), lambda i: (i // per_b, 0, k))

    def const(shape):
        return pl.BlockSpec(shape, lambda i: (0, 0), pipeline_mode=pl.Buffered(1))

    return pl.pallas_call(
        _merge_kernel,
        name="merge",
        grid=(t // tm,),
        in_specs=[pl.BlockSpec((tm, MOBA_W), lambda i: (i, 0)),
                  pl.BlockSpec((tm, GDN_V_W), lambda i: (i, 0)),
                  pl.BlockSpec((tm, d), lambda i: (i, COL_GATE_A // d)),
                  pl.BlockSpec((tm, d), lambda i: (i, COL_GATE_B // d)),
                  pl.BlockSpec((tm, d), lambda i: (i, 0)),
                  modspec(2), modspec(3), modspec(4),
                  const((1, d)), const((1, d)),
                  const((MOBA_W, d)), const((GDN_V_W, d)), const((d, d))],
        out_specs=[pl.BlockSpec((tm, d), lambda i: (i, 0)),
                   pl.BlockSpec((tm, d), lambda i: (i, 0))],
        out_shape=[jax.ShapeDtypeStruct((t, d), F32), jax.ShapeDtypeStruct((t, d), BF16)],
        compiler_params=_vmem(56),
    )(ya, yb, proj, proj, x2, mod3, mod3, mod3, ln_g, ln_b, wpm, wpg, wo)


def _ffn_kernel(h_ref, x1_ref, g2_ref, lng_ref, lnb_ref, wg_ref, wu_ref, wo_ref, o_ref, acc_ref):
    f = pl.program_id(1)

    @pl.when(f == 0)
    def _():
        acc_ref[...] = jnp.zeros_like(acc_ref)

    h = h_ref[...]
    gate = jnp.dot(h, wg_ref[...], preferred_element_type=F32)
    up = jnp.dot(h, wu_ref[...], preferred_element_type=F32)
    act = (_silu(gate) * up).astype(BF16)
    acc_ref[...] += jnp.dot(act, wo_ref[...], preferred_element_type=F32)

    @pl.when(f == pl.num_programs(1) - 1)
    def _():
        r = DEEPNORM_ALPHA * x1_ref[...] + g2_ref[...] * acc_ref[...]
        o_ref[...] = _layer_norm(r, lng_ref[...], lnb_ref[...])


def _ffn(h2, x1, mod3, ln_g, ln_b, w_in, w_out, seq):
    t = h2.shape[0]
    tm, tf = 512, 512
    per_b = seq // tm
    d = D_MODEL
    nf = D_FF // tf
    return pl.pallas_call(
        _ffn_kernel,
        name="ffn",
        grid=(t // tm, nf),
        in_specs=[pl.BlockSpec((tm, d), lambda i, f: (i, 0)),
                  pl.BlockSpec((tm, d), lambda i, f: (i, 0)),
                  pl.BlockSpec((None, 1, d), lambda i, f: (i // per_b, 0, 5)),
                  pl.BlockSpec((1, d), lambda i, f: (0, 0)),
                  pl.BlockSpec((1, d), lambda i, f: (0, 0)),
                  pl.BlockSpec((d, tf), lambda i, f: (0, f)),
                  pl.BlockSpec((d, tf), lambda i, f: (0, nf + f)),
                  pl.BlockSpec((tf, d), lambda i, f: (f, 0))],
        out_specs=pl.BlockSpec((tm, d), lambda i, f: (i, 0)),
        out_shape=jax.ShapeDtypeStruct((t, d), F32),
        scratch_shapes=[pltpu.VMEM((tm, d), F32)],
        compiler_params=_vmem(48),
    )(h2, x1, mod3, ln_g, ln_b, w_in, w_in, w_out)


def _rel_bucket(dist):
    max_exact = REL_BUCKETS // 2
    n = jnp.maximum(dist, 0)
    nf = jnp.maximum(n, 1).astype(F32)
    large = max_exact + (jnp.log(nf / max_exact) / math.log(REL_MAX_DIST / max_exact)
                         * (REL_BUCKETS - max_exact)).astype(jnp.int32)
    large = jnp.minimum(large, REL_BUCKETS - 1)
    return jnp.where(n < max_exact, n, large)


def _layer(x, c, w_ada, b_ada, w_in, conv_w, a_log, dt_bias, gdn_norm_w, rel_bias,
           w_proj_moba, w_proj_gdn, w_out, ln1_g, ln1_b, w_ffn_in, w_ffn_out, ln2_g, ln2_b):
    bsz, seq, d = x.shape
    t = bsz * seq
    x2 = x.reshape(t, d)

    mod = _ada_mod(c, w_ada, b_ada)
    mod3 = mod.reshape(bsz, 1, 6 * d)

    w_all = w_in.astype(BF16)
    n_gates = 2 * GDN_V_HEADS
    w_gate = w_all[:, N_HEAD + n_gates:]
    w_small = jnp.pad(w_all[:, N_HEAD:N_HEAD + n_gates], ((0, 0), (0, N_SMALL - n_gates)))

    proj, small = _in_proj(x2, mod3, w_all, w_gate, w_small, seq)

    ii = jnp.arange(MOBA_BLOCK, dtype=jnp.int32)
    dist = ii[None, :] - ii[:, None]
    bko = _rel_bucket(dist)
    bkp = _rel_bucket(dist + MOBA_BLOCK)
    ya = _moba(proj, rel_bias, bko, bkp, bsz, seq)

    nchunk = seq // GDN_CHUNK
    sm_t = small[:, :2 * GDN_V_HEADS].reshape(bsz, seq, 2, GDN_QK_HEADS, 2)
    rows = sm_t.transpose(0, 3, 2, 4, 1).reshape(bsz, GDN_QK_HEADS, 4, nchunk, GDN_CHUNK)
    yb = _gdn(proj, small, rows, conv_w, a_log, dt_bias, gdn_norm_w.reshape(1, HEAD_DIM),
              bsz, seq)

    x1, h2 = _merge(ya, yb, proj, x2, mod3, ln1_g.reshape(1, d), ln1_b.reshape(1, d),
                    w_proj_moba.astype(BF16), w_proj_gdn.astype(BF16), w_out.astype(BF16), seq)
    out = _ffn(h2, x1, mod3, ln2_g.reshape(1, d), ln2_b.reshape(1, d),
               w_ffn_in.astype(BF16), w_ffn_out.astype(BF16), seq)
    return out.reshape(bsz, seq, d)


def kernel(x, c, w_ada, b_ada, w_in, conv_w, a_log, dt_bias, gdn_norm_w, rel_bias, w_proj_moba,
           w_proj_gdn, w_out, ln1_g, ln1_b, w_ffn_in, w_ffn_out, ln2_g, ln2_b):
    depth = w_ada.shape[0]
    for l in range(depth):
        x = _layer(x, c, w_ada[l], b_ada[l], w_in[l], conv_w[l], a_log[l], dt_bias[l],
                   gdn_norm_w[l], rel_bias, w_proj_moba[l], w_proj_gdn[l], w_out[l],
                   ln1_g[l], ln1_b[l], w_ffn_in[l], w_ffn_out[l], ln2_g[l], ln2_b[l])
    return x
```

```python
import functools
import math

import jax
import jax.numpy as jnp
from jax import lax
from jax.experimental import pallas as pl
from jax.experimental.pallas import tpu as pltpu

F32 = jnp.float32
BF16 = jnp.bfloat16

D_MODEL = 2048
MOBA_HEADS = 8
HEAD_DIM = 128
MOBA_W = MOBA_HEADS * HEAD_DIM
MOBA_BLOCK = 256
MOBA_TOPK = 3
REL_BUCKETS = 32
REL_MAX_DIST = 128
GDN_QK_HEADS = 8
GDN_V_HEADS = 16
GDN_QK_W = GDN_QK_HEADS * HEAD_DIM
GDN_V_W = GDN_V_HEADS * HEAD_DIM
GDN_CONV = 4
GDN_CHUNK = 64
D_FF = 5632
DEEPNORM_ALPHA = 2.0 ** 0.25
LN_EPS = 1e-5
RMS_EPS = 1e-6
NEG_INF = -1e30

COL_GATE_A = 0
COL_GATE_B = 2048
COL_MOBA_Q = 4096
COL_MOBA_K = 5120
COL_MOBA_V = 6144
COL_GDN_Q = 7168
COL_GDN_K = 8192
COL_GDN_V = 9216
COL_GDN_Z = 11264
N_MAIN = 13312
N_GATE = 4096
N_HEAD = N_MAIN - N_GATE
N_SMALL = 128

V7X_VMEM_BYTES = 64 * 1024 * 1024
LANES = 128
SUBLANES = 8
BF16_SUBLANES = 16


def _vmem(mb):
    return pltpu.CompilerParams(vmem_limit_bytes=mb * 1024 * 1024)


def _silu(x):
    return x * jax.nn.sigmoid(x)


def _layer_norm(r, gain, bias):
    mu = jnp.mean(r, axis=-1, keepdims=True)
    d = r - mu
    var = jnp.mean(d * d, axis=-1, keepdims=True)
    return d * lax.rsqrt(var + LN_EPS) * gain + bias


def _ada_kernel(c_ref, w_ref, b_ref, o_ref):
    sc = _silu(c_ref[...])
    o_ref[...] = jnp.dot(sc, w_ref[...], precision=lax.Precision.HIGHEST,
                         preferred_element_type=F32) + b_ref[...]


def _ada_mod(c, w_ada, b_ada):
    bsz = c.shape[0]
    n = w_ada.shape[1]
    tn = 1024
    return pl.pallas_call(
        _ada_kernel,
        name="ada_mod",
        grid=(n // tn,),
        in_specs=[pl.BlockSpec((bsz, D_MODEL), lambda j: (0, 0)),
                  pl.BlockSpec((D_MODEL, tn), lambda j: (0, j)),
                  pl.BlockSpec((1, tn), lambda j: (0, j))],
        out_specs=pl.BlockSpec((bsz, tn), lambda j: (0, j)),
        out_shape=jax.ShapeDtypeStruct((bsz, n), F32),
        compiler_params=_vmem(40),
    )(c, w_ada, b_ada.reshape(1, n))


def _inproj_kernel(x_ref, sh_ref, sc_ref, w_ref, wg_ref, ws_ref, o_ref, os_ref, h_ref, *, n_head):
    j = pl.program_id(1)

    @pl.when(j == 0)
    def _():
        h = x_ref[...] * (1.0 + sc_ref[...]) + sh_ref[...]
        h_ref[...] = h.astype(BF16)
        os_ref[...] = jnp.dot(h_ref[...], ws_ref[...], preferred_element_type=F32)

    @pl.when(j < n_head)
    def _():
        o_ref[...] = jnp.dot(h_ref[...], w_ref[...], preferred_element_type=F32)

    @pl.when(j >= n_head)
    def _():
        o_ref[...] = jnp.dot(h_ref[...], wg_ref[...], preferred_element_type=F32)


def _in_proj(x2, mod3, w_all, w_gate, w_small, seq):
    t = x2.shape[0]
    tm, tn = 1024, 1024
    per_b = seq // tm
    n_head, n_gate, n_tiles = N_HEAD // tn, N_GATE // tn, N_MAIN // tn
    return pl.pallas_call(
        functools.partial(_inproj_kernel, n_head=n_head),
        name="in_proj",
        grid=(t // tm, n_tiles),
        in_specs=[pl.BlockSpec((tm, D_MODEL), lambda i, j: (i, 0)),
                  pl.BlockSpec((None, 1, D_MODEL), lambda i, j: (i // per_b, 0, 0)),
                  pl.BlockSpec((None, 1, D_MODEL), lambda i, j: (i // per_b, 0, 1)),
                  pl.BlockSpec((D_MODEL, tn), lambda i, j: (0, jnp.minimum(j, n_head - 1))),
                  pl.BlockSpec((D_MODEL, tn), lambda i, j: (0, jnp.maximum(j - n_head, 0))),
                  pl.BlockSpec((D_MODEL, N_SMALL), lambda i, j: (0, 0))],
        out_specs=[pl.BlockSpec((tm, tn), lambda i, j: (i, (j + n_gate) % n_tiles)),
                   pl.BlockSpec((tm, N_SMALL), lambda i, j: (i, 0))],
        out_shape=[jax.ShapeDtypeStruct((t, N_MAIN), F32),
                   jax.ShapeDtypeStruct((t, N_SMALL), F32)],
        scratch_shapes=[pltpu.VMEM((tm, D_MODEL), BF16)],
        compiler_params=_vmem(56),
    )(x2, mod3, mod3, w_all, w_gate, w_small)


def _moba_kernel(rel_ref, bko_ref, bkp_ref, q_ref, k_ref, v_ref, o_ref,
                 bias_own, bias_prev, kb_ref, vt_ref, *, nb):
    h = pl.program_id(0)
    blk = MOBA_BLOCK
    inv_scale = HEAD_DIM ** 0.5
    scale_log2e = HEAD_DIM ** -0.5 * math.log2(math.e)

    @pl.when(pl.program_id(1) == 0)
    def _():
        bo = bko_ref[...]
        bp = bkp_ref[...]
        acc_o = jnp.zeros((blk, blk), F32)
        acc_p = jnp.zeros((blk, blk), F32)
        for kk in range(REL_BUCKETS):
            val = rel_ref[kk, h] * inv_scale
            acc_o = jnp.where(bo == kk, val, acc_o)
            acc_p = jnp.where(bp == kk, val, acc_p)
        bias_own[...] = acc_o
        bias_prev[...] = acc_p

    bias_far = rel_ref[REL_BUCKETS - 1, h] * inv_scale
    kf = k_ref[...]
    kmean = jnp.mean(kf.reshape(nb, blk, HEAD_DIM), axis=1)
    kb_ref[...] = kf.astype(BF16)
    nt_dims = (((1,), (1,)), ((), ()))
    eye = (lax.broadcasted_iota(jnp.int32, (HEAD_DIM, HEAD_DIM), 0)
           == lax.broadcasted_iota(jnp.int32, (HEAD_DIM, HEAD_DIM), 1)).astype(BF16)
    vt_ref[:HEAD_DIM, :] = lax.dot_general(eye, v_ref[...].astype(BF16), nt_dims,
                                           preferred_element_type=F32).astype(BF16)
    pad_rows = vt_ref.shape[0] - HEAD_DIM
    vt_ref[HEAD_DIM:, :] = (lax.broadcasted_iota(jnp.int32, (pad_rows, vt_ref.shape[1]), 0)
                            == 0).astype(BF16)
    causal = (lax.broadcasted_iota(jnp.int32, (blk, blk), 0)
              <= lax.broadcasted_iota(jnp.int32, (blk, blk), 1))

    for i in range(nb):
        qi = q_ref[i * blk:(i + 1) * blk, :]
        qb = qi.astype(BF16)
        sel = None
        if i > MOBA_TOPK:
            route = lax.dot_general(kmean, qi, nt_dims, precision=lax.Precision.HIGHEST,
                                    preferred_element_type=F32)
            rc = [route[n:n + 1, :] for n in range(i)]
            sel = []
            for n in range(i):
                rank = jnp.zeros((1, blk), jnp.int32)
                for m in range(i):
                    if m == n:
                        continue
                    beats = (rc[m] >= rc[n]) if m < n else (rc[m] > rc[n])
                    rank = rank + beats.astype(jnp.int32)
                sel.append(rank < MOBA_TOPK)
        t_list = []
        for n in range(i + 1):
            t = lax.dot_general(kb_ref[n * blk:(n + 1) * blk, :], qb, nt_dims,
                                preferred_element_type=F32)
            if n == i:
                t = jnp.where(causal, t + bias_own[...], NEG_INF)
            else:
                if n == i - 1:
                    t = t + bias_prev[...]
                if sel is not None:
                    t = jnp.where(sel[n], t, NEG_INF)
            t_list.append(t)
        n_far = max(i - 1, 0)
        m_run = jnp.max(t_list[n_far], axis=0, keepdims=True)
        for t in t_list[n_far + 1:]:
            m_run = jnp.maximum(m_run, jnp.max(t, axis=0, keepdims=True))
        if n_far:
            m_far = jnp.max(t_list[0], axis=0, keepdims=True)
            for t in t_list[1:n_far]:
                m_far = jnp.maximum(m_far, jnp.max(t, axis=0, keepdims=True))
            m_run = jnp.maximum(m_run, m_far + bias_far)
        acc = jnp.zeros((vt_ref.shape[0], blk), F32)
        for n, t in enumerate(t_list):
            offset = m_run - bias_far if n < n_far else m_run
            p = jnp.exp2((t - offset) * scale_log2e)
            acc = acc + jnp.dot(vt_ref[:, n * blk:(n + 1) * blk], p.astype(BF16),
                                preferred_element_type=F32)
        out = acc[:HEAD_DIM] / acc[HEAD_DIM:HEAD_DIM + 1]
        o_ref[i * blk:(i + 1) * blk, :] = out.T.astype(BF16)


def _moba(proj, rel_bias, bko, bkp, bsz, seq):
    nb = seq // MOBA_BLOCK
    qo, ko, vo = COL_MOBA_Q // HEAD_DIM, COL_MOBA_K // HEAD_DIM, COL_MOBA_V // HEAD_DIM
    blk = MOBA_BLOCK
    return pl.pallas_call(
        functools.partial(_moba_kernel, nb=nb),
        name="moba",
        grid=(MOBA_HEADS, bsz),
        in_specs=[pl.BlockSpec(memory_space=pltpu.SMEM),
                  pl.BlockSpec((blk, blk), lambda h, b: (0, 0)),
                  pl.BlockSpec((blk, blk), lambda h, b: (0, 0)),
                  pl.BlockSpec((seq, HEAD_DIM), lambda h, b: (b, qo + h)),
                  pl.BlockSpec((seq, HEAD_DIM), lambda h, b: (b, ko + h)),
                  pl.BlockSpec((seq, HEAD_DIM), lambda h, b: (b, vo + h))],
        out_specs=pl.BlockSpec((seq, HEAD_DIM), lambda h, b: (b, h)),
        out_shape=jax.ShapeDtypeStruct((bsz * seq, MOBA_W), BF16),
        scratch_shapes=[pltpu.VMEM((blk, blk), F32), pltpu.VMEM((blk, blk), F32),
                        pltpu.VMEM((seq, HEAD_DIM), BF16),
                        pltpu.VMEM((HEAD_DIM + BF16_SUBLANES, seq), BF16)],
        compiler_params=_vmem(48),
    )(rel_bias, bko, bkp, proj, proj, proj)


def _conv_silu(xh, w):
    acc = xh[SUBLANES:, :] * w[GDN_CONV - 1:GDN_CONV, :]
    for s in range(1, GDN_CONV):
        acc = acc + pltpu.roll(xh, s, axis=0)[SUBLANES:, :] * w[GDN_CONV - 1 - s:GDN_CONV - s, :]
    return _silu(acc)


def _l2norm(x):
    return x * lax.rsqrt(jnp.sum(x * x, axis=-1, keepdims=True) + RMS_EPS)


def _softplus(x):
    return jnp.maximum(x, 0.0) + jnp.log1p(jnp.exp(-jnp.abs(x)))


ELIM_BLOCK = 8
MERGE_LEVELS = (GDN_CHUNK // ELIM_BLOCK).bit_length() - 1


def _block_diag_inverse(lpair):
    n = lpair.shape[0]
    nv = n // SUBLANES
    vpb = ELIM_BLOCK // SUBLANES
    rid = lax.broadcasted_iota(jnp.int32, (SUBLANES, 2 * n), 0)
    lid = lax.broadcasted_iota(jnp.int32, (SUBLANES, 2 * n), 1)
    t_rows = [(lid % n == rid + v * SUBLANES).astype(F32) for v in range(nv)]
    l_rows = [lpair[v * SUBLANES:(v + 1) * SUBLANES, :] for v in range(nv)]
    for m in range(n - 1):
        v0, s0 = divmod(m, SUBLANES)
        v_end = (v0 // vpb + 1) * vpb
        row = t_rows[v0][s0:s0 + 1, :]
        col = (lid // n) * n + m
        for v in range(v0 if s0 < SUBLANES - 1 else v0 + 1, v_end):
            t_rows[v] = t_rows[v] - jnp.take_along_axis(l_rows[v], col, axis=1) * row
    return jnp.concatenate(t_rows, axis=0)


def _lane_block_diag(pair):
    first = lax.broadcasted_iota(jnp.int32, pair.shape, 1) < pair.shape[1] // 2
    zero = jnp.zeros_like(pair)
    return jnp.concatenate([jnp.where(first, pair, zero), jnp.where(first, zero, pair)], axis=0)


def _merge_lower_products(lpair, tpair, k):
    n = lpair.shape[0]
    ri = lax.broadcasted_iota(jnp.int32, lpair.shape, 0)
    ci = lax.broadcasted_iota(jnp.int32, lpair.shape, 1) % n
    off = (ri // (2 * k) == ci // (2 * k)) & (ri // k > ci // k)
    lk = jnp.where(off, lpair, 0.0).astype(BF16)
    return jnp.dot(lk, _lane_block_diag(tpair.astype(BF16)), preferred_element_type=F32)


def _merge_apply(tpair, lt):
    return tpair - jnp.dot(tpair.astype(BF16), _lane_block_diag(lt), preferred_element_type=F32)


def _software_pipeline(stages, n):
    ns = len(stages)

    def run(it, lo, hi):
        conts = [stages[s](it - s) for s in reversed(range(lo, hi))]
        for cont in conts:
            if cont is not None:
                cont()

    for it in range(ns - 1):
        run(it, 0, it + 1)

    def body(it, carry):
        run(it, 0, ns)
        return carry

    lax.fori_loop(ns - 1, n, body, 0)
    for it in range(n, n + ns - 1):
        run(it, it - n + 1, ns)


def _lane_pick(x, lane, idx):
    return jnp.sum(jnp.where(lane == idx, x, 0.0), axis=-1, keepdims=True)


def _gdn_kernel(alog_ref, dtb_ref, gp_ref, nw_ref, cwq_ref, cwk_ref, cwv_ref,
                q_ref, k_ref, v_ref, z_ref, sm_ref, rw_ref, y_ref,
                kb_s, qn_s, kdp_s, rhs_s, qd_s, gcb_s, betab_s, gcr_s, gram_s, lm_s, lhs2_s,
                sol_s, mp_s, n_s, r_s, st_s, *merge_s, nchunk):
    tl_s, lt_s = merge_s[:MERGE_LEVELS + 1], merge_s[MERGE_LEVELS + 1:]
    hq = pl.program_id(1)
    c64 = GDN_CHUNK
    hd = HEAD_DIM
    heads = range(2)

    tri_u = (lax.broadcasted_iota(jnp.int32, (c64, c64), 0)
             <= lax.broadcasted_iota(jnp.int32, (c64, c64), 1)).astype(F32)
    gc_rows = []
    for j in heads:
        hv = 2 * hq + j
        a_neg_r = -jnp.exp(jnp.full((nchunk, c64), alog_ref[hv], F32))
        g_row = a_neg_r * _softplus(rw_ref[2 + j] + dtb_ref[hv])
        gc_rows.append(jnp.dot(g_row, tri_u, precision=lax.Precision.HIGHEST,
                               preferred_element_type=F32))
    gcr_s[...] = jnp.concatenate(gc_rows, axis=1)

    rows = lax.broadcasted_iota(jnp.int32, (c64, 2 * c64), 0)
    cols = lax.broadcasted_iota(jnp.int32, (c64, 2 * c64), 1) % c64
    tril = rows >= cols
    strict = rows > cols
    eye2 = (lax.broadcasted_iota(jnp.int32, (2 * c64, 2 * c64), 0)
            == lax.broadcasted_iota(jnp.int32, (2 * c64, 2 * c64), 1)).astype(BF16)
    nt_dims = (((1,), (1,)), ((), ()))

    def rows_of(c):
        start = c * c64
        return pl.ds(start if isinstance(c, int) else pl.multiple_of(start, c64), c64)

    def with_halo(x_ref, c):
        if isinstance(c, int) and c == 0:
            return jnp.concatenate([jnp.zeros((SUBLANES, x_ref.shape[1]), F32), x_ref[:c64, :]],
                                   axis=0)
        start = c * c64 - SUBLANES
        if not isinstance(c, int):
            start = pl.multiple_of(start, SUBLANES)
        return x_ref[pl.ds(start, c64 + SUBLANES), :]

    lane = lax.broadcasted_iota(jnp.int32, (c64, N_SMALL), 1)
    first = lane < c64
    pos = lax.broadcasted_iota(jnp.int32, (c64, N_SMALL), 0)

    def stage_prep(c):
        r = rows_of(c)
        qn = _l2norm(_conv_silu(with_halo(q_ref, c), cwq_ref[...])) * (hd ** -0.5)
        kn = _l2norm(_conv_silu(with_halo(k_ref, c), cwk_ref[...]))
        vc = _conv_silu(with_halo(v_ref, c), cwv_ref[...])
        qn_s[r, :] = qn.astype(BF16)
        kb_s[r, :] = kn.astype(BF16)
        sm = sm_ref[r, :]
        sig_all = jax.nn.sigmoid(sm)
        gc_all = -jnp.exp(gp_ref[0:1, :]) * _softplus(sm + gp_ref[1:2, :])
        sft = 1
        while sft < c64:
            gc_all = gc_all + jnp.where(pos >= sft, pltpu.roll(gc_all, sft, axis=0), 0.0)
            sft *= 2
        rest_all = gc_all[c64 - 1:, :] - gc_all
        betas, gc_cols = [], []
        for j in heads:
            hv = 2 * hq + j
            beta = _lane_pick(sig_all, lane, hv)
            gc_col = _lane_pick(gc_all, lane, GDN_V_HEADS + hv)
            rest_col = _lane_pick(rest_all, lane, GDN_V_HEADS + hv)
            eg = jnp.exp(gc_col)
            rhs_s[j, r, :hd] = (vc[:, j * hd:(j + 1) * hd] * beta).astype(BF16)
            rhs_s[j, r, hd:] = (kn * (beta * eg)).astype(BF16)
            qd_s[j, r, :] = (qn * eg).astype(BF16)
            kdp_s[c, j * c64:(j + 1) * c64, :] = (kn * jnp.exp(rest_col)).astype(BF16)
            betas.append(beta)
            gc_cols.append(gc_col)
        gcb_s[r, :] = jnp.where(first, gc_cols[0], gc_cols[1])
        betab_s[r, :] = jnp.where(first, betas[0], betas[1])

    def stage_gram(c):
        r = rows_of(c)
        kb = kb_s[r, :]
        kq = jnp.concatenate([kb, qn_s[r, :]], axis=0)
        gram = lax.dot_general(kq, jnp.concatenate([kb, kb], axis=0), nt_dims,
                               preferred_element_type=F32)
        kd_t = lax.dot_general(eye2, kdp_s[c], nt_dims, preferred_element_type=F32)

        def finish():
            gram_s[c] = gram
            lhs2_s[c, :2 * c64, :] = kd_t.astype(BF16)

        return finish

    def stage_factor(c):
        r = rows_of(c)
        dec = jnp.exp(jnp.where(tril, gcb_s[r, :] - gcr_s[pl.ds(c, 1), :], NEG_INF))
        lpair = jnp.where(strict, gram_s[c, :c64, :] * dec, 0.0) * betab_s[r, :]
        lm_s[r, :] = lpair
        lhs2_s[c, 2 * c64:, :] = (gram_s[c, c64:, :] * dec).astype(BF16)
        tl_s[0][r, :] = _block_diag_inverse(lpair)

    def stage_merge_products(level):
        def stage(c):
            r = rows_of(c)
            lt = _merge_lower_products(lm_s[r, :], tl_s[level][r, :], ELIM_BLOCK << level)

            def finish():
                lt_s[level][r, :] = lt.astype(BF16)

            return finish
        return stage

    def stage_merge_apply(level):
        def stage(c):
            r = rows_of(c)
            merged = _merge_apply(tl_s[level][r, :], lt_s[level][r, :])

            def finish():
                tl_s[level + 1][r, :] = merged

            return finish
        return stage

    def stage_solve(c):
        r = rows_of(c)
        zero = jnp.zeros((c64, 2 * hd), BF16)
        rhs_bd = jnp.concatenate([jnp.concatenate([rhs_s[0, r, :], zero], axis=1),
                                  jnp.concatenate([zero, rhs_s[1, r, :]], axis=1)], axis=0)
        sol = jnp.dot(tl_s[MERGE_LEVELS][r, :].astype(BF16), rhs_bd,
                      preferred_element_type=F32)

        def finish():
            sol_s[r, :] = sol.astype(BF16)

        return finish

    def stage_fold(c):
        r = rows_of(c)
        sol = sol_s[r, :]
        out = jnp.dot(lhs2_s[c], _lane_block_diag(sol), preferred_element_type=F32)

        def finish():
            for j in heads:
                u_col, w_col = 2 * j * hd, (2 * j + 1) * hd
                n_s[j, c] = out[:2 * c64, u_col:u_col + hd]
                mp_s[j, c, :2 * c64, :] = out[:2 * c64, w_col:w_col + hd].astype(BF16)
                mp_s[j, c, 2 * c64:, :] = (qd_s[j, r, :].astype(F32)
                                           - out[2 * c64:, w_col:w_col + hd]).astype(BF16)
                r_s[j, r, :] = out[2 * c64:, u_col:u_col + hd]

        return finish

    st_s[...] = jnp.zeros_like(st_s)
    nw = nw_ref[...]

    def stage_state(c):
        r = rows_of(c)
        sts = [st_s[j] for j in heads]
        outs = [jnp.dot(mp_s[j, c], sts[j].astype(BF16), preferred_element_type=F32)
                for j in heads]

        def finish():
            for j in heads:
                end = (j + 1) * c64
                gl = gcr_s[pl.ds(c, 1), end - 1:end]
                st_s[j] = sts[j] * jnp.exp(gl) - outs[j][:2 * c64] + n_s[j, c]
                r_s[j, r, :] = r_s[j, r, :] + outs[j][2 * c64:]

        return finish

    def stage_norm(c):
        r = rows_of(c)
        for j in heads:
            o = r_s[j, r, :]
            og = (o * lax.rsqrt(jnp.mean(o * o, axis=-1, keepdims=True) + RMS_EPS)
                  * nw * _silu(z_ref[r, j * hd:(j + 1) * hd]))
            y_ref[r, j * hd:(j + 1) * hd] = og.astype(BF16)

    merges = [stage(lv) for lv in range(MERGE_LEVELS)
              for stage in (stage_merge_products, stage_merge_apply)]
    _software_pipeline([stage_prep, stage_gram, stage_factor] + merges
                       + [stage_solve, stage_fold, stage_state, stage_norm], nchunk)


def _gdn(proj, small, rows, conv_w, a_log, dt_bias, norm_w, bsz, seq):
    nchunk = seq // GDN_CHUNK
    hd = HEAD_DIM
    c64 = GDN_CHUNK
    qo, ko = COL_GDN_Q // hd, COL_GDN_K // hd
    vo, zo = COL_GDN_V // (2 * hd), COL_GDN_Z // (2 * hd)
    cvo = (2 * GDN_QK_W) // (2 * hd)
    smem = pl.BlockSpec(memory_space=pltpu.SMEM)
    pad = (GDN_V_HEADS, N_SMALL - 2 * GDN_V_HEADS)
    gate_params = jnp.stack([jnp.pad(a_log, pad), jnp.pad(dt_bias, pad)])
    return pl.pallas_call(
        functools.partial(_gdn_kernel, nchunk=nchunk),
        name="gdn",
        grid=(bsz, GDN_QK_HEADS),
        in_specs=[smem, smem,
                  pl.BlockSpec((2, N_SMALL), lambda b, h: (0, 0)),
                  pl.BlockSpec((1, hd), lambda b, h: (0, 0)),
                  pl.BlockSpec((GDN_CONV, hd), lambda b, h: (0, h)),
                  pl.BlockSpec((GDN_CONV, hd), lambda b, h: (0, GDN_QK_HEADS + h)),
                  pl.BlockSpec((GDN_CONV, 2 * hd), lambda b, h: (0, cvo + h)),
                  pl.BlockSpec((seq, hd), lambda b, h: (b, qo + h)),
                  pl.BlockSpec((seq, hd), lambda b, h: (b, ko + h)),
                  pl.BlockSpec((seq, 2 * hd), lambda b, h: (b, vo + h)),
                  pl.BlockSpec((seq, 2 * hd), lambda b, h: (b, zo + h)),
                  pl.BlockSpec((seq, N_SMALL), lambda b, h: (b, 0)),
                  pl.BlockSpec((None, None, 4, nchunk, GDN_CHUNK), lambda b, h: (b, h, 0, 0, 0))],
        out_specs=pl.BlockSpec((seq, 2 * hd), lambda b, h: (b, h)),
        out_shape=jax.ShapeDtypeStruct((bsz * seq, GDN_V_W), BF16),
        scratch_shapes=[pltpu.VMEM((seq, hd), BF16),
                        pltpu.VMEM((seq, hd), BF16),
                        pltpu.VMEM((nchunk, 2 * c64, hd), BF16),
                        pltpu.VMEM((2, seq, 2 * hd), BF16),
                        pltpu.VMEM((2, seq, hd), BF16),
                        pltpu.VMEM((seq, 2 * c64), F32),
                        pltpu.VMEM((seq, 2 * c64), F32),
                        pltpu.VMEM((nchunk, 2 * c64), F32),
                        pltpu.VMEM((nchunk, 2 * c64, 2 * c64), F32),
                        pltpu.VMEM((seq, 2 * c64), F32),
                        pltpu.VMEM((nchunk, 3 * c64, hd), BF16),
                        pltpu.VMEM((seq, 4 * hd), BF16),
                        pltpu.VMEM((2, nchunk, 3 * c64, hd), BF16),
                        pltpu.VMEM((2, nchunk, hd, hd), F32),
                        pltpu.VMEM((2, seq, hd), F32),
                        pltpu.VMEM((2, hd, hd), F32)]
                       + [pltpu.VMEM((seq, 2 * c64), F32)] * (MERGE_LEVELS + 1)
                       + [pltpu.VMEM((seq, 2 * c64), BF16)] * MERGE_LEVELS,
        compiler_params=_vmem(58),
    )(a_log, dt_bias, gate_params, norm_w, conv_w, conv_w, conv_w, proj, proj, proj, proj, small, rows)


def _merge_kernel(ya_ref, yb_ref, ga_ref, gb_ref, x_ref, g1_ref, sh2_ref, sc2_ref,
                  lng_ref, lnb_ref, wpm_ref, wpg_ref, wo_ref, x1_ref, h2_ref):
    pa = jnp.dot(ya_ref[...], wpm_ref[...], preferred_element_type=F32)
    pb = jnp.dot(yb_ref[...], wpg_ref[...], preferred_element_type=F32)
    merged = jax.nn.sigmoid(ga_ref[...]) * pa + jax.nn.sigmoid(gb_ref[...]) * pb
    y = jnp.dot(merged.astype(BF16), wo_ref[...], preferred_element_type=F32)
    x1 = _layer_norm(DEEPNORM_ALPHA * x_ref[...] + g1_ref[...] * y, lng_ref[...], lnb_ref[...])
    x1_ref[...] = x1
    h2_ref[...] = (x1 * (1.0 + sc2_ref[...]) + sh2_ref[...]).astype(BF16)


def _merge(ya, yb, proj, x2, mod3, ln_g, ln_b, wpm, wpg, wo, seq):
    t = x2.shape[0]
    tm = 256
    per_b = seq // tm
    d = D_MODEL

    def modspec(k):
        return pl.BlockSpec((None, 1, d), lambda i: (i // per_b, 0, k))

    def const(shape):
        return pl.BlockSpec(shape, lambda i: (0, 0), pipeline_mode=pl.Buffered(1))

    return pl.pallas_call(
        _merge_kernel,
        name="merge",
        grid=(t // tm,),
        in_specs=[pl.BlockSpec((tm, MOBA_W), lambda i: (i, 0)),
                  pl.BlockSpec((tm, GDN_V_W), lambda i: (i, 0)),
                  pl.BlockSpec((tm, d), lambda i: (i, COL_GATE_A // d)),
                  pl.BlockSpec((tm, d), lambda i: (i, COL_GATE_B // d)),
                  pl.BlockSpec((tm, d), lambda i: (i, 0)),
                  modspec(2), modspec(3), modspec(4),
                  const((1, d)), const((1, d)),
                  const((MOBA_W, d)), const((GDN_V_W, d)), const((d, d))],
        out_specs=[pl.BlockSpec((tm, d), lambda i: (i, 0)),
                   pl.BlockSpec((tm, d), lambda i: (i, 0))],
        out_shape=[jax.ShapeDtypeStruct((t, d), F32), jax.ShapeDtypeStruct((t, d), BF16)],
        compiler_params=_vmem(56),
    )(ya, yb, proj, proj, x2, mod3, mod3, mod3, ln_g, ln_b, wpm, wpg, wo)


def _ffn_kernel(h_ref, x1_ref, g2_ref, lng_ref, lnb_ref, wg_ref, wu_ref, wo_ref, o_ref, acc_ref):
    f = pl.program_id(1)

    @pl.when(f == 0)
    def _():
        acc_ref[...] = jnp.zeros_like(acc_ref)

    h = h_ref[...]
    gate = jnp.dot(h, wg_ref[...], preferred_element_type=F32)
    up = jnp.dot(h, wu_ref[...], preferred_element_type=F32)
    act = (_silu(gate) * up).astype(BF16)
    acc_ref[...] += jnp.dot(act, wo_ref[...], preferred_element_type=F32)

    @pl.when(f == pl.num_programs(1) - 1)
    def _():
        r = DEEPNORM_ALPHA * x1_ref[...] + g2_ref[...] * acc_ref[...]
        o_ref[...] = _layer_norm(r, lng_ref[...], lnb_ref[...])


def _ffn(h2, x1, mod3, ln_g, ln_b, w_in, w_out, seq):
    t = h2.shape[0]
    tm, tf = 512, 512
    per_b = seq // tm
    d = D_MODEL
    nf = D_FF // tf
    return pl.pallas_call(
        _ffn_kernel,
        name="ffn",
        grid=(t // tm, nf),
        in_specs=[pl.BlockSpec((tm, d), lambda i, f: (i, 0)),
                  pl.BlockSpec((tm, d), lambda i, f: (i, 0)),
                  pl.BlockSpec((None, 1, d), lambda i, f: (i // per_b, 0, 5)),
                  pl.BlockSpec((1, d), lambda i, f: (0, 0)),
                  pl.BlockSpec((1, d), lambda i, f: (0, 0)),
                  pl.BlockSpec((d, tf), lambda i, f: (0, f)),
                  pl.BlockSpec((d, tf), lambda i, f: (0, nf + f)),
                  pl.BlockSpec((tf, d), lambda i, f: (f, 0))],
        out_specs=pl.BlockSpec((tm, d), lambda i, f: (i, 0)),
        out_shape=jax.ShapeDtypeStruct((t, d), F32),
        scratch_shapes=[pltpu.VMEM((tm, d), F32)],
        compiler_params=_vmem(48),
    )(h2, x1, mod3, ln_g, ln_b, w_in, w_in, w_out)


def _rel_bucket(dist):
    max_exact = REL_BUCKETS // 2
    n = jnp.maximum(dist, 0)
    nf = jnp.maximum(n, 1).astype(F32)
    large = max_exact + (jnp.log(nf / max_exact) / math.log(REL_MAX_DIST / max_exact)
                         * (REL_BUCKETS - max_exact)).astype(jnp.int32)
    large = jnp.minimum(large, REL_BUCKETS - 1)
    return jnp.where(n < max_exact, n, large)


def _layer(x, c, w_ada, b_ada, w_in, conv_w, a_log, dt_bias, gdn_norm_w, rel_bias,
           w_proj_moba, w_proj_gdn, w_out, ln1_g, ln1_b, w_ffn_in, w_ffn_out, ln2_g, ln2_b):
    bsz, seq, d = x.shape
    t = bsz * seq
    x2 = x.reshape(t, d)

    mod = _ada_mod(c, w_ada, b_ada)
    mod3 = mod.reshape(bsz, 1, 6 * d)

    w_all = w_in.astype(BF16)
    n_gates = 2 * GDN_V_HEADS
    w_gate = w_all[:, N_HEAD + n_gates:]
    w_small = jnp.pad(w_all[:, N_HEAD:N_HEAD + n_gates], ((0, 0), (0, N_SMALL - n_gates)))

    proj, small = _in_proj(x2, mod3, w_all, w_gate, w_small, seq)

    ii = jnp.arange(MOBA_BLOCK, dtype=jnp.int32)
    dist = ii[None, :] - ii[:, None]
    bko = _rel_bucket(dist)
    bkp = _rel_bucket(dist + MOBA_BLOCK)
    ya = _moba(proj, rel_bias, bko, bkp, bsz, seq)

    nchunk = seq // GDN_CHUNK
    sm_t = small[:, :2 * GDN_V_HEADS].reshape(bsz, seq, 2, GDN_QK_HEADS, 2)
    rows = sm_t.transpose(0, 3, 2, 4, 1).reshape(bsz, GDN_QK_HEADS, 4, nchunk, GDN_CHUNK)
    yb = _gdn(proj, small, rows, conv_w, a_log, dt_bias, gdn_norm_w.reshape(1, HEAD_DIM),
              bsz, seq)

    x1, h2 = _merge(ya, yb, proj, x2, mod3, ln1_g.reshape(1, d), ln1_b.reshape(1, d),
                    w_proj_moba.astype(BF16), w_proj_gdn.astype(BF16), w_out.astype(BF16), seq)
    out = _ffn(h2, x1, mod3, ln2_g.reshape(1, d), ln2_b.reshape(1, d),
               w_ffn_in.astype(BF16), w_ffn_out.astype(BF16), seq)
    return out.reshape(bsz, seq, d)


def kernel(x, c, w_ada, b_ada, w_in, conv_w, a_log, dt_bias, gdn_norm_w, rel_bias, w_proj_moba,
           w_proj_gdn, w_out, ln1_g, ln1_b, w_ffn_in, w_ffn_out, ln2_g, ln2_b):
    depth = w_ada.shape[0]
    for l in range(depth):
        x = _layer(x, c, w_ada[l], b_ada[l], w_in[l], conv_w[l], a_log[l], dt_bias[l],
                   gdn_norm_w[l], rel_bias, w_proj_moba[l], w_proj_gdn[l], w_out[l],
                   ln1_g[l], ln1_b[l], w_ffn_in[l], w_ffn_out[l], ln2_g[l], ln2_b[l])
    return x
```

```python
import functools
import math

import jax
import jax.numpy as jnp
from jax import lax
from jax.experimental import pallas as pl
from jax.experimental.pallas import tpu as pltpu

F32 = jnp.float32
BF16 = jnp.bfloat16

D_MODEL = 2048
MOBA_HEADS = 8
HEAD_DIM = 128
MOBA_W = MOBA_HEADS * HEAD_DIM
MOBA_BLOCK = 256
MOBA_TOPK = 3
REL_BUCKETS = 32
REL_MAX_DIST = 128
GDN_QK_HEADS = 8
GDN_V_HEADS = 16
GDN_QK_W = GDN_QK_HEADS * HEAD_DIM
GDN_V_W = GDN_V_HEADS * HEAD_DIM
GDN_CONV = 4
GDN_CHUNK = 64
D_FF = 5632
DEEPNORM_ALPHA = 2.0 ** 0.25
LN_EPS = 1e-5
RMS_EPS = 1e-6
NEG_INF = -1e30

COL_GATE_A = 0
COL_GATE_B = 2048
COL_MOBA_Q = 4096
COL_MOBA_K = 5120
COL_MOBA_V = 6144
COL_GDN_Q = 7168
COL_GDN_K = 8192
COL_GDN_V = 9216
COL_GDN_Z = 11264
N_MAIN = 13312
N_GATE = 4096
N_HEAD = N_MAIN - N_GATE
N_SMALL = 128

V7X_VMEM_BYTES = 64 * 1024 * 1024
LANES = 128
SUBLANES = 8
BF16_SUBLANES = 16


def _vmem(mb):
    return pltpu.CompilerParams(vmem_limit_bytes=mb * 1024 * 1024)


def _silu(x):
    return x * jax.nn.sigmoid(x)


def _layer_norm(r, gain, bias):
    mu = jnp.mean(r, axis=-1, keepdims=True)
    d = r - mu
    var = jnp.mean(d * d, axis=-1, keepdims=True)
    return d * lax.rsqrt(var + LN_EPS) * gain + bias


def _ada_kernel(c_ref, w_ref, b_ref, o_ref):
    sc = _silu(c_ref[...])
    o_ref[...] = jnp.dot(sc, w_ref[...], precision=lax.Precision.HIGHEST,
                         preferred_element_type=F32) + b_ref[...]


def _ada_mod(c, w_ada, b_ada):
    bsz = c.shape[0]
    n = w_ada.shape[1]
    tn = 1024
    return pl.pallas_call(
        _ada_kernel,
        name="ada_mod",
        grid=(n // tn,),
        in_specs=[pl.BlockSpec((bsz, D_MODEL), lambda j: (0, 0)),
                  pl.BlockSpec((D_MODEL, tn), lambda j: (0, j)),
                  pl.BlockSpec((1, tn), lambda j: (0, j))],
        out_specs=pl.BlockSpec((bsz, tn), lambda j: (0, j)),
        out_shape=jax.ShapeDtypeStruct((bsz, n), F32),
        compiler_params=_vmem(40),
    )(c, w_ada, b_ada.reshape(1, n))


def _wprep_kernel(wa_ref, wb_ref, ws_ref, o_ref, os_ref, *, n_head, shift):
    j = pl.program_id(0)
    tn = o_ref.shape[1]

    @pl.when(j == 0)
    def _():
        lane = lax.broadcasted_iota(jnp.int32, ws_ref.shape, 1)
        os_ref[...] = jnp.where(lane < shift, ws_ref[...], jnp.zeros_like(ws_ref[...]))

    @pl.when(j < n_head)
    def _():
        o_ref[...] = wa_ref[...]

    @pl.when(j >= n_head)
    def _():
        both = jnp.concatenate([wa_ref[...], wb_ref[...]], axis=1)
        o_ref[...] = both[:, shift:shift + tn]


def _prep_w_in(w_in):
    tn = 1024
    n_gates = 2 * GDN_V_HEADS
    n_head, n_gate, n_tiles = N_HEAD // tn, N_GATE // tn, N_MAIN // tn
    per_tile = tn // N_SMALL
    return pl.pallas_call(
        functools.partial(_wprep_kernel, n_head=n_head, shift=n_gates),
        name="w_in_prep",
        grid=(n_tiles,),
        in_specs=[pl.BlockSpec((D_MODEL, tn), lambda j: (0, j)),
                  pl.BlockSpec((D_MODEL, N_SMALL), lambda j: (0, (j + 1) * per_tile)),
                  pl.BlockSpec((D_MODEL, N_SMALL), lambda j: (0, N_HEAD // N_SMALL))],
        out_specs=[pl.BlockSpec((D_MODEL, tn), lambda j: (0, (j + n_gate) % n_tiles)),
                   pl.BlockSpec((D_MODEL, N_SMALL), lambda j: (0, 0))],
        out_shape=[jax.ShapeDtypeStruct((D_MODEL, N_MAIN), BF16),
                   jax.ShapeDtypeStruct((D_MODEL, N_SMALL), BF16)],
        compiler_params=_vmem(40),
    )(w_in, w_in, w_in)


def _inproj_kernel(x_ref, sh_ref, sc_ref, w_ref, ws_ref, o_ref, os_ref, h_ref):
    @pl.when(pl.program_id(1) == 0)
    def _():
        h = x_ref[...] * (1.0 + sc_ref[...]) + sh_ref[...]
        h_ref[...] = h.astype(BF16)
        os_ref[...] = jnp.dot(h_ref[...], ws_ref[...], preferred_element_type=F32)

    o_ref[...] = jnp.dot(h_ref[...], w_ref[...], preferred_element_type=F32)


def _in_proj(x2, mod3, w_main, w_small, seq):
    t = x2.shape[0]
    tm, tn = 1024, 1024
    per_b = seq // tm
    return pl.pallas_call(
        _inproj_kernel,
        name="in_proj",
        grid=(t // tm, N_MAIN // tn),
        in_specs=[pl.BlockSpec((tm, D_MODEL), lambda i, j: (i, 0)),
                  pl.BlockSpec((None, 1, D_MODEL), lambda i, j: (i // per_b, 0, 0)),
                  pl.BlockSpec((None, 1, D_MODEL), lambda i, j: (i // per_b, 0, 1)),
                  pl.BlockSpec((D_MODEL, tn), lambda i, j: (0, j)),
                  pl.BlockSpec((D_MODEL, N_SMALL), lambda i, j: (0, 0))],
        out_specs=[pl.BlockSpec((tm, tn), lambda i, j: (i, j)),
                   pl.BlockSpec((tm, N_SMALL), lambda i, j: (i, 0))],
        out_shape=[jax.ShapeDtypeStruct((t, N_MAIN), F32),
                   jax.ShapeDtypeStruct((t, N_SMALL), F32)],
        scratch_shapes=[pltpu.VMEM((tm, D_MODEL), BF16)],
        compiler_params=_vmem(48),
    )(x2, mod3, mod3, w_main, w_small)


def _moba_kernel(rel_ref, bko_ref, bkp_ref, q_ref, k_ref, v_ref, o_ref,
                 bias_own, bias_prev, kb_ref, vt_ref, *, nb):
    h = pl.program_id(0)
    blk = MOBA_BLOCK
    inv_scale = HEAD_DIM ** 0.5
    scale_log2e = HEAD_DIM ** -0.5 * math.log2(math.e)

    @pl.when(pl.program_id(1) == 0)
    def _():
        bo = bko_ref[...]
        bp = bkp_ref[...]
        acc_o = jnp.zeros((blk, blk), F32)
        acc_p = jnp.zeros((blk, blk), F32)
        for kk in range(REL_BUCKETS):
            val = rel_ref[kk, h] * inv_scale
            acc_o = jnp.where(bo == kk, val, acc_o)
            acc_p = jnp.where(bp == kk, val, acc_p)
        bias_own[...] = acc_o
        bias_prev[...] = acc_p

    bias_far = rel_ref[REL_BUCKETS - 1, h] * inv_scale
    kf = k_ref[...]
    kmean = jnp.mean(kf.reshape(nb, blk, HEAD_DIM), axis=1)
    kb_ref[...] = kf.astype(BF16)
    nt_dims = (((1,), (1,)), ((), ()))
    eye = (lax.broadcasted_iota(jnp.int32, (HEAD_DIM, HEAD_DIM), 0)
           == lax.broadcasted_iota(jnp.int32, (HEAD_DIM, HEAD_DIM), 1)).astype(BF16)
    vt_ref[:HEAD_DIM, :] = lax.dot_general(eye, v_ref[...].astype(BF16), nt_dims,
                                           preferred_element_type=F32).astype(BF16)
    pad_rows = vt_ref.shape[0] - HEAD_DIM
    vt_ref[HEAD_DIM:, :] = (lax.broadcasted_iota(jnp.int32, (pad_rows, vt_ref.shape[1]), 0)
                            == 0).astype(BF16)
    causal = (lax.broadcasted_iota(jnp.int32, (blk, blk), 0)
              <= lax.broadcasted_iota(jnp.int32, (blk, blk), 1))

    for i in range(nb):
        qi = q_ref[i * blk:(i + 1) * blk, :]
        qb = qi.astype(BF16)
        sel = None
        if i > MOBA_TOPK:
            route = lax.dot_general(kmean, qi, nt_dims, precision=lax.Precision.HIGHEST,
                                    preferred_element_type=F32)
            rc = [route[n:n + 1, :] for n in range(i)]
            sel = []
            for n in range(i):
                rank = jnp.zeros((1, blk), jnp.int32)
                for m in range(i):
                    if m == n:
                        continue
                    beats = (rc[m] >= rc[n]) if m < n else (rc[m] > rc[n])
                    rank = rank + beats.astype(jnp.int32)
                sel.append(rank < MOBA_TOPK)
        t_list = []
        for n in range(i + 1):
            t = lax.dot_general(kb_ref[n * blk:(n + 1) * blk, :], qb, nt_dims,
                                preferred_element_type=F32)
            if n == i:
                t = jnp.where(causal, t + bias_own[...], NEG_INF)
            else:
                if n == i - 1:
                    t = t + bias_prev[...]
                if sel is not None:
                    t = jnp.where(sel[n], t, NEG_INF)
            t_list.append(t)
        n_far = max(i - 1, 0)
        m_run = jnp.max(t_list[n_far], axis=0, keepdims=True)
        for t in t_list[n_far + 1:]:
            m_run = jnp.maximum(m_run, jnp.max(t, axis=0, keepdims=True))
        if n_far:
            m_far = jnp.max(t_list[0], axis=0, keepdims=True)
            for t in t_list[1:n_far]:
                m_far = jnp.maximum(m_far, jnp.max(t, axis=0, keepdims=True))
            m_run = jnp.maximum(m_run, m_far + bias_far)
        acc = jnp.zeros((vt_ref.shape[0], blk), F32)
        for n, t in enumerate(t_list):
            offset = m_run - bias_far if n < n_far else m_run
            p = jnp.exp2((t - offset) * scale_log2e)
            acc = acc + jnp.dot(vt_ref[:, n * blk:(n + 1) * blk], p.astype(BF16),
                                preferred_element_type=F32)
        out = acc[:HEAD_DIM] / acc[HEAD_DIM:HEAD_DIM + 1]
        o_ref[i * blk:(i + 1) * blk, :] = out.T.astype(BF16)


def _moba(proj, rel_bias, bko, bkp, bsz, seq):
    nb = seq // MOBA_BLOCK
    qo, ko, vo = COL_MOBA_Q // HEAD_DIM, COL_MOBA_K // HEAD_DIM, COL_MOBA_V // HEAD_DIM
    blk = MOBA_BLOCK
    return pl.pallas_call(
        functools.partial(_moba_kernel, nb=nb),
        name="moba",
        grid=(MOBA_HEADS, bsz),
        in_specs=[pl.BlockSpec(memory_space=pltpu.SMEM),
                  pl.BlockSpec((blk, blk), lambda h, b: (0, 0)),
                  pl.BlockSpec((blk, blk), lambda h, b: (0, 0)),
                  pl.BlockSpec((seq, HEAD_DIM), lambda h, b: (b, qo + h)),
                  pl.BlockSpec((seq, HEAD_DIM), lambda h, b: (b, ko + h)),
                  pl.BlockSpec((seq, HEAD_DIM), lambda h, b: (b, vo + h))],
        out_specs=pl.BlockSpec((seq, HEAD_DIM), lambda h, b: (b, h)),
        out_shape=jax.ShapeDtypeStruct((bsz * seq, MOBA_W), BF16),
        scratch_shapes=[pltpu.VMEM((blk, blk), F32), pltpu.VMEM((blk, blk), F32),
                        pltpu.VMEM((seq, HEAD_DIM), BF16),
                        pltpu.VMEM((HEAD_DIM + BF16_SUBLANES, seq), BF16)],
        compiler_params=_vmem(48),
    )(rel_bias, bko, bkp, proj, proj, proj)


def _conv_silu(xh, w):
    acc = xh[SUBLANES:, :] * w[GDN_CONV - 1:GDN_CONV, :]
    for s in range(1, GDN_CONV):
        acc = acc + pltpu.roll(xh, s, axis=0)[SUBLANES:, :] * w[GDN_CONV - 1 - s:GDN_CONV - s, :]
    return _silu(acc)


def _l2norm(x):
    return x * lax.rsqrt(jnp.sum(x * x, axis=-1, keepdims=True) + RMS_EPS)


def _softplus(x):
    return jnp.maximum(x, 0.0) + jnp.log1p(jnp.exp(-jnp.abs(x)))


ELIM_BLOCK = 8
MERGE_LEVELS = (GDN_CHUNK // ELIM_BLOCK).bit_length() - 1


def _block_diag_inverse(lpair):
    n = lpair.shape[0]
    nv = n // SUBLANES
    vpb = ELIM_BLOCK // SUBLANES
    rid = lax.broadcasted_iota(jnp.int32, (SUBLANES, 2 * n), 0)
    lid = lax.broadcasted_iota(jnp.int32, (SUBLANES, 2 * n), 1)
    t_rows = [(lid % n == rid + v * SUBLANES).astype(F32) for v in range(nv)]
    l_rows = [lpair[v * SUBLANES:(v + 1) * SUBLANES, :] for v in range(nv)]
    for m in range(n - 1):
        v0, s0 = divmod(m, SUBLANES)
        v_end = (v0 // vpb + 1) * vpb
        row = t_rows[v0][s0:s0 + 1, :]
        col = (lid // n) * n + m
        for v in range(v0 if s0 < SUBLANES - 1 else v0 + 1, v_end):
            t_rows[v] = t_rows[v] - jnp.take_along_axis(l_rows[v], col, axis=1) * row
    return jnp.concatenate(t_rows, axis=0)


def _lane_block_diag(pair):
    first = lax.broadcasted_iota(jnp.int32, pair.shape, 1) < pair.shape[1] // 2
    zero = jnp.zeros_like(pair)
    return jnp.concatenate([jnp.where(first, pair, zero), jnp.where(first, zero, pair)], axis=0)


def _merge_lower_products(lpair, tpair, k):
    n = lpair.shape[0]
    ri = lax.broadcasted_iota(jnp.int32, lpair.shape, 0)
    ci = lax.broadcasted_iota(jnp.int32, lpair.shape, 1) % n
    off = (ri // (2 * k) == ci // (2 * k)) & (ri // k > ci // k)
    lk = jnp.where(off, lpair, 0.0).astype(BF16)
    return jnp.dot(lk, _lane_block_diag(tpair.astype(BF16)), preferred_element_type=F32)


def _merge_apply(tpair, lt):
    return tpair - jnp.dot(tpair.astype(BF16), _lane_block_diag(lt), preferred_element_type=F32)


def _software_pipeline(stages, n):
    ns = len(stages)

    def run(it, lo, hi):
        conts = [stages[s](it - s) for s in reversed(range(lo, hi))]
        for cont in conts:
            if cont is not None:
                cont()

    for it in range(ns - 1):
        run(it, 0, it + 1)

    def body(it, carry):
        run(it, 0, ns)
        return carry

    lax.fori_loop(ns - 1, n, body, 0)
    for it in range(n, n + ns - 1):
        run(it, it - n + 1, ns)


def _lane_pick(x, lane, idx):
    return jnp.sum(jnp.where(lane == idx, x, 0.0), axis=-1, keepdims=True)


def _gdn_kernel(alog_ref, dtb_ref, gp_ref, nw_ref, cwq_ref, cwk_ref, cwv_ref,
                q_ref, k_ref, v_ref, z_ref, sm_ref, rw_ref, y_ref,
                kb_s, qn_s, kdp_s, rhs_s, qd_s, gcb_s, betab_s, gcr_s, gram_s, lm_s, lhs2_s,
                sol_s, mp_s, n_s, r_s, st_s, *merge_s, nchunk):
    tl_s, lt_s = merge_s[:MERGE_LEVELS + 1], merge_s[MERGE_LEVELS + 1:]
    hq = pl.program_id(1)
    c64 = GDN_CHUNK
    hd = HEAD_DIM
    heads = range(2)

    tri_u = (lax.broadcasted_iota(jnp.int32, (c64, c64), 0)
             <= lax.broadcasted_iota(jnp.int32, (c64, c64), 1)).astype(F32)
    gc_rows = []
    for j in heads:
        hv = 2 * hq + j
        a_neg_r = -jnp.exp(jnp.full((nchunk, c64), alog_ref[hv], F32))
        g_row = a_neg_r * _softplus(rw_ref[2 + j] + dtb_ref[hv])
        gc_rows.append(jnp.dot(g_row, tri_u, precision=lax.Precision.HIGHEST,
                               preferred_element_type=F32))
    gcr_s[...] = jnp.concatenate(gc_rows, axis=1)

    rows = lax.broadcasted_iota(jnp.int32, (c64, 2 * c64), 0)
    cols = lax.broadcasted_iota(jnp.int32, (c64, 2 * c64), 1) % c64
    tril = rows >= cols
    strict = rows > cols
    eye2 = (lax.broadcasted_iota(jnp.int32, (2 * c64, 2 * c64), 0)
            == lax.broadcasted_iota(jnp.int32, (2 * c64, 2 * c64), 1)).astype(BF16)
    nt_dims = (((1,), (1,)), ((), ()))

    def rows_of(c):
        start = c * c64
        return pl.ds(start if isinstance(c, int) else pl.multiple_of(start, c64), c64)

    def with_halo(x_ref, c):
        if isinstance(c, int) and c == 0:
            return jnp.concatenate([jnp.zeros((SUBLANES, x_ref.shape[1]), F32), x_ref[:c64, :]],
                                   axis=0)
        start = c * c64 - SUBLANES
        if not isinstance(c, int):
            start = pl.multiple_of(start, SUBLANES)
        return x_ref[pl.ds(start, c64 + SUBLANES), :]

    lane = lax.broadcasted_iota(jnp.int32, (c64, N_SMALL), 1)
    first = lane < c64
    pos = lax.broadcasted_iota(jnp.int32, (c64, N_SMALL), 0)

    def stage_prep(c):
        r = rows_of(c)
        qn = _l2norm(_conv_silu(with_halo(q_ref, c), cwq_ref[...])) * (hd ** -0.5)
        kn = _l2norm(_conv_silu(with_halo(k_ref, c), cwk_ref[...]))
        vc = _conv_silu(with_halo(v_ref, c), cwv_ref[...])
        qn_s[r, :] = qn.astype(BF16)
        kb_s[r, :] = kn.astype(BF16)
        sm = sm_ref[r, :]
        sig_all = jax.nn.sigmoid(sm)
        gc_all = -jnp.exp(gp_ref[0:1, :]) * _softplus(sm + gp_ref[1:2, :])
        sft = 1
        while sft < c64:
            gc_all = gc_all + jnp.where(pos >= sft, pltpu.roll(gc_all, sft, axis=0), 0.0)
            sft *= 2
        rest_all = gc_all[c64 - 1:, :] - gc_all
        betas, gc_cols = [], []
        for j in heads:
            hv = 2 * hq + j
            beta = _lane_pick(sig_all, lane, hv)
            gc_col = _lane_pick(gc_all, lane, GDN_V_HEADS + hv)
            rest_col = _lane_pick(rest_all, lane, GDN_V_HEADS + hv)
            eg = jnp.exp(gc_col)
            rhs_s[j, r, :hd] = (vc[:, j * hd:(j + 1) * hd] * beta).astype(BF16)
            rhs_s[j, r, hd:] = (kn * (beta * eg)).astype(BF16)
            qd_s[j, r, :] = (qn * eg).astype(BF16)
            kdp_s[c, j * c64:(j + 1) * c64, :] = (kn * jnp.exp(rest_col)).astype(BF16)
            betas.append(beta)
            gc_cols.append(gc_col)
        gcb_s[r, :] = jnp.where(first, gc_cols[0], gc_cols[1])
        betab_s[r, :] = jnp.where(first, betas[0], betas[1])

    def stage_gram(c):
        r = rows_of(c)
        kb = kb_s[r, :]
        kq = jnp.concatenate([kb, qn_s[r, :]], axis=0)
        gram = lax.dot_general(kq, jnp.concatenate([kb, kb], axis=0), nt_dims,
                               preferred_element_type=F32)
        kd_t = lax.dot_general(eye2, kdp_s[c], nt_dims, preferred_element_type=F32)

        def finish():
            gram_s[c] = gram
            lhs2_s[c, :2 * c64, :] = kd_t.astype(BF16)

        return finish

    def stage_factor(c):
        r = rows_of(c)
        dec = jnp.exp(jnp.where(tril, gcb_s[r, :] - gcr_s[pl.ds(c, 1), :], NEG_INF))
        lpair = jnp.where(strict, gram_s[c, :c64, :] * dec, 0.0) * betab_s[r, :]
        lm_s[r, :] = lpair
        lhs2_s[c, 2 * c64:, :] = (gram_s[c, c64:, :] * dec).astype(BF16)
        tl_s[0][r, :] = _block_diag_inverse(lpair)

    def stage_merge_products(level):
        def stage(c):
            r = rows_of(c)
            lt = _merge_lower_products(lm_s[r, :], tl_s[level][r, :], ELIM_BLOCK << level)

            def finish():
                lt_s[level][r, :] = lt.astype(BF16)

            return finish
        return stage

    def stage_merge_apply(level):
        def stage(c):
            r = rows_of(c)
            merged = _merge_apply(tl_s[level][r, :], lt_s[level][r, :])

            def finish():
                tl_s[level + 1][r, :] = merged

            return finish
        return stage

    def stage_solve(c):
        r = rows_of(c)
        zero = jnp.zeros((c64, 2 * hd), BF16)
        rhs_bd = jnp.concatenate([jnp.concatenate([rhs_s[0, r, :], zero], axis=1),
                                  jnp.concatenate([zero, rhs_s[1, r, :]], axis=1)], axis=0)
        sol = jnp.dot(tl_s[MERGE_LEVELS][r, :].astype(BF16), rhs_bd,
                      preferred_element_type=F32)

        def finish():
            sol_s[r, :] = sol.astype(BF16)

        return finish

    def stage_fold(c):
        r = rows_of(c)
        sol = sol_s[r, :]
        out = jnp.dot(lhs2_s[c], _lane_block_diag(sol), preferred_element_type=F32)

        def finish():
            for j in heads:
                u_col, w_col = 2 * j * hd, (2 * j + 1) * hd
                n_s[j, c] = out[:2 * c64, u_col:u_col + hd]
                mp_s[j, c, :2 * c64, :] = out[:2 * c64, w_col:w_col + hd].astype(BF16)
                mp_s[j, c, 2 * c64:, :] = (qd_s[j, r, :].astype(F32)
                                           - out[2 * c64:, w_col:w_col + hd]).astype(BF16)
                r_s[j, r, :] = out[2 * c64:, u_col:u_col + hd]

        return finish

    st_s[...] = jnp.zeros_like(st_s)
    nw = nw_ref[...]

    def stage_state(c):
        r = rows_of(c)
        sts = [st_s[j] for j in heads]
        outs = [jnp.dot(mp_s[j, c], sts[j].astype(BF16), preferred_element_type=F32)
                for j in heads]

        def finish():
            for j in heads:
                end = (j + 1) * c64
                gl = gcr_s[pl.ds(c, 1), end - 1:end]
                st_s[j] = sts[j] * jnp.exp(gl) - outs[j][:2 * c64] + n_s[j, c]
                r_s[j, r, :] = r_s[j, r, :] + outs[j][2 * c64:]

        return finish

    def stage_norm(c):
        r = rows_of(c)
        for j in heads:
            o = r_s[j, r, :]
            og = (o * lax.rsqrt(jnp.mean(o * o, axis=-1, keepdims=True) + RMS_EPS)
                  * nw * _silu(z_ref[r, j * hd:(j + 1) * hd]))
            y_ref[r, j * hd:(j + 1) * hd] = og.astype(BF16)

    merges = [stage(lv) for lv in range(MERGE_LEVELS)
              for stage in (stage_merge_products, stage_merge_apply)]
    _software_pipeline([stage_prep, stage_gram, stage_factor] + merges
                       + [stage_solve, stage_fold, stage_state, stage_norm], nchunk)


def _gdn(proj, small, rows, conv_w, a_log, dt_bias, norm_w, bsz, seq):
    nchunk = seq // GDN_CHUNK
    hd = HEAD_DIM
    c64 = GDN_CHUNK
    qo, ko = COL_GDN_Q // hd, COL_GDN_K // hd
    vo, zo = COL_GDN_V // (2 * hd), COL_GDN_Z // (2 * hd)
    cvo = (2 * GDN_QK_W) // (2 * hd)
    smem = pl.BlockSpec(memory_space=pltpu.SMEM)
    pad = (GDN_V_HEADS, N_SMALL - 2 * GDN_V_HEADS)
    gate_params = jnp.stack([jnp.pad(a_log, pad), jnp.pad(dt_bias, pad)])
    return pl.pallas_call(
        functools.partial(_gdn_kernel, nchunk=nchunk),
        name="gdn",
        grid=(bsz, GDN_QK_HEADS),
        in_specs=[smem, smem,
                  pl.BlockSpec((2, N_SMALL), lambda b, h: (0, 0)),
                  pl.BlockSpec((1, hd), lambda b, h: (0, 0)),
                  pl.BlockSpec((GDN_CONV, hd), lambda b, h: (0, h)),
                  pl.BlockSpec((GDN_CONV, hd), lambda b, h: (0, GDN_QK_HEADS + h)),
                  pl.BlockSpec((GDN_CONV, 2 * hd), lambda b, h: (0, cvo + h)),
                  pl.BlockSpec((seq, hd), lambda b, h: (b, qo + h)),
                  pl.BlockSpec((seq, hd), lambda b, h: (b, ko + h)),
                  pl.BlockSpec((seq, 2 * hd), lambda b, h: (b, vo + h)),
                  pl.BlockSpec((seq, 2 * hd), lambda b, h: (b, zo + h)),
                  pl.BlockSpec((seq, N_SMALL), lambda b, h: (b, 0)),
                  pl.BlockSpec((None, None, 4, nchunk, GDN_CHUNK), lambda b, h: (b, h, 0, 0, 0))],
        out_specs=pl.BlockSpec((seq, 2 * hd), lambda b, h: (b, h)),
        out_shape=jax.ShapeDtypeStruct((bsz * seq, GDN_V_W), BF16),
        scratch_shapes=[pltpu.VMEM((seq, hd), BF16),
                        pltpu.VMEM((seq, hd), BF16),
                        pltpu.VMEM((nchunk, 2 * c64, hd), BF16),
                        pltpu.VMEM((2, seq, 2 * hd), BF16),
                        pltpu.VMEM((2, seq, hd), BF16),
                        pltpu.VMEM((seq, 2 * c64), F32),
                        pltpu.VMEM((seq, 2 * c64), F32),
                        pltpu.VMEM((nchunk, 2 * c64), F32),
                        pltpu.VMEM((nchunk, 2 * c64, 2 * c64), F32),
                        pltpu.VMEM((seq, 2 * c64), F32),
                        pltpu.VMEM((nchunk, 3 * c64, hd), BF16),
                        pltpu.VMEM((seq, 4 * hd), BF16),
                        pltpu.VMEM((2, nchunk, 3 * c64, hd), BF16),
                        pltpu.VMEM((2, nchunk, hd, hd), F32),
                        pltpu.VMEM((2, seq, hd), F32),
                        pltpu.VMEM((2, hd, hd), F32)]
                       + [pltpu.VMEM((seq, 2 * c64), F32)] * (MERGE_LEVELS + 1)
                       + [pltpu.VMEM((seq, 2 * c64), BF16)] * MERGE_LEVELS,
        compiler_params=_vmem(58),
    )(a_log, dt_bias, gate_params, norm_w, conv_w, conv_w, conv_w, proj, proj, proj, proj, small, rows)


def _merge_kernel(ya_ref, yb_ref, ga_ref, gb_ref, x_ref, g1_ref, sh2_ref, sc2_ref,
                  lng_ref, lnb_ref, wpm_ref, wpg_ref, wo_ref, x1_ref, h2_ref):
    pa = jnp.dot(ya_ref[...], wpm_ref[...], preferred_element_type=F32)
    pb = jnp.dot(yb_ref[...], wpg_ref[...], preferred_element_type=F32)
    merged = jax.nn.sigmoid(ga_ref[...]) * pa + jax.nn.sigmoid(gb_ref[...]) * pb
    y = jnp.dot(merged.astype(BF16), wo_ref[...], preferred_element_type=F32)
    x1 = _layer_norm(DEEPNORM_ALPHA * x_ref[...] + g1_ref[...] * y, lng_ref[...], lnb_ref[...])
    x1_ref[...] = x1
    h2_ref[...] = (x1 * (1.0 + sc2_ref[...]) + sh2_ref[...]).astype(BF16)


def _merge(ya, yb, proj, x2, mod3, ln_g, ln_b, wpm, wpg, wo, seq):
    t = x2.shape[0]
    tm = 256
    per_b = seq // tm
    d = D_MODEL

    def modspec(k):
        return pl.BlockSpec((None, 1, d), lambda i: (i // per_b, 0, k))

    def const(shape):
        return pl.BlockSpec(shape, lambda i: (0, 0), pipeline_mode=pl.Buffered(1))

    return pl.pallas_call(
        _merge_kernel,
        name="merge",
        grid=(t // tm,),
        in_specs=[pl.BlockSpec((tm, MOBA_W), lambda i: (i, 0)),
                  pl.BlockSpec((tm, GDN_V_W), lambda i: (i, 0)),
                  pl.BlockSpec((tm, d), lambda i: (i, COL_GATE_A // d)),
                  pl.BlockSpec((tm, d), lambda i: (i, COL_GATE_B // d)),
                  pl.BlockSpec((tm, d), lambda i: (i, 0)),
                  modspec(2), modspec(3), modspec(4),
                  const((1, d)), const((1, d)),
                  const((MOBA_W, d)), const((GDN_V_W, d)), const((d, d))],
        out_specs=[pl.BlockSpec((tm, d), lambda i: (i, 0)),
                   pl.BlockSpec((tm, d), lambda i: (i, 0))],
        out_shape=[jax.ShapeDtypeStruct((t, d), F32), jax.ShapeDtypeStruct((t, d), BF16)],
        compiler_params=_vmem(56),
    )(ya, yb, proj, proj, x2, mod3, mod3, mod3, ln_g, ln_b, wpm, wpg, wo)


def _ffn_kernel(h_ref, x1_ref, g2_ref, lng_ref, lnb_ref, wg_ref, wu_ref, wo_ref, o_ref, acc_ref):
    f = pl.program_id(1)

    @pl.when(f == 0)
    def _():
        acc_ref[...] = jnp.zeros_like(acc_ref)

    h = h_ref[...]
    gate = jnp.dot(h, wg_ref[...], preferred_element_type=F32)
    up = jnp.dot(h, wu_ref[...], preferred_element_type=F32)
    act = (_silu(gate) * up).astype(BF16)
    acc_ref[...] += jnp.dot(act, wo_ref[...], preferred_element_type=F32)

    @pl.when(f == pl.num_programs(1) - 1)
    def _():
        r = DEEPNORM_ALPHA * x1_ref[...] + g2_ref[...] * acc_ref[...]
        o_ref[...] = _layer_norm(r, lng_ref[...], lnb_ref[...])


def _ffn(h2, x1, mod3, ln_g, ln_b, w_in, w_out, seq):
    t = h2.shape[0]
    tm, tf = 512, 512
    per_b = seq // tm
    d = D_MODEL
    nf = D_FF // tf
    return pl.pallas_call(
        _ffn_kernel,
        name="ffn",
        grid=(t // tm, nf),
        in_specs=[pl.BlockSpec((tm, d), lambda i, f: (i, 0)),
                  pl.BlockSpec((tm, d), lambda i, f: (i, 0)),
                  pl.BlockSpec((None, 1, d), lambda i, f: (i // per_b, 0, 5)),
                  pl.BlockSpec((1, d), lambda i, f: (0, 0)),
                  pl.BlockSpec((1, d), lambda i, f: (0, 0)),
                  pl.BlockSpec((d, tf), lambda i, f: (0, f)),
                  pl.BlockSpec((d, tf), lambda i, f: (0, nf + f)),
                  pl.BlockSpec((tf, d), lambda i, f: (f, 0))],
        out_specs=pl.BlockSpec((tm, d), lambda i, f: (i, 0)),
        out_shape=jax.ShapeDtypeStruct((t, d), F32),
        scratch_shapes=[pltpu.VMEM((tm, d), F32)],
        compiler_params=_vmem(48),
    )(h2, x1, mod3, ln_g, ln_b, w_in, w_in, w_out)


def _rel_bucket(dist):
    max_exact = REL_BUCKETS // 2
    n = jnp.maximum(dist, 0)
    nf = jnp.maximum(n, 1).astype(F32)
    large = max_exact + (jnp.log(nf / max_exact) / math.log(REL_MAX_DIST / max_exact)
                         * (REL_BUCKETS - max_exact)).astype(jnp.int32)
    large = jnp.minimum(large, REL_BUCKETS - 1)
    return jnp.where(n < max_exact, n, large)


def _layer(x, c, w_ada, b_ada, w_in, conv_w, a_log, dt_bias, gdn_norm_w, rel_bias,
           w_proj_moba, w_proj_gdn, w_out, ln1_g, ln1_b, w_ffn_in, w_ffn_out, ln2_g, ln2_b):
    bsz, seq, d = x.shape
    t = bsz * seq
    x2 = x.reshape(t, d)

    mod = _ada_mod(c, w_ada, b_ada)
    mod3 = mod.reshape(bsz, 1, 6 * d)

    w_main, w_small = _prep_w_in(w_in.astype(BF16))
    proj, small = _in_proj(x2, mod3, w_main, w_small, seq)

    ii = jnp.arange(MOBA_BLOCK, dtype=jnp.int32)
    dist = ii[None, :] - ii[:, None]
    bko = _rel_bucket(dist)
    bkp = _rel_bucket(dist + MOBA_BLOCK)
    ya = _moba(proj, rel_bias, bko, bkp, bsz, seq)

    nchunk = seq // GDN_CHUNK
    sm_t = small[:, :2 * GDN_V_HEADS].reshape(bsz, seq, 2, GDN_QK_HEADS, 2)
    rows = sm_t.transpose(0, 3, 2, 4, 1).reshape(bsz, GDN_QK_HEADS, 4, nchunk, GDN_CHUNK)
    yb = _gdn(proj, small, rows, conv_w, a_log, dt_bias, gdn_norm_w.reshape(1, HEAD_DIM),
              bsz, seq)

    x1, h2 = _merge(ya, yb, proj, x2, mod3, ln1_g.reshape(1, d), ln1_b.reshape(1, d),
                    w_proj_moba.astype(BF16), w_proj_gdn.astype(BF16), w_out.astype(BF16), seq)
    out = _ffn(h2, x1, mod3, ln2_g.reshape(1, d), ln2_b.reshape(1, d),
               w_ffn_in.astype(BF16), w_ffn_out.astype(BF16), seq)
    return out.reshape(bsz, seq, d)


def kernel(x, c, w_ada, b_ada, w_in, conv_w, a_log, dt_bias, gdn_norm_w, rel_bias, w_proj_moba,
           w_proj_gdn, w_out, ln1_g, ln1_b, w_ffn_in, w_ffn_out, ln2_g, ln2_b):
    depth = w_ada.shape[0]
    for l in range(depth):
        x = _layer(x, c, w_ada[l], b_ada[l], w_in[l], conv_w[l], a_log[l], dt_bias[l],
                   gdn_norm_w[l], rel_bias, w_proj_moba[l], w_proj_gdn[l], w_out[l],
                   ln1_g[l], ln1_b[l], w_ffn_in[l], w_ffn_out[l], ln2_g[l], ln2_b[l])
    return x
```

```python
import functools
import math

import jax
import jax.numpy as jnp
from jax import lax
from jax.experimental import pallas as pl
from jax.experimental.pallas import tpu as pltpu

F32 = jnp.float32
BF16 = jnp.bfloat16

D_MODEL = 2048
MOBA_HEADS = 8
HEAD_DIM = 128
MOBA_W = MOBA_HEADS * HEAD_DIM
MOBA_BLOCK = 256
MOBA_TOPK = 3
REL_BUCKETS = 32
REL_MAX_DIST = 128
GDN_QK_HEADS = 8
GDN_V_HEADS = 16
GDN_QK_W = GDN_QK_HEADS * HEAD_DIM
GDN_V_W = GDN_V_HEADS * HEAD_DIM
GDN_CONV = 4
GDN_CHUNK = 64
D_FF = 5632
DEEPNORM_ALPHA = 2.0 ** 0.25
LN_EPS = 1e-5
RMS_EPS = 1e-6
NEG_INF = -1e30

COL_GATE_A = 0
COL_GATE_B = 2048
COL_MOBA_Q = 4096
COL_MOBA_K = 5120
COL_MOBA_V = 6144
COL_GDN_Q = 7168
COL_GDN_K = 8192
COL_GDN_V = 9216
COL_GDN_Z = 11264
N_MAIN = 13312
N_GATE = 4096
N_HEAD = N_MAIN - N_GATE
N_SMALL = 128

V7X_VMEM_BYTES = 64 * 1024 * 1024
LANES = 128
SUBLANES = 8
BF16_SUBLANES = 16


def _vmem(mb):
    return pltpu.CompilerParams(vmem_limit_bytes=mb * 1024 * 1024)


def _silu(x):
    return x * jax.nn.sigmoid(x)


def _layer_norm(r, gain, bias):
    mu = jnp.mean(r, axis=-1, keepdims=True)
    d = r - mu
    var = jnp.mean(d * d, axis=-1, keepdims=True)
    return d * lax.rsqrt(var + LN_EPS) * gain + bias


def _ada_kernel(c_ref, w_ref, b_ref, o_ref):
    sc = _silu(c_ref[...])
    o_ref[...] = jnp.dot(sc, w_ref[...], precision=lax.Precision.HIGHEST,
                         preferred_element_type=F32) + b_ref[...]


def _ada_mod(c, w_ada, b_ada):
    bsz = c.shape[0]
    n = w_ada.shape[1]
    tn = 1024
    return pl.pallas_call(
        _ada_kernel,
        name="ada_mod",
        grid=(n // tn,),
        in_specs=[pl.BlockSpec((bsz, D_MODEL), lambda j: (0, 0)),
                  pl.BlockSpec((D_MODEL, tn), lambda j: (0, j)),
                  pl.BlockSpec((1, tn), lambda j: (0, j))],
        out_specs=pl.BlockSpec((bsz, tn), lambda j: (0, j)),
        out_shape=jax.ShapeDtypeStruct((bsz, n), F32),
        compiler_params=_vmem(40),
    )(c, w_ada, b_ada.reshape(1, n))


def _inproj_kernel(x_ref, sh_ref, sc_ref, w_ref, wg_ref, ws_ref, o_ref, os_ref, h_ref, *, n_head):
    j = pl.program_id(1)

    @pl.when(j == 0)
    def _():
        h = x_ref[...] * (1.0 + sc_ref[...]) + sh_ref[...]
        h_ref[...] = h.astype(BF16)
        os_ref[...] = jnp.dot(h_ref[...], ws_ref[...], preferred_element_type=F32)

    @pl.when(j < n_head)
    def _():
        o_ref[...] = jnp.dot(h_ref[...], w_ref[...], preferred_element_type=F32)

    @pl.when(j >= n_head)
    def _():
        o_ref[...] = jnp.dot(h_ref[...], wg_ref[...], preferred_element_type=F32)


def _in_proj(x2, mod3, w_all, w_gate, w_small, seq):
    t = x2.shape[0]
    tm, tn = 1024, 1024
    per_b = seq // tm
    n_head, n_gate, n_tiles = N_HEAD // tn, N_GATE // tn, N_MAIN // tn
    return pl.pallas_call(
        functools.partial(_inproj_kernel, n_head=n_head),
        name="in_proj",
        grid=(t // tm, n_tiles),
        in_specs=[pl.BlockSpec((tm, D_MODEL), lambda i, j: (i, 0)),
                  pl.BlockSpec((None, 1, D_MODEL), lambda i, j: (i // per_b, 0, 0)),
                  pl.BlockSpec((None, 1, D_MODEL), lambda i, j: (i // per_b, 0, 1)),
                  pl.BlockSpec((D_MODEL, tn), lambda i, j: (0, jnp.minimum(j, n_head - 1))),
                  pl.BlockSpec((D_MODEL, tn), lambda i, j: (0, jnp.maximum(j - n_head, 0))),
                  pl.BlockSpec((D_MODEL, N_SMALL), lambda i, j: (0, 0))],
        out_specs=[pl.BlockSpec((tm, tn), lambda i, j: (i, (j + n_gate) % n_tiles)),
                   pl.BlockSpec((tm, N_SMALL), lambda i, j: (i, 0))],
        out_shape=[jax.ShapeDtypeStruct((t, N_MAIN), F32),
                   jax.ShapeDtypeStruct((t, N_SMALL), F32)],
        scratch_shapes=[pltpu.VMEM((tm, D_MODEL), BF16)],
        compiler_params=_vmem(56),
    )(x2, mod3, mod3, w_all, w_gate, w_small)


def _moba_kernel(rel_ref, bko_ref, bkp_ref, q_ref, k_ref, v_ref, o_ref,
                 bias_own, bias_prev, kb_ref, vt_ref, *, nb):
    h = pl.program_id(0)
    blk = MOBA_BLOCK
    inv_scale = HEAD_DIM ** 0.5
    scale_log2e = HEAD_DIM ** -0.5 * math.log2(math.e)

    @pl.when(pl.program_id(1) == 0)
    def _():
        bo = bko_ref[...]
        bp = bkp_ref[...]
        acc_o = jnp.zeros((blk, blk), F32)
        acc_p = jnp.zeros((blk, blk), F32)
        for kk in range(REL_BUCKETS):
            val = rel_ref[kk, h] * inv_scale
            acc_o = jnp.where(bo == kk, val, acc_o)
            acc_p = jnp.where(bp == kk, val, acc_p)
        bias_own[...] = acc_o
        bias_prev[...] = acc_p

    bias_far = rel_ref[REL_BUCKETS - 1, h] * inv_scale
    kf = k_ref[...]
    kmean = jnp.mean(kf.reshape(nb, blk, HEAD_DIM), axis=1)
    kb_ref[...] = kf.astype(BF16)
    nt_dims = (((1,), (1,)), ((), ()))
    eye = (lax.broadcasted_iota(jnp.int32, (HEAD_DIM, HEAD_DIM), 0)
           == lax.broadcasted_iota(jnp.int32, (HEAD_DIM, HEAD_DIM), 1)).astype(BF16)
    vt_ref[:HEAD_DIM, :] = lax.dot_general(eye, v_ref[...].astype(BF16), nt_dims,
                                           preferred_element_type=F32).astype(BF16)
    pad_rows = vt_ref.shape[0] - HEAD_DIM
    vt_ref[HEAD_DIM:, :] = (lax.broadcasted_iota(jnp.int32, (pad_rows, vt_ref.shape[1]), 0)
                            == 0).astype(BF16)
    causal = (lax.broadcasted_iota(jnp.int32, (blk, blk), 0)
              <= lax.broadcasted_iota(jnp.int32, (blk, blk), 1))

    def scores(i):
        qi = q_ref[i * blk:(i + 1) * blk, :]
        qb = qi.astype(BF16)
        sel = None
        if i > MOBA_TOPK:
            route = lax.dot_general(kmean, qi, nt_dims, precision=lax.Precision.HIGHEST,
                                    preferred_element_type=F32)
            rc = [route[n:n + 1, :] for n in range(i)]
            sel = []
            for n in range(i):
                rank = jnp.zeros((1, blk), jnp.int32)
                for m in range(i):
                    if m == n:
                        continue
                    beats = (rc[m] >= rc[n]) if m < n else (rc[m] > rc[n])
                    rank = rank + beats.astype(jnp.int32)
                sel.append(rank < MOBA_TOPK)
        t_list = []
        for n in range(i + 1):
            t = lax.dot_general(kb_ref[n * blk:(n + 1) * blk, :], qb, nt_dims,
                                preferred_element_type=F32)
            if n == i:
                t = jnp.where(causal, t + bias_own[...], NEG_INF)
            else:
                if n == i - 1:
                    t = t + bias_prev[...]
                if sel is not None:
                    t = jnp.where(sel[n], t, NEG_INF)
            t_list.append(t)
        return t_list

    t_next = scores(0)
    for i in range(nb):
        t_list = t_next
        if i + 1 < nb:
            t_next = scores(i + 1)
        n_far = max(i - 1, 0)
        m_run = jnp.max(t_list[n_far], axis=0, keepdims=True)
        for t in t_list[n_far + 1:]:
            m_run = jnp.maximum(m_run, jnp.max(t, axis=0, keepdims=True))
        if n_far:
            m_far = jnp.max(t_list[0], axis=0, keepdims=True)
            for t in t_list[1:n_far]:
                m_far = jnp.maximum(m_far, jnp.max(t, axis=0, keepdims=True))
            m_run = jnp.maximum(m_run, m_far + bias_far)
        acc = jnp.zeros((vt_ref.shape[0], blk), F32)
        for n, t in enumerate(t_list):
            offset = m_run - bias_far if n < n_far else m_run
            p = jnp.exp2((t - offset) * scale_log2e)
            acc = acc + jnp.dot(vt_ref[:, n * blk:(n + 1) * blk], p.astype(BF16),
                                preferred_element_type=F32)
        out = acc[:HEAD_DIM] / acc[HEAD_DIM:HEAD_DIM + 1]
        o_ref[i * blk:(i + 1) * blk, :] = out.T.astype(BF16)


def _moba(proj, rel_bias, bko, bkp, bsz, seq):
    nb = seq // MOBA_BLOCK
    qo, ko, vo = COL_MOBA_Q // HEAD_DIM, COL_MOBA_K // HEAD_DIM, COL_MOBA_V // HEAD_DIM
    blk = MOBA_BLOCK
    return pl.pallas_call(
        functools.partial(_moba_kernel, nb=nb),
        name="moba",
        grid=(MOBA_HEADS, bsz),
        in_specs=[pl.BlockSpec(memory_space=pltpu.SMEM),
                  pl.BlockSpec((blk, blk), lambda h, b: (0, 0)),
                  pl.BlockSpec((blk, blk), lambda h, b: (0, 0)),
                  pl.BlockSpec((seq, HEAD_DIM), lambda h, b: (b, qo + h)),
                  pl.BlockSpec((seq, HEAD_DIM), lambda h, b: (b, ko + h)),
                  pl.BlockSpec((seq, HEAD_DIM), lambda h, b: (b, vo + h))],
        out_specs=pl.BlockSpec((seq, HEAD_DIM), lambda h, b: (b, h)),
        out_shape=jax.ShapeDtypeStruct((bsz * seq, MOBA_W), BF16),
        scratch_shapes=[pltpu.VMEM((blk, blk), F32), pltpu.VMEM((blk, blk), F32),
                        pltpu.VMEM((seq, HEAD_DIM), BF16),
                        pltpu.VMEM((HEAD_DIM + BF16_SUBLANES, seq), BF16)],
        compiler_params=_vmem(48),
    )(rel_bias, bko, bkp, proj, proj, proj)


def _conv_silu(xh, w):
    acc = xh[SUBLANES:, :] * w[GDN_CONV - 1:GDN_CONV, :]
    for s in range(1, GDN_CONV):
        acc = acc + pltpu.roll(xh, s, axis=0)[SUBLANES:, :] * w[GDN_CONV - 1 - s:GDN_CONV - s, :]
    return _silu(acc)


def _l2norm(x):
    return x * lax.rsqrt(jnp.sum(x * x, axis=-1, keepdims=True) + RMS_EPS)


def _softplus(x):
    return jnp.maximum(x, 0.0) + jnp.log1p(jnp.exp(-jnp.abs(x)))


ELIM_BLOCK = 8
MERGE_LEVELS = (GDN_CHUNK // ELIM_BLOCK).bit_length() - 1


def _block_diag_inverse(lpair):
    n = lpair.shape[0]
    nv = n // SUBLANES
    vpb = ELIM_BLOCK // SUBLANES
    rid = lax.broadcasted_iota(jnp.int32, (SUBLANES, 2 * n), 0)
    lid = lax.broadcasted_iota(jnp.int32, (SUBLANES, 2 * n), 1)
    t_rows = [(lid % n == rid + v * SUBLANES).astype(F32) for v in range(nv)]
    l_rows = [lpair[v * SUBLANES:(v + 1) * SUBLANES, :] for v in range(nv)]
    for m in range(n - 1):
        v0, s0 = divmod(m, SUBLANES)
        v_end = (v0 // vpb + 1) * vpb
        row = t_rows[v0][s0:s0 + 1, :]
        col = (lid // n) * n + m
        for v in range(v0 if s0 < SUBLANES - 1 else v0 + 1, v_end):
            t_rows[v] = t_rows[v] - jnp.take_along_axis(l_rows[v], col, axis=1) * row
    return jnp.concatenate(t_rows, axis=0)


def _lane_block_diag(pair):
    first = lax.broadcasted_iota(jnp.int32, pair.shape, 1) < pair.shape[1] // 2
    zero = jnp.zeros_like(pair)
    return jnp.concatenate([jnp.where(first, pair, zero), jnp.where(first, zero, pair)], axis=0)


def _merge_lower_products(lpair, tpair, k):
    n = lpair.shape[0]
    ri = lax.broadcasted_iota(jnp.int32, lpair.shape, 0)
    ci = lax.broadcasted_iota(jnp.int32, lpair.shape, 1) % n
    off = (ri // (2 * k) == ci // (2 * k)) & (ri // k > ci // k)
    lk = jnp.where(off, lpair, 0.0).astype(BF16)
    return jnp.dot(lk, _lane_block_diag(tpair.astype(BF16)), preferred_element_type=F32)


def _merge_apply(tpair, lt):
    return tpair - jnp.dot(tpair.astype(BF16), _lane_block_diag(lt), preferred_element_type=F32)


def _software_pipeline(stages, n):
    ns = len(stages)

    def run(it, lo, hi):
        conts = [stages[s](it - s) for s in reversed(range(lo, hi))]
        for cont in conts:
            if cont is not None:
                cont()

    for it in range(ns - 1):
        run(it, 0, it + 1)

    def body(it, carry):
        run(it, 0, ns)
        return carry

    lax.fori_loop(ns - 1, n, body, 0)
    for it in range(n, n + ns - 1):
        run(it, it - n + 1, ns)


def _lane_pick(x, lane, idx):
    return jnp.sum(jnp.where(lane == idx, x, 0.0), axis=-1, keepdims=True)


def _gdn_kernel(alog_ref, dtb_ref, gp_ref, nw_ref, cwq_ref, cwk_ref, cwv_ref,
                q_ref, k_ref, v_ref, z_ref, sm_ref, rw_ref, y_ref,
                kb_s, qn_s, kdp_s, rhs_s, qd_s, gcb_s, betab_s, gcr_s, gram_s, lm_s, lhs2_s,
                sol_s, mp_s, n_s, r_s, st_s, *merge_s, nchunk):
    tl_s, lt_s = merge_s[:MERGE_LEVELS + 1], merge_s[MERGE_LEVELS + 1:]
    hq = pl.program_id(1)
    c64 = GDN_CHUNK
    hd = HEAD_DIM
    heads = range(2)

    tri_u = (lax.broadcasted_iota(jnp.int32, (c64, c64), 0)
             <= lax.broadcasted_iota(jnp.int32, (c64, c64), 1)).astype(F32)
    gc_rows = []
    for j in heads:
        hv = 2 * hq + j
        a_neg_r = -jnp.exp(jnp.full((nchunk, c64), alog_ref[hv], F32))
        g_row = a_neg_r * _softplus(rw_ref[2 + j] + dtb_ref[hv])
        gc_rows.append(jnp.dot(g_row, tri_u, precision=lax.Precision.HIGHEST,
                               preferred_element_type=F32))
    gcr_s[...] = jnp.concatenate(gc_rows, axis=1)

    rows = lax.broadcasted_iota(jnp.int32, (c64, 2 * c64), 0)
    cols = lax.broadcasted_iota(jnp.int32, (c64, 2 * c64), 1) % c64
    tril = rows >= cols
    strict = rows > cols
    eye2 = (lax.broadcasted_iota(jnp.int32, (2 * c64, 2 * c64), 0)
            == lax.broadcasted_iota(jnp.int32, (2 * c64, 2 * c64), 1)).astype(BF16)
    nt_dims = (((1,), (1,)), ((), ()))

    def rows_of(c):
        start = c * c64
        return pl.ds(start if isinstance(c, int) else pl.multiple_of(start, c64), c64)

    def with_halo(x_ref, c):
        if isinstance(c, int) and c == 0:
            return jnp.concatenate([jnp.zeros((SUBLANES, x_ref.shape[1]), F32), x_ref[:c64, :]],
                                   axis=0)
        start = c * c64 - SUBLANES
        if not isinstance(c, int):
            start = pl.multiple_of(start, SUBLANES)
        return x_ref[pl.ds(start, c64 + SUBLANES), :]

    lane = lax.broadcasted_iota(jnp.int32, (c64, N_SMALL), 1)
    first = lane < c64
    pos = lax.broadcasted_iota(jnp.int32, (c64, N_SMALL), 0)

    def stage_prep(c):
        r = rows_of(c)
        qn = _l2norm(_conv_silu(with_halo(q_ref, c), cwq_ref[...])) * (hd ** -0.5)
        kn = _l2norm(_conv_silu(with_halo(k_ref, c), cwk_ref[...]))
        vc = _conv_silu(with_halo(v_ref, c), cwv_ref[...])
        qn_s[r, :] = qn.astype(BF16)
        kb_s[r, :] = kn.astype(BF16)
        sm = sm_ref[r, :]
        sig_all = jax.nn.sigmoid(sm)
        gc_all = -jnp.exp(gp_ref[0:1, :]) * _softplus(sm + gp_ref[1:2, :])
        sft = 1
        while sft < c64:
            gc_all = gc_all + jnp.where(pos >= sft, pltpu.roll(gc_all, sft, axis=0), 0.0)
            sft *= 2
        rest_all = gc_all[c64 - 1:, :] - gc_all
        betas, gc_cols = [], []
        for j in heads:
            hv = 2 * hq + j
            beta = _lane_pick(sig_all, lane, hv)
            gc_col = _lane_pick(gc_all, lane, GDN_V_HEADS + hv)
            rest_col = _lane_pick(rest_all, lane, GDN_V_HEADS + hv)
            eg = jnp.exp(gc_col)
            rhs_s[j, r, :hd] = (vc[:, j * hd:(j + 1) * hd] * beta).astype(BF16)
            rhs_s[j, r, hd:] = (kn * (beta * eg)).astype(BF16)
            qd_s[j, r, :] = (qn * eg).astype(BF16)
            kdp_s[c, j * c64:(j + 1) * c64, :] = (kn * jnp.exp(rest_col)).astype(BF16)
            betas.append(beta)
            gc_cols.append(gc_col)
        gcb_s[r, :] = jnp.where(first, gc_cols[0], gc_cols[1])
        betab_s[r, :] = jnp.where(first, betas[0], betas[1])

    def stage_gram(c):
        r = rows_of(c)
        kb = kb_s[r, :]
        kq = jnp.concatenate([kb, qn_s[r, :]], axis=0)
        gram = lax.dot_general(kq, jnp.concatenate([kb, kb], axis=0), nt_dims,
                               preferred_element_type=F32)
        kd_t = lax.dot_general(eye2, kdp_s[c], nt_dims, preferred_element_type=F32)

        def finish():
            gram_s[c] = gram
            lhs2_s[c, :2 * c64, :] = kd_t.astype(BF16)

        return finish

    def stage_factor(c):
        r = rows_of(c)
        dec = jnp.exp(jnp.where(tril, gcb_s[r, :] - gcr_s[pl.ds(c, 1), :], NEG_INF))
        lpair = jnp.where(strict, gram_s[c, :c64, :] * dec, 0.0) * betab_s[r, :]
        lm_s[r, :] = lpair
        lhs2_s[c, 2 * c64:, :] = (gram_s[c, c64:, :] * dec).astype(BF16)
        tl_s[0][r, :] = _block_diag_inverse(lpair)

    def stage_merge_products(level):
        def stage(c):
            r = rows_of(c)
            lt = _merge_lower_products(lm_s[r, :], tl_s[level][r, :], ELIM_BLOCK << level)

            def finish():
                lt_s[level][r, :] = lt.astype(BF16)

            return finish
        return stage

    def stage_merge_apply(level):
        def stage(c):
            r = rows_of(c)
            merged = _merge_apply(tl_s[level][r, :], lt_s[level][r, :])

            def finish():
                tl_s[level + 1][r, :] = merged

            return finish
        return stage

    def stage_solve(c):
        r = rows_of(c)
        zero = jnp.zeros((c64, 2 * hd), BF16)
        rhs_bd = jnp.concatenate([jnp.concatenate([rhs_s[0, r, :], zero], axis=1),
                                  jnp.concatenate([zero, rhs_s[1, r, :]], axis=1)], axis=0)
        sol = jnp.dot(tl_s[MERGE_LEVELS][r, :].astype(BF16), rhs_bd,
                      preferred_element_type=F32)

        def finish():
            sol_s[r, :] = sol.astype(BF16)

        return finish

    def stage_fold(c):
        r = rows_of(c)
        sol = sol_s[r, :]
        out = jnp.dot(lhs2_s[c], _lane_block_diag(sol), preferred_element_type=F32)

        def finish():
            for j in heads:
                u_col, w_col = 2 * j * hd, (2 * j + 1) * hd
                n_s[j, c] = out[:2 * c64, u_col:u_col + hd]
                mp_s[j, c, :2 * c64, :] = out[:2 * c64, w_col:w_col + hd].astype(BF16)
                mp_s[j, c, 2 * c64:, :] = (qd_s[j, r, :].astype(F32)
                                           - out[2 * c64:, w_col:w_col + hd]).astype(BF16)
                r_s[j, r, :] = out[2 * c64:, u_col:u_col + hd]

        return finish

    st_s[...] = jnp.zeros_like(st_s)
    nw = nw_ref[...]

    def stage_state(c):
        r = rows_of(c)
        sts = [st_s[j] for j in heads]
        outs = [jnp.dot(mp_s[j, c], sts[j].astype(BF16), preferred_element_type=F32)
                for j in heads]

        def finish():
            for j in heads:
                end = (j + 1) * c64
                gl = gcr_s[pl.ds(c, 1), end - 1:end]
                st_s[j] = sts[j] * jnp.exp(gl) - outs[j][:2 * c64] + n_s[j, c]
                r_s[j, r, :] = r_s[j, r, :] + outs[j][2 * c64:]

        return finish

    def stage_norm(c):
        r = rows_of(c)
        for j in heads:
            o = r_s[j, r, :]
            og = (o * lax.rsqrt(jnp.mean(o * o, axis=-1, keepdims=True) + RMS_EPS)
                  * nw * _silu(z_ref[r, j * hd:(j + 1) * hd]))
            y_ref[r, j * hd:(j + 1) * hd] = og.astype(BF16)

    merges = [stage(lv) for lv in range(MERGE_LEVELS)
              for stage in (stage_merge_products, stage_merge_apply)]
    _software_pipeline([stage_prep, stage_gram, stage_factor] + merges
                       + [stage_solve, stage_fold, stage_state, stage_norm], nchunk)


def _gdn(proj, small, rows, conv_w, a_log, dt_bias, norm_w, bsz, seq):
    nchunk = seq // GDN_CHUNK
    hd = HEAD_DIM
    c64 = GDN_CHUNK
    qo, ko = COL_GDN_Q // hd, COL_GDN_K // hd
    vo, zo = COL_GDN_V // (2 * hd), COL_GDN_Z // (2 * hd)
    cvo = (2 * GDN_QK_W) // (2 * hd)
    smem = pl.BlockSpec(memory_space=pltpu.SMEM)
    pad = (GDN_V_HEADS, N_SMALL - 2 * GDN_V_HEADS)
    gate_params = jnp.stack([jnp.pad(a_log, pad), jnp.pad(dt_bias, pad)])
    return pl.pallas_call(
        functools.partial(_gdn_kernel, nchunk=nchunk),
        name="gdn",
        grid=(bsz, GDN_QK_HEADS),
        in_specs=[smem, smem,
                  pl.BlockSpec((2, N_SMALL), lambda b, h: (0, 0)),
                  pl.BlockSpec((1, hd), lambda b, h: (0, 0)),
                  pl.BlockSpec((GDN_CONV, hd), lambda b, h: (0, h)),
                  pl.BlockSpec((GDN_CONV, hd), lambda b, h: (0, GDN_QK_HEADS + h)),
                  pl.BlockSpec((GDN_CONV, 2 * hd), lambda b, h: (0, cvo + h)),
                  pl.BlockSpec((seq, hd), lambda b, h: (b, qo + h)),
                  pl.BlockSpec((seq, hd), lambda b, h: (b, ko + h)),
                  pl.BlockSpec((seq, 2 * hd), lambda b, h: (b, vo + h)),
                  pl.BlockSpec((seq, 2 * hd), lambda b, h: (b, zo + h)),
                  pl.BlockSpec((seq, N_SMALL), lambda b, h: (b, 0)),
                  pl.BlockSpec((None, None, 4, nchunk, GDN_CHUNK), lambda b, h: (b, h, 0, 0, 0))],
        out_specs=pl.BlockSpec((seq, 2 * hd), lambda b, h: (b, h)),
        out_shape=jax.ShapeDtypeStruct((bsz * seq, GDN_V_W), BF16),
        scratch_shapes=[pltpu.VMEM((seq, hd), BF16),
                        pltpu.VMEM((seq, hd), BF16),
                        pltpu.VMEM((nchunk, 2 * c64, hd), BF16),
                        pltpu.VMEM((2, seq, 2 * hd), BF16),
                        pltpu.VMEM((2, seq, hd), BF16),
                        pltpu.VMEM((seq, 2 * c64), F32),
                        pltpu.VMEM((seq, 2 * c64), F32),
                        pltpu.VMEM((nchunk, 2 * c64), F32),
                        pltpu.VMEM((nchunk, 2 * c64, 2 * c64), F32),
                        pltpu.VMEM((seq, 2 * c64), F32),
                        pltpu.VMEM((nchunk, 3 * c64, hd), BF16),
                        pltpu.VMEM((seq, 4 * hd), BF16),
                        pltpu.VMEM((2, nchunk, 3 * c64, hd), BF16),
                        pltpu.VMEM((2, nchunk, hd, hd), F32),
                        pltpu.VMEM((2, seq, hd), F32),
                        pltpu.VMEM((2, hd, hd), F32)]
                       + [pltpu.VMEM((seq, 2 * c64), F32)] * (MERGE_LEVELS + 1)
                       + [pltpu.VMEM((seq, 2 * c64), BF16)] * MERGE_LEVELS,
        compiler_params=_vmem(58),
    )(a_log, dt_bias, gate_params, norm_w, conv_w, conv_w, conv_w, proj, proj, proj, proj, small, rows)


def _merge_kernel(ya_ref, yb_ref, ga_ref, gb_ref, x_ref, g1_ref, sh2_ref, sc2_ref,
                  lng_ref, lnb_ref, wpm_ref, wpg_ref, wo_ref, x1_ref, h2_ref):
    pa = jnp.dot(ya_ref[...], wpm_ref[...], preferred_element_type=F32)
    pb = jnp.dot(yb_ref[...], wpg_ref[...], preferred_element_type=F32)
    merged = jax.nn.sigmoid(ga_ref[...]) * pa + jax.nn.sigmoid(gb_ref[...]) * pb
    y = jnp.dot(merged.astype(BF16), wo_ref[...], preferred_element_type=F32)
    x1 = _layer_norm(DEEPNORM_ALPHA * x_ref[...] + g1_ref[...] * y, lng_ref[...], lnb_ref[...])
    x1_ref[...] = x1
    h2_ref[...] = (x1 * (1.0 + sc2_ref[...]) + sh2_ref[...]).astype(BF16)


def _merge(ya, yb, proj, x2, mod3, ln_g, ln_b, wpm, wpg, wo, seq):
    t = x2.shape[0]
    tm = 256
    per_b = seq // tm
    d = D_MODEL

    def modspec(k):
        return pl.BlockSpec((None, 1, d), lambda i: (i // per_b, 0, k))

    def const(shape):
        return pl.BlockSpec(shape, lambda i: (0, 0), pipeline_mode=pl.Buffered(1))

    return pl.pallas_call(
        _merge_kernel,
        name="merge",
        grid=(t // tm,),
        in_specs=[pl.BlockSpec((tm, MOBA_W), lambda i: (i, 0)),
                  pl.BlockSpec((tm, GDN_V_W), lambda i: (i, 0)),
                  pl.BlockSpec((tm, d), lambda i: (i, COL_GATE_A // d)),
                  pl.BlockSpec((tm, d), lambda i: (i, COL_GATE_B // d)),
                  pl.BlockSpec((tm, d), lambda i: (i, 0)),
                  modspec(2), modspec(3), modspec(4),
                  const((1, d)), const((1, d)),
                  const((MOBA_W, d)), const((GDN_V_W, d)), const((d, d))],
        out_specs=[pl.BlockSpec((tm, d), lambda i: (i, 0)),
                   pl.BlockSpec((tm, d), lambda i: (i, 0))],
        out_shape=[jax.ShapeDtypeStruct((t, d), F32), jax.ShapeDtypeStruct((t, d), BF16)],
        compiler_params=_vmem(56),
    )(ya, yb, proj, proj, x2, mod3, mod3, mod3, ln_g, ln_b, wpm, wpg, wo)


def _ffn_kernel(h_ref, x1_ref, g2_ref, lng_ref, lnb_ref, wg_ref, wu_ref, wo_ref, o_ref, acc_ref):
    f = pl.program_id(1)

    @pl.when(f == 0)
    def _():
        acc_ref[...] = jnp.zeros_like(acc_ref)

    h = h_ref[...]
    gate = jnp.dot(h, wg_ref[...], preferred_element_type=F32)
    up = jnp.dot(h, wu_ref[...], preferred_element_type=F32)
    act = (_silu(gate) * up).astype(BF16)
    acc_ref[...] += jnp.dot(act, wo_ref[...], preferred_element_type=F32)

    @pl.when(f == pl.num_programs(1) - 1)
    def _():
        r = DEEPNORM_ALPHA * x1_ref[...] + g2_ref[...] * acc_ref[...]
        o_ref[...] = _layer_norm(r, lng_ref[...], lnb_ref[...])


def _ffn(h2, x1, mod3, ln_g, ln_b, w_in, w_out, seq):
    t = h2.shape[0]
    tm, tf = 512, 512
    per_b = seq // tm
    d = D_MODEL
    nf = D_FF // tf
    return pl.pallas_call(
        _ffn_kernel,
        name="ffn",
        grid=(t // tm, nf),
        in_specs=[pl.BlockSpec((tm, d), lambda i, f: (i, 0)),
                  pl.BlockSpec((tm, d), lambda i, f: (i, 0)),
                  pl.BlockSpec((None, 1, d), lambda i, f: (i // per_b, 0, 5)),
                  pl.BlockSpec((1, d), lambda i, f: (0, 0)),
                  pl.BlockSpec((1, d), lambda i, f: (0, 0)),
                  pl.BlockSpec((d, tf), lambda i, f: (0, f)),
                  pl.BlockSpec((d, tf), lambda i, f: (0, nf + f)),
                  pl.BlockSpec((tf, d), lambda i, f: (f, 0))],
        out_specs=pl.BlockSpec((tm, d), lambda i, f: (i, 0)),
        out_shape=jax.ShapeDtypeStruct((t, d), F32),
        scratch_shapes=[pltpu.VMEM((tm, d), F32)],
        compiler_params=_vmem(48),
    )(h2, x1, mod3, ln_g, ln_b, w_in, w_in, w_out)


def _rel_bucket(dist):
    max_exact = REL_BUCKETS // 2
    n = jnp.maximum(dist, 0)
    nf = jnp.maximum(n, 1).astype(F32)
    large = max_exact + (jnp.log(nf / max_exact) / math.log(REL_MAX_DIST / max_exact)
                         * (REL_BUCKETS - max_exact)).astype(jnp.int32)
    large = jnp.minimum(large, REL_BUCKETS - 1)
    return jnp.where(n < max_exact, n, large)


def _layer(x, c, w_ada, b_ada, w_in, conv_w, a_log, dt_bias, gdn_norm_w, rel_bias,
           w_proj_moba, w_proj_gdn, w_out, ln1_g, ln1_b, w_ffn_in, w_ffn_out, ln2_g, ln2_b):
    bsz, seq, d = x.shape
    t = bsz * seq
    x2 = x.reshape(t, d)

    mod = _ada_mod(c, w_ada, b_ada)
    mod3 = mod.reshape(bsz, 1, 6 * d)

    w_all = w_in.astype(BF16)
    n_gates = 2 * GDN_V_HEADS
    w_gate = w_all[:, N_HEAD + n_gates:]
    w_small = jnp.pad(w_all[:, N_HEAD:N_HEAD + n_gates], ((0, 0), (0, N_SMALL - n_gates)))

    proj, small = _in_proj(x2, mod3, w_all, w_gate, w_small, seq)

    ii = jnp.arange(MOBA_BLOCK, dtype=jnp.int32)
    dist = ii[None, :] - ii[:, None]
    bko = _rel_bucket(dist)
    bkp = _rel_bucket(dist + MOBA_BLOCK)
    ya = _moba(proj, rel_bias, bko, bkp, bsz, seq)

    nchunk = seq // GDN_CHUNK
    sm_t = small[:, :2 * GDN_V_HEADS].reshape(bsz, seq, 2, GDN_QK_HEADS, 2)
    rows = sm_t.transpose(0, 3, 2, 4, 1).reshape(bsz, GDN_QK_HEADS, 4, nchunk, GDN_CHUNK)
    yb = _gdn(proj, small, rows, conv_w, a_log, dt_bias, gdn_norm_w.reshape(1, HEAD_DIM),
              bsz, seq)

    x1, h2 = _merge(ya, yb, proj, x2, mod3, ln1_g.reshape(1, d), ln1_b.reshape(1, d),
                    w_proj_moba.astype(BF16), w_proj_gdn.astype(BF16), w_out.astype(BF16), seq)
    out = _ffn(h2, x1, mod3, ln2_g.reshape(1, d), ln2_b.reshape(1, d),
               w_ffn_in.astype(BF16), w_ffn_out.astype(BF16), seq)
    return out.reshape(bsz, seq, d)


def kernel(x, c, w_ada, b_ada, w_in, conv_w, a_log, dt_bias, gdn_norm_w, rel_bias, w_proj_moba,
           w_proj_gdn, w_out, ln1_g, ln1_b, w_ffn_in, w_ffn_out, ln2_g, ln2_b):
    depth = w_ada.shape[0]
    for l in range(depth):
        x = _layer(x, c, w_ada[l], b_ada[l], w_in[l], conv_w[l], a_log[l], dt_bias[l],
                   gdn_norm_w[l], rel_bias, w_proj_moba[l], w_proj_gdn[l], w_out[l],
                   ln1_g[l], ln1_b[l], w_ffn_in[l], w_ffn_out[l], ln2_g[l], ln2_b[l])
    return x
```

```python
import functools
import math

import jax
import jax.numpy as jnp
from jax import lax
from jax.experimental import pallas as pl
from jax.experimental.pallas import tpu as pltpu

F32 = jnp.float32
BF16 = jnp.bfloat16

D_MODEL = 2048
MOBA_HEADS = 8
HEAD_DIM = 128
MOBA_W = MOBA_HEADS * HEAD_DIM
MOBA_BLOCK = 256
MOBA_TOPK = 3
REL_BUCKETS = 32
REL_MAX_DIST = 128
GDN_QK_HEADS = 8
GDN_V_HEADS = 16
GDN_QK_W = GDN_QK_HEADS * HEAD_DIM
GDN_V_W = GDN_V_HEADS * HEAD_DIM
GDN_CONV = 4
GDN_CHUNK = 64
D_FF = 5632
DEEPNORM_ALPHA = 2.0 ** 0.25
LN_EPS = 1e-5
RMS_EPS = 1e-6
NEG_INF = -1e30

COL_GATE_A = 0
COL_GATE_B = 2048
COL_MOBA_Q = 4096
COL_MOBA_K = 5120
COL_MOBA_V = 6144
COL_GDN_Q = 7168
COL_GDN_K = 8192
COL_GDN_V = 9216
COL_GDN_Z = 11264
N_MAIN = 13312
N_GATE = 4096
N_HEAD = N_MAIN - N_GATE
N_SMALL = 128

V7X_VMEM_MIB = 64
SUBLANES = 8
BF16_SUBLANES = 16

ADA_TN = 1024
INPROJ_TM, INPROJ_TN = 1024, 1024
MERGE_TM = 256
FFN_TM, FFN_TF = 512, 512
VMEM_LIMIT_MIB = {"ada_mod": 40, "in_proj": 56, "moba": 48, "gdn": 58, "merge": 56, "ffn": 48}
assert max(VMEM_LIMIT_MIB.values()) < V7X_VMEM_MIB


def _vmem(name):
    return pltpu.CompilerParams(vmem_limit_bytes=VMEM_LIMIT_MIB[name] * 1024 * 1024)


def _silu(x):
    return x * jax.nn.sigmoid(x)


def _layer_norm(r, gain, bias):
    mu = jnp.mean(r, axis=-1, keepdims=True)
    d = r - mu
    var = jnp.mean(d * d, axis=-1, keepdims=True)
    return d * lax.rsqrt(var + LN_EPS) * gain + bias


def _ada_kernel(c_ref, w_ref, b_ref, o_ref):
    sc = _silu(c_ref[...])
    o_ref[...] = jnp.dot(sc, w_ref[...], precision=lax.Precision.HIGHEST,
                         preferred_element_type=F32) + b_ref[...]


def _ada_mod(c, w_ada, b_ada):
    bsz = c.shape[0]
    n = w_ada.shape[1]
    tn = ADA_TN
    return pl.pallas_call(
        _ada_kernel,
        name="ada_mod",
        grid=(n // tn,),
        in_specs=[pl.BlockSpec((bsz, D_MODEL), lambda j: (0, 0)),
                  pl.BlockSpec((D_MODEL, tn), lambda j: (0, j)),
                  pl.BlockSpec((1, tn), lambda j: (0, j))],
        out_specs=pl.BlockSpec((bsz, tn), lambda j: (0, j)),
        out_shape=jax.ShapeDtypeStruct((bsz, n), F32),
        compiler_params=_vmem("ada_mod"),
    )(c, w_ada, b_ada.reshape(1, n))


def _inproj_kernel(x_ref, sh_ref, sc_ref, w_ref, wg_ref, ws_ref, o_ref, os_ref, h_ref, *, n_head):
    j = pl.program_id(1)

    @pl.when(j == 0)
    def _():
        h = x_ref[...] * (1.0 + sc_ref[...]) + sh_ref[...]
        h_ref[...] = h.astype(BF16)
        os_ref[...] = jnp.dot(h_ref[...], ws_ref[...], preferred_element_type=F32)

    @pl.when(j < n_head)
    def _():
        o_ref[...] = jnp.dot(h_ref[...], w_ref[...], preferred_element_type=F32)

    @pl.when(j >= n_head)
    def _():
        o_ref[...] = jnp.dot(h_ref[...], wg_ref[...], preferred_element_type=F32)


def _in_proj(x2, mod3, w_all, w_gate, w_small, seq):
    t = x2.shape[0]
    tm, tn = INPROJ_TM, INPROJ_TN
    per_b = seq // tm
    n_head, n_gate, n_tiles = N_HEAD // tn, N_GATE // tn, N_MAIN // tn
    return pl.pallas_call(
        functools.partial(_inproj_kernel, n_head=n_head),
        name="in_proj",
        grid=(t // tm, n_tiles),
        in_specs=[pl.BlockSpec((tm, D_MODEL), lambda i, j: (i, 0)),
                  pl.BlockSpec((None, 1, D_MODEL), lambda i, j: (i // per_b, 0, 0)),
                  pl.BlockSpec((None, 1, D_MODEL), lambda i, j: (i // per_b, 0, 1)),
                  pl.BlockSpec((D_MODEL, tn), lambda i, j: (0, jnp.minimum(j, n_head - 1))),
                  pl.BlockSpec((D_MODEL, tn), lambda i, j: (0, jnp.maximum(j - n_head, 0))),
                  pl.BlockSpec((D_MODEL, N_SMALL), lambda i, j: (0, 0))],
        out_specs=[pl.BlockSpec((tm, tn), lambda i, j: (i, (j + n_gate) % n_tiles)),
                   pl.BlockSpec((tm, N_SMALL), lambda i, j: (i, 0))],
        out_shape=[jax.ShapeDtypeStruct((t, N_MAIN), F32),
                   jax.ShapeDtypeStruct((t, N_SMALL), F32)],
        scratch_shapes=[pltpu.VMEM((tm, D_MODEL), BF16)],
        compiler_params=_vmem("in_proj"),
    )(x2, mod3, mod3, w_all, w_gate, w_small)


def _moba_kernel(rel_ref, bko_ref, bkp_ref, q_ref, k_ref, v_ref, o_ref,
                 bias_own, bias_prev, kb_ref, vt_ref, *, nb):
    h = pl.program_id(0)
    blk = MOBA_BLOCK
    inv_scale = HEAD_DIM ** 0.5
    scale_log2e = HEAD_DIM ** -0.5 * math.log2(math.e)

    @pl.when(pl.program_id(1) == 0)
    def _():
        bo = bko_ref[...]
        bp = bkp_ref[...]
        acc_o = jnp.zeros((blk, blk), F32)
        acc_p = jnp.zeros((blk, blk), F32)
        for kk in range(REL_BUCKETS):
            val = rel_ref[kk, h] * inv_scale
            acc_o = jnp.where(bo == kk, val, acc_o)
            acc_p = jnp.where(bp == kk, val, acc_p)
        bias_own[...] = acc_o
        bias_prev[...] = acc_p

    bias_far = rel_ref[REL_BUCKETS - 1, h] * inv_scale
    kf = k_ref[...]
    kmean = jnp.mean(kf.reshape(nb, blk, HEAD_DIM), axis=1)
    kb_ref[...] = kf.astype(BF16)
    nt_dims = (((1,), (1,)), ((), ()))
    eye = (lax.broadcasted_iota(jnp.int32, (HEAD_DIM, HEAD_DIM), 0)
           == lax.broadcasted_iota(jnp.int32, (HEAD_DIM, HEAD_DIM), 1)).astype(BF16)
    vt_ref[:HEAD_DIM, :] = lax.dot_general(eye, v_ref[...].astype(BF16), nt_dims,
                                           preferred_element_type=F32).astype(BF16)
    pad_rows = vt_ref.shape[0] - HEAD_DIM
    vt_ref[HEAD_DIM:, :] = (lax.broadcasted_iota(jnp.int32, (pad_rows, vt_ref.shape[1]), 0)
                            == 0).astype(BF16)
    causal = (lax.broadcasted_iota(jnp.int32, (blk, blk), 0)
              <= lax.broadcasted_iota(jnp.int32, (blk, blk), 1))

    def scores(i):
        qi = q_ref[i * blk:(i + 1) * blk, :]
        qb = qi.astype(BF16)
        sel = None
        if i > MOBA_TOPK:
            route = lax.dot_general(kmean, qi, nt_dims, precision=lax.Precision.HIGHEST,
                                    preferred_element_type=F32)
            rc = [route[n:n + 1, :] for n in range(i)]
            sel = []
            for n in range(i):
                rank = jnp.zeros((1, blk), jnp.int32)
                for m in range(i):
                    if m == n:
                        continue
                    beats = (rc[m] >= rc[n]) if m < n else (rc[m] > rc[n])
                    rank = rank + beats.astype(jnp.int32)
                sel.append(rank < MOBA_TOPK)
        t_list = []
        for n in range(i + 1):
            t = lax.dot_general(kb_ref[n * blk:(n + 1) * blk, :], qb, nt_dims,
                                preferred_element_type=F32)
            if n == i:
                t = jnp.where(causal, t + bias_own[...], NEG_INF)
            else:
                if n == i - 1:
                    t = t + bias_prev[...]
                if sel is not None:
                    t = jnp.where(sel[n], t, NEG_INF)
            t_list.append(t)
        return t_list

    t_next = scores(0)
    for i in range(nb):
        t_list = t_next
        if i + 1 < nb:
            t_next = scores(i + 1)
        n_far = max(i - 1, 0)
        m_run = jnp.max(t_list[n_far], axis=0, keepdims=True)
        for t in t_list[n_far + 1:]:
            m_run = jnp.maximum(m_run, jnp.max(t, axis=0, keepdims=True))
        if n_far:
            m_far = jnp.max(t_list[0], axis=0, keepdims=True)
            for t in t_list[1:n_far]:
                m_far = jnp.maximum(m_far, jnp.max(t, axis=0, keepdims=True))
            m_run = jnp.maximum(m_run, m_far + bias_far)
        acc = jnp.zeros((vt_ref.shape[0], blk), F32)
        for n, t in enumerate(t_list):
            offset = m_run - bias_far if n < n_far else m_run
            p = jnp.exp2((t - offset) * scale_log2e)
            acc = acc + jnp.dot(vt_ref[:, n * blk:(n + 1) * blk], p.astype(BF16),
                                preferred_element_type=F32)
        out = acc[:HEAD_DIM] / acc[HEAD_DIM:HEAD_DIM + 1]
        o_ref[i * blk:(i + 1) * blk, :] = out.T.astype(BF16)


def _moba(proj, rel_bias, bko, bkp, bsz, seq):
    nb = seq // MOBA_BLOCK
    qo, ko, vo = COL_MOBA_Q // HEAD_DIM, COL_MOBA_K // HEAD_DIM, COL_MOBA_V // HEAD_DIM
    blk = MOBA_BLOCK
    return pl.pallas_call(
        functools.partial(_moba_kernel, nb=nb),
        name="moba",
        grid=(MOBA_HEADS, bsz),
        in_specs=[pl.BlockSpec(memory_space=pltpu.SMEM),
                  pl.BlockSpec((blk, blk), lambda h, b: (0, 0)),
                  pl.BlockSpec((blk, blk), lambda h, b: (0, 0)),
                  pl.BlockSpec((seq, HEAD_DIM), lambda h, b: (b, qo + h)),
                  pl.BlockSpec((seq, HEAD_DIM), lambda h, b: (b, ko + h)),
                  pl.BlockSpec((seq, HEAD_DIM), lambda h, b: (b, vo + h))],
        out_specs=pl.BlockSpec((seq, HEAD_DIM), lambda h, b: (b, h)),
        out_shape=jax.ShapeDtypeStruct((bsz * seq, MOBA_W), BF16),
        scratch_shapes=[pltpu.VMEM((blk, blk), F32), pltpu.VMEM((blk, blk), F32),
                        pltpu.VMEM((seq, HEAD_DIM), BF16),
                        pltpu.VMEM((HEAD_DIM + BF16_SUBLANES, seq), BF16)],
        compiler_params=_vmem("moba"),
    )(rel_bias, bko, bkp, proj, proj, proj)


def _conv_silu(xh, w):
    acc = xh[SUBLANES:, :] * w[GDN_CONV - 1:GDN_CONV, :]
    for s in range(1, GDN_CONV):
        acc = acc + pltpu.roll(xh, s, axis=0)[SUBLANES:, :] * w[GDN_CONV - 1 - s:GDN_CONV - s, :]
    return _silu(acc)


def _l2norm(x):
    return x * lax.rsqrt(jnp.sum(x * x, axis=-1, keepdims=True) + RMS_EPS)


def _softplus(x):
    return jnp.maximum(x, 0.0) + jnp.log1p(jnp.exp(-jnp.abs(x)))


ELIM_BLOCK = 8
MERGE_LEVELS = (GDN_CHUNK // ELIM_BLOCK).bit_length() - 1


def _block_diag_inverse(lpair):
    n = lpair.shape[0]
    nv = n // SUBLANES
    vpb = ELIM_BLOCK // SUBLANES
    rid = lax.broadcasted_iota(jnp.int32, (SUBLANES, 2 * n), 0)
    lid = lax.broadcasted_iota(jnp.int32, (SUBLANES, 2 * n), 1)
    t_rows = [(lid % n == rid + v * SUBLANES).astype(F32) for v in range(nv)]
    l_rows = [lpair[v * SUBLANES:(v + 1) * SUBLANES, :] for v in range(nv)]
    for m in range(n - 1):
        v0, s0 = divmod(m, SUBLANES)
        v_end = (v0 // vpb + 1) * vpb
        row = t_rows[v0][s0:s0 + 1, :]
        col = (lid // n) * n + m
        for v in range(v0 if s0 < SUBLANES - 1 else v0 + 1, v_end):
            t_rows[v] = t_rows[v] - jnp.take_along_axis(l_rows[v], col, axis=1) * row
    return jnp.concatenate(t_rows, axis=0)


def _lane_block_diag(pair):
    first = lax.broadcasted_iota(jnp.int32, pair.shape, 1) < pair.shape[1] // 2
    zero = jnp.zeros_like(pair)
    return jnp.concatenate([jnp.where(first, pair, zero), jnp.where(first, zero, pair)], axis=0)


def _merge_lower_products(lpair, tpair, k):
    n = lpair.shape[0]
    ri = lax.broadcasted_iota(jnp.int32, lpair.shape, 0)
    ci = lax.broadcasted_iota(jnp.int32, lpair.shape, 1) % n
    off = (ri // (2 * k) == ci // (2 * k)) & (ri // k > ci // k)
    lk = jnp.where(off, lpair, 0.0).astype(BF16)
    return jnp.dot(lk, _lane_block_diag(tpair.astype(BF16)), preferred_element_type=F32)


def _merge_apply(tpair, lt):
    return tpair - jnp.dot(tpair.astype(BF16), _lane_block_diag(lt), preferred_element_type=F32)


PIPELINE_UNROLL = 2


def _software_pipeline(stages, n):
    ns = len(stages)

    def run(it, lo, hi):
        conts = [stages[s](it - s) for s in reversed(range(lo, hi))]
        for cont in conts:
            if cont is not None:
                cont()

    for it in range(ns - 1):
        run(it, 0, it + 1)

    steady = n - (ns - 1)
    assert steady % PIPELINE_UNROLL == 0, (n, ns)

    def body(k, carry):
        first = ns - 1 + k * PIPELINE_UNROLL
        for u in range(PIPELINE_UNROLL):
            run(first + u, 0, ns)
        return carry

    lax.fori_loop(0, steady // PIPELINE_UNROLL, body, 0)
    for it in range(n, n + ns - 1):
        run(it, it - n + 1, ns)


def _lane_pick(x, lane, idx):
    return jnp.sum(jnp.where(lane == idx, x, 0.0), axis=-1, keepdims=True)


def _gdn_kernel(alog_ref, dtb_ref, gp_ref, nw_ref, cwq_ref, cwk_ref, cwv_ref,
                q_ref, k_ref, v_ref, z_ref, sm_ref, rw_ref, y_ref,
                kb_s, qn_s, kdp_s, rhs_s, qd_s, gcb_s, betab_s, gcr_s, gram_s, lm_s, lhs2_s,
                sol_s, mp_s, n_s, r_s, st_s, *merge_s, nchunk):
    tl_s, lt_s = merge_s[:MERGE_LEVELS + 1], merge_s[MERGE_LEVELS + 1:]
    hq = pl.program_id(1)
    c64 = GDN_CHUNK
    hd = HEAD_DIM
    heads = range(2)

    tri_u = (lax.broadcasted_iota(jnp.int32, (c64, c64), 0)
             <= lax.broadcasted_iota(jnp.int32, (c64, c64), 1)).astype(F32)
    gc_rows = []
    for j in heads:
        hv = 2 * hq + j
        a_neg_r = -jnp.exp(jnp.full((nchunk, c64), alog_ref[hv], F32))
        g_row = a_neg_r * _softplus(rw_ref[2 + j] + dtb_ref[hv])
        gc_rows.append(jnp.dot(g_row, tri_u, precision=lax.Precision.HIGHEST,
                               preferred_element_type=F32))
    gcr_s[...] = jnp.concatenate(gc_rows, axis=1)

    rows = lax.broadcasted_iota(jnp.int32, (c64, 2 * c64), 0)
    cols = lax.broadcasted_iota(jnp.int32, (c64, 2 * c64), 1) % c64
    tril = rows >= cols
    strict = rows > cols
    eye2 = (lax.broadcasted_iota(jnp.int32, (2 * c64, 2 * c64), 0)
            == lax.broadcasted_iota(jnp.int32, (2 * c64, 2 * c64), 1)).astype(BF16)
    nt_dims = (((1,), (1,)), ((), ()))

    def rows_of(c):
        start = c * c64
        return pl.ds(start if isinstance(c, int) else pl.multiple_of(start, c64), c64)

    def with_halo(x_ref, c):
        if isinstance(c, int) and c == 0:
            return jnp.concatenate([jnp.zeros((SUBLANES, x_ref.shape[1]), F32), x_ref[:c64, :]],
                                   axis=0)
        start = c * c64 - SUBLANES
        if not isinstance(c, int):
            start = pl.multiple_of(start, SUBLANES)
        return x_ref[pl.ds(start, c64 + SUBLANES), :]

    lane = lax.broadcasted_iota(jnp.int32, (c64, N_SMALL), 1)
    first = lane < c64
    pos = lax.broadcasted_iota(jnp.int32, (c64, N_SMALL), 0)

    def stage_prep(c):
        r = rows_of(c)
        qn = _l2norm(_conv_silu(with_halo(q_ref, c), cwq_ref[...])) * (hd ** -0.5)
        kn = _l2norm(_conv_silu(with_halo(k_ref, c), cwk_ref[...]))
        vc = _conv_silu(with_halo(v_ref, c), cwv_ref[...])
        qn_s[r, :] = qn.astype(BF16)
        kb_s[r, :] = kn.astype(BF16)
        sm = sm_ref[r, :]
        sig_all = jax.nn.sigmoid(sm)
        gc_all = -jnp.exp(gp_ref[0:1, :]) * _softplus(sm + gp_ref[1:2, :])
        sft = 1
        while sft < c64:
            gc_all = gc_all + jnp.where(pos >= sft, pltpu.roll(gc_all, sft, axis=0), 0.0)
            sft *= 2
        rest_all = gc_all[c64 - 1:, :] - gc_all
        betas, gc_cols = [], []
        for j in heads:
            hv = 2 * hq + j
            beta = _lane_pick(sig_all, lane, hv)
            gc_col = _lane_pick(gc_all, lane, GDN_V_HEADS + hv)
            rest_col = _lane_pick(rest_all, lane, GDN_V_HEADS + hv)
            eg = jnp.exp(gc_col)
            rhs_s[j, r, :hd] = (vc[:, j * hd:(j + 1) * hd] * beta).astype(BF16)
            rhs_s[j, r, hd:] = (kn * (beta * eg)).astype(BF16)
            qd_s[j, r, :] = (qn * eg).astype(BF16)
            kdp_s[c, j * c64:(j + 1) * c64, :] = (kn * jnp.exp(rest_col)).astype(BF16)
            betas.append(beta)
            gc_cols.append(gc_col)
        gcb_s[r, :] = jnp.where(first, gc_cols[0], gc_cols[1])
        betab_s[r, :] = jnp.where(first, betas[0], betas[1])

    def stage_gram(c):
        r = rows_of(c)
        kb = kb_s[r, :]
        kq = jnp.concatenate([kb, qn_s[r, :]], axis=0)
        gram = lax.dot_general(kq, jnp.concatenate([kb, kb], axis=0), nt_dims,
                               preferred_element_type=F32)
        kd_t = lax.dot_general(eye2, kdp_s[c], nt_dims, preferred_element_type=F32)

        def finish():
            gram_s[c] = gram
            lhs2_s[c, :2 * c64, :] = kd_t.astype(BF16)

        return finish

    def stage_factor(c):
        r = rows_of(c)
        dec = jnp.exp(jnp.where(tril, gcb_s[r, :] - gcr_s[pl.ds(c, 1), :], NEG_INF))
        lpair = jnp.where(strict, gram_s[c, :c64, :] * dec, 0.0) * betab_s[r, :]
        lm_s[r, :] = lpair
        lhs2_s[c, 2 * c64:, :] = (gram_s[c, c64:, :] * dec).astype(BF16)
        tl_s[0][r, :] = _block_diag_inverse(lpair)

    def stage_merge_products(level):
        def stage(c):
            r = rows_of(c)
            lt = _merge_lower_products(lm_s[r, :], tl_s[level][r, :], ELIM_BLOCK << level)

            def finish():
                lt_s[level][r, :] = lt.astype(BF16)

            return finish
        return stage

    def stage_merge_apply(level):
        def stage(c):
            r = rows_of(c)
            merged = _merge_apply(tl_s[level][r, :], lt_s[level][r, :])

            def finish():
                tl_s[level + 1][r, :] = merged

            return finish
        return stage

    def stage_solve(c):
        r = rows_of(c)
        zero = jnp.zeros((c64, 2 * hd), BF16)
        rhs_bd = jnp.concatenate([jnp.concatenate([rhs_s[0, r, :], zero], axis=1),
                                  jnp.concatenate([zero, rhs_s[1, r, :]], axis=1)], axis=0)
        sol = jnp.dot(tl_s[MERGE_LEVELS][r, :].astype(BF16), rhs_bd,
                      preferred_element_type=F32)

        def finish():
            sol_s[r, :] = sol.astype(BF16)

        return finish

    def stage_fold(c):
        r = rows_of(c)
        sol = sol_s[r, :]
        out = jnp.dot(lhs2_s[c], _lane_block_diag(sol), preferred_element_type=F32)

        def finish():
            for j in heads:
                u_col, w_col = 2 * j * hd, (2 * j + 1) * hd
                n_s[j, c] = out[:2 * c64, u_col:u_col + hd]
                mp_s[j, c, :2 * c64, :] = out[:2 * c64, w_col:w_col + hd].astype(BF16)
                mp_s[j, c, 2 * c64:, :] = (qd_s[j, r, :].astype(F32)
                                           - out[2 * c64:, w_col:w_col + hd]).astype(BF16)
                r_s[j, r, :] = out[2 * c64:, u_col:u_col + hd]

        return finish

    st_s[...] = jnp.zeros_like(st_s)
    nw = nw_ref[...]

    def stage_state(c):
        r = rows_of(c)
        sts = [st_s[j] for j in heads]
        outs = [jnp.dot(mp_s[j, c], sts[j].astype(BF16), preferred_element_type=F32)
                for j in heads]

        def finish():
            for j in heads:
                end = (j + 1) * c64
                gl = gcr_s[pl.ds(c, 1), end - 1:end]
                st_s[j] = sts[j] * jnp.exp(gl) - outs[j][:2 * c64] + n_s[j, c]
                r_s[j, r, :] = r_s[j, r, :] + outs[j][2 * c64:]

        return finish

    def stage_norm(c):
        r = rows_of(c)
        for j in heads:
            o = r_s[j, r, :]
            og = (o * lax.rsqrt(jnp.mean(o * o, axis=-1, keepdims=True) + RMS_EPS)
                  * nw * _silu(z_ref[r, j * hd:(j + 1) * hd]))
            y_ref[r, j * hd:(j + 1) * hd] = og.astype(BF16)

    merges = [stage(lv) for lv in range(MERGE_LEVELS)
              for stage in (stage_merge_products, stage_merge_apply)]
    _software_pipeline([stage_prep, stage_gram, stage_factor] + merges
                       + [stage_solve, stage_fold, stage_state, stage_norm], nchunk)


def _gdn(proj, small, rows, conv_w, a_log, dt_bias, norm_w, bsz, seq):
    nchunk = seq // GDN_CHUNK
    hd = HEAD_DIM
    c64 = GDN_CHUNK
    qo, ko = COL_GDN_Q // hd, COL_GDN_K // hd
    vo, zo = COL_GDN_V // (2 * hd), COL_GDN_Z // (2 * hd)
    cvo = (2 * GDN_QK_W) // (2 * hd)
    smem = pl.BlockSpec(memory_space=pltpu.SMEM)
    pad = (GDN_V_HEADS, N_SMALL - 2 * GDN_V_HEADS)
    gate_params = jnp.stack([jnp.pad(a_log, pad), jnp.pad(dt_bias, pad)])
    return pl.pallas_call(
        functools.partial(_gdn_kernel, nchunk=nchunk),
        name="gdn",
        grid=(bsz, GDN_QK_HEADS),
        in_specs=[smem, smem,
                  pl.BlockSpec((2, N_SMALL), lambda b, h: (0, 0)),
                  pl.BlockSpec((1, hd), lambda b, h: (0, 0)),
                  pl.BlockSpec((GDN_CONV, hd), lambda b, h: (0, h)),
                  pl.BlockSpec((GDN_CONV, hd), lambda b, h: (0, GDN_QK_HEADS + h)),
                  pl.BlockSpec((GDN_CONV, 2 * hd), lambda b, h: (0, cvo + h)),
                  pl.BlockSpec((seq, hd), lambda b, h: (b, qo + h)),
                  pl.BlockSpec((seq, hd), lambda b, h: (b, ko + h)),
                  pl.BlockSpec((seq, 2 * hd), lambda b, h: (b, vo + h)),
                  pl.BlockSpec((seq, 2 * hd), lambda b, h: (b, zo + h)),
                  pl.BlockSpec((seq, N_SMALL), lambda b, h: (b, 0)),
                  pl.BlockSpec((None, None, 4, nchunk, GDN_CHUNK), lambda b, h: (b, h, 0, 0, 0))],
        out_specs=pl.BlockSpec((seq, 2 * hd), lambda b, h: (b, h)),
        out_shape=jax.ShapeDtypeStruct((bsz * seq, GDN_V_W), BF16),
        scratch_shapes=[pltpu.VMEM((seq, hd), BF16),
                        pltpu.VMEM((seq, hd), BF16),
                        pltpu.VMEM((nchunk, 2 * c64, hd), BF16),
                        pltpu.VMEM((2, seq, 2 * hd), BF16),
                        pltpu.VMEM((2, seq, hd), BF16),
                        pltpu.VMEM((seq, 2 * c64), F32),
                        pltpu.VMEM((seq, 2 * c64), F32),
                        pltpu.VMEM((nchunk, 2 * c64), F32),
                        pltpu.VMEM((nchunk, 2 * c64, 2 * c64), F32),
                        pltpu.VMEM((seq, 2 * c64), F32),
                        pltpu.VMEM((nchunk, 3 * c64, hd), BF16),
                        pltpu.VMEM((seq, 4 * hd), BF16),
                        pltpu.VMEM((2, nchunk, 3 * c64, hd), BF16),
                        pltpu.VMEM((2, nchunk, hd, hd), F32),
                        pltpu.VMEM((2, seq, hd), F32),
                        pltpu.VMEM((2, hd, hd), F32)]
                       + [pltpu.VMEM((seq, 2 * c64), F32)] * (MERGE_LEVELS + 1)
                       + [pltpu.VMEM((seq, 2 * c64), BF16)] * MERGE_LEVELS,
        compiler_params=_vmem("gdn"),
    )(a_log, dt_bias, gate_params, norm_w, conv_w, conv_w, conv_w, proj, proj, proj, proj, small, rows)


def _merge_kernel(ya_ref, yb_ref, ga_ref, gb_ref, x_ref, g1_ref, sh2_ref, sc2_ref,
                  lng_ref, lnb_ref, wpm_ref, wpg_ref, wo_ref, x1_ref, h2_ref):
    pa = jnp.dot(ya_ref[...], wpm_ref[...], preferred_element_type=F32)
    pb = jnp.dot(yb_ref[...], wpg_ref[...], preferred_element_type=F32)
    merged = jax.nn.sigmoid(ga_ref[...]) * pa + jax.nn.sigmoid(gb_ref[...]) * pb
    y = jnp.dot(merged.astype(BF16), wo_ref[...], preferred_element_type=F32)
    x1 = _layer_norm(DEEPNORM_ALPHA * x_ref[...] + g1_ref[...] * y, lng_ref[...], lnb_ref[...])
    x1_ref[...] = x1
    h2_ref[...] = (x1 * (1.0 + sc2_ref[...]) + sh2_ref[...]).astype(BF16)


def _merge(ya, yb, proj, x2, mod3, ln_g, ln_b, wpm, wpg, wo, seq):
    t = x2.shape[0]
    tm = MERGE_TM
    per_b = seq // tm
    d = D_MODEL

    def modspec(k):
        return pl.BlockSpec((None, 1, d), lambda i: (i // per_b, 0, k))

    def const(shape):
        return pl.BlockSpec(shape, lambda i: (0, 0), pipeline_mode=pl.Buffered(1))

    return pl.pallas_call(
        _merge_kernel,
        name="merge",
        grid=(t // tm,),
        in_specs=[pl.BlockSpec((tm, MOBA_W), lambda i: (i, 0)),
                  pl.BlockSpec((tm, GDN_V_W), lambda i: (i, 0)),
                  pl.BlockSpec((tm, d), lambda i: (i, COL_GATE_A // d)),
                  pl.BlockSpec((tm, d), lambda i: (i, COL_GATE_B // d)),
                  pl.BlockSpec((tm, d), lambda i: (i, 0)),
                  modspec(2), modspec(3), modspec(4),
                  const((1, d)), const((1, d)),
                  const((MOBA_W, d)), const((GDN_V_W, d)), const((d, d))],
        out_specs=[pl.BlockSpec((tm, d), lambda i: (i, 0)),
                   pl.BlockSpec((tm, d), lambda i: (i, 0))],
        out_shape=[jax.ShapeDtypeStruct((t, d), F32), jax.ShapeDtypeStruct((t, d), BF16)],
        compiler_params=_vmem("merge"),
    )(ya, yb, proj, proj, x2, mod3, mod3, mod3, ln_g, ln_b, wpm, wpg, wo)


def _ffn_kernel(h_ref, x1_ref, g2_ref, lng_ref, lnb_ref, wg_ref, wu_ref, wo_ref, o_ref, acc_ref):
    f = pl.program_id(1)

    @pl.when(f == 0)
    def _():
        acc_ref[...] = jnp.zeros_like(acc_ref)

    h = h_ref[...]
    gate = jnp.dot(h, wg_ref[...], preferred_element_type=F32)
    up = jnp.dot(h, wu_ref[...], preferred_element_type=F32)
    act = (_silu(gate) * up).astype(BF16)
    acc_ref[...] += jnp.dot(act, wo_ref[...], preferred_element_type=F32)

    @pl.when(f == pl.num_programs(1) - 1)
    def _():
        r = DEEPNORM_ALPHA * x1_ref[...] + g2_ref[...] * acc_ref[...]
        o_ref[...] = _layer_norm(r, lng_ref[...], lnb_ref[...])


def _ffn(h2, x1, mod3, ln_g, ln_b, w_in, w_out, seq):
    t = h2.shape[0]
    tm, tf = FFN_TM, FFN_TF
    per_b = seq // tm
    d = D_MODEL
    nf = D_FF // tf
    return pl.pallas_call(
        _ffn_kernel,
        name="ffn",
        grid=(t // tm, nf),
        in_specs=[pl.BlockSpec((tm, d), lambda i, f: (i, 0)),
                  pl.BlockSpec((tm, d), lambda i, f: (i, 0)),
                  pl.BlockSpec((None, 1, d), lambda i, f: (i // per_b, 0, 5)),
                  pl.BlockSpec((1, d), lambda i, f: (0, 0)),
                  pl.BlockSpec((1, d), lambda i, f: (0, 0)),
                  pl.BlockSpec((d, tf), lambda i, f: (0, f)),
                  pl.BlockSpec((d, tf), lambda i, f: (0, nf + f)),
                  pl.BlockSpec((tf, d), lambda i, f: (f, 0))],
        out_specs=pl.BlockSpec((tm, d), lambda i, f: (i, 0)),
        out_shape=jax.ShapeDtypeStruct((t, d), F32),
        scratch_shapes=[pltpu.VMEM((tm, d), F32)],
        compiler_params=_vmem("ffn"),
    )(h2, x1, mod3, ln_g, ln_b, w_in, w_in, w_out)


def _rel_bucket(dist):
    max_exact = REL_BUCKETS // 2
    n = jnp.maximum(dist, 0)
    nf = jnp.maximum(n, 1).astype(F32)
    large = max_exact + (jnp.log(nf / max_exact) / math.log(REL_MAX_DIST / max_exact)
                         * (REL_BUCKETS - max_exact)).astype(jnp.int32)
    large = jnp.minimum(large, REL_BUCKETS - 1)
    return jnp.where(n < max_exact, n, large)


def _layer(x, c, w_ada, b_ada, w_in, conv_w, a_log, dt_bias, gdn_norm_w, rel_bias,
           w_proj_moba, w_proj_gdn, w_out, ln1_g, ln1_b, w_ffn_in, w_ffn_out, ln2_g, ln2_b):
    bsz, seq, d = x.shape
    t = bsz * seq
    x2 = x.reshape(t, d)

    mod = _ada_mod(c, w_ada, b_ada)
    mod3 = mod.reshape(bsz, 1, 6 * d)

    w_all = w_in.astype(BF16)
    n_gates = 2 * GDN_V_HEADS
    w_gate = w_all[:, N_HEAD + n_gates:]
    w_small = jnp.pad(w_all[:, N_HEAD:N_HEAD + n_gates], ((0, 0), (0, N_SMALL - n_gates)))

    proj, small = _in_proj(x2, mod3, w_all, w_gate, w_small, seq)

    ii = jnp.arange(MOBA_BLOCK, dtype=jnp.int32)
    dist = ii[None, :] - ii[:, None]
    bko = _rel_bucket(dist)
    bkp = _rel_bucket(dist + MOBA_BLOCK)
    ya = _moba(proj, rel_bias, bko, bkp, bsz, seq)

    nchunk = seq // GDN_CHUNK
    sm_t = small[:, :2 * GDN_V_HEADS].reshape(bsz, seq, 2, GDN_QK_HEADS, 2)
    rows = sm_t.transpose(0, 3, 2, 4, 1).reshape(bsz, GDN_QK_HEADS, 4, nchunk, GDN_CHUNK)
    yb = _gdn(proj, small, rows, conv_w, a_log, dt_bias, gdn_norm_w.reshape(1, HEAD_DIM),
              bsz, seq)

    x1, h2 = _merge(ya, yb, proj, x2, mod3, ln1_g.reshape(1, d), ln1_b.reshape(1, d),
                    w_proj_moba.astype(BF16), w_proj_gdn.astype(BF16), w_out.astype(BF16), seq)
    out = _ffn(h2, x1, mod3, ln2_g.reshape(1, d), ln2_b.reshape(1, d),
               w_ffn_in.astype(BF16), w_ffn_out.astype(BF16), seq)
    return out.reshape(bsz, seq, d)


def kernel(x, c, w_ada, b_ada, w_in, conv_w, a_log, dt_bias, gdn_norm_w, rel_bias, w_proj_moba,
           w_proj_gdn, w_out, ln1_g, ln1_b, w_ffn_in, w_ffn_out, ln2_g, ln2_b):
    depth = w_ada.shape[0]
    for l in range(depth):
        x = _layer(x, c, w_ada[l], b_ada[l], w_in[l], conv_w[l], a_log[l], dt_bias[l],
                   gdn_norm_w[l], rel_bias, w_proj_moba[l], w_proj_gdn[l], w_out[l],
                   ln1_g[l], ln1_b[l], w_ffn_in[l], w_ffn_out[l], ln2_g[l], ln2_b[l])
    return x
```

```python
import functools
import math

import jax
import jax.numpy as jnp
from jax import lax
from jax.experimental import pallas as pl
from jax.experimental.pallas import tpu as pltpu

F32 = jnp.float32
BF16 = jnp.bfloat16

D_MODEL = 2048
MOBA_HEADS = 8
HEAD_DIM = 128
MOBA_W = MOBA_HEADS * HEAD_DIM
MOBA_BLOCK = 256
MOBA_TOPK = 3
REL_BUCKETS = 32
REL_MAX_DIST = 128
GDN_QK_HEADS = 8
GDN_V_HEADS = 16
GDN_QK_W = GDN_QK_HEADS * HEAD_DIM
GDN_V_W = GDN_V_HEADS * HEAD_DIM
GDN_CONV = 4
GDN_CHUNK = 64
D_FF = 5632
DEEPNORM_ALPHA = 2.0 ** 0.25
LN_EPS = 1e-5
RMS_EPS = 1e-6
NEG_INF = -1e30

COL_GATE_A = 0
COL_GATE_B = 2048
COL_MOBA_Q = 4096
COL_MOBA_K = 5120
COL_MOBA_V = 6144
COL_GDN_Q = 7168
COL_GDN_K = 8192
COL_GDN_V = 9216
COL_GDN_Z = 11264
N_MAIN = 13312
N_GATE = 4096
N_HEAD = N_MAIN - N_GATE
N_SMALL = 128

V7X_VMEM_MIB = 64
SUBLANES = 8
BF16_SUBLANES = 16

ADA_TN = 1024
INPROJ_TM, INPROJ_TN = 1024, 1024
MERGE_TM = 256
FFN_TM, FFN_TF = 512, 512
VMEM_LIMIT_MIB = {"ada_mod": 40, "in_proj": 56, "moba": 48, "gdn": 58, "merge": 56, "ffn": 48}
assert max(VMEM_LIMIT_MIB.values()) < V7X_VMEM_MIB


def _vmem(name):
    return pltpu.CompilerParams(vmem_limit_bytes=VMEM_LIMIT_MIB[name] * 1024 * 1024)


def _silu(x):
    return x * jax.nn.sigmoid(x)


def _layer_norm(r, gain, bias):
    mu = jnp.mean(r, axis=-1, keepdims=True)
    d = r - mu
    var = jnp.mean(d * d, axis=-1, keepdims=True)
    return d * lax.rsqrt(var + LN_EPS) * gain + bias


def _ada_kernel(c_ref, w_ref, b_ref, o_ref):
    sc = _silu(c_ref[...])
    o_ref[...] = jnp.dot(sc, w_ref[...], precision=lax.Precision.HIGHEST,
                         preferred_element_type=F32) + b_ref[...]


def _ada_mod(c, w_ada, b_ada):
    bsz = c.shape[0]
    n = w_ada.shape[1]
    tn = ADA_TN
    return pl.pallas_call(
        _ada_kernel,
        name="ada_mod",
        grid=(n // tn,),
        in_specs=[pl.BlockSpec((bsz, D_MODEL), lambda j: (0, 0)),
                  pl.BlockSpec((D_MODEL, tn), lambda j: (0, j)),
                  pl.BlockSpec((1, tn), lambda j: (0, j))],
        out_specs=pl.BlockSpec((bsz, tn), lambda j: (0, j)),
        out_shape=jax.ShapeDtypeStruct((bsz, n), F32),
        compiler_params=_vmem("ada_mod"),
    )(c, w_ada, b_ada.reshape(1, n))


def _inproj_kernel(x_ref, sh_ref, sc_ref, w_ref, wg_ref, ws_ref, o_ref, os_ref, h_ref, *, n_head):
    j = pl.program_id(1)

    @pl.when(j == 0)
    def _():
        h = x_ref[...] * (1.0 + sc_ref[...]) + sh_ref[...]
        h_ref[...] = h.astype(BF16)
        os_ref[...] = jnp.dot(h_ref[...], ws_ref[...], preferred_element_type=F32)

    @pl.when(j < n_head)
    def _():
        o_ref[...] = jnp.dot(h_ref[...], w_ref[...], preferred_element_type=F32)

    @pl.when(j >= n_head)
    def _():
        o_ref[...] = jnp.dot(h_ref[...], wg_ref[...], preferred_element_type=F32)


def _in_proj(x2, mod3, w_all, w_gate, w_small, seq):
    t = x2.shape[0]
    tm, tn = INPROJ_TM, INPROJ_TN
    per_b = seq // tm
    n_head, n_gate, n_tiles = N_HEAD // tn, N_GATE // tn, N_MAIN // tn
    return pl.pallas_call(
        functools.partial(_inproj_kernel, n_head=n_head),
        name="in_proj",
        grid=(t // tm, n_tiles),
        in_specs=[pl.BlockSpec((tm, D_MODEL), lambda i, j: (i, 0)),
                  pl.BlockSpec((None, 1, D_MODEL), lambda i, j: (i // per_b, 0, 0)),
                  pl.BlockSpec((None, 1, D_MODEL), lambda i, j: (i // per_b, 0, 1)),
                  pl.BlockSpec((D_MODEL, tn), lambda i, j: (0, jnp.minimum(j, n_head - 1))),
                  pl.BlockSpec((D_MODEL, tn), lambda i, j: (0, jnp.maximum(j - n_head, 0))),
                  pl.BlockSpec((D_MODEL, N_SMALL), lambda i, j: (0, 0))],
        out_specs=[pl.BlockSpec((tm, tn), lambda i, j: (i, (j + n_gate) % n_tiles)),
                   pl.BlockSpec((tm, N_SMALL), lambda i, j: (i, 0))],
        out_shape=[jax.ShapeDtypeStruct((t, N_MAIN), F32),
                   jax.ShapeDtypeStruct((t, N_SMALL), F32)],
        scratch_shapes=[pltpu.VMEM((tm, D_MODEL), BF16)],
        compiler_params=_vmem("in_proj"),
    )(x2, mod3, mod3, w_all, w_gate, w_small)


def _moba_kernel(rel_ref, bko_ref, bkp_ref, q_ref, k_ref, v_ref, o_ref,
                 bias_own, bias_prev, kb_ref, vt_ref, *, nb):
    h = pl.program_id(0)
    blk = MOBA_BLOCK
    inv_scale = HEAD_DIM ** 0.5
    scale_log2e = HEAD_DIM ** -0.5 * math.log2(math.e)

    @pl.when(pl.program_id(1) == 0)
    def _():
        bo = bko_ref[...]
        bp = bkp_ref[...]
        acc_o = jnp.zeros((blk, blk), F32)
        acc_p = jnp.zeros((blk, blk), F32)
        for kk in range(REL_BUCKETS):
            val = rel_ref[kk, h] * inv_scale
            acc_o = jnp.where(bo == kk, val, acc_o)
            acc_p = jnp.where(bp == kk, val, acc_p)
        bias_own[...] = acc_o
        bias_prev[...] = acc_p

    bias_far = rel_ref[REL_BUCKETS - 1, h] * inv_scale
    kf = k_ref[...]
    kmean = jnp.mean(kf.reshape(nb, blk, HEAD_DIM), axis=1)
    kb_ref[...] = kf.astype(BF16)
    nt_dims = (((1,), (1,)), ((), ()))
    eye = (lax.broadcasted_iota(jnp.int32, (HEAD_DIM, HEAD_DIM), 0)
           == lax.broadcasted_iota(jnp.int32, (HEAD_DIM, HEAD_DIM), 1)).astype(BF16)
    vt_ref[:HEAD_DIM, :] = lax.dot_general(eye, v_ref[...].astype(BF16), nt_dims,
                                           preferred_element_type=F32).astype(BF16)
    pad_rows = vt_ref.shape[0] - HEAD_DIM
    vt_ref[HEAD_DIM:, :] = (lax.broadcasted_iota(jnp.int32, (pad_rows, vt_ref.shape[1]), 0)
                            == 0).astype(BF16)
    causal = (lax.broadcasted_iota(jnp.int32, (blk, blk), 0)
              <= lax.broadcasted_iota(jnp.int32, (blk, blk), 1))

    def scores(i):
        qi = q_ref[i * blk:(i + 1) * blk, :]
        qb = qi.astype(BF16)
        sel = None
        if i > MOBA_TOPK:
            route = lax.dot_general(kmean, qi, nt_dims, precision=lax.Precision.HIGHEST,
                                    preferred_element_type=F32)
            rc = [route[n:n + 1, :] for n in range(i)]
            sel = []
            for n in range(i):
                rank = jnp.zeros((1, blk), jnp.int32)
                for m in range(i):
                    if m == n:
                        continue
                    beats = (rc[m] >= rc[n]) if m < n else (rc[m] > rc[n])
                    rank = rank + beats.astype(jnp.int32)
                sel.append(rank < MOBA_TOPK)
        t_list = []
        for n in range(i + 1):
            t = lax.dot_general(kb_ref[n * blk:(n + 1) * blk, :], qb, nt_dims,
                                preferred_element_type=F32)
            if n == i:
                t = jnp.where(causal, t + bias_own[...], NEG_INF)
            else:
                if n == i - 1:
                    t = t + bias_prev[...]
                if sel is not None:
                    t = jnp.where(sel[n], t, NEG_INF)
            t_list.append(t)
        return t_list

    t_next = scores(0)
    for i in range(nb):
        t_list = t_next
        if i + 1 < nb:
            t_next = scores(i + 1)
        n_far = max(i - 1, 0)
        m_run = jnp.max(t_list[n_far], axis=0, keepdims=True)
        for t in t_list[n_far + 1:]:
            m_run = jnp.maximum(m_run, jnp.max(t, axis=0, keepdims=True))
        if n_far:
            m_far = jnp.max(t_list[0], axis=0, keepdims=True)
            for t in t_list[1:n_far]:
                m_far = jnp.maximum(m_far, jnp.max(t, axis=0, keepdims=True))
            m_run = jnp.maximum(m_run, m_far + bias_far)
        acc = jnp.zeros((vt_ref.shape[0], blk), F32)
        for n, t in enumerate(t_list):
            offset = m_run - bias_far if n < n_far else m_run
            p = jnp.exp2((t - offset) * scale_log2e)
            acc = acc + jnp.dot(vt_ref[:, n * blk:(n + 1) * blk], p.astype(BF16),
                                preferred_element_type=F32)
        out = acc[:HEAD_DIM] / acc[HEAD_DIM:HEAD_DIM + 1]
        o_ref[i * blk:(i + 1) * blk, :] = out.T.astype(BF16)


def _moba(proj, rel_bias, bko, bkp, bsz, seq):
    nb = seq // MOBA_BLOCK
    qo, ko, vo = COL_MOBA_Q // HEAD_DIM, COL_MOBA_K // HEAD_DIM, COL_MOBA_V // HEAD_DIM
    blk = MOBA_BLOCK
    return pl.pallas_call(
        functools.partial(_moba_kernel, nb=nb),
        name="moba",
        grid=(MOBA_HEADS, bsz),
        in_specs=[pl.BlockSpec(memory_space=pltpu.SMEM),
                  pl.BlockSpec((blk, blk), lambda h, b: (0, 0)),
                  pl.BlockSpec((blk, blk), lambda h, b: (0, 0)),
                  pl.BlockSpec((seq, HEAD_DIM), lambda h, b: (b, qo + h)),
                  pl.BlockSpec((seq, HEAD_DIM), lambda h, b: (b, ko + h)),
                  pl.BlockSpec((seq, HEAD_DIM), lambda h, b: (b, vo + h))],
        out_specs=pl.BlockSpec((seq, HEAD_DIM), lambda h, b: (b, h)),
        out_shape=jax.ShapeDtypeStruct((bsz * seq, MOBA_W), BF16),
        scratch_shapes=[pltpu.VMEM((blk, blk), F32), pltpu.VMEM((blk, blk), F32),
                        pltpu.VMEM((seq, HEAD_DIM), BF16),
                        pltpu.VMEM((HEAD_DIM + BF16_SUBLANES, seq), BF16)],
        compiler_params=_vmem("moba"),
    )(rel_bias, bko, bkp, proj, proj, proj)


def _conv_silu(xh, w):
    acc = xh[SUBLANES:, :] * w[GDN_CONV - 1:GDN_CONV, :]
    for s in range(1, GDN_CONV):
        acc = acc + pltpu.roll(xh, s, axis=0)[SUBLANES:, :] * w[GDN_CONV - 1 - s:GDN_CONV - s, :]
    return _silu(acc)


def _l2norm(x):
    return x * lax.rsqrt(jnp.sum(x * x, axis=-1, keepdims=True) + RMS_EPS)


def _softplus(x):
    return jnp.maximum(x, 0.0) + jnp.log1p(jnp.exp(-jnp.abs(x)))


ELIM_BLOCK = 8
MERGE_LEVELS = (GDN_CHUNK // ELIM_BLOCK).bit_length() - 1


def _block_diag_inverse(lpair):
    n = lpair.shape[0]
    nv = n // SUBLANES
    vpb = ELIM_BLOCK // SUBLANES
    rid = lax.broadcasted_iota(jnp.int32, (SUBLANES, 2 * n), 0)
    lid = lax.broadcasted_iota(jnp.int32, (SUBLANES, 2 * n), 1)
    t_rows = [(lid % n == rid + v * SUBLANES).astype(F32) for v in range(nv)]
    l_rows = [lpair[v * SUBLANES:(v + 1) * SUBLANES, :] for v in range(nv)]
    for m in range(n - 1):
        v0, s0 = divmod(m, SUBLANES)
        v_end = (v0 // vpb + 1) * vpb
        row = t_rows[v0][s0:s0 + 1, :]
        col = (lid // n) * n + m
        for v in range(v0 if s0 < SUBLANES - 1 else v0 + 1, v_end):
            t_rows[v] = t_rows[v] - jnp.take_along_axis(l_rows[v], col, axis=1) * row
    return jnp.concatenate(t_rows, axis=0)


def _lane_block_diag(pair):
    first = lax.broadcasted_iota(jnp.int32, pair.shape, 1) < pair.shape[1] // 2
    zero = jnp.zeros_like(pair)
    return jnp.concatenate([jnp.where(first, pair, zero), jnp.where(first, zero, pair)], axis=0)


def _merge_lower_products(lpair, tpair, k):
    n = lpair.shape[0]
    ri = lax.broadcasted_iota(jnp.int32, lpair.shape, 0)
    ci = lax.broadcasted_iota(jnp.int32, lpair.shape, 1) % n
    off = (ri // (2 * k) == ci // (2 * k)) & (ri // k > ci // k)
    lk = jnp.where(off, lpair, 0.0).astype(BF16)
    return jnp.dot(lk, _lane_block_diag(tpair.astype(BF16)), preferred_element_type=F32)


def _merge_apply(tpair, lt):
    return tpair - jnp.dot(tpair.astype(BF16), _lane_block_diag(lt), preferred_element_type=F32)


PIPELINE_UNROLL = 4


def _software_pipeline(stages, n):
    ns = len(stages)

    def run(it, lo, hi):
        conts = [stages[s](it - s) for s in reversed(range(lo, hi))]
        for cont in conts:
            if cont is not None:
                cont()

    for it in range(ns - 1):
        run(it, 0, it + 1)

    steady = n - (ns - 1)
    assert steady % PIPELINE_UNROLL == 0, (n, ns)

    def body(k, carry):
        first = ns - 1 + k * PIPELINE_UNROLL
        for u in range(PIPELINE_UNROLL):
            run(first + u, 0, ns)
        return carry

    lax.fori_loop(0, steady // PIPELINE_UNROLL, body, 0)
    for it in range(n, n + ns - 1):
        run(it, it - n + 1, ns)


def _lane_pick(x, lane, idx):
    return jnp.sum(jnp.where(lane == idx, x, 0.0), axis=-1, keepdims=True)


def _gdn_kernel(alog_ref, dtb_ref, gp_ref, nw_ref, cwq_ref, cwk_ref, cwv_ref,
                q_ref, k_ref, v_ref, z_ref, sm_ref, rw_ref, y_ref,
                kb_s, qn_s, kdp_s, rhs_s, qd_s, gcb_s, betab_s, gcr_s, gram_s, lm_s, lhs2_s,
                sol_s, mp_s, n_s, r_s, st_s, *merge_s, nchunk):
    tl_s, lt_s = merge_s[:MERGE_LEVELS + 1], merge_s[MERGE_LEVELS + 1:]
    hq = pl.program_id(1)
    c64 = GDN_CHUNK
    hd = HEAD_DIM
    heads = range(2)

    tri_u = (lax.broadcasted_iota(jnp.int32, (c64, c64), 0)
             <= lax.broadcasted_iota(jnp.int32, (c64, c64), 1)).astype(F32)
    gc_rows = []
    for j in heads:
        hv = 2 * hq + j
        a_neg_r = -jnp.exp(jnp.full((nchunk, c64), alog_ref[hv], F32))
        g_row = a_neg_r * _softplus(rw_ref[2 + j] + dtb_ref[hv])
        gc_rows.append(jnp.dot(g_row, tri_u, precision=lax.Precision.HIGHEST,
                               preferred_element_type=F32))
    gcr_s[...] = jnp.concatenate(gc_rows, axis=1)

    rows = lax.broadcasted_iota(jnp.int32, (c64, 2 * c64), 0)
    cols = lax.broadcasted_iota(jnp.int32, (c64, 2 * c64), 1) % c64
    tril = rows >= cols
    strict = rows > cols
    eye2 = (lax.broadcasted_iota(jnp.int32, (2 * c64, 2 * c64), 0)
            == lax.broadcasted_iota(jnp.int32, (2 * c64, 2 * c64), 1)).astype(BF16)
    nt_dims = (((1,), (1,)), ((), ()))

    def rows_of(c):
        start = c * c64
        return pl.ds(start if isinstance(c, int) else pl.multiple_of(start, c64), c64)

    def with_halo(x_ref, c):
        if isinstance(c, int) and c == 0:
            return jnp.concatenate([jnp.zeros((SUBLANES, x_ref.shape[1]), F32), x_ref[:c64, :]],
                                   axis=0)
        start = c * c64 - SUBLANES
        if not isinstance(c, int):
            start = pl.multiple_of(start, SUBLANES)
        return x_ref[pl.ds(start, c64 + SUBLANES), :]

    lane = lax.broadcasted_iota(jnp.int32, (c64, N_SMALL), 1)
    first = lane < c64
    pos = lax.broadcasted_iota(jnp.int32, (c64, N_SMALL), 0)

    def stage_prep(c):
        r = rows_of(c)
        qn = _l2norm(_conv_silu(with_halo(q_ref, c), cwq_ref[...])) * (hd ** -0.5)
        kn = _l2norm(_conv_silu(with_halo(k_ref, c), cwk_ref[...]))
        vc = _conv_silu(with_halo(v_ref, c), cwv_ref[...])
        qn_s[r, :] = qn.astype(BF16)
        kb_s[r, :] = kn.astype(BF16)
        sm = sm_ref[r, :]
        sig_all = jax.nn.sigmoid(sm)
        gc_all = -jnp.exp(gp_ref[0:1, :]) * _softplus(sm + gp_ref[1:2, :])
        sft = 1
        while sft < c64:
            gc_all = gc_all + jnp.where(pos >= sft, pltpu.roll(gc_all, sft, axis=0), 0.0)
            sft *= 2
        rest_all = gc_all[c64 - 1:, :] - gc_all
        betas, gc_cols = [], []
        for j in heads:
            hv = 2 * hq + j
            beta = _lane_pick(sig_all, lane, hv)
            gc_col = _lane_pick(gc_all, lane, GDN_V_HEADS + hv)
            rest_col = _lane_pick(rest_all, lane, GDN_V_HEADS + hv)
            eg = jnp.exp(gc_col)
            rhs_s[j, r, :hd] = (vc[:, j * hd:(j + 1) * hd] * beta).astype(BF16)
            rhs_s[j, r, hd:] = (kn * (beta * eg)).astype(BF16)
            qd_s[j, r, :] = (qn * eg).astype(BF16)
            kdp_s[c, j * c64:(j + 1) * c64, :] = (kn * jnp.exp(rest_col)).astype(BF16)
            betas.append(beta)
            gc_cols.append(gc_col)
        gcb_s[r, :] = jnp.where(first, gc_cols[0], gc_cols[1])
        betab_s[r, :] = jnp.where(first, betas[0], betas[1])

    def stage_gram(c):
        r = rows_of(c)
        kb = kb_s[r, :]
        kq = jnp.concatenate([kb, qn_s[r, :]], axis=0)
        gram = lax.dot_general(kq, jnp.concatenate([kb, kb], axis=0), nt_dims,
                               preferred_element_type=F32)
        kd_t = lax.dot_general(eye2, kdp_s[c], nt_dims, preferred_element_type=F32)

        def finish():
            gram_s[c] = gram
            lhs2_s[c, :2 * c64, :] = kd_t.astype(BF16)

        return finish

    def stage_factor(c):
        r = rows_of(c)
        dec = jnp.exp(jnp.where(tril, gcb_s[r, :] - gcr_s[pl.ds(c, 1), :], NEG_INF))
        lpair = jnp.where(strict, gram_s[c, :c64, :] * dec, 0.0) * betab_s[r, :]
        lm_s[r, :] = lpair
        lhs2_s[c, 2 * c64:, :] = (gram_s[c, c64:, :] * dec).astype(BF16)
        tl_s[0][r, :] = _block_diag_inverse(lpair)

    def stage_merge_products(level):
        def stage(c):
            r = rows_of(c)
            lt = _merge_lower_products(lm_s[r, :], tl_s[level][r, :], ELIM_BLOCK << level)

            def finish():
                lt_s[level][r, :] = lt.astype(BF16)

            return finish
        return stage

    def stage_merge_apply(level):
        def stage(c):
            r = rows_of(c)
            merged = _merge_apply(tl_s[level][r, :], lt_s[level][r, :])

            def finish():
                tl_s[level + 1][r, :] = merged

            return finish
        return stage

    def stage_solve(c):
        r = rows_of(c)
        zero = jnp.zeros((c64, 2 * hd), BF16)
        rhs_bd = jnp.concatenate([jnp.concatenate([rhs_s[0, r, :], zero], axis=1),
                                  jnp.concatenate([zero, rhs_s[1, r, :]], axis=1)], axis=0)
        sol = jnp.dot(tl_s[MERGE_LEVELS][r, :].astype(BF16), rhs_bd,
                      preferred_element_type=F32)

        def finish():
            sol_s[r, :] = sol.astype(BF16)

        return finish

    def stage_fold(c):
        r = rows_of(c)
        sol = sol_s[r, :]
        out = jnp.dot(lhs2_s[c], _lane_block_diag(sol), preferred_element_type=F32)

        def finish():
            for j in heads:
                u_col, w_col = 2 * j * hd, (2 * j + 1) * hd
                n_s[j, c] = out[:2 * c64, u_col:u_col + hd]
                mp_s[j, c, :2 * c64, :] = out[:2 * c64, w_col:w_col + hd].astype(BF16)
                mp_s[j, c, 2 * c64:, :] = (qd_s[j, r, :].astype(F32)
                                           - out[2 * c64:, w_col:w_col + hd]).astype(BF16)
                r_s[j, r, :] = out[2 * c64:, u_col:u_col + hd]

        return finish

    st_s[...] = jnp.zeros_like(st_s)
    nw = nw_ref[...]

    def stage_state(c):
        r = rows_of(c)
        sts = [st_s[j] for j in heads]
        outs = [jnp.dot(mp_s[j, c], sts[j].astype(BF16), preferred_element_type=F32)
                for j in heads]

        def finish():
            for j in heads:
                end = (j + 1) * c64
                gl = gcr_s[pl.ds(c, 1), end - 1:end]
                st_s[j] = sts[j] * jnp.exp(gl) - outs[j][:2 * c64] + n_s[j, c]
                r_s[j, r, :] = r_s[j, r, :] + outs[j][2 * c64:]

        return finish

    def stage_norm(c):
        r = rows_of(c)
        for j in heads:
            o = r_s[j, r, :]
            og = (o * lax.rsqrt(jnp.mean(o * o, axis=-1, keepdims=True) + RMS_EPS)
                  * nw * _silu(z_ref[r, j * hd:(j + 1) * hd]))
            y_ref[r, j * hd:(j + 1) * hd] = og.astype(BF16)

    merges = [stage(lv) for lv in range(MERGE_LEVELS)
              for stage in (stage_merge_products, stage_merge_apply)]
    _software_pipeline([stage_prep, stage_gram, stage_factor] + merges
                       + [stage_solve, stage_fold, stage_state, stage_norm], nchunk)


def _gdn(proj, small, rows, conv_w, a_log, dt_bias, norm_w, bsz, seq):
    nchunk = seq // GDN_CHUNK
    hd = HEAD_DIM
    c64 = GDN_CHUNK
    qo, ko = COL_GDN_Q // hd, COL_GDN_K // hd
    vo, zo = COL_GDN_V // (2 * hd), COL_GDN_Z // (2 * hd)
    cvo = (2 * GDN_QK_W) // (2 * hd)
    smem = pl.BlockSpec(memory_space=pltpu.SMEM)
    pad = (GDN_V_HEADS, N_SMALL - 2 * GDN_V_HEADS)
    gate_params = jnp.stack([jnp.pad(a_log, pad), jnp.pad(dt_bias, pad)])
    return pl.pallas_call(
        functools.partial(_gdn_kernel, nchunk=nchunk),
        name="gdn",
        grid=(bsz, GDN_QK_HEADS),
        in_specs=[smem, smem,
                  pl.BlockSpec((2, N_SMALL), lambda b, h: (0, 0)),
                  pl.BlockSpec((1, hd), lambda b, h: (0, 0)),
                  pl.BlockSpec((GDN_CONV, hd), lambda b, h: (0, h)),
                  pl.BlockSpec((GDN_CONV, hd), lambda b, h: (0, GDN_QK_HEADS + h)),
                  pl.BlockSpec((GDN_CONV, 2 * hd), lambda b, h: (0, cvo + h)),
                  pl.BlockSpec((seq, hd), lambda b, h: (b, qo + h)),
                  pl.BlockSpec((seq, hd), lambda b, h: (b, ko + h)),
                  pl.BlockSpec((seq, 2 * hd), lambda b, h: (b, vo + h)),
                  pl.BlockSpec((seq, 2 * hd), lambda b, h: (b, zo + h)),
                  pl.BlockSpec((seq, N_SMALL), lambda b, h: (b, 0)),
                  pl.BlockSpec((None, None, 4, nchunk, GDN_CHUNK), lambda b, h: (b, h, 0, 0, 0))],
        out_specs=pl.BlockSpec((seq, 2 * hd), lambda b, h: (b, h)),
        out_shape=jax.ShapeDtypeStruct((bsz * seq, GDN_V_W), BF16),
        scratch_shapes=[pltpu.VMEM((seq, hd), BF16),
                        pltpu.VMEM((seq, hd), BF16),
                        pltpu.VMEM((nchunk, 2 * c64, hd), BF16),
                        pltpu.VMEM((2, seq, 2 * hd), BF16),
                        pltpu.VMEM((2, seq, hd), BF16),
                        pltpu.VMEM((seq, 2 * c64), F32),
                        pltpu.VMEM((seq, 2 * c64), F32),
                        pltpu.VMEM((nchunk, 2 * c64), F32),
                        pltpu.VMEM((nchunk, 2 * c64, 2 * c64), F32),
                        pltpu.VMEM((seq, 2 * c64), F32),
                        pltpu.VMEM((nchunk, 3 * c64, hd), BF16),
                        pltpu.VMEM((seq, 4 * hd), BF16),
                        pltpu.VMEM((2, nchunk, 3 * c64, hd), BF16),
                        pltpu.VMEM((2, nchunk, hd, hd), F32),
                        pltpu.VMEM((2, seq, hd), F32),
                        pltpu.VMEM((2, hd, hd), F32)]
                       + [pltpu.VMEM((seq, 2 * c64), F32)] * (MERGE_LEVELS + 1)
                       + [pltpu.VMEM((seq, 2 * c64), BF16)] * MERGE_LEVELS,
        compiler_params=_vmem("gdn"),
    )(a_log, dt_bias, gate_params, norm_w, conv_w, conv_w, conv_w, proj, proj, proj, proj, small, rows)


def _merge_kernel(ya_ref, yb_ref, ga_ref, gb_ref, x_ref, g1_ref, sh2_ref, sc2_ref,
                  lng_ref, lnb_ref, wpm_ref, wpg_ref, wo_ref, x1_ref, h2_ref):
    pa = jnp.dot(ya_ref[...], wpm_ref[...], preferred_element_type=F32)
    pb = jnp.dot(yb_ref[...], wpg_ref[...], preferred_element_type=F32)
    merged = jax.nn.sigmoid(ga_ref[...]) * pa + jax.nn.sigmoid(gb_ref[...]) * pb
    y = jnp.dot(merged.astype(BF16), wo_ref[...], preferred_element_type=F32)
    x1 = _layer_norm(DEEPNORM_ALPHA * x_ref[...] + g1_ref[...] * y, lng_ref[...], lnb_ref[...])
    x1_ref[...] = x1
    h2_ref[...] = (x1 * (1.0 + sc2_ref[...]) + sh2_ref[...]).astype(BF16)


def _merge(ya, yb, proj, x2, mod3, ln_g, ln_b, wpm, wpg, wo, seq):
    t = x2.shape[0]
    tm = MERGE_TM
    per_b = seq // tm
    d = D_MODEL

    def modspec(k):
        return pl.BlockSpec((None, 1, d), lambda i: (i // per_b, 0, k))

    def const(shape):
        return pl.BlockSpec(shape, lambda i: (0, 0), pipeline_mode=pl.Buffered(1))

    return pl.pallas_call(
        _merge_kernel,
        name="merge",
        grid=(t // tm,),
        in_specs=[pl.BlockSpec((tm, MOBA_W), lambda i: (i, 0)),
                  pl.BlockSpec((tm, GDN_V_W), lambda i: (i, 0)),
                  pl.BlockSpec((tm, d), lambda i: (i, COL_GATE_A // d)),
                  pl.BlockSpec((tm, d), lambda i: (i, COL_GATE_B // d)),
                  pl.BlockSpec((tm, d), lambda i: (i, 0)),
                  modspec(2), modspec(3), modspec(4),
                  const((1, d)), const((1, d)),
                  const((MOBA_W, d)), const((GDN_V_W, d)), const((d, d))],
        out_specs=[pl.BlockSpec((tm, d), lambda i: (i, 0)),
                   pl.BlockSpec((tm, d), lambda i: (i, 0))],
        out_shape=[jax.ShapeDtypeStruct((t, d), F32), jax.ShapeDtypeStruct((t, d), BF16)],
        compiler_params=_vmem("merge"),
    )(ya, yb, proj, proj, x2, mod3, mod3, mod3, ln_g, ln_b, wpm, wpg, wo)


def _ffn_kernel(h_ref, x1_ref, g2_ref, lng_ref, lnb_ref, wg_ref, wu_ref, wo_ref, o_ref, acc_ref):
    f = pl.program_id(1)

    @pl.when(f == 0)
    def _():
        acc_ref[...] = jnp.zeros_like(acc_ref)

    h = h_ref[...]
    gate = jnp.dot(h, wg_ref[...], preferred_element_type=F32)
    up = jnp.dot(h, wu_ref[...], preferred_element_type=F32)
    act = (_silu(gate) * up).astype(BF16)
    acc_ref[...] += jnp.dot(act, wo_ref[...], preferred_element_type=F32)

    @pl.when(f == pl.num_programs(1) - 1)
    def _():
        r = DEEPNORM_ALPHA * x1_ref[...] + g2_ref[...] * acc_ref[...]
        o_ref[...] = _layer_norm(r, lng_ref[...], lnb_ref[...])


def _ffn(h2, x1, mod3, ln_g, ln_b, w_in, w_out, seq):
    t = h2.shape[0]
    tm, tf = FFN_TM, FFN_TF
    per_b = seq // tm
    d = D_MODEL
    nf = D_FF // tf
    return pl.pallas_call(
        _ffn_kernel,
        name="ffn",
        grid=(t // tm, nf),
        in_specs=[pl.BlockSpec((tm, d), lambda i, f: (i, 0)),
                  pl.BlockSpec((tm, d), lambda i, f: (i, 0)),
                  pl.BlockSpec((None, 1, d), lambda i, f: (i // per_b, 0, 5)),
                  pl.BlockSpec((1, d), lambda i, f: (0, 0)),
                  pl.BlockSpec((1, d), lambda i, f: (0, 0)),
                  pl.BlockSpec((d, tf), lambda i, f: (0, f)),
                  pl.BlockSpec((d, tf), lambda i, f: (0, nf + f)),
                  pl.BlockSpec((tf, d), lambda i, f: (f, 0))],
        out_specs=pl.BlockSpec((tm, d), lambda i, f: (i, 0)),
        out_shape=jax.ShapeDtypeStruct((t, d), F32),
        scratch_shapes=[pltpu.VMEM((tm, d), F32)],
        compiler_params=_vmem("ffn"),
    )(h2, x1, mod3, ln_g, ln_b, w_in, w_in, w_out)


def _rel_bucket(dist):
    max_exact = REL_BUCKETS // 2
    n = jnp.maximum(dist, 0)
    nf = jnp.maximum(n, 1).astype(F32)
    large = max_exact + (jnp.log(nf / max_exact) / math.log(REL_MAX_DIST / max_exact)
                         * (REL_BUCKETS - max_exact)).astype(jnp.int32)
    large = jnp.minimum(large, REL_BUCKETS - 1)
    return jnp.where(n < max_exact, n, large)


def _layer(x, c, w_ada, b_ada, w_in, conv_w, a_log, dt_bias, gdn_norm_w, rel_bias,
           w_proj_moba, w_proj_gdn, w_out, ln1_g, ln1_b, w_ffn_in, w_ffn_out, ln2_g, ln2_b):
    bsz, seq, d = x.shape
    t = bsz * seq
    x2 = x.reshape(t, d)

    mod = _ada_mod(c, w_ada, b_ada)
    mod3 = mod.reshape(bsz, 1, 6 * d)

    w_all = w_in.astype(BF16)
    n_gates = 2 * GDN_V_HEADS
    w_gate = w_all[:, N_HEAD + n_gates:]
    w_small = jnp.pad(w_all[:, N_HEAD:N_HEAD + n_gates], ((0, 0), (0, N_SMALL - n_gates)))

    proj, small = _in_proj(x2, mod3, w_all, w_gate, w_small, seq)

    ii = jnp.arange(MOBA_BLOCK, dtype=jnp.int32)
    dist = ii[None, :] - ii[:, None]
    bko = _rel_bucket(dist)
    bkp = _rel_bucket(dist + MOBA_BLOCK)
    ya = _moba(proj, rel_bias, bko, bkp, bsz, seq)

    nchunk = seq // GDN_CHUNK
    sm_t = small[:, :2 * GDN_V_HEADS].reshape(bsz, seq, 2, GDN_QK_HEADS, 2)
    rows = sm_t.transpose(0, 3, 2, 4, 1).reshape(bsz, GDN_QK_HEADS, 4, nchunk, GDN_CHUNK)
    yb = _gdn(proj, small, rows, conv_w, a_log, dt_bias, gdn_norm_w.reshape(1, HEAD_DIM),
              bsz, seq)

    x1, h2 = _merge(ya, yb, proj, x2, mod3, ln1_g.reshape(1, d), ln1_b.reshape(1, d),
                    w_proj_moba.astype(BF16), w_proj_gdn.astype(BF16), w_out.astype(BF16), seq)
    out = _ffn(h2, x1, mod3, ln2_g.reshape(1, d), ln2_b.reshape(1, d),
               w_ffn_in.astype(BF16), w_ffn_out.astype(BF16), seq)
    return out.reshape(bsz, seq, d)


def kernel(x, c, w_ada, b_ada, w_in, conv_w, a_log, dt_bias, gdn_norm_w, rel_bias, w_proj_moba,
           w_proj_gdn, w_out, ln1_g, ln1_b, w_ffn_in, w_ffn_out, ln2_g, ln2_b):
    depth = w_ada.shape[0]
    for l in range(depth):
        x = _layer(x, c, w_ada[l], b_ada[l], w_in[l], conv_w[l], a_log[l], dt_bias[l],
                   gdn_norm_w[l], rel_bias, w_proj_moba[l], w_proj_gdn[l], w_out[l],
                   ln1_g[l], ln1_b[l], w_ffn_in[l], w_ffn_out[l], ln2_g[l], ln2_b[l])
    return x
```

```python
import functools
import math

import jax
import jax.numpy as jnp
from jax import lax
from jax.experimental import pallas as pl
from jax.experimental.pallas import tpu as pltpu

F32 = jnp.float32
BF16 = jnp.bfloat16

D_MODEL = 2048
MOBA_HEADS = 8
HEAD_DIM = 128
MOBA_W = MOBA_HEADS * HEAD_DIM
MOBA_BLOCK = 256
MOBA_TOPK = 3
REL_BUCKETS = 32
REL_MAX_DIST = 128
GDN_QK_HEADS = 8
GDN_V_HEADS = 16
GDN_QK_W = GDN_QK_HEADS * HEAD_DIM
GDN_V_W = GDN_V_HEADS * HEAD_DIM
GDN_CONV = 4
GDN_CHUNK = 64
D_FF = 5632
DEEPNORM_ALPHA = 2.0 ** 0.25
LN_EPS = 1e-5
RMS_EPS = 1e-6
NEG_INF = -1e30

COL_GATE_A = 0
COL_GATE_B = 2048
COL_MOBA_Q = 4096
COL_MOBA_K = 5120
COL_MOBA_V = 6144
COL_GDN_Q = 7168
COL_GDN_K = 8192
COL_GDN_V = 9216
COL_GDN_Z = 11264
N_MAIN = 13312
N_GATE = 4096
N_HEAD = N_MAIN - N_GATE
N_SMALL = 128

V7X_VMEM_MIB = 64
SUBLANES = 8
BF16_SUBLANES = 16

ADA_TN = 1024
INPROJ_TM, INPROJ_TN = 1024, 1024
MERGE_TM = 256
FFN_TM, FFN_TF = 1024, 512
VMEM_LIMIT_MIB = {"ada_mod": 40, "in_proj": 56, "moba": 48, "gdn": 58, "merge": 56, "ffn": 60}
assert max(VMEM_LIMIT_MIB.values()) < V7X_VMEM_MIB


def _vmem(name):
    return pltpu.CompilerParams(vmem_limit_bytes=VMEM_LIMIT_MIB[name] * 1024 * 1024)


def _silu(x):
    return x * jax.nn.sigmoid(x)


def _layer_norm(r, gain, bias):
    mu = jnp.mean(r, axis=-1, keepdims=True)
    d = r - mu
    var = jnp.mean(d * d, axis=-1, keepdims=True)
    return d * lax.rsqrt(var + LN_EPS) * gain + bias


def _ada_kernel(c_ref, w_ref, b_ref, o_ref):
    sc = _silu(c_ref[...])
    o_ref[...] = jnp.dot(sc, w_ref[...], precision=lax.Precision.HIGHEST,
                         preferred_element_type=F32) + b_ref[...]


def _ada_mod(c, w_ada, b_ada):
    bsz = c.shape[0]
    n = w_ada.shape[1]
    tn = ADA_TN
    return pl.pallas_call(
        _ada_kernel,
        name="ada_mod",
        grid=(n // tn,),
        in_specs=[pl.BlockSpec((bsz, D_MODEL), lambda j: (0, 0)),
                  pl.BlockSpec((D_MODEL, tn), lambda j: (0, j)),
                  pl.BlockSpec((1, tn), lambda j: (0, j))],
        out_specs=pl.BlockSpec((bsz, tn), lambda j: (0, j)),
        out_shape=jax.ShapeDtypeStruct((bsz, n), F32),
        compiler_params=_vmem("ada_mod"),
    )(c, w_ada, b_ada.reshape(1, n))


def _inproj_kernel(x_ref, sh_ref, sc_ref, w_ref, wg_ref, ws_ref, o_ref, os_ref, h_ref, *, n_head):
    j = pl.program_id(1)

    @pl.when(j == 0)
    def _():
        h = x_ref[...] * (1.0 + sc_ref[...]) + sh_ref[...]
        h_ref[...] = h.astype(BF16)
        os_ref[...] = jnp.dot(h_ref[...], ws_ref[...], preferred_element_type=F32)

    @pl.when(j < n_head)
    def _():
        o_ref[...] = jnp.dot(h_ref[...], w_ref[...], preferred_element_type=F32)

    @pl.when(j >= n_head)
    def _():
        o_ref[...] = jnp.dot(h_ref[...], wg_ref[...], preferred_element_type=F32)


def _in_proj(x2, mod3, w_all, w_gate, w_small, seq):
    t = x2.shape[0]
    tm, tn = INPROJ_TM, INPROJ_TN
    per_b = seq // tm
    n_head, n_gate, n_tiles = N_HEAD // tn, N_GATE // tn, N_MAIN // tn
    return pl.pallas_call(
        functools.partial(_inproj_kernel, n_head=n_head),
        name="in_proj",
        grid=(t // tm, n_tiles),
        in_specs=[pl.BlockSpec((tm, D_MODEL), lambda i, j: (i, 0)),
                  pl.BlockSpec((None, 1, D_MODEL), lambda i, j: (i // per_b, 0, 0)),
                  pl.BlockSpec((None, 1, D_MODEL), lambda i, j: (i // per_b, 0, 1)),
                  pl.BlockSpec((D_MODEL, tn), lambda i, j: (0, jnp.minimum(j, n_head - 1))),
                  pl.BlockSpec((D_MODEL, tn), lambda i, j: (0, jnp.maximum(j - n_head, 0))),
                  pl.BlockSpec((D_MODEL, N_SMALL), lambda i, j: (0, 0))],
        out_specs=[pl.BlockSpec((tm, tn), lambda i, j: (i, (j + n_gate) % n_tiles)),
                   pl.BlockSpec((tm, N_SMALL), lambda i, j: (i, 0))],
        out_shape=[jax.ShapeDtypeStruct((t, N_MAIN), F32),
                   jax.ShapeDtypeStruct((t, N_SMALL), F32)],
        scratch_shapes=[pltpu.VMEM((tm, D_MODEL), BF16)],
        compiler_params=_vmem("in_proj"),
    )(x2, mod3, mod3, w_all, w_gate, w_small)


def _moba_kernel(rel_ref, bko_ref, bkp_ref, q_ref, k_ref, v_ref, o_ref,
                 bias_own, bias_prev, kb_ref, vt_ref, *, nb):
    h = pl.program_id(0)
    blk = MOBA_BLOCK
    inv_scale = HEAD_DIM ** 0.5
    scale_log2e = HEAD_DIM ** -0.5 * math.log2(math.e)

    @pl.when(pl.program_id(1) == 0)
    def _():
        bo = bko_ref[...]
        bp = bkp_ref[...]
        acc_o = jnp.zeros((blk, blk), F32)
        acc_p = jnp.zeros((blk, blk), F32)
        for kk in range(REL_BUCKETS):
            val = rel_ref[kk, h] * inv_scale
            acc_o = jnp.where(bo == kk, val, acc_o)
            acc_p = jnp.where(bp == kk, val, acc_p)
        bias_own[...] = acc_o
        bias_prev[...] = acc_p

    bias_far = rel_ref[REL_BUCKETS - 1, h] * inv_scale
    kf = k_ref[...]
    kmean = jnp.mean(kf.reshape(nb, blk, HEAD_DIM), axis=1)
    kb_ref[...] = kf.astype(BF16)
    nt_dims = (((1,), (1,)), ((), ()))
    eye = (lax.broadcasted_iota(jnp.int32, (HEAD_DIM, HEAD_DIM), 0)
           == lax.broadcasted_iota(jnp.int32, (HEAD_DIM, HEAD_DIM), 1)).astype(BF16)
    vt_ref[:HEAD_DIM, :] = lax.dot_general(eye, v_ref[...].astype(BF16), nt_dims,
                                           preferred_element_type=F32).astype(BF16)
    pad_rows = vt_ref.shape[0] - HEAD_DIM
    vt_ref[HEAD_DIM:, :] = (lax.broadcasted_iota(jnp.int32, (pad_rows, vt_ref.shape[1]), 0)
                            == 0).astype(BF16)
    causal = (lax.broadcasted_iota(jnp.int32, (blk, blk), 0)
              <= lax.broadcasted_iota(jnp.int32, (blk, blk), 1))

    def scores(i):
        qi = q_ref[i * blk:(i + 1) * blk, :]
        qb = qi.astype(BF16)
        sel = None
        if i > MOBA_TOPK:
            route = lax.dot_general(kmean, qi, nt_dims, precision=lax.Precision.HIGHEST,
                                    preferred_element_type=F32)
            rc = [route[n:n + 1, :] for n in range(i)]
            sel = []
            for n in range(i):
                rank = jnp.zeros((1, blk), jnp.int32)
                for m in range(i):
                    if m == n:
                        continue
                    beats = (rc[m] >= rc[n]) if m < n else (rc[m] > rc[n])
                    rank = rank + beats.astype(jnp.int32)
                sel.append(rank < MOBA_TOPK)
        t_list = []
        for n in range(i + 1):
            t = lax.dot_general(kb_ref[n * blk:(n + 1) * blk, :], qb, nt_dims,
                                preferred_element_type=F32)
            if n == i:
                t = jnp.where(causal, t + bias_own[...], NEG_INF)
            else:
                if n == i - 1:
                    t = t + bias_prev[...]
                if sel is not None:
                    t = jnp.where(sel[n], t, NEG_INF)
            t_list.append(t)
        return t_list

    t_next = scores(0)
    for i in range(nb):
        t_list = t_next
        if i + 1 < nb:
            t_next = scores(i + 1)
        n_far = max(i - 1, 0)
        m_run = jnp.max(t_list[n_far], axis=0, keepdims=True)
        for t in t_list[n_far + 1:]:
            m_run = jnp.maximum(m_run, jnp.max(t, axis=0, keepdims=True))
        if n_far:
            m_far = jnp.max(t_list[0], axis=0, keepdims=True)
            for t in t_list[1:n_far]:
                m_far = jnp.maximum(m_far, jnp.max(t, axis=0, keepdims=True))
            m_run = jnp.maximum(m_run, m_far + bias_far)
        acc = jnp.zeros((vt_ref.shape[0], blk), F32)
        for n, t in enumerate(t_list):
            offset = m_run - bias_far if n < n_far else m_run
            p = jnp.exp2((t - offset) * scale_log2e)
            acc = acc + jnp.dot(vt_ref[:, n * blk:(n + 1) * blk], p.astype(BF16),
                                preferred_element_type=F32)
        out = acc[:HEAD_DIM] / acc[HEAD_DIM:HEAD_DIM + 1]
        o_ref[i * blk:(i + 1) * blk, :] = out.T.astype(BF16)


def _moba(proj, rel_bias, bko, bkp, bsz, seq):
    nb = seq // MOBA_BLOCK
    qo, ko, vo = COL_MOBA_Q // HEAD_DIM, COL_MOBA_K // HEAD_DIM, COL_MOBA_V // HEAD_DIM
    blk = MOBA_BLOCK
    return pl.pallas_call(
        functools.partial(_moba_kernel, nb=nb),
        name="moba",
        grid=(MOBA_HEADS, bsz),
        in_specs=[pl.BlockSpec(memory_space=pltpu.SMEM),
                  pl.BlockSpec((blk, blk), lambda h, b: (0, 0)),
                  pl.BlockSpec((blk, blk), lambda h, b: (0, 0)),
                  pl.BlockSpec((seq, HEAD_DIM), lambda h, b: (b, qo + h)),
                  pl.BlockSpec((seq, HEAD_DIM), lambda h, b: (b, ko + h)),
                  pl.BlockSpec((seq, HEAD_DIM), lambda h, b: (b, vo + h))],
        out_specs=pl.BlockSpec((seq, HEAD_DIM), lambda h, b: (b, h)),
        out_shape=jax.ShapeDtypeStruct((bsz * seq, MOBA_W), BF16),
        scratch_shapes=[pltpu.VMEM((blk, blk), F32), pltpu.VMEM((blk, blk), F32),
                        pltpu.VMEM((seq, HEAD_DIM), BF16),
                        pltpu.VMEM((HEAD_DIM + BF16_SUBLANES, seq), BF16)],
        compiler_params=_vmem("moba"),
    )(rel_bias, bko, bkp, proj, proj, proj)


def _conv_silu(xh, w):
    acc = xh[SUBLANES:, :] * w[GDN_CONV - 1:GDN_CONV, :]
    for s in range(1, GDN_CONV):
        acc = acc + pltpu.roll(xh, s, axis=0)[SUBLANES:, :] * w[GDN_CONV - 1 - s:GDN_CONV - s, :]
    return _silu(acc)


def _l2norm(x):
    return x * lax.rsqrt(jnp.sum(x * x, axis=-1, keepdims=True) + RMS_EPS)


def _softplus(x):
    return jnp.maximum(x, 0.0) + jnp.log1p(jnp.exp(-jnp.abs(x)))


ELIM_BLOCK = 8
MERGE_LEVELS = (GDN_CHUNK // ELIM_BLOCK).bit_length() - 1


def _block_diag_inverse(lpair):
    n = lpair.shape[0]
    nv = n // SUBLANES
    vpb = ELIM_BLOCK // SUBLANES
    rid = lax.broadcasted_iota(jnp.int32, (SUBLANES, 2 * n), 0)
    lid = lax.broadcasted_iota(jnp.int32, (SUBLANES, 2 * n), 1)
    t_rows = [(lid % n == rid + v * SUBLANES).astype(F32) for v in range(nv)]
    l_rows = [lpair[v * SUBLANES:(v + 1) * SUBLANES, :] for v in range(nv)]
    for m in range(n - 1):
        v0, s0 = divmod(m, SUBLANES)
        v_end = (v0 // vpb + 1) * vpb
        row = t_rows[v0][s0:s0 + 1, :]
        col = (lid // n) * n + m
        for v in range(v0 if s0 < SUBLANES - 1 else v0 + 1, v_end):
            t_rows[v] = t_rows[v] - jnp.take_along_axis(l_rows[v], col, axis=1) * row
    return jnp.concatenate(t_rows, axis=0)


def _lane_block_diag(pair):
    first = lax.broadcasted_iota(jnp.int32, pair.shape, 1) < pair.shape[1] // 2
    zero = jnp.zeros_like(pair)
    return jnp.concatenate([jnp.where(first, pair, zero), jnp.where(first, zero, pair)], axis=0)


def _merge_lower_products(lpair, tpair, k):
    n = lpair.shape[0]
    ri = lax.broadcasted_iota(jnp.int32, lpair.shape, 0)
    ci = lax.broadcasted_iota(jnp.int32, lpair.shape, 1) % n
    off = (ri // (2 * k) == ci // (2 * k)) & (ri // k > ci // k)
    lk = jnp.where(off, lpair, 0.0).astype(BF16)
    return jnp.dot(lk, _lane_block_diag(tpair.astype(BF16)), preferred_element_type=F32)


def _merge_apply(tpair, lt):
    return tpair - jnp.dot(tpair.astype(BF16), _lane_block_diag(lt), preferred_element_type=F32)


PIPELINE_UNROLL = 4


def _software_pipeline(stages, n):
    ns = len(stages)

    def run(it, lo, hi):
        conts = [stages[s](it - s) for s in reversed(range(lo, hi))]
        for cont in conts:
            if cont is not None:
                cont()

    for it in range(ns - 1):
        run(it, 0, it + 1)

    steady = n - (ns - 1)
    assert steady % PIPELINE_UNROLL == 0, (n, ns)

    def body(k, carry):
        first = ns - 1 + k * PIPELINE_UNROLL
        for u in range(PIPELINE_UNROLL):
            run(first + u, 0, ns)
        return carry

    lax.fori_loop(0, steady // PIPELINE_UNROLL, body, 0)
    for it in range(n, n + ns - 1):
        run(it, it - n + 1, ns)


def _lane_pick(x, lane, idx):
    return jnp.sum(jnp.where(lane == idx, x, 0.0), axis=-1, keepdims=True)


def _gdn_kernel(alog_ref, dtb_ref, gp_ref, nw_ref, cwq_ref, cwk_ref, cwv_ref,
                q_ref, k_ref, v_ref, z_ref, sm_ref, rw_ref, y_ref,
                kb_s, qn_s, kdp_s, rhs_s, qd_s, gcb_s, betab_s, gcr_s, gram_s, lm_s, lhs2_s,
                sol_s, mp_s, n_s, r_s, st_s, *merge_s, nchunk):
    tl_s, lt_s = merge_s[:MERGE_LEVELS + 1], merge_s[MERGE_LEVELS + 1:]
    hq = pl.program_id(1)
    c64 = GDN_CHUNK
    hd = HEAD_DIM
    heads = range(2)

    tri_u = (lax.broadcasted_iota(jnp.int32, (c64, c64), 0)
             <= lax.broadcasted_iota(jnp.int32, (c64, c64), 1)).astype(F32)
    gc_rows = []
    for j in heads:
        hv = 2 * hq + j
        a_neg_r = -jnp.exp(jnp.full((nchunk, c64), alog_ref[hv], F32))
        g_row = a_neg_r * _softplus(rw_ref[2 + j] + dtb_ref[hv])
        gc_rows.append(jnp.dot(g_row, tri_u, precision=lax.Precision.HIGHEST,
                               preferred_element_type=F32))
    gcr_s[...] = jnp.concatenate(gc_rows, axis=1)

    rows = lax.broadcasted_iota(jnp.int32, (c64, 2 * c64), 0)
    cols = lax.broadcasted_iota(jnp.int32, (c64, 2 * c64), 1) % c64
    tril = rows >= cols
    strict = rows > cols
    eye2 = (lax.broadcasted_iota(jnp.int32, (2 * c64, 2 * c64), 0)
            == lax.broadcasted_iota(jnp.int32, (2 * c64, 2 * c64), 1)).astype(BF16)
    nt_dims = (((1,), (1,)), ((), ()))

    def rows_of(c):
        start = c * c64
        return pl.ds(start if isinstance(c, int) else pl.multiple_of(start, c64), c64)

    def with_halo(x_ref, c):
        if isinstance(c, int) and c == 0:
            return jnp.concatenate([jnp.zeros((SUBLANES, x_ref.shape[1]), F32), x_ref[:c64, :]],
                                   axis=0)
        start = c * c64 - SUBLANES
        if not isinstance(c, int):
            start = pl.multiple_of(start, SUBLANES)
        return x_ref[pl.ds(start, c64 + SUBLANES), :]

    lane = lax.broadcasted_iota(jnp.int32, (c64, N_SMALL), 1)
    first = lane < c64
    pos = lax.broadcasted_iota(jnp.int32, (c64, N_SMALL), 0)

    def stage_prep(c):
        r = rows_of(c)
        qn = _l2norm(_conv_silu(with_halo(q_ref, c), cwq_ref[...])) * (hd ** -0.5)
        kn = _l2norm(_conv_silu(with_halo(k_ref, c), cwk_ref[...]))
        vc = _conv_silu(with_halo(v_ref, c), cwv_ref[...])
        qn_s[r, :] = qn.astype(BF16)
        kb_s[r, :] = kn.astype(BF16)
        sm = sm_ref[r, :]
        sig_all = jax.nn.sigmoid(sm)
        gc_all = -jnp.exp(gp_ref[0:1, :]) * _softplus(sm + gp_ref[1:2, :])
        sft = 1
        while sft < c64:
            gc_all = gc_all + jnp.where(pos >= sft, pltpu.roll(gc_all, sft, axis=0), 0.0)
            sft *= 2
        rest_all = gc_all[c64 - 1:, :] - gc_all
        betas, gc_cols = [], []
        for j in heads:
            hv = 2 * hq + j
            beta = _lane_pick(sig_all, lane, hv)
            gc_col = _lane_pick(gc_all, lane, GDN_V_HEADS + hv)
            rest_col = _lane_pick(rest_all, lane, GDN_V_HEADS + hv)
            eg = jnp.exp(gc_col)
            rhs_s[j, r, :hd] = (vc[:, j * hd:(j + 1) * hd] * beta).astype(BF16)
            rhs_s[j, r, hd:] = (kn * (beta * eg)).astype(BF16)
            qd_s[j, r, :] = (qn * eg).astype(BF16)
            kdp_s[c, j * c64:(j + 1) * c64, :] = (kn * jnp.exp(rest_col)).astype(BF16)
            betas.append(beta)
            gc_cols.append(gc_col)
        gcb_s[r, :] = jnp.where(first, gc_cols[0], gc_cols[1])
        betab_s[r, :] = jnp.where(first, betas[0], betas[1])

    def stage_gram(c):
        r = rows_of(c)
        kb = kb_s[r, :]
        kq = jnp.concatenate([kb, qn_s[r, :]], axis=0)
        gram = lax.dot_general(kq, jnp.concatenate([kb, kb], axis=0), nt_dims,
                               preferred_element_type=F32)
        kd_t = lax.dot_general(eye2, kdp_s[c], nt_dims, preferred_element_type=F32)

        def finish():
            gram_s[c] = gram
            lhs2_s[c, :2 * c64, :] = kd_t.astype(BF16)

        return finish

    def stage_factor(c):
        r = rows_of(c)
        dec = jnp.exp(jnp.where(tril, gcb_s[r, :] - gcr_s[pl.ds(c, 1), :], NEG_INF))
        lpair = jnp.where(strict, gram_s[c, :c64, :] * dec, 0.0) * betab_s[r, :]
        lm_s[r, :] = lpair
        lhs2_s[c, 2 * c64:, :] = (gram_s[c, c64:, :] * dec).astype(BF16)
        tl_s[0][r, :] = _block_diag_inverse(lpair)

    def stage_merge_products(level):
        def stage(c):
            r = rows_of(c)
            lt = _merge_lower_products(lm_s[r, :], tl_s[level][r, :], ELIM_BLOCK << level)

            def finish():
                lt_s[level][r, :] = lt.astype(BF16)

            return finish
        return stage

    def stage_merge_apply(level):
        def stage(c):
            r = rows_of(c)
            merged = _merge_apply(tl_s[level][r, :], lt_s[level][r, :])

            def finish():
                tl_s[level + 1][r, :] = merged

            return finish
        return stage

    def stage_solve(c):
        r = rows_of(c)
        zero = jnp.zeros((c64, 2 * hd), BF16)
        rhs_bd = jnp.concatenate([jnp.concatenate([rhs_s[0, r, :], zero], axis=1),
                                  jnp.concatenate([zero, rhs_s[1, r, :]], axis=1)], axis=0)
        sol = jnp.dot(tl_s[MERGE_LEVELS][r, :].astype(BF16), rhs_bd,
                      preferred_element_type=F32)

        def finish():
            sol_s[r, :] = sol.astype(BF16)

        return finish

    def stage_fold(c):
        r = rows_of(c)
        sol = sol_s[r, :]
        out = jnp.dot(lhs2_s[c], _lane_block_diag(sol), preferred_element_type=F32)

        def finish():
            for j in heads:
                u_col, w_col = 2 * j * hd, (2 * j + 1) * hd
                n_s[j, c] = out[:2 * c64, u_col:u_col + hd]
                mp_s[j, c, :2 * c64, :] = out[:2 * c64, w_col:w_col + hd].astype(BF16)
                mp_s[j, c, 2 * c64:, :] = (qd_s[j, r, :].astype(F32)
                                           - out[2 * c64:, w_col:w_col + hd]).astype(BF16)
                r_s[j, r, :] = out[2 * c64:, u_col:u_col + hd]

        return finish

    st_s[...] = jnp.zeros_like(st_s)
    nw = nw_ref[...]

    def stage_state(c):
        r = rows_of(c)
        sts = [st_s[j] for j in heads]
        outs = [jnp.dot(mp_s[j, c], sts[j].astype(BF16), preferred_element_type=F32)
                for j in heads]

        def finish():
            for j in heads:
                end = (j + 1) * c64
                gl = gcr_s[pl.ds(c, 1), end - 1:end]
                st_s[j] = sts[j] * jnp.exp(gl) - outs[j][:2 * c64] + n_s[j, c]
                r_s[j, r, :] = r_s[j, r, :] + outs[j][2 * c64:]

        return finish

    def stage_norm(c):
        r = rows_of(c)
        for j in heads:
            o = r_s[j, r, :]
            og = (o * lax.rsqrt(jnp.mean(o * o, axis=-1, keepdims=True) + RMS_EPS)
                  * nw * _silu(z_ref[r, j * hd:(j + 1) * hd]))
            y_ref[r, j * hd:(j + 1) * hd] = og.astype(BF16)

    merges = [stage(lv) for lv in range(MERGE_LEVELS)
              for stage in (stage_merge_products, stage_merge_apply)]
    _software_pipeline([stage_prep, stage_gram, stage_factor] + merges
                       + [stage_solve, stage_fold, stage_state, stage_norm], nchunk)


def _gdn(proj, small, rows, conv_w, a_log, dt_bias, norm_w, bsz, seq):
    nchunk = seq // GDN_CHUNK
    hd = HEAD_DIM
    c64 = GDN_CHUNK
    qo, ko = COL_GDN_Q // hd, COL_GDN_K // hd
    vo, zo = COL_GDN_V // (2 * hd), COL_GDN_Z // (2 * hd)
    cvo = (2 * GDN_QK_W) // (2 * hd)
    smem = pl.BlockSpec(memory_space=pltpu.SMEM)
    pad = (GDN_V_HEADS, N_SMALL - 2 * GDN_V_HEADS)
    gate_params = jnp.stack([jnp.pad(a_log, pad), jnp.pad(dt_bias, pad)])
    return pl.pallas_call(
        functools.partial(_gdn_kernel, nchunk=nchunk),
        name="gdn",
        grid=(bsz, GDN_QK_HEADS),
        in_specs=[smem, smem,
                  pl.BlockSpec((2, N_SMALL), lambda b, h: (0, 0)),
                  pl.BlockSpec((1, hd), lambda b, h: (0, 0)),
                  pl.BlockSpec((GDN_CONV, hd), lambda b, h: (0, h)),
                  pl.BlockSpec((GDN_CONV, hd), lambda b, h: (0, GDN_QK_HEADS + h)),
                  pl.BlockSpec((GDN_CONV, 2 * hd), lambda b, h: (0, cvo + h)),
                  pl.BlockSpec((seq, hd), lambda b, h: (b, qo + h)),
                  pl.BlockSpec((seq, hd), lambda b, h: (b, ko + h)),
                  pl.BlockSpec((seq, 2 * hd), lambda b, h: (b, vo + h)),
                  pl.BlockSpec((seq, 2 * hd), lambda b, h: (b, zo + h)),
                  pl.BlockSpec((seq, N_SMALL), lambda b, h: (b, 0)),
                  pl.BlockSpec((None, None, 4, nchunk, GDN_CHUNK), lambda b, h: (b, h, 0, 0, 0))],
        out_specs=pl.BlockSpec((seq, 2 * hd), lambda b, h: (b, h)),
        out_shape=jax.ShapeDtypeStruct((bsz * seq, GDN_V_W), BF16),
        scratch_shapes=[pltpu.VMEM((seq, hd), BF16),
                        pltpu.VMEM((seq, hd), BF16),
                        pltpu.VMEM((nchunk, 2 * c64, hd), BF16),
                        pltpu.VMEM((2, seq, 2 * hd), BF16),
                        pltpu.VMEM((2, seq, hd), BF16),
                        pltpu.VMEM((seq, 2 * c64), F32),
                        pltpu.VMEM((seq, 2 * c64), F32),
                        pltpu.VMEM((nchunk, 2 * c64), F32),
                        pltpu.VMEM((nchunk, 2 * c64, 2 * c64), F32),
                        pltpu.VMEM((seq, 2 * c64), F32),
                        pltpu.VMEM((nchunk, 3 * c64, hd), BF16),
                        pltpu.VMEM((seq, 4 * hd), BF16),
                        pltpu.VMEM((2, nchunk, 3 * c64, hd), BF16),
                        pltpu.VMEM((2, nchunk, hd, hd), F32),
                        pltpu.VMEM((2, seq, hd), F32),
                        pltpu.VMEM((2, hd, hd), F32)]
                       + [pltpu.VMEM((seq, 2 * c64), F32)] * (MERGE_LEVELS + 1)
                       + [pltpu.VMEM((seq, 2 * c64), BF16)] * MERGE_LEVELS,
        compiler_params=_vmem("gdn"),
    )(a_log, dt_bias, gate_params, norm_w, conv_w, conv_w, conv_w, proj, proj, proj, proj, small, rows)


def _merge_kernel(ya_ref, yb_ref, ga_ref, gb_ref, x_ref, g1_ref, sh2_ref, sc2_ref,
                  lng_ref, lnb_ref, wpm_ref, wpg_ref, wo_ref, x1_ref, h2_ref):
    pa = jnp.dot(ya_ref[...], wpm_ref[...], preferred_element_type=F32)
    pb = jnp.dot(yb_ref[...], wpg_ref[...], preferred_element_type=F32)
    merged = jax.nn.sigmoid(ga_ref[...]) * pa + jax.nn.sigmoid(gb_ref[...]) * pb
    y = jnp.dot(merged.astype(BF16), wo_ref[...], preferred_element_type=F32)
    x1 = _layer_norm(DEEPNORM_ALPHA * x_ref[...] + g1_ref[...] * y, lng_ref[...], lnb_ref[...])
    x1_ref[...] = x1
    h2_ref[...] = (x1 * (1.0 + sc2_ref[...]) + sh2_ref[...]).astype(BF16)


def _merge(ya, yb, proj, x2, mod3, ln_g, ln_b, wpm, wpg, wo, seq):
    t = x2.shape[0]
    tm = MERGE_TM
    per_b = seq // tm
    d = D_MODEL

    def modspec(k):
        return pl.BlockSpec((None, 1, d), lambda i: (i // per_b, 0, k))

    def const(shape):
        return pl.BlockSpec(shape, lambda i: (0, 0), pipeline_mode=pl.Buffered(1))

    return pl.pallas_call(
        _merge_kernel,
        name="merge",
        grid=(t // tm,),
        in_specs=[pl.BlockSpec((tm, MOBA_W), lambda i: (i, 0)),
                  pl.BlockSpec((tm, GDN_V_W), lambda i: (i, 0)),
                  pl.BlockSpec((tm, d), lambda i: (i, COL_GATE_A // d)),
                  pl.BlockSpec((tm, d), lambda i: (i, COL_GATE_B // d)),
                  pl.BlockSpec((tm, d), lambda i: (i, 0)),
                  modspec(2), modspec(3), modspec(4),
                  const((1, d)), const((1, d)),
                  const((MOBA_W, d)), const((GDN_V_W, d)), const((d, d))],
        out_specs=[pl.BlockSpec((tm, d), lambda i: (i, 0)),
                   pl.BlockSpec((tm, d), lambda i: (i, 0))],
        out_shape=[jax.ShapeDtypeStruct((t, d), F32), jax.ShapeDtypeStruct((t, d), BF16)],
        compiler_params=_vmem("merge"),
    )(ya, yb, proj, proj, x2, mod3, mod3, mod3, ln_g, ln_b, wpm, wpg, wo)


def _ffn_kernel(h_ref, x1_ref, g2_ref, lng_ref, lnb_ref, wg_ref, wu_ref, wo_ref, o_ref):
    f = pl.program_id(1)

    @pl.when(f == 0)
    def _():
        o_ref[...] = jnp.zeros_like(o_ref)

    h = h_ref[...]
    gate = jnp.dot(h, wg_ref[...], preferred_element_type=F32)
    up = jnp.dot(h, wu_ref[...], preferred_element_type=F32)
    act = (_silu(gate) * up).astype(BF16)
    o_ref[...] += jnp.dot(act, wo_ref[...], preferred_element_type=F32)

    @pl.when(f == pl.num_programs(1) - 1)
    def _():
        r = DEEPNORM_ALPHA * x1_ref[...] + g2_ref[...] * o_ref[...]
        o_ref[...] = _layer_norm(r, lng_ref[...], lnb_ref[...])


def _ffn(h2, x1, mod3, ln_g, ln_b, w_in, w_out, seq):
    t = h2.shape[0]
    tm, tf = FFN_TM, FFN_TF
    per_b = seq // tm
    d = D_MODEL
    nf = D_FF // tf
    return pl.pallas_call(
        _ffn_kernel,
        name="ffn",
        grid=(t // tm, nf),
        in_specs=[pl.BlockSpec((tm, d), lambda i, f: (i, 0)),
                  pl.BlockSpec((tm, d), lambda i, f: (i, 0), pipeline_mode=pl.Buffered(1)),
                  pl.BlockSpec((None, 1, d), lambda i, f: (i // per_b, 0, 5)),
                  pl.BlockSpec((1, d), lambda i, f: (0, 0)),
                  pl.BlockSpec((1, d), lambda i, f: (0, 0)),
                  pl.BlockSpec((d, tf), lambda i, f: (0, f)),
                  pl.BlockSpec((d, tf), lambda i, f: (0, nf + f)),
                  pl.BlockSpec((tf, d), lambda i, f: (f, 0))],
        out_specs=pl.BlockSpec((tm, d), lambda i, f: (i, 0)),
        out_shape=jax.ShapeDtypeStruct((t, d), F32),
        compiler_params=_vmem("ffn"),
    )(h2, x1, mod3, ln_g, ln_b, w_in, w_in, w_out)


def _rel_bucket(dist):
    max_exact = REL_BUCKETS // 2
    n = jnp.maximum(dist, 0)
    nf = jnp.maximum(n, 1).astype(F32)
    large = max_exact + (jnp.log(nf / max_exact) / math.log(REL_MAX_DIST / max_exact)
                         * (REL_BUCKETS - max_exact)).astype(jnp.int32)
    large = jnp.minimum(large, REL_BUCKETS - 1)
    return jnp.where(n < max_exact, n, large)


def _layer(x, c, w_ada, b_ada, w_in, conv_w, a_log, dt_bias, gdn_norm_w, rel_bias,
           w_proj_moba, w_proj_gdn, w_out, ln1_g, ln1_b, w_ffn_in, w_ffn_out, ln2_g, ln2_b):
    bsz, seq, d = x.shape
    t = bsz * seq
    x2 = x.reshape(t, d)

    mod = _ada_mod(c, w_ada, b_ada)
    mod3 = mod.reshape(bsz, 1, 6 * d)

    w_all = w_in.astype(BF16)
    n_gates = 2 * GDN_V_HEADS
    w_gate = w_all[:, N_HEAD + n_gates:]
    w_small = jnp.pad(w_all[:, N_HEAD:N_HEAD + n_gates], ((0, 0), (0, N_SMALL - n_gates)))

    proj, small = _in_proj(x2, mod3, w_all, w_gate, w_small, seq)

    ii = jnp.arange(MOBA_BLOCK, dtype=jnp.int32)
    dist = ii[None, :] - ii[:, None]
    bko = _rel_bucket(dist)
    bkp = _rel_bucket(dist + MOBA_BLOCK)
    ya = _moba(proj, rel_bias, bko, bkp, bsz, seq)

    nchunk = seq // GDN_CHUNK
    sm_t = small[:, :2 * GDN_V_HEADS].reshape(bsz, seq, 2, GDN_QK_HEADS, 2)
    rows = sm_t.transpose(0, 3, 2, 4, 1).reshape(bsz, GDN_QK_HEADS, 4, nchunk, GDN_CHUNK)
    yb = _gdn(proj, small, rows, conv_w, a_log, dt_bias, gdn_norm_w.reshape(1, HEAD_DIM),
              bsz, seq)

    x1, h2 = _merge(ya, yb, proj, x2, mod3, ln1_g.reshape(1, d), ln1_b.reshape(1, d),
                    w_proj_moba.astype(BF16), w_proj_gdn.astype(BF16), w_out.astype(BF16), seq)
    out = _ffn(h2, x1, mod3, ln2_g.reshape(1, d), ln2_b.reshape(1, d),
               w_ffn_in.astype(BF16), w_ffn_out.astype(BF16), seq)
    return out.reshape(bsz, seq, d)


def kernel(x, c, w_ada, b_ada, w_in, conv_w, a_log, dt_bias, gdn_norm_w, rel_bias, w_proj_moba,
           w_proj_gdn, w_out, ln1_g, ln1_b, w_ffn_in, w_ffn_out, ln2_g, ln2_b):
    depth = w_ada.shape[0]
    for l in range(depth):
        x = _layer(x, c, w_ada[l], b_ada[l], w_in[l], conv_w[l], a_log[l], dt_bias[l],
                   gdn_norm_w[l], rel_bias, w_proj_moba[l], w_proj_gdn[l], w_out[l],
                   ln1_g[l], ln1_b[l], w_ffn_in[l], w_ffn_out[l], ln2_g[l], ln2_b[l])
    return x
```

```python
import functools
import math

import jax
import jax.numpy as jnp
from jax import lax
from jax.experimental import pallas as pl
from jax.experimental.pallas import tpu as pltpu

F32 = jnp.float32
BF16 = jnp.bfloat16

D_MODEL = 2048
MOBA_HEADS = 8
HEAD_DIM = 128
MOBA_W = MOBA_HEADS * HEAD_DIM
MOBA_BLOCK = 256
MOBA_TOPK = 3
REL_BUCKETS = 32
REL_MAX_DIST = 128
GDN_QK_HEADS = 8
GDN_V_HEADS = 16
GDN_QK_W = GDN_QK_HEADS * HEAD_DIM
GDN_V_W = GDN_V_HEADS * HEAD_DIM
GDN_CONV = 4
GDN_CHUNK = 64
D_FF = 5632
DEEPNORM_ALPHA = 2.0 ** 0.25
LN_EPS = 1e-5
RMS_EPS = 1e-6
NEG_INF = -1e30

COL_GATE_A = 0
COL_GATE_B = 2048
COL_MOBA_Q = 4096
COL_MOBA_K = 5120
COL_MOBA_V = 6144
COL_GDN_Q = 7168
COL_GDN_K = 8192
COL_GDN_V = 9216
COL_GDN_Z = 11264
N_MAIN = 13312
N_GATE = 4096
N_HEAD = N_MAIN - N_GATE
N_SMALL = 128

V7X_VMEM_MIB = 64
SUBLANES = 8
BF16_SUBLANES = 16

ADA_TN = 1024
INPROJ_TM, INPROJ_TN = 1024, 1024
MERGE_TM = 256
FFN_TM, FFN_TF = 512, 512
VMEM_LIMIT_MIB = {"ada_mod": 40, "in_proj": 56, "moba": 48, "gdn": 58, "merge": 56, "ffn": 48}
assert max(VMEM_LIMIT_MIB.values()) < V7X_VMEM_MIB


def _vmem(name):
    return pltpu.CompilerParams(vmem_limit_bytes=VMEM_LIMIT_MIB[name] * 1024 * 1024)


def _silu(x):
    return x * jax.nn.sigmoid(x)


def _layer_norm(r, gain, bias):
    mu = jnp.mean(r, axis=-1, keepdims=True)
    d = r - mu
    var = jnp.mean(d * d, axis=-1, keepdims=True)
    return d * lax.rsqrt(var + LN_EPS) * gain + bias


def _ada_kernel(c_ref, w_ref, b_ref, o_ref):
    sc = _silu(c_ref[...])
    o_ref[...] = jnp.dot(sc, w_ref[...], precision=lax.Precision.HIGHEST,
                         preferred_element_type=F32) + b_ref[...]


def _ada_mod(c, w_ada, b_ada):
    bsz = c.shape[0]
    n = w_ada.shape[1]
    tn = ADA_TN
    return pl.pallas_call(
        _ada_kernel,
        name="ada_mod",
        grid=(n // tn,),
        in_specs=[pl.BlockSpec((bsz, D_MODEL), lambda j: (0, 0)),
                  pl.BlockSpec((D_MODEL, tn), lambda j: (0, j)),
                  pl.BlockSpec((1, tn), lambda j: (0, j))],
        out_specs=pl.BlockSpec((bsz, tn), lambda j: (0, j)),
        out_shape=jax.ShapeDtypeStruct((bsz, n), F32),
        compiler_params=_vmem("ada_mod"),
    )(c, w_ada, b_ada.reshape(1, n))


def _inproj_kernel(x_ref, sh_ref, sc_ref, w_ref, wg_ref, ws_ref, o_ref, os_ref, h_ref, *, n_head):
    j = pl.program_id(1)

    @pl.when(j == 0)
    def _():
        h = x_ref[...] * (1.0 + sc_ref[...]) + sh_ref[...]
        h_ref[...] = h.astype(BF16)
        os_ref[...] = jnp.dot(h_ref[...], ws_ref[...], preferred_element_type=F32)

    @pl.when(j < n_head)
    def _():
        o_ref[...] = jnp.dot(h_ref[...], w_ref[...], preferred_element_type=F32)

    @pl.when(j >= n_head)
    def _():
        o_ref[...] = jnp.dot(h_ref[...], wg_ref[...], preferred_element_type=F32)


def _in_proj(x2, mod3, w_all, w_gate, w_small, seq):
    t = x2.shape[0]
    tm, tn = INPROJ_TM, INPROJ_TN
    per_b = seq // tm
    n_head, n_gate, n_tiles = N_HEAD // tn, N_GATE // tn, N_MAIN // tn
    return pl.pallas_call(
        functools.partial(_inproj_kernel, n_head=n_head),
        name="in_proj",
        grid=(t // tm, n_tiles),
        in_specs=[pl.BlockSpec((tm, D_MODEL), lambda i, j: (i, 0)),
                  pl.BlockSpec((None, 1, D_MODEL), lambda i, j: (i // per_b, 0, 0)),
                  pl.BlockSpec((None, 1, D_MODEL), lambda i, j: (i // per_b, 0, 1)),
                  pl.BlockSpec((D_MODEL, tn), lambda i, j: (0, jnp.minimum(j, n_head - 1))),
                  pl.BlockSpec((D_MODEL, tn),
                               lambda i, j: (0, jnp.where(j >= n_head, j - n_head, n_gate - 1))),
                  pl.BlockSpec((D_MODEL, N_SMALL), lambda i, j: (0, 0))],
        out_specs=[pl.BlockSpec((tm, tn), lambda i, j: (i, (j + n_gate) % n_tiles)),
                   pl.BlockSpec((tm, N_SMALL), lambda i, j: (i, 0))],
        out_shape=[jax.ShapeDtypeStruct((t, N_MAIN), F32),
                   jax.ShapeDtypeStruct((t, N_SMALL), F32)],
        scratch_shapes=[pltpu.VMEM((tm, D_MODEL), BF16)],
        compiler_params=_vmem("in_proj"),
    )(x2, mod3, mod3, w_all, w_gate, w_small)


def _moba_kernel(rel_ref, bko_ref, bkp_ref, q_ref, k_ref, v_ref, o_ref,
                 bias_own, bias_prev, kb_ref, vt_ref, *, nb):
    h = pl.program_id(0)
    blk = MOBA_BLOCK
    inv_scale = HEAD_DIM ** 0.5
    scale_log2e = HEAD_DIM ** -0.5 * math.log2(math.e)

    @pl.when(pl.program_id(1) == 0)
    def _():
        bo = bko_ref[...]
        bp = bkp_ref[...]
        acc_o = jnp.zeros((blk, blk), F32)
        acc_p = jnp.zeros((blk, blk), F32)
        for kk in range(REL_BUCKETS):
            val = rel_ref[kk, h] * inv_scale
            acc_o = jnp.where(bo == kk, val, acc_o)
            acc_p = jnp.where(bp == kk, val, acc_p)
        bias_own[...] = acc_o
        bias_prev[...] = acc_p

    bias_far = rel_ref[REL_BUCKETS - 1, h] * inv_scale
    kf = k_ref[...]
    kmean = jnp.mean(kf.reshape(nb, blk, HEAD_DIM), axis=1)
    kb_ref[...] = kf.astype(BF16)
    nt_dims = (((1,), (1,)), ((), ()))
    eye = (lax.broadcasted_iota(jnp.int32, (HEAD_DIM, HEAD_DIM), 0)
           == lax.broadcasted_iota(jnp.int32, (HEAD_DIM, HEAD_DIM), 1)).astype(BF16)
    vt_ref[:HEAD_DIM, :] = lax.dot_general(eye, v_ref[...].astype(BF16), nt_dims,
                                           preferred_element_type=F32).astype(BF16)
    pad_rows = vt_ref.shape[0] - HEAD_DIM
    vt_ref[HEAD_DIM:, :] = (lax.broadcasted_iota(jnp.int32, (pad_rows, vt_ref.shape[1]), 0)
                            == 0).astype(BF16)
    causal = (lax.broadcasted_iota(jnp.int32, (blk, blk), 0)
              <= lax.broadcasted_iota(jnp.int32, (blk, blk), 1))

    def scores(i):
        qi = q_ref[i * blk:(i + 1) * blk, :]
        qb = qi.astype(BF16)
        sel = None
        if i > MOBA_TOPK:
            route = lax.dot_general(kmean, qi, nt_dims, precision=lax.Precision.HIGHEST,
                                    preferred_element_type=F32)
            rc = [route[n:n + 1, :] for n in range(i)]
            sel = []
            for n in range(i):
                rank = jnp.zeros((1, blk), jnp.int32)
                for m in range(i):
                    if m == n:
                        continue
                    beats = (rc[m] >= rc[n]) if m < n else (rc[m] > rc[n])
                    rank = rank + beats.astype(jnp.int32)
                sel.append(rank < MOBA_TOPK)
        t_list = []
        for n in range(i + 1):
            t = lax.dot_general(kb_ref[n * blk:(n + 1) * blk, :], qb, nt_dims,
                                preferred_element_type=F32)
            if n == i:
                t = jnp.where(causal, t + bias_own[...], NEG_INF)
            else:
                if n == i - 1:
                    t = t + bias_prev[...]
                if sel is not None:
                    t = jnp.where(sel[n], t, NEG_INF)
            t_list.append(t)
        return t_list

    t_next = scores(0)
    for i in range(nb):
        t_list = t_next
        if i + 1 < nb:
            t_next = scores(i + 1)
        n_far = max(i - 1, 0)
        m_run = jnp.max(t_list[n_far], axis=0, keepdims=True)
        for t in t_list[n_far + 1:]:
            m_run = jnp.maximum(m_run, jnp.max(t, axis=0, keepdims=True))
        if n_far:
            m_far = jnp.max(t_list[0], axis=0, keepdims=True)
            for t in t_list[1:n_far]:
                m_far = jnp.maximum(m_far, jnp.max(t, axis=0, keepdims=True))
            m_run = jnp.maximum(m_run, m_far + bias_far)
        acc = jnp.zeros((vt_ref.shape[0], blk), F32)
        for n, t in enumerate(t_list):
            offset = m_run - bias_far if n < n_far else m_run
            p = jnp.exp2((t - offset) * scale_log2e)
            acc = acc + jnp.dot(vt_ref[:, n * blk:(n + 1) * blk], p.astype(BF16),
                                preferred_element_type=F32)
        out = acc[:HEAD_DIM] / acc[HEAD_DIM:HEAD_DIM + 1]
        o_ref[i * blk:(i + 1) * blk, :] = out.T.astype(BF16)


def _moba(proj, rel_bias, bko, bkp, bsz, seq):
    nb = seq // MOBA_BLOCK
    qo, ko, vo = COL_MOBA_Q // HEAD_DIM, COL_MOBA_K // HEAD_DIM, COL_MOBA_V // HEAD_DIM
    blk = MOBA_BLOCK
    return pl.pallas_call(
        functools.partial(_moba_kernel, nb=nb),
        name="moba",
        grid=(MOBA_HEADS, bsz),
        in_specs=[pl.BlockSpec(memory_space=pltpu.SMEM),
                  pl.BlockSpec((blk, blk), lambda h, b: (0, 0)),
                  pl.BlockSpec((blk, blk), lambda h, b: (0, 0)),
                  pl.BlockSpec((seq, HEAD_DIM), lambda h, b: (b, qo + h)),
                  pl.BlockSpec((seq, HEAD_DIM), lambda h, b: (b, ko + h)),
                  pl.BlockSpec((seq, HEAD_DIM), lambda h, b: (b, vo + h))],
        out_specs=pl.BlockSpec((seq, HEAD_DIM), lambda h, b: (b, h)),
        out_shape=jax.ShapeDtypeStruct((bsz * seq, MOBA_W), BF16),
        scratch_shapes=[pltpu.VMEM((blk, blk), F32), pltpu.VMEM((blk, blk), F32),
                        pltpu.VMEM((seq, HEAD_DIM), BF16),
                        pltpu.VMEM((HEAD_DIM + BF16_SUBLANES, seq), BF16)],
        compiler_params=_vmem("moba"),
    )(rel_bias, bko, bkp, proj, proj, proj)


def _conv_silu(xh, w):
    acc = xh[SUBLANES:, :] * w[GDN_CONV - 1:GDN_CONV, :]
    for s in range(1, GDN_CONV):
        acc = acc + pltpu.roll(xh, s, axis=0)[SUBLANES:, :] * w[GDN_CONV - 1 - s:GDN_CONV - s, :]
    return _silu(acc)


def _l2norm(x):
    return x * lax.rsqrt(jnp.sum(x * x, axis=-1, keepdims=True) + RMS_EPS)


def _softplus(x):
    return jnp.maximum(x, 0.0) + jnp.log1p(jnp.exp(-jnp.abs(x)))


ELIM_BLOCK = 8
MERGE_LEVELS = (GDN_CHUNK // ELIM_BLOCK).bit_length() - 1


def _block_diag_inverse(lpair):
    n = lpair.shape[0]
    nv = n // SUBLANES
    vpb = ELIM_BLOCK // SUBLANES
    rid = lax.broadcasted_iota(jnp.int32, (SUBLANES, 2 * n), 0)
    lid = lax.broadcasted_iota(jnp.int32, (SUBLANES, 2 * n), 1)
    t_rows = [(lid % n == rid + v * SUBLANES).astype(F32) for v in range(nv)]
    l_rows = [lpair[v * SUBLANES:(v + 1) * SUBLANES, :] for v in range(nv)]
    for m in range(n - 1):
        v0, s0 = divmod(m, SUBLANES)
        v_end = (v0 // vpb + 1) * vpb
        row = t_rows[v0][s0:s0 + 1, :]
        col = (lid // n) * n + m
        for v in range(v0 if s0 < SUBLANES - 1 else v0 + 1, v_end):
            t_rows[v] = t_rows[v] - jnp.take_along_axis(l_rows[v], col, axis=1) * row
    return jnp.concatenate(t_rows, axis=0)


def _lane_block_diag(pair):
    first = lax.broadcasted_iota(jnp.int32, pair.shape, 1) < pair.shape[1] // 2
    zero = jnp.zeros_like(pair)
    return jnp.concatenate([jnp.where(first, pair, zero), jnp.where(first, zero, pair)], axis=0)


def _merge_lower_products(lpair, tpair, k):
    n = lpair.shape[0]
    ri = lax.broadcasted_iota(jnp.int32, lpair.shape, 0)
    ci = lax.broadcasted_iota(jnp.int32, lpair.shape, 1) % n
    off = (ri // (2 * k) == ci // (2 * k)) & (ri // k > ci // k)
    lk = jnp.where(off, lpair, 0.0).astype(BF16)
    return jnp.dot(lk, _lane_block_diag(tpair.astype(BF16)), preferred_element_type=F32)


def _merge_apply(tpair, lt):
    return tpair - jnp.dot(tpair.astype(BF16), _lane_block_diag(lt), preferred_element_type=F32)


PIPELINE_UNROLL = 4


def _software_pipeline(stages, n):
    ns = len(stages)

    def run(it, lo, hi):
        conts = [stages[s](it - s) for s in reversed(range(lo, hi))]
        for cont in conts:
            if cont is not None:
                cont()

    for it in range(ns - 1):
        run(it, 0, it + 1)

    steady = n - (ns - 1)
    assert steady % PIPELINE_UNROLL == 0, (n, ns)

    def body(k, carry):
        first = ns - 1 + k * PIPELINE_UNROLL
        for u in range(PIPELINE_UNROLL):
            run(first + u, 0, ns)
        return carry

    lax.fori_loop(0, steady // PIPELINE_UNROLL, body, 0)
    for it in range(n, n + ns - 1):
        run(it, it - n + 1, ns)


def _lane_pick(x, lane, idx):
    return jnp.sum(jnp.where(lane == idx, x, 0.0), axis=-1, keepdims=True)


def _gdn_kernel(alog_ref, dtb_ref, gp_ref, nw_ref, cwq_ref, cwk_ref, cwv_ref,
                q_ref, k_ref, v_ref, z_ref, sm_ref, rw_ref, y_ref,
                kb_s, qn_s, kdp_s, rhs_s, qd_s, gcb_s, betab_s, gcr_s, gram_s, lm_s, lhs2_s,
                sol_s, mp_s, n_s, r_s, st_s, *merge_s, nchunk):
    tl_s, lt_s = merge_s[:MERGE_LEVELS + 1], merge_s[MERGE_LEVELS + 1:]
    hq = pl.program_id(1)
    c64 = GDN_CHUNK
    hd = HEAD_DIM
    heads = range(2)

    tri_u = (lax.broadcasted_iota(jnp.int32, (c64, c64), 0)
             <= lax.broadcasted_iota(jnp.int32, (c64, c64), 1)).astype(F32)
    gc_rows = []
    for j in heads:
        hv = 2 * hq + j
        a_neg_r = -jnp.exp(jnp.full((nchunk, c64), alog_ref[hv], F32))
        g_row = a_neg_r * _softplus(rw_ref[2 + j] + dtb_ref[hv])
        gc_rows.append(jnp.dot(g_row, tri_u, precision=lax.Precision.HIGHEST,
                               preferred_element_type=F32))
    gcr_s[...] = jnp.concatenate(gc_rows, axis=1)

    rows = lax.broadcasted_iota(jnp.int32, (c64, 2 * c64), 0)
    cols = lax.broadcasted_iota(jnp.int32, (c64, 2 * c64), 1) % c64
    tril = rows >= cols
    strict = rows > cols
    eye2 = (lax.broadcasted_iota(jnp.int32, (2 * c64, 2 * c64), 0)
            == lax.broadcasted_iota(jnp.int32, (2 * c64, 2 * c64), 1)).astype(BF16)
    nt_dims = (((1,), (1,)), ((), ()))

    def rows_of(c):
        start = c * c64
        return pl.ds(start if isinstance(c, int) else pl.multiple_of(start, c64), c64)

    def with_halo(x_ref, c):
        if isinstance(c, int) and c == 0:
            return jnp.concatenate([jnp.zeros((SUBLANES, x_ref.shape[1]), F32), x_ref[:c64, :]],
                                   axis=0)
        start = c * c64 - SUBLANES
        if not isinstance(c, int):
            start = pl.multiple_of(start, SUBLANES)
        return x_ref[pl.ds(start, c64 + SUBLANES), :]

    lane = lax.broadcasted_iota(jnp.int32, (c64, N_SMALL), 1)
    first = lane < c64
    pos = lax.broadcasted_iota(jnp.int32, (c64, N_SMALL), 0)

    def stage_prep(c):
        r = rows_of(c)
        qn = _l2norm(_conv_silu(with_halo(q_ref, c), cwq_ref[...])) * (hd ** -0.5)
        kn = _l2norm(_conv_silu(with_halo(k_ref, c), cwk_ref[...]))
        vc = _conv_silu(with_halo(v_ref, c), cwv_ref[...])
        qn_s[r, :] = qn.astype(BF16)
        kb_s[r, :] = kn.astype(BF16)
        sm = sm_ref[r, :]
        sig_all = jax.nn.sigmoid(sm)
        gc_all = -jnp.exp(gp_ref[0:1, :]) * _softplus(sm + gp_ref[1:2, :])
        sft = 1
        while sft < c64:
            gc_all = gc_all + jnp.where(pos >= sft, pltpu.roll(gc_all, sft, axis=0), 0.0)
            sft *= 2
        rest_all = gc_all[c64 - 1:, :] - gc_all
        betas, gc_cols = [], []
        for j in heads:
            hv = 2 * hq + j
            beta = _lane_pick(sig_all, lane, hv)
            gc_col = _lane_pick(gc_all, lane, GDN_V_HEADS + hv)
            rest_col = _lane_pick(rest_all, lane, GDN_V_HEADS + hv)
            eg = jnp.exp(gc_col)
            rhs_s[j, r, :hd] = (vc[:, j * hd:(j + 1) * hd] * beta).astype(BF16)
            rhs_s[j, r, hd:] = (kn * (beta * eg)).astype(BF16)
            qd_s[j, r, :] = (qn * eg).astype(BF16)
            kdp_s[c, j * c64:(j + 1) * c64, :] = (kn * jnp.exp(rest_col)).astype(BF16)
            betas.append(beta)
            gc_cols.append(gc_col)
        gcb_s[r, :] = jnp.where(first, gc_cols[0], gc_cols[1])
        betab_s[r, :] = jnp.where(first, betas[0], betas[1])

    def stage_gram(c):
        r = rows_of(c)
        kb = kb_s[r, :]
        kq = jnp.concatenate([kb, qn_s[r, :]], axis=0)
        gram = lax.dot_general(kq, jnp.concatenate([kb, kb], axis=0), nt_dims,
                               preferred_element_type=F32)
        kd_t = lax.dot_general(eye2, kdp_s[c], nt_dims, preferred_element_type=F32)

        def finish():
            gram_s[c] = gram
            lhs2_s[c, :2 * c64, :] = kd_t.astype(BF16)

        return finish

    def stage_factor(c):
        r = rows_of(c)
        dec = jnp.exp(jnp.where(tril, gcb_s[r, :] - gcr_s[pl.ds(c, 1), :], NEG_INF))
        lpair = jnp.where(strict, gram_s[c, :c64, :] * dec, 0.0) * betab_s[r, :]
        lm_s[r, :] = lpair
        lhs2_s[c, 2 * c64:, :] = (gram_s[c, c64:, :] * dec).astype(BF16)
        tl_s[0][r, :] = _block_diag_inverse(lpair)

    def stage_merge_products(level):
        def stage(c):
            r = rows_of(c)
            lt = _merge_lower_products(lm_s[r, :], tl_s[level][r, :], ELIM_BLOCK << level)

            def finish():
                lt_s[level][r, :] = lt.astype(BF16)

            return finish
        return stage

    def stage_merge_apply(level):
        def stage(c):
            r = rows_of(c)
            merged = _merge_apply(tl_s[level][r, :], lt_s[level][r, :])

            def finish():
                tl_s[level + 1][r, :] = merged

            return finish
        return stage

    def stage_solve(c):
        r = rows_of(c)
        zero = jnp.zeros((c64, 2 * hd), BF16)
        rhs_bd = jnp.concatenate([jnp.concatenate([rhs_s[0, r, :], zero], axis=1),
                                  jnp.concatenate([zero, rhs_s[1, r, :]], axis=1)], axis=0)
        sol = jnp.dot(tl_s[MERGE_LEVELS][r, :].astype(BF16), rhs_bd,
                      preferred_element_type=F32)

        def finish():
            sol_s[r, :] = sol.astype(BF16)

        return finish

    def stage_fold(c):
        r = rows_of(c)
        sol = sol_s[r, :]
        out = jnp.dot(lhs2_s[c], _lane_block_diag(sol), preferred_element_type=F32)

        def finish():
            for j in heads:
                u_col, w_col = 2 * j * hd, (2 * j + 1) * hd
                n_s[j, c] = out[:2 * c64, u_col:u_col + hd]
                mp_s[j, c, :2 * c64, :] = out[:2 * c64, w_col:w_col + hd].astype(BF16)
                mp_s[j, c, 2 * c64:, :] = (qd_s[j, r, :].astype(F32)
                                           - out[2 * c64:, w_col:w_col + hd]).astype(BF16)
                r_s[j, r, :] = out[2 * c64:, u_col:u_col + hd]

        return finish

    st_s[...] = jnp.zeros_like(st_s)
    nw = nw_ref[...]

    def stage_state(c):
        r = rows_of(c)
        sts = [st_s[j] for j in heads]
        outs = [jnp.dot(mp_s[j, c], sts[j].astype(BF16), preferred_element_type=F32)
                for j in heads]

        def finish():
            for j in heads:
                end = (j + 1) * c64
                gl = gcr_s[pl.ds(c, 1), end - 1:end]
                st_s[j] = sts[j] * jnp.exp(gl) - outs[j][:2 * c64] + n_s[j, c]
                r_s[j, r, :] = r_s[j, r, :] + outs[j][2 * c64:]

        return finish

    def stage_norm(c):
        r = rows_of(c)
        for j in heads:
            o = r_s[j, r, :]
            og = (o * lax.rsqrt(jnp.mean(o * o, axis=-1, keepdims=True) + RMS_EPS)
                  * nw * _silu(z_ref[r, j * hd:(j + 1) * hd]))
            y_ref[r, j * hd:(j + 1) * hd] = og.astype(BF16)

    merges = [stage(lv) for lv in range(MERGE_LEVELS)
              for stage in (stage_merge_products, stage_merge_apply)]
    _software_pipeline([stage_prep, stage_gram, stage_factor] + merges
                       + [stage_solve, stage_fold, stage_state, stage_norm], nchunk)


def _gdn(proj, small, rows, conv_w, a_log, dt_bias, norm_w, bsz, seq):
    nchunk = seq // GDN_CHUNK
    hd = HEAD_DIM
    c64 = GDN_CHUNK
    qo, ko = COL_GDN_Q // hd, COL_GDN_K // hd
    vo, zo = COL_GDN_V // (2 * hd), COL_GDN_Z // (2 * hd)
    cvo = (2 * GDN_QK_W) // (2 * hd)
    smem = pl.BlockSpec(memory_space=pltpu.SMEM)
    pad = (GDN_V_HEADS, N_SMALL - 2 * GDN_V_HEADS)
    gate_params = jnp.stack([jnp.pad(a_log, pad), jnp.pad(dt_bias, pad)])
    return pl.pallas_call(
        functools.partial(_gdn_kernel, nchunk=nchunk),
        name="gdn",
        grid=(bsz, GDN_QK_HEADS),
        in_specs=[smem, smem,
                  pl.BlockSpec((2, N_SMALL), lambda b, h: (0, 0)),
                  pl.BlockSpec((1, hd), lambda b, h: (0, 0)),
                  pl.BlockSpec((GDN_CONV, hd), lambda b, h: (0, h)),
                  pl.BlockSpec((GDN_CONV, hd), lambda b, h: (0, GDN_QK_HEADS + h)),
                  pl.BlockSpec((GDN_CONV, 2 * hd), lambda b, h: (0, cvo + h)),
                  pl.BlockSpec((seq, hd), lambda b, h: (b, qo + h)),
                  pl.BlockSpec((seq, hd), lambda b, h: (b, ko + h)),
                  pl.BlockSpec((seq, 2 * hd), lambda b, h: (b, vo + h)),
                  pl.BlockSpec((seq, 2 * hd), lambda b, h: (b, zo + h)),
                  pl.BlockSpec((seq, N_SMALL), lambda b, h: (b, 0)),
                  pl.BlockSpec((None, None, 4, nchunk, GDN_CHUNK), lambda b, h: (b, h, 0, 0, 0))],
        out_specs=pl.BlockSpec((seq, 2 * hd), lambda b, h: (b, h)),
        out_shape=jax.ShapeDtypeStruct((bsz * seq, GDN_V_W), BF16),
        scratch_shapes=[pltpu.VMEM((seq, hd), BF16),
                        pltpu.VMEM((seq, hd), BF16),
                        pltpu.VMEM((nchunk, 2 * c64, hd), BF16),
                        pltpu.VMEM((2, seq, 2 * hd), BF16),
                        pltpu.VMEM((2, seq, hd), BF16),
                        pltpu.VMEM((seq, 2 * c64), F32),
                        pltpu.VMEM((seq, 2 * c64), F32),
                        pltpu.VMEM((nchunk, 2 * c64), F32),
                        pltpu.VMEM((nchunk, 2 * c64, 2 * c64), F32),
                        pltpu.VMEM((seq, 2 * c64), F32),
                        pltpu.VMEM((nchunk, 3 * c64, hd), BF16),
                        pltpu.VMEM((seq, 4 * hd), BF16),
                        pltpu.VMEM((2, nchunk, 3 * c64, hd), BF16),
                        pltpu.VMEM((2, nchunk, hd, hd), F32),
                        pltpu.VMEM((2, seq, hd), F32),
                        pltpu.VMEM((2, hd, hd), F32)]
                       + [pltpu.VMEM((seq, 2 * c64), F32)] * (MERGE_LEVELS + 1)
                       + [pltpu.VMEM((seq, 2 * c64), BF16)] * MERGE_LEVELS,
        compiler_params=_vmem("gdn"),
    )(a_log, dt_bias, gate_params, norm_w, conv_w, conv_w, conv_w, proj, proj, proj, proj, small, rows)


def _merge_kernel(ya_ref, yb_ref, ga_ref, gb_ref, x_ref, g1_ref, sh2_ref, sc2_ref,
                  lng_ref, lnb_ref, wpm_ref, wpg_ref, wo_ref, x1_ref, h2_ref):
    pa = jnp.dot(ya_ref[...], wpm_ref[...], preferred_element_type=F32)
    pb = jnp.dot(yb_ref[...], wpg_ref[...], preferred_element_type=F32)
    merged = jax.nn.sigmoid(ga_ref[...]) * pa + jax.nn.sigmoid(gb_ref[...]) * pb
    y = jnp.dot(merged.astype(BF16), wo_ref[...], preferred_element_type=F32)
    x1 = _layer_norm(DEEPNORM_ALPHA * x_ref[...] + g1_ref[...] * y, lng_ref[...], lnb_ref[...])
    x1_ref[...] = x1
    h2_ref[...] = (x1 * (1.0 + sc2_ref[...]) + sh2_ref[...]).astype(BF16)


def _merge(ya, yb, proj, x2, mod3, ln_g, ln_b, wpm, wpg, wo, seq):
    t = x2.shape[0]
    tm = MERGE_TM
    per_b = seq // tm
    d = D_MODEL

    def modspec(k):
        return pl.BlockSpec((None, 1, d), lambda i: (i // per_b, 0, k))

    def const(shape):
        return pl.BlockSpec(shape, lambda i: (0, 0), pipeline_mode=pl.Buffered(1))

    return pl.pallas_call(
        _merge_kernel,
        name="merge",
        grid=(t // tm,),
        in_specs=[pl.BlockSpec((tm, MOBA_W), lambda i: (i, 0)),
                  pl.BlockSpec((tm, GDN_V_W), lambda i: (i, 0)),
                  pl.BlockSpec((tm, d), lambda i: (i, COL_GATE_A // d)),
                  pl.BlockSpec((tm, d), lambda i: (i, COL_GATE_B // d)),
                  pl.BlockSpec((tm, d), lambda i: (i, 0)),
                  modspec(2), modspec(3), modspec(4),
                  const((1, d)), const((1, d)),
                  const((MOBA_W, d)), const((GDN_V_W, d)), const((d, d))],
        out_specs=[pl.BlockSpec((tm, d), lambda i: (i, 0)),
                   pl.BlockSpec((tm, d), lambda i: (i, 0))],
        out_shape=[jax.ShapeDtypeStruct((t, d), F32), jax.ShapeDtypeStruct((t, d), BF16)],
        compiler_params=_vmem("merge"),
    )(ya, yb, proj, proj, x2, mod3, mod3, mod3, ln_g, ln_b, wpm, wpg, wo)


def _ffn_kernel(h_ref, x1_ref, g2_ref, lng_ref, lnb_ref, wg_ref, wu_ref, wo_ref, o_ref, acc_ref):
    f = pl.program_id(1)

    @pl.when(f == 0)
    def _():
        acc_ref[...] = jnp.zeros_like(acc_ref)

    h = h_ref[...]
    gate = jnp.dot(h, wg_ref[...], preferred_element_type=F32)
    up = jnp.dot(h, wu_ref[...], preferred_element_type=F32)
    act = (_silu(gate) * up).astype(BF16)
    acc_ref[...] += jnp.dot(act, wo_ref[...], preferred_element_type=F32)

    @pl.when(f == pl.num_programs(1) - 1)
    def _():
        r = DEEPNORM_ALPHA * x1_ref[...] + g2_ref[...] * acc_ref[...]
        o_ref[...] = _layer_norm(r, lng_ref[...], lnb_ref[...])


def _ffn(h2, x1, mod3, ln_g, ln_b, w_in, w_out, seq):
    t = h2.shape[0]
    tm, tf = FFN_TM, FFN_TF
    per_b = seq // tm
    d = D_MODEL
    nf = D_FF // tf
    return pl.pallas_call(
        _ffn_kernel,
        name="ffn",
        grid=(t // tm, nf),
        in_specs=[pl.BlockSpec((tm, d), lambda i, f: (i, 0)),
                  pl.BlockSpec((tm, d), lambda i, f: (i, 0)),
                  pl.BlockSpec((None, 1, d), lambda i, f: (i // per_b, 0, 5)),
                  pl.BlockSpec((1, d), lambda i, f: (0, 0)),
                  pl.BlockSpec((1, d), lambda i, f: (0, 0)),
                  pl.BlockSpec((d, tf), lambda i, f: (0, f)),
                  pl.BlockSpec((d, tf), lambda i, f: (0, nf + f)),
                  pl.BlockSpec((tf, d), lambda i, f: (f, 0))],
        out_specs=pl.BlockSpec((tm, d), lambda i, f: (i, 0)),
        out_shape=jax.ShapeDtypeStruct((t, d), F32),
        scratch_shapes=[pltpu.VMEM((tm, d), F32)],
        compiler_params=_vmem("ffn"),
    )(h2, x1, mod3, ln_g, ln_b, w_in, w_in, w_out)


def _rel_bucket(dist):
    max_exact = REL_BUCKETS // 2
    n = jnp.maximum(dist, 0)
    nf = jnp.maximum(n, 1).astype(F32)
    large = max_exact + (jnp.log(nf / max_exact) / math.log(REL_MAX_DIST / max_exact)
                         * (REL_BUCKETS - max_exact)).astype(jnp.int32)
    large = jnp.minimum(large, REL_BUCKETS - 1)
    return jnp.where(n < max_exact, n, large)


def _layer(x, c, w_ada, b_ada, w_in, conv_w, a_log, dt_bias, gdn_norm_w, rel_bias,
           w_proj_moba, w_proj_gdn, w_out, ln1_g, ln1_b, w_ffn_in, w_ffn_out, ln2_g, ln2_b):
    bsz, seq, d = x.shape
    t = bsz * seq
    x2 = x.reshape(t, d)

    mod = _ada_mod(c, w_ada, b_ada)
    mod3 = mod.reshape(bsz, 1, 6 * d)

    w_all = w_in.astype(BF16)
    n_gates = 2 * GDN_V_HEADS
    w_gate = w_all[:, N_HEAD + n_gates:]
    w_small = jnp.pad(w_all[:, N_HEAD:N_HEAD + n_gates], ((0, 0), (0, N_SMALL - n_gates)))

    proj, small = _in_proj(x2, mod3, w_all, w_gate, w_small, seq)

    ii = jnp.arange(MOBA_BLOCK, dtype=jnp.int32)
    dist = ii[None, :] - ii[:, None]
    bko = _rel_bucket(dist)
    bkp = _rel_bucket(dist + MOBA_BLOCK)
    ya = _moba(proj, rel_bias, bko, bkp, bsz, seq)

    nchunk = seq // GDN_CHUNK
    sm_t = small[:, :2 * GDN_V_HEADS].reshape(bsz, seq, 2, GDN_QK_HEADS, 2)
    rows = sm_t.transpose(0, 3, 2, 4, 1).reshape(bsz, GDN_QK_HEADS, 4, nchunk, GDN_CHUNK)
    yb = _gdn(proj, small, rows, conv_w, a_log, dt_bias, gdn_norm_w.reshape(1, HEAD_DIM),
              bsz, seq)

    x1, h2 = _merge(ya, yb, proj, x2, mod3, ln1_g.reshape(1, d), ln1_b.reshape(1, d),
                    w_proj_moba.astype(BF16), w_proj_gdn.astype(BF16), w_out.astype(BF16), seq)
    out = _ffn(h2, x1, mod3, ln2_g.reshape(1, d), ln2_b.reshape(1, d),
               w_ffn_in.astype(BF16), w_ffn_out.astype(BF16), seq)
    return out.reshape(bsz, seq, d)


def kernel(x, c, w_ada, b_ada, w_in, conv_w, a_log, dt_bias, gdn_norm_w, rel_bias, w_proj_moba,
           w_proj_gdn, w_out, ln1_g, ln1_b, w_ffn_in, w_ffn_out, ln2_g, ln2_b):
    depth = w_ada.shape[0]
    for l in range(depth):
        x = _layer(x, c, w_ada[l], b_ada[l], w_in[l], conv_w[l], a_log[l], dt_bias[l],
                   gdn_norm_w[l], rel_bias, w_proj_moba[l], w_proj_gdn[l], w_out[l],
                   ln1_g[l], ln1_b[l], w_ffn_in[l], w_ffn_out[l], ln2_g[l], ln2_b[l])
    return x
```

```python
import functools
import math

import jax
import jax.numpy as jnp
from jax import lax
from jax.experimental import pallas as pl
from jax.experimental.pallas import tpu as pltpu

F32 = jnp.float32
BF16 = jnp.bfloat16

D_MODEL = 2048
MOBA_HEADS = 8
HEAD_DIM = 128
MOBA_W = MOBA_HEADS * HEAD_DIM
MOBA_BLOCK = 256
MOBA_TOPK = 3
REL_BUCKETS = 32
REL_MAX_DIST = 128
GDN_QK_HEADS = 8
GDN_V_HEADS = 16
GDN_QK_W = GDN_QK_HEADS * HEAD_DIM
GDN_V_W = GDN_V_HEADS * HEAD_DIM
GDN_CONV = 4
GDN_CHUNK = 64
D_FF = 5632
DEEPNORM_ALPHA = 2.0 ** 0.25
LN_EPS = 1e-5
RMS_EPS = 1e-6
NEG_INF = -1e30

COL_GATE_A = 0
COL_GATE_B = 2048
COL_MOBA_Q = 4096
COL_MOBA_K = 5120
COL_MOBA_V = 6144
COL_GDN_Q = 7168
COL_GDN_K = 8192
COL_GDN_V = 9216
COL_GDN_Z = 11264
N_MAIN = 13312
N_GATE = 4096
N_HEAD = N_MAIN - N_GATE
N_SMALL = 128

V7X_VMEM_MIB = 64
SUBLANES = 8
BF16_SUBLANES = 16

ADA_TN = 1024
INPROJ_TM, INPROJ_TN = 1024, 1024
MERGE_TM = 256
FFN_TM, FFN_TF = 512, 512
VMEM_LIMIT_MIB = {"ada_mod": 40, "in_proj": 56, "moba": 48, "gdn": 58, "merge": 56, "ffn": 48}
assert max(VMEM_LIMIT_MIB.values()) < V7X_VMEM_MIB


def _vmem(name):
    return pltpu.CompilerParams(vmem_limit_bytes=VMEM_LIMIT_MIB[name] * 1024 * 1024)


def _silu(x):
    return x * jax.nn.sigmoid(x)


def _layer_norm(r, gain, bias):
    mu = jnp.mean(r, axis=-1, keepdims=True)
    d = r - mu
    var = jnp.mean(d * d, axis=-1, keepdims=True)
    return d * lax.rsqrt(var + LN_EPS) * gain + bias


def _ada_kernel(c_ref, w_ref, b_ref, o_ref):
    sc = _silu(c_ref[...])
    o_ref[...] = jnp.dot(sc, w_ref[...], precision=lax.Precision.HIGHEST,
                         preferred_element_type=F32) + b_ref[...]


def _ada_mod(c, w_ada, b_ada):
    bsz = c.shape[0]
    n = w_ada.shape[1]
    tn = ADA_TN
    return pl.pallas_call(
        _ada_kernel,
        name="ada_mod",
        grid=(n // tn,),
        in_specs=[pl.BlockSpec((bsz, D_MODEL), lambda j: (0, 0)),
                  pl.BlockSpec((D_MODEL, tn), lambda j: (0, j)),
                  pl.BlockSpec((1, tn), lambda j: (0, j))],
        out_specs=pl.BlockSpec((bsz, tn), lambda j: (0, j)),
        out_shape=jax.ShapeDtypeStruct((bsz, n), F32),
        compiler_params=_vmem("ada_mod"),
    )(c, w_ada, b_ada.reshape(1, n))


def _inproj_kernel(x_ref, sh_ref, sc_ref, w_ref, wg_ref, ws_ref, o_ref, os_ref, h_ref, *, n_head):
    j = pl.program_id(1)

    @pl.when(j == 0)
    def _():
        h = x_ref[...] * (1.0 + sc_ref[...]) + sh_ref[...]
        h_ref[...] = h.astype(BF16)
        os_ref[...] = jnp.dot(h_ref[...], ws_ref[...], preferred_element_type=F32)

    @pl.when(j < n_head)
    def _():
        o_ref[...] = jnp.dot(h_ref[...], w_ref[...], preferred_element_type=F32)

    @pl.when(j >= n_head)
    def _():
        o_ref[...] = jnp.dot(h_ref[...], wg_ref[...], preferred_element_type=F32)


def _in_proj(x2, mod3, w_all, w_gate, w_small, seq):
    t = x2.shape[0]
    tm, tn = INPROJ_TM, INPROJ_TN
    per_b = seq // tm
    n_head, n_gate, n_tiles = N_HEAD // tn, N_GATE // tn, N_MAIN // tn
    return pl.pallas_call(
        functools.partial(_inproj_kernel, n_head=n_head),
        name="in_proj",
        grid=(t // tm, n_tiles),
        in_specs=[pl.BlockSpec((tm, D_MODEL), lambda i, j: (i, 0)),
                  pl.BlockSpec((None, 1, D_MODEL), lambda i, j: (i // per_b, 0, 0)),
                  pl.BlockSpec((None, 1, D_MODEL), lambda i, j: (i // per_b, 0, 1)),
                  pl.BlockSpec((D_MODEL, tn), lambda i, j: (0, jnp.minimum(j, n_head - 1))),
                  pl.BlockSpec((D_MODEL, tn),
                               lambda i, j: (0, jnp.where(j >= n_head, j - n_head, n_gate - 1))),
                  pl.BlockSpec((D_MODEL, N_SMALL), lambda i, j: (0, 0))],
        out_specs=[pl.BlockSpec((tm, tn), lambda i, j: (i, (j + n_gate) % n_tiles)),
                   pl.BlockSpec((tm, N_SMALL), lambda i, j: (i, 0))],
        out_shape=[jax.ShapeDtypeStruct((t, N_MAIN), F32),
                   jax.ShapeDtypeStruct((t, N_SMALL), F32)],
        scratch_shapes=[pltpu.VMEM((tm, D_MODEL), BF16)],
        compiler_params=_vmem("in_proj"),
    )(x2, mod3, mod3, w_all, w_gate, w_small)


def _moba_kernel(rel_ref, bko_ref, bkp_ref, q_ref, k_ref, v_ref, o_ref,
                 bias_own, bias_prev, kb_ref, vt_ref, *, nb):
    h = pl.program_id(0)
    blk = MOBA_BLOCK
    inv_scale = HEAD_DIM ** 0.5
    scale_log2e = HEAD_DIM ** -0.5 * math.log2(math.e)

    @pl.when(pl.program_id(1) == 0)
    def _():
        bo = bko_ref[...]
        bp = bkp_ref[...]
        acc_o = jnp.zeros((blk, blk), F32)
        acc_p = jnp.zeros((blk, blk), F32)
        for kk in range(REL_BUCKETS):
            val = rel_ref[kk, h] * inv_scale
            acc_o = jnp.where(bo == kk, val, acc_o)
            acc_p = jnp.where(bp == kk, val, acc_p)
        bias_own[...] = acc_o
        bias_prev[...] = acc_p

    bias_far = rel_ref[REL_BUCKETS - 1, h] * inv_scale
    kf = k_ref[...]
    kmean = jnp.mean(kf.reshape(nb, blk, HEAD_DIM), axis=1)
    kb_ref[...] = kf.astype(BF16)
    nt_dims = (((1,), (1,)), ((), ()))
    eye = (lax.broadcasted_iota(jnp.int32, (HEAD_DIM, HEAD_DIM), 0)
           == lax.broadcasted_iota(jnp.int32, (HEAD_DIM, HEAD_DIM), 1)).astype(BF16)
    vt_ref[:HEAD_DIM, :] = lax.dot_general(eye, v_ref[...].astype(BF16), nt_dims,
                                           preferred_element_type=F32).astype(BF16)
    pad_rows = vt_ref.shape[0] - HEAD_DIM
    vt_ref[HEAD_DIM:, :] = (lax.broadcasted_iota(jnp.int32, (pad_rows, vt_ref.shape[1]), 0)
                            == 0).astype(BF16)
    causal = (lax.broadcasted_iota(jnp.int32, (blk, blk), 0)
              <= lax.broadcasted_iota(jnp.int32, (blk, blk), 1))

    def scores(i):
        qi = q_ref[i * blk:(i + 1) * blk, :]
        qb = qi.astype(BF16)
        sel = None
        if i > MOBA_TOPK:
            route = lax.dot_general(kmean, qi, nt_dims, precision=lax.Precision.HIGHEST,
                                    preferred_element_type=F32)
            rc = [route[n:n + 1, :] for n in range(i)]
            sel = []
            for n in range(i):
                rank = jnp.zeros((1, blk), jnp.int32)
                for m in range(i):
                    if m == n:
                        continue
                    beats = (rc[m] >= rc[n]) if m < n else (rc[m] > rc[n])
                    rank = rank + beats.astype(jnp.int32)
                sel.append(rank < MOBA_TOPK)
        t_list = []
        for n in range(i + 1):
            t = lax.dot_general(kb_ref[n * blk:(n + 1) * blk, :], qb, nt_dims,
                                preferred_element_type=F32)
            if n == i:
                t = jnp.where(causal, t + bias_own[...], NEG_INF)
            else:
                if n == i - 1:
                    t = t + bias_prev[...]
                if sel is not None:
                    t = jnp.where(sel[n], t, NEG_INF)
            t_list.append(t)
        return t_list

    t_next = scores(0)
    for i in range(nb):
        t_list = t_next
        if i + 1 < nb:
            t_next = scores(i + 1)
        n_far = max(i - 1, 0)
        m_run = jnp.max(t_list[n_far], axis=0, keepdims=True)
        for t in t_list[n_far + 1:]:
            m_run = jnp.maximum(m_run, jnp.max(t, axis=0, keepdims=True))
        if n_far:
            m_far = jnp.max(t_list[0], axis=0, keepdims=True)
            for t in t_list[1:n_far]:
                m_far = jnp.maximum(m_far, jnp.max(t, axis=0, keepdims=True))
            m_run = jnp.maximum(m_run, m_far + bias_far)
        acc = jnp.zeros((vt_ref.shape[0], blk), F32)
        for n, t in enumerate(t_list):
            offset = m_run - bias_far if n < n_far else m_run
            p = jnp.exp2((t - offset) * scale_log2e)
            acc = acc + jnp.dot(vt_ref[:, n * blk:(n + 1) * blk], p.astype(BF16),
                                preferred_element_type=F32)
        out = acc[:HEAD_DIM] / acc[HEAD_DIM:HEAD_DIM + 1]
        o_ref[i * blk:(i + 1) * blk, :] = out.T.astype(BF16)


def _moba(proj, rel_bias, bko, bkp, bsz, seq):
    nb = seq // MOBA_BLOCK
    qo, ko, vo = COL_MOBA_Q // HEAD_DIM, COL_MOBA_K // HEAD_DIM, COL_MOBA_V // HEAD_DIM
    blk = MOBA_BLOCK
    return pl.pallas_call(
        functools.partial(_moba_kernel, nb=nb),
        name="moba",
        grid=(MOBA_HEADS, bsz),
        in_specs=[pl.BlockSpec(memory_space=pltpu.SMEM),
                  pl.BlockSpec((blk, blk), lambda h, b: (0, 0)),
                  pl.BlockSpec((blk, blk), lambda h, b: (0, 0)),
                  pl.BlockSpec((seq, HEAD_DIM), lambda h, b: (b, qo + h)),
                  pl.BlockSpec((seq, HEAD_DIM), lambda h, b: (b, ko + h)),
                  pl.BlockSpec((seq, HEAD_DIM), lambda h, b: (b, vo + h))],
        out_specs=pl.BlockSpec((seq, HEAD_DIM), lambda h, b: (b, h)),
        out_shape=jax.ShapeDtypeStruct((bsz * seq, MOBA_W), BF16),
        scratch_shapes=[pltpu.VMEM((blk, blk), F32), pltpu.VMEM((blk, blk), F32),
                        pltpu.VMEM((seq, HEAD_DIM), BF16),
                        pltpu.VMEM((HEAD_DIM + BF16_SUBLANES, seq), BF16)],
        compiler_params=_vmem("moba"),
    )(rel_bias, bko, bkp, proj, proj, proj)


def _conv_silu(xh, w):
    acc = xh[SUBLANES:, :] * w[GDN_CONV - 1:GDN_CONV, :]
    for s in range(1, GDN_CONV):
        acc = acc + pltpu.roll(xh, s, axis=0)[SUBLANES:, :] * w[GDN_CONV - 1 - s:GDN_CONV - s, :]
    return _silu(acc)


def _l2norm(x):
    return x * lax.rsqrt(jnp.sum(x * x, axis=-1, keepdims=True) + RMS_EPS)


def _softplus(x):
    return jnp.maximum(x, 0.0) + jnp.log1p(jnp.exp(-jnp.abs(x)))


ELIM_BLOCK = 8
MERGE_LEVELS = (GDN_CHUNK // ELIM_BLOCK).bit_length() - 1


def _block_diag_inverse(lpair):
    n = lpair.shape[0]
    nv = n // SUBLANES
    vpb = ELIM_BLOCK // SUBLANES
    rid = lax.broadcasted_iota(jnp.int32, (SUBLANES, 2 * n), 0)
    lid = lax.broadcasted_iota(jnp.int32, (SUBLANES, 2 * n), 1)
    t_rows = [(lid % n == rid + v * SUBLANES).astype(F32) for v in range(nv)]
    l_rows = [lpair[v * SUBLANES:(v + 1) * SUBLANES, :] for v in range(nv)]
    for m in range(n - 1):
        v0, s0 = divmod(m, SUBLANES)
        v_end = (v0 // vpb + 1) * vpb
        row = t_rows[v0][s0:s0 + 1, :]
        col = (lid // n) * n + m
        for v in range(v0 if s0 < SUBLANES - 1 else v0 + 1, v_end):
            t_rows[v] = t_rows[v] - jnp.take_along_axis(l_rows[v], col, axis=1) * row
    return jnp.concatenate(t_rows, axis=0)


def _lane_block_diag(pair):
    first = lax.broadcasted_iota(jnp.int32, pair.shape, 1) < pair.shape[1] // 2
    zero = jnp.zeros_like(pair)
    return jnp.concatenate([jnp.where(first, pair, zero), jnp.where(first, zero, pair)], axis=0)


def _merge_lower_products(lpair, tpair, k):
    n = lpair.shape[0]
    ri = lax.broadcasted_iota(jnp.int32, lpair.shape, 0)
    ci = lax.broadcasted_iota(jnp.int32, lpair.shape, 1) % n
    off = (ri // (2 * k) == ci // (2 * k)) & (ri // k > ci // k)
    lk = jnp.where(off, lpair, 0.0).astype(BF16)
    return jnp.dot(lk, _lane_block_diag(tpair.astype(BF16)), preferred_element_type=F32)


def _merge_apply(tpair, lt):
    return tpair - jnp.dot(tpair.astype(BF16), _lane_block_diag(lt), preferred_element_type=F32)


PIPELINE_UNROLL = 4


def _software_pipeline(stages, n):
    ns = len(stages)

    def run(it, lo, hi):
        conts = [stages[s](it - s) for s in reversed(range(lo, hi))]
        for cont in conts:
            if cont is not None:
                cont()

    for it in range(ns - 1):
        run(it, 0, it + 1)

    steady = n - (ns - 1)
    assert steady % PIPELINE_UNROLL == 0, (n, ns)

    def body(k, carry):
        first = ns - 1 + k * PIPELINE_UNROLL
        for u in range(PIPELINE_UNROLL):
            run(first + u, 0, ns)
        return carry

    lax.fori_loop(0, steady // PIPELINE_UNROLL, body, 0)
    for it in range(n, n + ns - 1):
        run(it, it - n + 1, ns)


def _lane_pick(x, lane, idx):
    return jnp.sum(jnp.where(lane == idx, x, 0.0), axis=-1, keepdims=True)


def _gdn_kernel(alog_ref, dtb_ref, gp_ref, nw_ref, cwq_ref, cwk_ref, cwv_ref,
                q_ref, k_ref, v_ref, z_ref, sm_ref, rw_ref, y_ref,
                kb_s, qn_s, kdp_s, rhs_s, qd_s, gcb_s, betab_s, gcr_s, gram_s, lm_s, lhs2_s,
                sol_s, mp_s, n_s, r_s, st_s, *merge_s, nchunk):
    tl_s, lt_s = merge_s[:MERGE_LEVELS + 1], merge_s[MERGE_LEVELS + 1:]
    hq = pl.program_id(1)
    c64 = GDN_CHUNK
    hd = HEAD_DIM
    heads = range(2)

    tri_u = (lax.broadcasted_iota(jnp.int32, (c64, c64), 0)
             <= lax.broadcasted_iota(jnp.int32, (c64, c64), 1)).astype(F32)
    gc_rows = []
    for j in heads:
        hv = 2 * hq + j
        a_neg_r = -jnp.exp(jnp.full((nchunk, c64), alog_ref[hv], F32))
        g_row = a_neg_r * _softplus(rw_ref[2 + j] + dtb_ref[hv])
        gc_rows.append(jnp.dot(g_row, tri_u, precision=lax.Precision.HIGHEST,
                               preferred_element_type=F32))
    gcr_s[...] = jnp.concatenate(gc_rows, axis=1)

    rows = lax.broadcasted_iota(jnp.int32, (c64, 2 * c64), 0)
    cols = lax.broadcasted_iota(jnp.int32, (c64, 2 * c64), 1) % c64
    tril = rows >= cols
    strict = rows > cols
    eye2 = (lax.broadcasted_iota(jnp.int32, (2 * c64, 2 * c64), 0)
            == lax.broadcasted_iota(jnp.int32, (2 * c64, 2 * c64), 1)).astype(BF16)
    nt_dims = (((1,), (1,)), ((), ()))

    def rows_of(c):
        start = c * c64
        return pl.ds(start if isinstance(c, int) else pl.multiple_of(start, c64), c64)

    def with_halo(x_ref, c):
        if isinstance(c, int) and c == 0:
            return jnp.concatenate([jnp.zeros((SUBLANES, x_ref.shape[1]), F32), x_ref[:c64, :]],
                                   axis=0)
        start = c * c64 - SUBLANES
        if not isinstance(c, int):
            start = pl.multiple_of(start, SUBLANES)
        return x_ref[pl.ds(start, c64 + SUBLANES), :]

    lane = lax.broadcasted_iota(jnp.int32, (c64, N_SMALL), 1)
    first = lane < c64
    pos = lax.broadcasted_iota(jnp.int32, (c64, N_SMALL), 0)

    def stage_prep(c):
        r = rows_of(c)
        qn = _l2norm(_conv_silu(with_halo(q_ref, c), cwq_ref[...])) * (hd ** -0.5)
        kn = _l2norm(_conv_silu(with_halo(k_ref, c), cwk_ref[...]))
        vc = _conv_silu(with_halo(v_ref, c), cwv_ref[...])
        qn_s[r, :] = qn.astype(BF16)
        kb_s[r, :] = kn.astype(BF16)
        sm = sm_ref[r, :]
        sig_all = jax.nn.sigmoid(sm)
        gc_all = -jnp.exp(gp_ref[0:1, :]) * _softplus(sm + gp_ref[1:2, :])
        sft = 1
        while sft < c64:
            gc_all = gc_all + jnp.where(pos >= sft, pltpu.roll(gc_all, sft, axis=0), 0.0)
            sft *= 2
        rest_all = gc_all[c64 - 1:, :] - gc_all
        betas, gc_cols = [], []
        for j in heads:
            hv = 2 * hq + j
            beta = _lane_pick(sig_all, lane, hv)
            gc_col = _lane_pick(gc_all, lane, GDN_V_HEADS + hv)
            rest_col = _lane_pick(rest_all, lane, GDN_V_HEADS + hv)
            eg = jnp.exp(gc_col)
            rhs_s[j, r, :hd] = (vc[:, j * hd:(j + 1) * hd] * beta).astype(BF16)
            rhs_s[j, r, hd:] = (kn * (beta * eg)).astype(BF16)
            qd_s[j, r, :] = (qn * eg).astype(BF16)
            kdp_s[c, j * c64:(j + 1) * c64, :] = (kn * jnp.exp(rest_col)).astype(BF16)
            betas.append(beta)
            gc_cols.append(gc_col)
        gcb_s[r, :] = jnp.where(first, gc_cols[0], gc_cols[1])
        betab_s[r, :] = jnp.where(first, betas[0], betas[1])

    def stage_gram(c):
        r = rows_of(c)
        kb = kb_s[r, :]
        kq = jnp.concatenate([kb, qn_s[r, :]], axis=0)
        gram = lax.dot_general(kq, jnp.concatenate([kb, kb], axis=0), nt_dims,
                               preferred_element_type=F32)
        kd_t = lax.dot_general(eye2, kdp_s[c], nt_dims, preferred_element_type=F32)

        def finish():
            gram_s[c] = gram
            lhs2_s[c, :2 * c64, :] = kd_t.astype(BF16)

        return finish

    def stage_factor(c):
        r = rows_of(c)
        dec = jnp.exp(jnp.where(tril, gcb_s[r, :] - gcr_s[pl.ds(c, 1), :], NEG_INF))
        lpair = jnp.where(strict, gram_s[c, :c64, :] * dec, 0.0) * betab_s[r, :]
        lm_s[r, :] = lpair
        lhs2_s[c, 2 * c64:, :] = (gram_s[c, c64:, :] * dec).astype(BF16)
        tl_s[0][r, :] = _block_diag_inverse(lpair)

    def stage_merge_products(level):
        def stage(c):
            r = rows_of(c)
            lt = _merge_lower_products(lm_s[r, :], tl_s[level][r, :], ELIM_BLOCK << level)

            def finish():
                lt_s[level][r, :] = lt.astype(BF16)

            return finish
        return stage

    def stage_merge_apply(level):
        def stage(c):
            r = rows_of(c)
            merged = _merge_apply(tl_s[level][r, :], lt_s[level][r, :])

            def finish():
                tl_s[level + 1][r, :] = merged

            return finish
        return stage

    def stage_solve(c):
        r = rows_of(c)
        zero = jnp.zeros((c64, 2 * hd), BF16)
        rhs_bd = jnp.concatenate([jnp.concatenate([rhs_s[0, r, :], zero], axis=1),
                                  jnp.concatenate([zero, rhs_s[1, r, :]], axis=1)], axis=0)
        sol = jnp.dot(tl_s[MERGE_LEVELS][r, :].astype(BF16), rhs_bd,
                      preferred_element_type=F32)

        def finish():
            sol_s[r, :] = sol.astype(BF16)

        return finish

    def stage_fold(c):
        r = rows_of(c)
        sol = sol_s[r, :]
        out = jnp.dot(lhs2_s[c], _lane_block_diag(sol), preferred_element_type=F32)

        def finish():
            for j in heads:
                u_col, w_col = 2 * j * hd, (2 * j + 1) * hd
                n_s[j, c] = out[:2 * c64, u_col:u_col + hd]
                mp_s[j, c, :2 * c64, :] = out[:2 * c64, w_col:w_col + hd].astype(BF16)
                mp_s[j, c, 2 * c64:, :] = (qd_s[j, r, :].astype(F32)
                                           - out[2 * c64:, w_col:w_col + hd]).astype(BF16)
                r_s[j, r, :] = out[2 * c64:, u_col:u_col + hd]

        return finish

    st_s[...] = jnp.zeros_like(st_s)
    nw = nw_ref[...]

    def stage_state(c):
        r = rows_of(c)
        sts = [st_s[j] for j in heads]
        outs = [jnp.dot(mp_s[j, c], sts[j].astype(BF16), preferred_element_type=F32)
                for j in heads]

        def finish():
            for j in heads:
                end = (j + 1) * c64
                gl = gcr_s[pl.ds(c, 1), end - 1:end]
                st_s[j] = sts[j] * jnp.exp(gl) - outs[j][:2 * c64] + n_s[j, c]
                r_s[j, r, :] = r_s[j, r, :] + outs[j][2 * c64:]

        return finish

    def stage_norm(c):
        r = rows_of(c)
        for j in heads:
            o = r_s[j, r, :]
            og = (o * lax.rsqrt(jnp.mean(o * o, axis=-1, keepdims=True) + RMS_EPS)
                  * nw * _silu(z_ref[r, j * hd:(j + 1) * hd]))
            y_ref[r, j * hd:(j + 1) * hd] = og.astype(BF16)

    merges = [stage(lv) for lv in range(MERGE_LEVELS)
              for stage in (stage_merge_products, stage_merge_apply)]
    _software_pipeline([stage_prep, stage_gram, stage_factor] + merges
                       + [stage_solve, stage_fold, stage_state, stage_norm], nchunk)


def _gdn(proj, small, rows, conv_w, a_log, dt_bias, norm_w, bsz, seq):
    nchunk = seq // GDN_CHUNK
    hd = HEAD_DIM
    c64 = GDN_CHUNK
    qo, ko = COL_GDN_Q // hd, COL_GDN_K // hd
    vo, zo = COL_GDN_V // (2 * hd), COL_GDN_Z // (2 * hd)
    cvo = (2 * GDN_QK_W) // (2 * hd)
    smem = pl.BlockSpec(memory_space=pltpu.SMEM)
    pad = (GDN_V_HEADS, N_SMALL - 2 * GDN_V_HEADS)
    gate_params = jnp.stack([jnp.pad(a_log, pad), jnp.pad(dt_bias, pad)])
    return pl.pallas_call(
        functools.partial(_gdn_kernel, nchunk=nchunk),
        name="gdn",
        grid=(bsz, GDN_QK_HEADS),
        in_specs=[smem, smem,
                  pl.BlockSpec((2, N_SMALL), lambda b, h: (0, 0)),
                  pl.BlockSpec((1, hd), lambda b, h: (0, 0)),
                  pl.BlockSpec((GDN_CONV, hd), lambda b, h: (0, h)),
                  pl.BlockSpec((GDN_CONV, hd), lambda b, h: (0, GDN_QK_HEADS + h)),
                  pl.BlockSpec((GDN_CONV, 2 * hd), lambda b, h: (0, cvo + h)),
                  pl.BlockSpec((seq, hd), lambda b, h: (b, qo + h)),
                  pl.BlockSpec((seq, hd), lambda b, h: (b, ko + h)),
                  pl.BlockSpec((seq, 2 * hd), lambda b, h: (b, vo + h)),
                  pl.BlockSpec((seq, 2 * hd), lambda b, h: (b, zo + h)),
                  pl.BlockSpec((seq, N_SMALL), lambda b, h: (b, 0)),
                  pl.BlockSpec((None, None, 4, nchunk, GDN_CHUNK), lambda b, h: (b, h, 0, 0, 0))],
        out_specs=pl.BlockSpec((seq, 2 * hd), lambda b, h: (b, h)),
        out_shape=jax.ShapeDtypeStruct((bsz * seq, GDN_V_W), BF16),
        scratch_shapes=[pltpu.VMEM((seq, hd), BF16),
                        pltpu.VMEM((seq, hd), BF16),
                        pltpu.VMEM((nchunk, 2 * c64, hd), BF16),
                        pltpu.VMEM((2, seq, 2 * hd), BF16),
                        pltpu.VMEM((2, seq, hd), BF16),
                        pltpu.VMEM((seq, 2 * c64), F32),
                        pltpu.VMEM((seq, 2 * c64), F32),
                        pltpu.VMEM((nchunk, 2 * c64), F32),
                        pltpu.VMEM((nchunk, 2 * c64, 2 * c64), F32),
                        pltpu.VMEM((seq, 2 * c64), F32),
                        pltpu.VMEM((nchunk, 3 * c64, hd), BF16),
                        pltpu.VMEM((seq, 4 * hd), BF16),
                        pltpu.VMEM((2, nchunk, 3 * c64, hd), BF16),
                        pltpu.VMEM((2, nchunk, hd, hd), F32),
                        pltpu.VMEM((2, seq, hd), F32),
                        pltpu.VMEM((2, hd, hd), F32)]
                       + [pltpu.VMEM((seq, 2 * c64), F32)] * (MERGE_LEVELS + 1)
                       + [pltpu.VMEM((seq, 2 * c64), BF16)] * MERGE_LEVELS,
        compiler_params=_vmem("gdn"),
    )(a_log, dt_bias, gate_params, norm_w, conv_w, conv_w, conv_w, proj, proj, proj, proj, small, rows)


def _merge_kernel(ya_ref, yb_ref, ga_ref, gb_ref, x_ref, g1_ref, sh2_ref, sc2_ref,
                  lng_ref, lnb_ref, wpm_ref, wpg_ref, wo_ref, x1_ref, h2_ref):
    pa = jnp.dot(ya_ref[...], wpm_ref[...], preferred_element_type=F32)
    pb = jnp.dot(yb_ref[...], wpg_ref[...], preferred_element_type=F32)
    merged = jax.nn.sigmoid(ga_ref[...]) * pa + jax.nn.sigmoid(gb_ref[...]) * pb
    y = jnp.dot(merged.astype(BF16), wo_ref[...], preferred_element_type=F32)
    x1 = _layer_norm(DEEPNORM_ALPHA * x_ref[...] + g1_ref[...] * y, lng_ref[...], lnb_ref[...])
    x1_ref[...] = x1
    h2_ref[...] = (x1 * (1.0 + sc2_ref[...]) + sh2_ref[...]).astype(BF16)


def _merge(ya, yb, proj, x2, mod3, ln_g, ln_b, wpm, wpg, wo, seq):
    t = x2.shape[0]
    tm = MERGE_TM
    per_b = seq // tm
    d = D_MODEL

    def modspec(k):
        return pl.BlockSpec((None, 1, d), lambda i: (i // per_b, 0, k))

    def const(shape):
        return pl.BlockSpec(shape, lambda i: (0, 0), pipeline_mode=pl.Buffered(1))

    return pl.pallas_call(
        _merge_kernel,
        name="merge",
        grid=(t // tm,),
        in_specs=[pl.BlockSpec((tm, MOBA_W), lambda i: (i, 0)),
                  pl.BlockSpec((tm, GDN_V_W), lambda i: (i, 0)),
                  pl.BlockSpec((tm, d), lambda i: (i, COL_GATE_A // d)),
                  pl.BlockSpec((tm, d), lambda i: (i, COL_GATE_B // d)),
                  pl.BlockSpec((tm, d), lambda i: (i, 0)),
                  modspec(2), modspec(3), modspec(4),
                  const((1, d)), const((1, d)),
                  const((MOBA_W, d)), const((GDN_V_W, d)), const((d, d))],
        out_specs=[pl.BlockSpec((tm, d), lambda i: (i, 0)),
                   pl.BlockSpec((tm, d), lambda i: (i, 0))],
        out_shape=[jax.ShapeDtypeStruct((t, d), F32), jax.ShapeDtypeStruct((t, d), BF16)],
        compiler_params=_vmem("merge"),
    )(ya, yb, proj, proj, x2, mod3, mod3, mod3, ln_g, ln_b, wpm, wpg, wo)


def _ffn_kernel(h_ref, x1_ref, g2_ref, lng_ref, lnb_ref, wg_ref, wu_ref, wo_ref, o_ref, acc_ref):
    f = pl.program_id(1)

    @pl.when(f == 0)
    def _():
        acc_ref[...] = jnp.zeros_like(acc_ref)

    h = h_ref[...]
    gate = jnp.dot(h, wg_ref[...], preferred_element_type=F32)
    up = jnp.dot(h, wu_ref[...], preferred_element_type=F32)
    act = (_silu(gate) * up).astype(BF16)
    acc_ref[...] += jnp.dot(act, wo_ref[...], preferred_element_type=F32)

    @pl.when(f == pl.num_programs(1) - 1)
    def _():
        r = DEEPNORM_ALPHA * x1_ref[...] + g2_ref[...] * acc_ref[...]
        o_ref[...] = _layer_norm(r, lng_ref[...], lnb_ref[...])


def _ffn(h2, x1, mod3, ln_g, ln_b, w_in, w_out, seq):
    t = h2.shape[0]
    tm, tf = FFN_TM, FFN_TF
    per_b = seq // tm
    d = D_MODEL
    nf = D_FF // tf
    return pl.pallas_call(
        _ffn_kernel,
        name="ffn",
        grid=(t // tm, nf),
        in_specs=[pl.BlockSpec((tm, d), lambda i, f: (i, 0)),
                  pl.BlockSpec((tm, d), lambda i, f: (jnp.where(f == 0, jnp.maximum(i - 1, 0), i), 0)),
                  pl.BlockSpec((None, 1, d), lambda i, f: (i // per_b, 0, 5)),
                  pl.BlockSpec((1, d), lambda i, f: (0, 0)),
                  pl.BlockSpec((1, d), lambda i, f: (0, 0)),
                  pl.BlockSpec((d, tf), lambda i, f: (0, f)),
                  pl.BlockSpec((d, tf), lambda i, f: (0, nf + f)),
                  pl.BlockSpec((tf, d), lambda i, f: (f, 0))],
        out_specs=pl.BlockSpec((tm, d), lambda i, f: (i, 0)),
        out_shape=jax.ShapeDtypeStruct((t, d), F32),
        scratch_shapes=[pltpu.VMEM((tm, d), F32)],
        compiler_params=_vmem("ffn"),
    )(h2, x1, mod3, ln_g, ln_b, w_in, w_in, w_out)


def _rel_bucket(dist):
    max_exact = REL_BUCKETS // 2
    n = jnp.maximum(dist, 0)
    nf = jnp.maximum(n, 1).astype(F32)
    large = max_exact + (jnp.log(nf / max_exact) / math.log(REL_MAX_DIST / max_exact)
                         * (REL_BUCKETS - max_exact)).astype(jnp.int32)
    large = jnp.minimum(large, REL_BUCKETS - 1)
    return jnp.where(n < max_exact, n, large)


def _layer(x, c, w_ada, b_ada, w_in, conv_w, a_log, dt_bias, gdn_norm_w, rel_bias,
           w_proj_moba, w_proj_gdn, w_out, ln1_g, ln1_b, w_ffn_in, w_ffn_out, ln2_g, ln2_b):
    bsz, seq, d = x.shape
    t = bsz * seq
    x2 = x.reshape(t, d)

    mod = _ada_mod(c, w_ada, b_ada)
    mod3 = mod.reshape(bsz, 1, 6 * d)

    w_all = w_in.astype(BF16)
    n_gates = 2 * GDN_V_HEADS
    w_gate = w_all[:, N_HEAD + n_gates:]
    w_small = jnp.pad(w_all[:, N_HEAD:N_HEAD + n_gates], ((0, 0), (0, N_SMALL - n_gates)))

    proj, small = _in_proj(x2, mod3, w_all, w_gate, w_small, seq)

    ii = jnp.arange(MOBA_BLOCK, dtype=jnp.int32)
    dist = ii[None, :] - ii[:, None]
    bko = _rel_bucket(dist)
    bkp = _rel_bucket(dist + MOBA_BLOCK)
    ya = _moba(proj, rel_bias, bko, bkp, bsz, seq)

    nchunk = seq // GDN_CHUNK
    sm_t = small[:, :2 * GDN_V_HEADS].reshape(bsz, seq, 2, GDN_QK_HEADS, 2)
    rows = sm_t.transpose(0, 3, 2, 4, 1).reshape(bsz, GDN_QK_HEADS, 4, nchunk, GDN_CHUNK)
    yb = _gdn(proj, small, rows, conv_w, a_log, dt_bias, gdn_norm_w.reshape(1, HEAD_DIM),
              bsz, seq)

    x1, h2 = _merge(ya, yb, proj, x2, mod3, ln1_g.reshape(1, d), ln1_b.reshape(1, d),
                    w_proj_moba.astype(BF16), w_proj_gdn.astype(BF16), w_out.astype(BF16), seq)
    out = _ffn(h2, x1, mod3, ln2_g.reshape(1, d), ln2_b.reshape(1, d),
               w_ffn_in.astype(BF16), w_ffn_out.astype(BF16), seq)
    return out.reshape(bsz, seq, d)


def kernel(x, c, w_ada, b_ada, w_in, conv_w, a_log, dt_bias, gdn_norm_w, rel_bias, w_proj_moba,
           w_proj_gdn, w_out, ln1_g, ln1_b, w_ffn_in, w_ffn_out, ln2_g, ln2_b):
    depth = w_ada.shape[0]
    for l in range(depth):
        x = _layer(x, c, w_ada[l], b_ada[l], w_in[l], conv_w[l], a_log[l], dt_bias[l],
                   gdn_norm_w[l], rel_bias, w_proj_moba[l], w_proj_gdn[l], w_out[l],
                   ln1_g[l], ln1_b[l], w_ffn_in[l], w_ffn_out[l], ln2_g[l], ln2_b[l])
    return x
```

```python
import functools
import math

import jax
import jax.numpy as jnp
from jax import lax
from jax.experimental import pallas as pl
from jax.experimental.pallas import tpu as pltpu

F32 = jnp.float32
BF16 = jnp.bfloat16

D_MODEL = 2048
MOBA_HEADS = 8
HEAD_DIM = 128
MOBA_W = MOBA_HEADS * HEAD_DIM
MOBA_BLOCK = 256
MOBA_TOPK = 3
REL_BUCKETS = 32
REL_MAX_DIST = 128
GDN_QK_HEADS = 8
GDN_V_HEADS = 16
GDN_QK_W = GDN_QK_HEADS * HEAD_DIM
GDN_V_W = GDN_V_HEADS * HEAD_DIM
GDN_CONV = 4
GDN_CHUNK = 64
D_FF = 5632
DEEPNORM_ALPHA = 2.0 ** 0.25
LN_EPS = 1e-5
RMS_EPS = 1e-6
NEG_INF = -1e30

COL_GATE_A = 0
COL_GATE_B = 2048
COL_MOBA_Q = 4096
COL_MOBA_K = 5120
COL_MOBA_V = 6144
COL_GDN_Q = 7168
COL_GDN_K = 8192
COL_GDN_V = 9216
COL_GDN_Z = 11264
N_MAIN = 13312
N_GATE = 4096
N_HEAD = N_MAIN - N_GATE
N_SMALL = 128

V7X_VMEM_MIB = 64
SUBLANES = 8
BF16_SUBLANES = 16

ADA_TN = 1024
INPROJ_TM, INPROJ_TN = 1024, 1024
MERGE_TM = 256
FFN_TM, FFN_TF = 512, 512
VMEM_LIMIT_MIB = {"ada_mod": 40, "in_proj": 56, "moba": 48, "gdn": 58, "merge": 56, "ffn": 48}
assert max(VMEM_LIMIT_MIB.values()) < V7X_VMEM_MIB


def _vmem(name):
    return pltpu.CompilerParams(vmem_limit_bytes=VMEM_LIMIT_MIB[name] * 1024 * 1024)


def _silu(x):
    return x * jax.nn.sigmoid(x)


def _layer_norm(r, gain, bias):
    mu = jnp.mean(r, axis=-1, keepdims=True)
    d = r - mu
    var = jnp.mean(d * d, axis=-1, keepdims=True)
    return d * lax.rsqrt(var + LN_EPS) * gain + bias


def _ada_kernel(c_ref, w_ref, b_ref, o_ref):
    sc = _silu(c_ref[...])
    o_ref[...] = jnp.dot(sc, w_ref[...], precision=lax.Precision.HIGHEST,
                         preferred_element_type=F32) + b_ref[...]


def _ada_mod(c, w_ada, b_ada):
    bsz = c.shape[0]
    n = w_ada.shape[1]
    tn = ADA_TN
    return pl.pallas_call(
        _ada_kernel,
        name="ada_mod",
        grid=(n // tn,),
        in_specs=[pl.BlockSpec((bsz, D_MODEL), lambda j: (0, 0)),
                  pl.BlockSpec((D_MODEL, tn), lambda j: (0, j)),
                  pl.BlockSpec((1, tn), lambda j: (0, j))],
        out_specs=pl.BlockSpec((bsz, tn), lambda j: (0, j)),
        out_shape=jax.ShapeDtypeStruct((bsz, n), F32),
        compiler_params=_vmem("ada_mod"),
    )(c, w_ada, b_ada.reshape(1, n))


def _serpentine(i, j, n):
    return jnp.where(i % 2 == 0, j, n - 1 - j)


def _inproj_kernel(x_ref, sh_ref, sc_ref, w_ref, wg_ref, ws_ref, o_ref, os_ref, h_ref, *, n_head,
                   n_tiles):
    @pl.when(pl.program_id(1) == 0)
    def _():
        h = x_ref[...] * (1.0 + sc_ref[...]) + sh_ref[...]
        h_ref[...] = h.astype(BF16)
        os_ref[...] = jnp.dot(h_ref[...], ws_ref[...], preferred_element_type=F32)

    col = _serpentine(pl.program_id(0), pl.program_id(1), n_tiles)

    @pl.when(col < n_head)
    def _():
        o_ref[...] = jnp.dot(h_ref[...], w_ref[...], preferred_element_type=F32)

    @pl.when(col >= n_head)
    def _():
        o_ref[...] = jnp.dot(h_ref[...], wg_ref[...], preferred_element_type=F32)


def _in_proj(x2, mod3, w_all, w_gate, w_small, seq):
    t = x2.shape[0]
    tm, tn = INPROJ_TM, INPROJ_TN
    per_b = seq // tm
    n_head, n_gate, n_tiles = N_HEAD // tn, N_GATE // tn, N_MAIN // tn

    def col(i, j):
        return _serpentine(i, j, n_tiles)

    return pl.pallas_call(
        functools.partial(_inproj_kernel, n_head=n_head, n_tiles=n_tiles),
        name="in_proj",
        grid=(t // tm, n_tiles),
        in_specs=[pl.BlockSpec((tm, D_MODEL), lambda i, j: (i, 0)),
                  pl.BlockSpec((None, 1, D_MODEL), lambda i, j: (i // per_b, 0, 0)),
                  pl.BlockSpec((None, 1, D_MODEL), lambda i, j: (i // per_b, 0, 1)),
                  pl.BlockSpec((D_MODEL, tn), lambda i, j: (0, jnp.minimum(col(i, j), n_head - 1))),
                  pl.BlockSpec((D_MODEL, tn), lambda i, j: (0, jnp.maximum(col(i, j) - n_head, 0))),
                  pl.BlockSpec((D_MODEL, N_SMALL), lambda i, j: (0, 0))],
        out_specs=[pl.BlockSpec((tm, tn), lambda i, j: (i, (col(i, j) + n_gate) % n_tiles)),
                   pl.BlockSpec((tm, N_SMALL), lambda i, j: (i, 0))],
        out_shape=[jax.ShapeDtypeStruct((t, N_MAIN), F32),
                   jax.ShapeDtypeStruct((t, N_SMALL), F32)],
        scratch_shapes=[pltpu.VMEM((tm, D_MODEL), BF16)],
        compiler_params=_vmem("in_proj"),
    )(x2, mod3, mod3, w_all, w_gate, w_small)


def _moba_kernel(rel_ref, bko_ref, bkp_ref, q_ref, k_ref, v_ref, o_ref,
                 bias_own, bias_prev, kb_ref, vt_ref, *, nb):
    h = pl.program_id(0)
    blk = MOBA_BLOCK
    inv_scale = HEAD_DIM ** 0.5
    scale_log2e = HEAD_DIM ** -0.5 * math.log2(math.e)

    @pl.when(pl.program_id(1) == 0)
    def _():
        bo = bko_ref[...]
        bp = bkp_ref[...]
        acc_o = jnp.zeros((blk, blk), F32)
        acc_p = jnp.zeros((blk, blk), F32)
        for kk in range(REL_BUCKETS):
            val = rel_ref[kk, h] * inv_scale
            acc_o = jnp.where(bo == kk, val, acc_o)
            acc_p = jnp.where(bp == kk, val, acc_p)
        bias_own[...] = acc_o
        bias_prev[...] = acc_p

    bias_far = rel_ref[REL_BUCKETS - 1, h] * inv_scale
    kf = k_ref[...]
    kmean = jnp.mean(kf.reshape(nb, blk, HEAD_DIM), axis=1)
    kb_ref[...] = kf.astype(BF16)
    nt_dims = (((1,), (1,)), ((), ()))
    eye = (lax.broadcasted_iota(jnp.int32, (HEAD_DIM, HEAD_DIM), 0)
           == lax.broadcasted_iota(jnp.int32, (HEAD_DIM, HEAD_DIM), 1)).astype(BF16)
    vt_ref[:HEAD_DIM, :] = lax.dot_general(eye, v_ref[...].astype(BF16), nt_dims,
                                           preferred_element_type=F32).astype(BF16)
    pad_rows = vt_ref.shape[0] - HEAD_DIM
    vt_ref[HEAD_DIM:, :] = (lax.broadcasted_iota(jnp.int32, (pad_rows, vt_ref.shape[1]), 0)
                            == 0).astype(BF16)
    causal = (lax.broadcasted_iota(jnp.int32, (blk, blk), 0)
              <= lax.broadcasted_iota(jnp.int32, (blk, blk), 1))

    def scores(i):
        qi = q_ref[i * blk:(i + 1) * blk, :]
        qb = qi.astype(BF16)
        sel = None
        if i > MOBA_TOPK:
            route = lax.dot_general(kmean, qi, nt_dims, precision=lax.Precision.HIGHEST,
                                    preferred_element_type=F32)
            rc = [route[n:n + 1, :] for n in range(i)]
            sel = []
            for n in range(i):
                rank = jnp.zeros((1, blk), jnp.int32)
                for m in range(i):
                    if m == n:
                        continue
                    beats = (rc[m] >= rc[n]) if m < n else (rc[m] > rc[n])
                    rank = rank + beats.astype(jnp.int32)
                sel.append(rank < MOBA_TOPK)
        t_list = []
        for n in range(i + 1):
            t = lax.dot_general(kb_ref[n * blk:(n + 1) * blk, :], qb, nt_dims,
                                preferred_element_type=F32)
            if n == i:
                t = jnp.where(causal, t + bias_own[...], NEG_INF)
            else:
                if n == i - 1:
                    t = t + bias_prev[...]
                if sel is not None:
                    t = jnp.where(sel[n], t, NEG_INF)
            t_list.append(t)
        return t_list

    t_next = scores(0)
    for i in range(nb):
        t_list = t_next
        if i + 1 < nb:
            t_next = scores(i + 1)
        n_far = max(i - 1, 0)
        m_run = jnp.max(t_list[n_far], axis=0, keepdims=True)
        for t in t_list[n_far + 1:]:
            m_run = jnp.maximum(m_run, jnp.max(t, axis=0, keepdims=True))
        if n_far:
            m_far = jnp.max(t_list[0], axis=0, keepdims=True)
            for t in t_list[1:n_far]:
                m_far = jnp.maximum(m_far, jnp.max(t, axis=0, keepdims=True))
            m_run = jnp.maximum(m_run, m_far + bias_far)
        acc = jnp.zeros((vt_ref.shape[0], blk), F32)
        for n, t in enumerate(t_list):
            offset = m_run - bias_far if n < n_far else m_run
            p = jnp.exp2((t - offset) * scale_log2e)
            acc = acc + jnp.dot(vt_ref[:, n * blk:(n + 1) * blk], p.astype(BF16),
                                preferred_element_type=F32)
        out = acc[:HEAD_DIM] / acc[HEAD_DIM:HEAD_DIM + 1]
        o_ref[i * blk:(i + 1) * blk, :] = out.T.astype(BF16)


def _moba(proj, rel_bias, bko, bkp, bsz, seq):
    nb = seq // MOBA_BLOCK
    qo, ko, vo = COL_MOBA_Q // HEAD_DIM, COL_MOBA_K // HEAD_DIM, COL_MOBA_V // HEAD_DIM
    blk = MOBA_BLOCK
    return pl.pallas_call(
        functools.partial(_moba_kernel, nb=nb),
        name="moba",
        grid=(MOBA_HEADS, bsz),
        in_specs=[pl.BlockSpec(memory_space=pltpu.SMEM),
                  pl.BlockSpec((blk, blk), lambda h, b: (0, 0)),
                  pl.BlockSpec((blk, blk), lambda h, b: (0, 0)),
                  pl.BlockSpec((seq, HEAD_DIM), lambda h, b: (b, qo + h)),
                  pl.BlockSpec((seq, HEAD_DIM), lambda h, b: (b, ko + h)),
                  pl.BlockSpec((seq, HEAD_DIM), lambda h, b: (b, vo + h))],
        out_specs=pl.BlockSpec((seq, HEAD_DIM), lambda h, b: (b, h)),
        out_shape=jax.ShapeDtypeStruct((bsz * seq, MOBA_W), BF16),
        scratch_shapes=[pltpu.VMEM((blk, blk), F32), pltpu.VMEM((blk, blk), F32),
                        pltpu.VMEM((seq, HEAD_DIM), BF16),
                        pltpu.VMEM((HEAD_DIM + BF16_SUBLANES, seq), BF16)],
        compiler_params=_vmem("moba"),
    )(rel_bias, bko, bkp, proj, proj, proj)


def _conv_silu(xh, w):
    acc = xh[SUBLANES:, :] * w[GDN_CONV - 1:GDN_CONV, :]
    for s in range(1, GDN_CONV):
        acc = acc + pltpu.roll(xh, s, axis=0)[SUBLANES:, :] * w[GDN_CONV - 1 - s:GDN_CONV - s, :]
    return _silu(acc)


def _l2norm(x):
    return x * lax.rsqrt(jnp.sum(x * x, axis=-1, keepdims=True) + RMS_EPS)


def _softplus(x):
    return jnp.maximum(x, 0.0) + jnp.log1p(jnp.exp(-jnp.abs(x)))


ELIM_BLOCK = 8
MERGE_LEVELS = (GDN_CHUNK // ELIM_BLOCK).bit_length() - 1


def _block_diag_inverse(lpair):
    n = lpair.shape[0]
    nv = n // SUBLANES
    vpb = ELIM_BLOCK // SUBLANES
    rid = lax.broadcasted_iota(jnp.int32, (SUBLANES, 2 * n), 0)
    lid = lax.broadcasted_iota(jnp.int32, (SUBLANES, 2 * n), 1)
    t_rows = [(lid % n == rid + v * SUBLANES).astype(F32) for v in range(nv)]
    l_rows = [lpair[v * SUBLANES:(v + 1) * SUBLANES, :] for v in range(nv)]
    for m in range(n - 1):
        v0, s0 = divmod(m, SUBLANES)
        v_end = (v0 // vpb + 1) * vpb
        row = t_rows[v0][s0:s0 + 1, :]
        col = (lid // n) * n + m
        for v in range(v0 if s0 < SUBLANES - 1 else v0 + 1, v_end):
            t_rows[v] = t_rows[v] - jnp.take_along_axis(l_rows[v], col, axis=1) * row
    return jnp.concatenate(t_rows, axis=0)


def _lane_block_diag(pair):
    first = lax.broadcasted_iota(jnp.int32, pair.shape, 1) < pair.shape[1] // 2
    zero = jnp.zeros_like(pair)
    return jnp.concatenate([jnp.where(first, pair, zero), jnp.where(first, zero, pair)], axis=0)


def _merge_lower_products(lpair, tpair, k):
    n = lpair.shape[0]
    ri = lax.broadcasted_iota(jnp.int32, lpair.shape, 0)
    ci = lax.broadcasted_iota(jnp.int32, lpair.shape, 1) % n
    off = (ri // (2 * k) == ci // (2 * k)) & (ri // k > ci // k)
    lk = jnp.where(off, lpair, 0.0).astype(BF16)
    return jnp.dot(lk, _lane_block_diag(tpair.astype(BF16)), preferred_element_type=F32)


def _merge_apply(tpair, lt):
    return tpair - jnp.dot(tpair.astype(BF16), _lane_block_diag(lt), preferred_element_type=F32)


PIPELINE_UNROLL = 4


def _software_pipeline(stages, n):
    ns = len(stages)

    def run(it, lo, hi):
        conts = [stages[s](it - s) for s in reversed(range(lo, hi))]
        for cont in conts:
            if cont is not None:
                cont()

    for it in range(ns - 1):
        run(it, 0, it + 1)

    steady = n - (ns - 1)
    assert steady % PIPELINE_UNROLL == 0, (n, ns)

    def body(k, carry):
        first = ns - 1 + k * PIPELINE_UNROLL
        for u in range(PIPELINE_UNROLL):
            run(first + u, 0, ns)
        return carry

    lax.fori_loop(0, steady // PIPELINE_UNROLL, body, 0)
    for it in range(n, n + ns - 1):
        run(it, it - n + 1, ns)


def _lane_pick(x, lane, idx):
    return jnp.sum(jnp.where(lane == idx, x, 0.0), axis=-1, keepdims=True)


def _gdn_kernel(alog_ref, dtb_ref, gp_ref, nw_ref, cwq_ref, cwk_ref, cwv_ref,
                q_ref, k_ref, v_ref, z_ref, sm_ref, rw_ref, y_ref,
                kb_s, qn_s, kdp_s, rhs_s, qd_s, gcb_s, betab_s, gcr_s, gram_s, lm_s, lhs2_s,
                sol_s, mp_s, n_s, r_s, st_s, *merge_s, nchunk):
    tl_s, lt_s = merge_s[:MERGE_LEVELS + 1], merge_s[MERGE_LEVELS + 1:]
    hq = pl.program_id(1)
    c64 = GDN_CHUNK
    hd = HEAD_DIM
    heads = range(2)

    tri_u = (lax.broadcasted_iota(jnp.int32, (c64, c64), 0)
             <= lax.broadcasted_iota(jnp.int32, (c64, c64), 1)).astype(F32)
    gc_rows = []
    for j in heads:
        hv = 2 * hq + j
        a_neg_r = -jnp.exp(jnp.full((nchunk, c64), alog_ref[hv], F32))
        g_row = a_neg_r * _softplus(rw_ref[2 + j] + dtb_ref[hv])
        gc_rows.append(jnp.dot(g_row, tri_u, precision=lax.Precision.HIGHEST,
                               preferred_element_type=F32))
    gcr_s[...] = jnp.concatenate(gc_rows, axis=1)

    rows = lax.broadcasted_iota(jnp.int32, (c64, 2 * c64), 0)
    cols = lax.broadcasted_iota(jnp.int32, (c64, 2 * c64), 1) % c64
    tril = rows >= cols
    strict = rows > cols
    eye2 = (lax.broadcasted_iota(jnp.int32, (2 * c64, 2 * c64), 0)
            == lax.broadcasted_iota(jnp.int32, (2 * c64, 2 * c64), 1)).astype(BF16)
    nt_dims = (((1,), (1,)), ((), ()))

    def rows_of(c):
        start = c * c64
        return pl.ds(start if isinstance(c, int) else pl.multiple_of(start, c64), c64)

    def with_halo(x_ref, c):
        if isinstance(c, int) and c == 0:
            return jnp.concatenate([jnp.zeros((SUBLANES, x_ref.shape[1]), F32), x_ref[:c64, :]],
                                   axis=0)
        start = c * c64 - SUBLANES
        if not isinstance(c, int):
            start = pl.multiple_of(start, SUBLANES)
        return x_ref[pl.ds(start, c64 + SUBLANES), :]

    lane = lax.broadcasted_iota(jnp.int32, (c64, N_SMALL), 1)
    first = lane < c64
    pos = lax.broadcasted_iota(jnp.int32, (c64, N_SMALL), 0)

    def stage_prep(c):
        r = rows_of(c)
        qn = _l2norm(_conv_silu(with_halo(q_ref, c), cwq_ref[...])) * (hd ** -0.5)
        kn = _l2norm(_conv_silu(with_halo(k_ref, c), cwk_ref[...]))
        vc = _conv_silu(with_halo(v_ref, c), cwv_ref[...])
        qn_s[r, :] = qn.astype(BF16)
        kb_s[r, :] = kn.astype(BF16)
        sm = sm_ref[r, :]
        sig_all = jax.nn.sigmoid(sm)
        gc_all = -jnp.exp(gp_ref[0:1, :]) * _softplus(sm + gp_ref[1:2, :])
        sft = 1
        while sft < c64:
            gc_all = gc_all + jnp.where(pos >= sft, pltpu.roll(gc_all, sft, axis=0), 0.0)
            sft *= 2
        rest_all = gc_all[c64 - 1:, :] - gc_all
        betas, gc_cols = [], []
        for j in heads:
            hv = 2 * hq + j
            beta = _lane_pick(sig_all, lane, hv)
            gc_col = _lane_pick(gc_all, lane, GDN_V_HEADS + hv)
            rest_col = _lane_pick(rest_all, lane, GDN_V_HEADS + hv)
            eg = jnp.exp(gc_col)
            rhs_s[j, r, :hd] = (vc[:, j * hd:(j + 1) * hd] * beta).astype(BF16)
            rhs_s[j, r, hd:] = (kn * (beta * eg)).astype(BF16)
            qd_s[j, r, :] = (qn * eg).astype(BF16)
            kdp_s[c, j * c64:(j + 1) * c64, :] = (kn * jnp.exp(rest_col)).astype(BF16)
            betas.append(beta)
            gc_cols.append(gc_col)
        gcb_s[r, :] = jnp.where(first, gc_cols[0], gc_cols[1])
        betab_s[r, :] = jnp.where(first, betas[0], betas[1])

    def stage_gram(c):
        r = rows_of(c)
        kb = kb_s[r, :]
        kq = jnp.concatenate([kb, qn_s[r, :]], axis=0)
        gram = lax.dot_general(kq, jnp.concatenate([kb, kb], axis=0), nt_dims,
                               preferred_element_type=F32)
        kd_t = lax.dot_general(eye2, kdp_s[c], nt_dims, preferred_element_type=F32)

        def finish():
            gram_s[c] = gram
            lhs2_s[c, :2 * c64, :] = kd_t.astype(BF16)

        return finish

    def stage_factor(c):
        r = rows_of(c)
        dec = jnp.exp(jnp.where(tril, gcb_s[r, :] - gcr_s[pl.ds(c, 1), :], NEG_INF))
        lpair = jnp.where(strict, gram_s[c, :c64, :] * dec, 0.0) * betab_s[r, :]
        lm_s[r, :] = lpair
        lhs2_s[c, 2 * c64:, :] = (gram_s[c, c64:, :] * dec).astype(BF16)
        tl_s[0][r, :] = _block_diag_inverse(lpair)

    def stage_merge_products(level):
        def stage(c):
            r = rows_of(c)
            lt = _merge_lower_products(lm_s[r, :], tl_s[level][r, :], ELIM_BLOCK << level)

            def finish():
                lt_s[level][r, :] = lt.astype(BF16)

            return finish
        return stage

    def stage_merge_apply(level):
        def stage(c):
            r = rows_of(c)
            merged = _merge_apply(tl_s[level][r, :], lt_s[level][r, :])

            def finish():
                tl_s[level + 1][r, :] = merged

            return finish
        return stage

    def stage_solve(c):
        r = rows_of(c)
        zero = jnp.zeros((c64, 2 * hd), BF16)
        rhs_bd = jnp.concatenate([jnp.concatenate([rhs_s[0, r, :], zero], axis=1),
                                  jnp.concatenate([zero, rhs_s[1, r, :]], axis=1)], axis=0)
        sol = jnp.dot(tl_s[MERGE_LEVELS][r, :].astype(BF16), rhs_bd,
                      preferred_element_type=F32)

        def finish():
            sol_s[r, :] = sol.astype(BF16)

        return finish

    def stage_fold(c):
        r = rows_of(c)
        sol = sol_s[r, :]
        out = jnp.dot(lhs2_s[c], _lane_block_diag(sol), preferred_element_type=F32)

        def finish():
            for j in heads:
                u_col, w_col = 2 * j * hd, (2 * j + 1) * hd
                n_s[j, c] = out[:2 * c64, u_col:u_col + hd]
                mp_s[j, c, :2 * c64, :] = out[:2 * c64, w_col:w_col + hd].astype(BF16)
                mp_s[j, c, 2 * c64:, :] = (qd_s[j, r, :].astype(F32)
                                           - out[2 * c64:, w_col:w_col + hd]).astype(BF16)
                r_s[j, r, :] = out[2 * c64:, u_col:u_col + hd]

        return finish

    st_s[...] = jnp.zeros_like(st_s)
    nw = nw_ref[...]

    def stage_state(c):
        r = rows_of(c)
        sts = [st_s[j] for j in heads]
        outs = [jnp.dot(mp_s[j, c], sts[j].astype(BF16), preferred_element_type=F32)
                for j in heads]

        def finish():
            for j in heads:
                end = (j + 1) * c64
                gl = gcr_s[pl.ds(c, 1), end - 1:end]
                st_s[j] = sts[j] * jnp.exp(gl) - outs[j][:2 * c64] + n_s[j, c]
                r_s[j, r, :] = r_s[j, r, :] + outs[j][2 * c64:]

        return finish

    def stage_norm(c):
        r = rows_of(c)
        for j in heads:
            o = r_s[j, r, :]
            og = (o * lax.rsqrt(jnp.mean(o * o, axis=-1, keepdims=True) + RMS_EPS)
                  * nw * _silu(z_ref[r, j * hd:(j + 1) * hd]))
            y_ref[r, j * hd:(j + 1) * hd] = og.astype(BF16)

    merges = [stage(lv) for lv in range(MERGE_LEVELS)
              for stage in (stage_merge_products, stage_merge_apply)]
    _software_pipeline([stage_prep, stage_gram, stage_factor] + merges
                       + [stage_solve, stage_fold, stage_state, stage_norm], nchunk)


def _gdn(proj, small, rows, conv_w, a_log, dt_bias, norm_w, bsz, seq):
    nchunk = seq // GDN_CHUNK
    hd = HEAD_DIM
    c64 = GDN_CHUNK
    qo, ko = COL_GDN_Q // hd, COL_GDN_K // hd
    vo, zo = COL_GDN_V // (2 * hd), COL_GDN_Z // (2 * hd)
    cvo = (2 * GDN_QK_W) // (2 * hd)
    smem = pl.BlockSpec(memory_space=pltpu.SMEM)
    pad = (GDN_V_HEADS, N_SMALL - 2 * GDN_V_HEADS)
    gate_params = jnp.stack([jnp.pad(a_log, pad), jnp.pad(dt_bias, pad)])
    return pl.pallas_call(
        functools.partial(_gdn_kernel, nchunk=nchunk),
        name="gdn",
        grid=(bsz, GDN_QK_HEADS),
        in_specs=[smem, smem,
                  pl.BlockSpec((2, N_SMALL), lambda b, h: (0, 0)),
                  pl.BlockSpec((1, hd), lambda b, h: (0, 0)),
                  pl.BlockSpec((GDN_CONV, hd), lambda b, h: (0, h)),
                  pl.BlockSpec((GDN_CONV, hd), lambda b, h: (0, GDN_QK_HEADS + h)),
                  pl.BlockSpec((GDN_CONV, 2 * hd), lambda b, h: (0, cvo + h)),
                  pl.BlockSpec((seq, hd), lambda b, h: (b, qo + h)),
                  pl.BlockSpec((seq, hd), lambda b, h: (b, ko + h)),
                  pl.BlockSpec((seq, 2 * hd), lambda b, h: (b, vo + h)),
                  pl.BlockSpec((seq, 2 * hd), lambda b, h: (b, zo + h)),
                  pl.BlockSpec((seq, N_SMALL), lambda b, h: (b, 0)),
                  pl.BlockSpec((None, None, 4, nchunk, GDN_CHUNK), lambda b, h: (b, h, 0, 0, 0))],
        out_specs=pl.BlockSpec((seq, 2 * hd), lambda b, h: (b, h)),
        out_shape=jax.ShapeDtypeStruct((bsz * seq, GDN_V_W), BF16),
        scratch_shapes=[pltpu.VMEM((seq, hd), BF16),
                        pltpu.VMEM((seq, hd), BF16),
                        pltpu.VMEM((nchunk, 2 * c64, hd), BF16),
                        pltpu.VMEM((2, seq, 2 * hd), BF16),
                        pltpu.VMEM((2, seq, hd), BF16),
                        pltpu.VMEM((seq, 2 * c64), F32),
                        pltpu.VMEM((seq, 2 * c64), F32),
                        pltpu.VMEM((nchunk, 2 * c64), F32),
                        pltpu.VMEM((nchunk, 2 * c64, 2 * c64), F32),
                        pltpu.VMEM((seq, 2 * c64), F32),
                        pltpu.VMEM((nchunk, 3 * c64, hd), BF16),
                        pltpu.VMEM((seq, 4 * hd), BF16),
                        pltpu.VMEM((2, nchunk, 3 * c64, hd), BF16),
                        pltpu.VMEM((2, nchunk, hd, hd), F32),
                        pltpu.VMEM((2, seq, hd), F32),
                        pltpu.VMEM((2, hd, hd), F32)]
                       + [pltpu.VMEM((seq, 2 * c64), F32)] * (MERGE_LEVELS + 1)
                       + [pltpu.VMEM((seq, 2 * c64), BF16)] * MERGE_LEVELS,
        compiler_params=_vmem("gdn"),
    )(a_log, dt_bias, gate_params, norm_w, conv_w, conv_w, conv_w, proj, proj, proj, proj, small, rows)


def _merge_kernel(ya_ref, yb_ref, ga_ref, gb_ref, x_ref, g1_ref, sh2_ref, sc2_ref,
                  lng_ref, lnb_ref, wpm_ref, wpg_ref, wo_ref, x1_ref, h2_ref):
    pa = jnp.dot(ya_ref[...], wpm_ref[...], preferred_element_type=F32)
    pb = jnp.dot(yb_ref[...], wpg_ref[...], preferred_element_type=F32)
    merged = jax.nn.sigmoid(ga_ref[...]) * pa + jax.nn.sigmoid(gb_ref[...]) * pb
    y = jnp.dot(merged.astype(BF16), wo_ref[...], preferred_element_type=F32)
    x1 = _layer_norm(DEEPNORM_ALPHA * x_ref[...] + g1_ref[...] * y, lng_ref[...], lnb_ref[...])
    x1_ref[...] = x1
    h2_ref[...] = (x1 * (1.0 + sc2_ref[...]) + sh2_ref[...]).astype(BF16)


def _merge(ya, yb, proj, x2, mod3, ln_g, ln_b, wpm, wpg, wo, seq):
    t = x2.shape[0]
    tm = MERGE_TM
    per_b = seq // tm
    d = D_MODEL

    def modspec(k):
        return pl.BlockSpec((None, 1, d), lambda i: (i // per_b, 0, k))

    def const(shape):
        return pl.BlockSpec(shape, lambda i: (0, 0), pipeline_mode=pl.Buffered(1))

    return pl.pallas_call(
        _merge_kernel,
        name="merge",
        grid=(t // tm,),
        in_specs=[pl.BlockSpec((tm, MOBA_W), lambda i: (i, 0)),
                  pl.BlockSpec((tm, GDN_V_W), lambda i: (i, 0)),
                  pl.BlockSpec((tm, d), lambda i: (i, COL_GATE_A // d)),
                  pl.BlockSpec((tm, d), lambda i: (i, COL_GATE_B // d)),
                  pl.BlockSpec((tm, d), lambda i: (i, 0)),
                  modspec(2), modspec(3), modspec(4),
                  const((1, d)), const((1, d)),
                  const((MOBA_W, d)), const((GDN_V_W, d)), const((d, d))],
        out_specs=[pl.BlockSpec((tm, d), lambda i: (i, 0)),
                   pl.BlockSpec((tm, d), lambda i: (i, 0))],
        out_shape=[jax.ShapeDtypeStruct((t, d), F32), jax.ShapeDtypeStruct((t, d), BF16)],
        compiler_params=_vmem("merge"),
    )(ya, yb, proj, proj, x2, mod3, mod3, mod3, ln_g, ln_b, wpm, wpg, wo)


def _ffn_kernel(h_ref, x1_ref, g2_ref, lng_ref, lnb_ref, wg_ref, wu_ref, wo_ref, o_ref, acc_ref):
    f = pl.program_id(1)

    @pl.when(f == 0)
    def _():
        acc_ref[...] = jnp.zeros_like(acc_ref)

    h = h_ref[...]
    gate = jnp.dot(h, wg_ref[...], preferred_element_type=F32)
    up = jnp.dot(h, wu_ref[...], preferred_element_type=F32)
    act = (_silu(gate) * up).astype(BF16)
    acc_ref[...] += jnp.dot(act, wo_ref[...], preferred_element_type=F32)

    @pl.when(f == pl.num_programs(1) - 1)
    def _():
        r = DEEPNORM_ALPHA * x1_ref[...] + g2_ref[...] * acc_ref[...]
        o_ref[...] = _layer_norm(r, lng_ref[...], lnb_ref[...])


def _ffn(h2, x1, mod3, ln_g, ln_b, w_in, w_out, seq):
    t = h2.shape[0]
    tm, tf = FFN_TM, FFN_TF
    per_b = seq // tm
    d = D_MODEL
    nf = D_FF // tf
    return pl.pallas_call(
        _ffn_kernel,
        name="ffn",
        grid=(t // tm, nf),
        in_specs=[pl.BlockSpec((tm, d), lambda i, f: (i, 0)),
                  pl.BlockSpec((tm, d), lambda i, f: (i, 0)),
                  pl.BlockSpec((None, 1, d), lambda i, f: (i // per_b, 0, 5)),
                  pl.BlockSpec((1, d), lambda i, f: (0, 0)),
                  pl.BlockSpec((1, d), lambda i, f: (0, 0)),
                  pl.BlockSpec((d, tf), lambda i, f: (0, _serpentine(i, f, nf))),
                  pl.BlockSpec((d, tf), lambda i, f: (0, nf + _serpentine(i, f, nf))),
                  pl.BlockSpec((tf, d), lambda i, f: (_serpentine(i, f, nf), 0))],
        out_specs=pl.BlockSpec((tm, d), lambda i, f: (i, 0)),
        out_shape=jax.ShapeDtypeStruct((t, d), F32),
        scratch_shapes=[pltpu.VMEM((tm, d), F32)],
        compiler_params=_vmem("ffn"),
    )(h2, x1, mod3, ln_g, ln_b, w_in, w_in, w_out)


def _rel_bucket(dist):
    max_exact = REL_BUCKETS // 2
    n = jnp.maximum(dist, 0)
    nf = jnp.maximum(n, 1).astype(F32)
    large = max_exact + (jnp.log(nf / max_exact) / math.log(REL_MAX_DIST / max_exact)
                         * (REL_BUCKETS - max_exact)).astype(jnp.int32)
    large = jnp.minimum(large, REL_BUCKETS - 1)
    return jnp.where(n < max_exact, n, large)


def _layer(x, c, w_ada, b_ada, w_in, conv_w, a_log, dt_bias, gdn_norm_w, rel_bias,
           w_proj_moba, w_proj_gdn, w_out, ln1_g, ln1_b, w_ffn_in, w_ffn_out, ln2_g, ln2_b):
    bsz, seq, d = x.shape
    t = bsz * seq
    x2 = x.reshape(t, d)

    mod = _ada_mod(c, w_ada, b_ada)
    mod3 = mod.reshape(bsz, 1, 6 * d)

    w_all = w_in.astype(BF16)
    n_gates = 2 * GDN_V_HEADS
    w_gate = w_all[:, N_HEAD + n_gates:]
    w_small = jnp.pad(w_all[:, N_HEAD:N_HEAD + n_gates], ((0, 0), (0, N_SMALL - n_gates)))

    proj, small = _in_proj(x2, mod3, w_all, w_gate, w_small, seq)

    ii = jnp.arange(MOBA_BLOCK, dtype=jnp.int32)
    dist = ii[None, :] - ii[:, None]
    bko = _rel_bucket(dist)
    bkp = _rel_bucket(dist + MOBA_BLOCK)
    ya = _moba(proj, rel_bias, bko, bkp, bsz, seq)

    nchunk = seq // GDN_CHUNK
    sm_t = small[:, :2 * GDN_V_HEADS].reshape(bsz, seq, 2, GDN_QK_HEADS, 2)
    rows = sm_t.transpose(0, 3, 2, 4, 1).reshape(bsz, GDN_QK_HEADS, 4, nchunk, GDN_CHUNK)
    yb = _gdn(proj, small, rows, conv_w, a_log, dt_bias, gdn_norm_w.reshape(1, HEAD_DIM),
              bsz, seq)

    x1, h2 = _merge(ya, yb, proj, x2, mod3, ln1_g.reshape(1, d), ln1_b.reshape(1, d),
                    w_proj_moba.astype(BF16), w_proj_gdn.astype(BF16), w_out.astype(BF16), seq)
    out = _ffn(h2, x1, mod3, ln2_g.reshape(1, d), ln2_b.reshape(1, d),
               w_ffn_in.astype(BF16), w_ffn_out.astype(BF16), seq)
    return out.reshape(bsz, seq, d)


def kernel(x, c, w_ada, b_ada, w_in, conv_w, a_log, dt_bias, gdn_norm_w, rel_bias, w_proj_moba,
           w_proj_gdn, w_out, ln1_g, ln1_b, w_ffn_in, w_ffn_out, ln2_g, ln2_b):
    depth = w_ada.shape[0]
    for l in range(depth):
        x = _layer(x, c, w_ada[l], b_ada[l], w_in[l], conv_w[l], a_log[l], dt_bias[l],
                   gdn_norm_w[l], rel_bias, w_proj_moba[l], w_proj_gdn[l], w_out[l],
                   ln1_g[l], ln1_b[l], w_ffn_in[l], w_ffn_out[l], ln2_g[l], ln2_b[l])
    return x
```

```python
import functools
import math

import jax
import jax.numpy as jnp
from jax import lax
from jax.experimental import pallas as pl
from jax.experimental.pallas import tpu as pltpu

F32 = jnp.float32
BF16 = jnp.bfloat16

D_MODEL = 2048
MOBA_HEADS = 8
HEAD_DIM = 128
MOBA_W = MOBA_HEADS * HEAD_DIM
MOBA_BLOCK = 256
MOBA_TOPK = 3
REL_BUCKETS = 32
REL_MAX_DIST = 128
GDN_QK_HEADS = 8
GDN_V_HEADS = 16
GDN_QK_W = GDN_QK_HEADS * HEAD_DIM
GDN_V_W = GDN_V_HEADS * HEAD_DIM
GDN_CONV = 4
GDN_CHUNK = 64
D_FF = 5632
DEEPNORM_ALPHA = 2.0 ** 0.25
LN_EPS = 1e-5
RMS_EPS = 1e-6
NEG_INF = -1e30

COL_GATE_A = 0
COL_GATE_B = 2048
COL_MOBA_Q = 4096
COL_MOBA_K = 5120
COL_MOBA_V = 6144
COL_GDN_Q = 7168
COL_GDN_K = 8192
COL_GDN_V = 9216
COL_GDN_Z = 11264
N_MAIN = 13312
N_GATE = 4096
N_HEAD = N_MAIN - N_GATE
N_SMALL = 128

V7X_VMEM_MIB = 64
SUBLANES = 8
BF16_SUBLANES = 16

ADA_TN = 1024
INPROJ_TM, INPROJ_TN = 1024, 1024
MERGE_TM = 256
FFN_TM, FFN_TF = 512, 512
VMEM_LIMIT_MIB = {"ada_mod": 40, "in_proj": 56, "moba": 48, "gdn": 58, "merge": 56, "ffn": 48}
assert max(VMEM_LIMIT_MIB.values()) < V7X_VMEM_MIB


def _vmem(name):
    return pltpu.CompilerParams(vmem_limit_bytes=VMEM_LIMIT_MIB[name] * 1024 * 1024)


def _silu(x):
    return x * jax.nn.sigmoid(x)


def _layer_norm(r, gain, bias):
    mu = jnp.mean(r, axis=-1, keepdims=True)
    d = r - mu
    var = jnp.mean(d * d, axis=-1, keepdims=True)
    return d * lax.rsqrt(var + LN_EPS) * gain + bias


def _ada_kernel(c_ref, w_ref, b_ref, o_ref):
    sc = _silu(c_ref[...])
    o_ref[...] = jnp.dot(sc, w_ref[...], precision=lax.Precision.HIGHEST,
                         preferred_element_type=F32) + b_ref[...]


def _ada_mod(c, w_ada, b_ada):
    bsz = c.shape[0]
    n = w_ada.shape[1]
    tn = ADA_TN
    return pl.pallas_call(
        _ada_kernel,
        name="ada_mod",
        grid=(n // tn,),
        in_specs=[pl.BlockSpec((bsz, D_MODEL), lambda j: (0, 0)),
                  pl.BlockSpec((D_MODEL, tn), lambda j: (0, j)),
                  pl.BlockSpec((1, tn), lambda j: (0, j))],
        out_specs=pl.BlockSpec((bsz, tn), lambda j: (0, j)),
        out_shape=jax.ShapeDtypeStruct((bsz, n), F32),
        compiler_params=_vmem("ada_mod"),
    )(c, w_ada, b_ada.reshape(1, n))


def _serpentine(i, j, n):
    return jnp.where(i % 2 == 0, j, n - 1 - j)


def _inproj_kernel(x_ref, sh_ref, sc_ref, w_ref, wg_ref, ws_ref, o_ref, os_ref, h_ref, *, n_head,
                   n_tiles):
    @pl.when(pl.program_id(1) == 0)
    def _():
        h = x_ref[...] * (1.0 + sc_ref[...]) + sh_ref[...]
        h_ref[...] = h.astype(BF16)
        os_ref[...] = jnp.dot(h_ref[...], ws_ref[...], preferred_element_type=F32)

    col = _serpentine(pl.program_id(0), pl.program_id(1), n_tiles)

    @pl.when(col < n_head)
    def _():
        o_ref[...] = jnp.dot(h_ref[...], w_ref[...], preferred_element_type=F32)

    @pl.when(col >= n_head)
    def _():
        o_ref[...] = jnp.dot(h_ref[...], wg_ref[...], preferred_element_type=F32)


def _in_proj(x2, mod3, w_all, w_gate, w_small, seq):
    t = x2.shape[0]
    tm, tn = INPROJ_TM, INPROJ_TN
    per_b = seq // tm
    n_head, n_gate, n_tiles = N_HEAD // tn, N_GATE // tn, N_MAIN // tn

    def col(i, j):
        return _serpentine(i, j, n_tiles)

    return pl.pallas_call(
        functools.partial(_inproj_kernel, n_head=n_head, n_tiles=n_tiles),
        name="in_proj",
        grid=(t // tm, n_tiles),
        in_specs=[pl.BlockSpec((tm, D_MODEL), lambda i, j: (i, 0)),
                  pl.BlockSpec((None, 1, D_MODEL), lambda i, j: (i // per_b, 0, 0)),
                  pl.BlockSpec((None, 1, D_MODEL), lambda i, j: (i // per_b, 0, 1)),
                  pl.BlockSpec((D_MODEL, tn), lambda i, j: (0, jnp.minimum(col(i, j), n_head - 1))),
                  pl.BlockSpec((D_MODEL, tn), lambda i, j: (0, jnp.maximum(col(i, j) - n_head, 0))),
                  pl.BlockSpec((D_MODEL, N_SMALL), lambda i, j: (0, 0))],
        out_specs=[pl.BlockSpec((tm, tn), lambda i, j: (i, (col(i, j) + n_gate) % n_tiles)),
                   pl.BlockSpec((tm, N_SMALL), lambda i, j: (i, 0))],
        out_shape=[jax.ShapeDtypeStruct((t, N_MAIN), F32),
                   jax.ShapeDtypeStruct((t, N_SMALL), F32)],
        scratch_shapes=[pltpu.VMEM((tm, D_MODEL), BF16)],
        compiler_params=_vmem("in_proj"),
    )(x2, mod3, mod3, w_all, w_gate, w_small)


def _moba_kernel(rel_ref, bko_ref, bkp_ref, q_ref, k_ref, v_ref, o_ref,
                 bias_own, bias_prev, kb_ref, vt_ref, *, nb):
    h = pl.program_id(0)
    blk = MOBA_BLOCK
    inv_scale = HEAD_DIM ** 0.5
    scale_log2e = HEAD_DIM ** -0.5 * math.log2(math.e)

    @pl.when(pl.program_id(1) == 0)
    def _():
        bo = bko_ref[...]
        bp = bkp_ref[...]
        acc_o = jnp.zeros((blk, blk), F32)
        acc_p = jnp.zeros((blk, blk), F32)
        for kk in range(REL_BUCKETS):
            val = rel_ref[kk, h] * inv_scale
            acc_o = jnp.where(bo == kk, val, acc_o)
            acc_p = jnp.where(bp == kk, val, acc_p)
        bias_own[...] = acc_o
        bias_prev[...] = acc_p

    bias_far = rel_ref[REL_BUCKETS - 1, h] * inv_scale
    kf = k_ref[...]
    kmean = jnp.mean(kf.reshape(nb, blk, HEAD_DIM), axis=1)
    kb_ref[...] = kf.astype(BF16)
    nt_dims = (((1,), (1,)), ((), ()))
    eye = (lax.broadcasted_iota(jnp.int32, (HEAD_DIM, HEAD_DIM), 0)
           == lax.broadcasted_iota(jnp.int32, (HEAD_DIM, HEAD_DIM), 1)).astype(BF16)
    vt_ref[:HEAD_DIM, :] = lax.dot_general(eye, v_ref[...].astype(BF16), nt_dims,
                                           preferred_element_type=F32).astype(BF16)
    pad_rows = vt_ref.shape[0] - HEAD_DIM
    vt_ref[HEAD_DIM:, :] = (lax.broadcasted_iota(jnp.int32, (pad_rows, vt_ref.shape[1]), 0)
                            == 0).astype(BF16)
    causal = (lax.broadcasted_iota(jnp.int32, (blk, blk), 0)
              <= lax.broadcasted_iota(jnp.int32, (blk, blk), 1))

    def scores(i):
        qi = q_ref[i * blk:(i + 1) * blk, :]
        qb = qi.astype(BF16)
        sel = None
        if i > MOBA_TOPK:
            route = lax.dot_general(kmean, qi, nt_dims, precision=lax.Precision.HIGHEST,
                                    preferred_element_type=F32)
            rc = [route[n:n + 1, :] for n in range(i)]
            sel = []
            for n in range(i):
                rank = jnp.zeros((1, blk), jnp.int32)
                for m in range(i):
                    if m == n:
                        continue
                    beats = (rc[m] >= rc[n]) if m < n else (rc[m] > rc[n])
                    rank = rank + beats.astype(jnp.int32)
                sel.append(rank < MOBA_TOPK)
        t_list = []
        for n in range(i + 1):
            t = lax.dot_general(kb_ref[n * blk:(n + 1) * blk, :], qb, nt_dims,
                                preferred_element_type=F32)
            if n == i:
                t = jnp.where(causal, t + bias_own[...], NEG_INF)
            else:
                if n == i - 1:
                    t = t + bias_prev[...]
                if sel is not None:
                    t = jnp.where(sel[n], t, NEG_INF)
            t_list.append(t)
        return t_list

    t_next = scores(0)
    for i in range(nb):
        t_list = t_next
        if i + 1 < nb:
            t_next = scores(i + 1)
        n_far = max(i - 1, 0)
        m_run = jnp.max(t_list[n_far], axis=0, keepdims=True)
        for t in t_list[n_far + 1:]:
            m_run = jnp.maximum(m_run, jnp.max(t, axis=0, keepdims=True))
        if n_far:
            m_far = jnp.max(t_list[0], axis=0, keepdims=True)
            for t in t_list[1:n_far]:
                m_far = jnp.maximum(m_far, jnp.max(t, axis=0, keepdims=True))
            m_run = jnp.maximum(m_run, m_far + bias_far)
        acc = jnp.zeros((vt_ref.shape[0], blk), F32)
        for n, t in enumerate(t_list):
            offset = m_run - bias_far if n < n_far else m_run
            p = jnp.exp2((t - offset) * scale_log2e)
            acc = acc + jnp.dot(vt_ref[:, n * blk:(n + 1) * blk], p.astype(BF16),
                                preferred_element_type=F32)
        out = acc[:HEAD_DIM] / acc[HEAD_DIM:HEAD_DIM + 1]
        o_ref[i * blk:(i + 1) * blk, :] = out.T.astype(BF16)


def _moba(proj, rel_bias, bko, bkp, bsz, seq):
    nb = seq // MOBA_BLOCK
    qo, ko, vo = COL_MOBA_Q // HEAD_DIM, COL_MOBA_K // HEAD_DIM, COL_MOBA_V // HEAD_DIM
    blk = MOBA_BLOCK
    return pl.pallas_call(
        functools.partial(_moba_kernel, nb=nb),
        name="moba",
        grid=(MOBA_HEADS, bsz),
        in_specs=[pl.BlockSpec(memory_space=pltpu.SMEM),
                  pl.BlockSpec((blk, blk), lambda h, b: (0, 0)),
                  pl.BlockSpec((blk, blk), lambda h, b: (0, 0)),
                  pl.BlockSpec((seq, HEAD_DIM), lambda h, b: (b, qo + h)),
                  pl.BlockSpec((seq, HEAD_DIM), lambda h, b: (b, ko + h)),
                  pl.BlockSpec((seq, HEAD_DIM), lambda h, b: (b, vo + h))],
        out_specs=pl.BlockSpec((seq, HEAD_DIM), lambda h, b: (b, h)),
        out_shape=jax.ShapeDtypeStruct((bsz * seq, MOBA_W), BF16),
        scratch_shapes=[pltpu.VMEM((blk, blk), F32), pltpu.VMEM((blk, blk), F32),
                        pltpu.VMEM((seq, HEAD_DIM), BF16),
                        pltpu.VMEM((HEAD_DIM + BF16_SUBLANES, seq), BF16)],
        compiler_params=_vmem("moba"),
    )(rel_bias, bko, bkp, proj, proj, proj)


def _conv_silu(xh, w):
    acc = xh[SUBLANES:, :] * w[GDN_CONV - 1:GDN_CONV, :]
    for s in range(1, GDN_CONV):
        acc = acc + pltpu.roll(xh, s, axis=0)[SUBLANES:, :] * w[GDN_CONV - 1 - s:GDN_CONV - s, :]
    return _silu(acc)


def _l2norm(x):
    return x * lax.rsqrt(jnp.sum(x * x, axis=-1, keepdims=True) + RMS_EPS)


def _softplus(x):
    return jnp.maximum(x, 0.0) + jnp.log1p(jnp.exp(-jnp.abs(x)))


ELIM_BLOCK = 8
MERGE_LEVELS = (GDN_CHUNK // ELIM_BLOCK).bit_length() - 1


def _block_diag_inverse(lpair):
    n = lpair.shape[0]
    nv = n // SUBLANES
    vpb = ELIM_BLOCK // SUBLANES
    rid = lax.broadcasted_iota(jnp.int32, (SUBLANES, 2 * n), 0)
    lid = lax.broadcasted_iota(jnp.int32, (SUBLANES, 2 * n), 1)
    t_rows = [(lid % n == rid + v * SUBLANES).astype(F32) for v in range(nv)]
    l_rows = [lpair[v * SUBLANES:(v + 1) * SUBLANES, :] for v in range(nv)]
    for m in range(n - 1):
        v0, s0 = divmod(m, SUBLANES)
        v_end = (v0 // vpb + 1) * vpb
        row = t_rows[v0][s0:s0 + 1, :]
        col = (lid // n) * n + m
        for v in range(v0 if s0 < SUBLANES - 1 else v0 + 1, v_end):
            t_rows[v] = t_rows[v] - jnp.take_along_axis(l_rows[v], col, axis=1) * row
    return jnp.concatenate(t_rows, axis=0)


def _lane_block_diag(pair):
    first = lax.broadcasted_iota(jnp.int32, pair.shape, 1) < pair.shape[1] // 2
    zero = jnp.zeros_like(pair)
    return jnp.concatenate([jnp.where(first, pair, zero), jnp.where(first, zero, pair)], axis=0)


def _merge_lower_products(lpair, tpair, k):
    n = lpair.shape[0]
    ri = lax.broadcasted_iota(jnp.int32, lpair.shape, 0)
    ci = lax.broadcasted_iota(jnp.int32, lpair.shape, 1) % n
    off = (ri // (2 * k) == ci // (2 * k)) & (ri // k > ci // k)
    lk = jnp.where(off, lpair, 0.0).astype(BF16)
    return jnp.dot(lk, _lane_block_diag(tpair.astype(BF16)), preferred_element_type=F32)


def _merge_apply(tpair, lt):
    return tpair - jnp.dot(tpair.astype(BF16), _lane_block_diag(lt), preferred_element_type=F32)


PIPELINE_UNROLL = 4


def _software_pipeline(stages, n):
    ns = len(stages)

    def run(it, lo, hi):
        conts = [stages[s](it - s) for s in reversed(range(lo, hi))]
        for cont in conts:
            if cont is not None:
                cont()

    for it in range(ns - 1):
        run(it, 0, it + 1)

    steady = n - (ns - 1)
    assert steady % PIPELINE_UNROLL == 0, (n, ns)

    def body(k, carry):
        first = ns - 1 + k * PIPELINE_UNROLL
        for u in range(PIPELINE_UNROLL):
            run(first + u, 0, ns)
        return carry

    lax.fori_loop(0, steady // PIPELINE_UNROLL, body, 0)
    for it in range(n, n + ns - 1):
        run(it, it - n + 1, ns)


def _lane_pick(x, lane, idx):
    return jnp.sum(jnp.where(lane == idx, x, 0.0), axis=-1, keepdims=True)


def _gdn_kernel(alog_ref, dtb_ref, gp_ref, nw_ref, cwq_ref, cwk_ref, cwv_ref,
                q_ref, k_ref, v_ref, z_ref, sm_ref, rw_ref, y_ref,
                kb_s, qn_s, kdp_s, rhs_s, qd_s, gcb_s, betab_s, gcr_s, gram_s, lm_s, lhs2_s,
                sol_s, mp_s, n_s, r_s, st_s, *merge_s, nchunk):
    tl_s, lt_s = merge_s[:MERGE_LEVELS + 1], merge_s[MERGE_LEVELS + 1:]
    hq = pl.program_id(1)
    c64 = GDN_CHUNK
    hd = HEAD_DIM
    heads = range(2)

    tri_u = (lax.broadcasted_iota(jnp.int32, (c64, c64), 0)
             <= lax.broadcasted_iota(jnp.int32, (c64, c64), 1)).astype(F32)
    gc_rows = []
    for j in heads:
        hv = 2 * hq + j
        a_neg_r = -jnp.exp(jnp.full((nchunk, c64), alog_ref[hv], F32))
        g_row = a_neg_r * _softplus(rw_ref[2 + j] + dtb_ref[hv])
        gc_rows.append(jnp.dot(g_row, tri_u, precision=lax.Precision.HIGHEST,
                               preferred_element_type=F32))
    gcr_s[...] = jnp.concatenate(gc_rows, axis=1)

    rows = lax.broadcasted_iota(jnp.int32, (c64, 2 * c64), 0)
    cols = lax.broadcasted_iota(jnp.int32, (c64, 2 * c64), 1) % c64
    tril = rows >= cols
    strict = rows > cols
    eye2 = (lax.broadcasted_iota(jnp.int32, (2 * c64, 2 * c64), 0)
            == lax.broadcasted_iota(jnp.int32, (2 * c64, 2 * c64), 1)).astype(BF16)
    nt_dims = (((1,), (1,)), ((), ()))

    def rows_of(c):
        start = c * c64
        return pl.ds(start if isinstance(c, int) else pl.multiple_of(start, c64), c64)

    def with_halo(x_ref, c):
        if isinstance(c, int) and c == 0:
            return jnp.concatenate([jnp.zeros((SUBLANES, x_ref.shape[1]), F32), x_ref[:c64, :]],
                                   axis=0)
        start = c * c64 - SUBLANES
        if not isinstance(c, int):
            start = pl.multiple_of(start, SUBLANES)
        return x_ref[pl.ds(start, c64 + SUBLANES), :]

    lane = lax.broadcasted_iota(jnp.int32, (c64, N_SMALL), 1)
    first = lane < c64
    pos = lax.broadcasted_iota(jnp.int32, (c64, N_SMALL), 0)

    def stage_prep(c):
        r = rows_of(c)
        qn = _l2norm(_conv_silu(with_halo(q_ref, c), cwq_ref[...])) * (hd ** -0.5)
        kn = _l2norm(_conv_silu(with_halo(k_ref, c), cwk_ref[...]))
        vc = _conv_silu(with_halo(v_ref, c), cwv_ref[...])
        qn_s[r, :] = qn.astype(BF16)
        kb_s[r, :] = kn.astype(BF16)
        sm = sm_ref[r, :]
        sig_all = jax.nn.sigmoid(sm)
        gc_all = -jnp.exp(gp_ref[0:1, :]) * _softplus(sm + gp_ref[1:2, :])
        sft = 1
        while sft < c64:
            gc_all = gc_all + jnp.where(pos >= sft, pltpu.roll(gc_all, sft, axis=0), 0.0)
            sft *= 2
        rest_all = gc_all[c64 - 1:, :] - gc_all
        betas, gc_cols = [], []
        for j in heads:
            hv = 2 * hq + j
            beta = _lane_pick(sig_all, lane, hv)
            gc_col = _lane_pick(gc_all, lane, GDN_V_HEADS + hv)
            rest_col = _lane_pick(rest_all, lane, GDN_V_HEADS + hv)
            eg = jnp.exp(gc_col)
            rhs_s[j, r, :hd] = (vc[:, j * hd:(j + 1) * hd] * beta).astype(BF16)
            rhs_s[j, r, hd:] = (kn * (beta * eg)).astype(BF16)
            qd_s[j, r, :] = (qn * eg).astype(BF16)
            kdp_s[c, j * c64:(j + 1) * c64, :] = (kn * jnp.exp(rest_col)).astype(BF16)
            betas.append(beta)
            gc_cols.append(gc_col)
        gcb_s[r, :] = jnp.where(first, gc_cols[0], gc_cols[1])
        betab_s[r, :] = jnp.where(first, betas[0], betas[1])

    def stage_gram(c):
        r = rows_of(c)
        kb = kb_s[r, :]
        kq = jnp.concatenate([kb, qn_s[r, :]], axis=0)
        gram = lax.dot_general(kq, jnp.concatenate([kb, kb], axis=0), nt_dims,
                               preferred_element_type=F32)
        kd_t = lax.dot_general(eye2, kdp_s[c], nt_dims, preferred_element_type=F32)

        def finish():
            gram_s[c] = gram
            lhs2_s[c, :2 * c64, :] = kd_t.astype(BF16)

        return finish

    def stage_factor(c):
        r = rows_of(c)
        dec = jnp.exp(jnp.where(tril, gcb_s[r, :] - gcr_s[pl.ds(c, 1), :], NEG_INF))
        lpair = jnp.where(strict, gram_s[c, :c64, :] * dec, 0.0) * betab_s[r, :]
        lm_s[r, :] = lpair
        lhs2_s[c, 2 * c64:, :] = (gram_s[c, c64:, :] * dec).astype(BF16)
        tl_s[0][r, :] = _block_diag_inverse(lpair)

    def stage_merge_products(level):
        def stage(c):
            r = rows_of(c)
            lt = _merge_lower_products(lm_s[r, :], tl_s[level][r, :], ELIM_BLOCK << level)

            def finish():
                lt_s[level][r, :] = lt.astype(BF16)

            return finish
        return stage

    def stage_merge_apply(level):
        def stage(c):
            r = rows_of(c)
            merged = _merge_apply(tl_s[level][r, :], lt_s[level][r, :])

            def finish():
                tl_s[level + 1][r, :] = merged

            return finish
        return stage

    def stage_solve(c):
        r = rows_of(c)
        zero = jnp.zeros((c64, 2 * hd), BF16)
        rhs_bd = jnp.concatenate([jnp.concatenate([rhs_s[0, r, :], zero], axis=1),
                                  jnp.concatenate([zero, rhs_s[1, r, :]], axis=1)], axis=0)
        sol = jnp.dot(tl_s[MERGE_LEVELS][r, :].astype(BF16), rhs_bd,
                      preferred_element_type=F32)

        def finish():
            sol_s[r, :] = sol.astype(BF16)

        return finish

    def stage_fold(c):
        r = rows_of(c)
        sol = sol_s[r, :]
        out = jnp.dot(lhs2_s[c], _lane_block_diag(sol), preferred_element_type=F32)

        def finish():
            for j in heads:
                u_col, w_col = 2 * j * hd, (2 * j + 1) * hd
                n_s[j, c] = out[:2 * c64, u_col:u_col + hd]
                mp_s[j, c, :2 * c64, :] = out[:2 * c64, w_col:w_col + hd].astype(BF16)
                mp_s[j, c, 2 * c64:, :] = (qd_s[j, r, :].astype(F32)
                                           - out[2 * c64:, w_col:w_col + hd]).astype(BF16)
                r_s[j, r, :] = out[2 * c64:, u_col:u_col + hd]

        return finish

    st_s[...] = jnp.zeros_like(st_s)
    nw = nw_ref[...]

    def stage_state(c):
        r = rows_of(c)
        sts = [st_s[j] for j in heads]
        outs = [jnp.dot(mp_s[j, c], sts[j].astype(BF16), preferred_element_type=F32)
                for j in heads]

        def finish():
            for j in heads:
                end = (j + 1) * c64
                gl = gcr_s[pl.ds(c, 1), end - 1:end]
                st_s[j] = sts[j] * jnp.exp(gl) - outs[j][:2 * c64] + n_s[j, c]
                r_s[j, r, :] = r_s[j, r, :] + outs[j][2 * c64:]

        return finish

    def stage_norm(c):
        r = rows_of(c)
        for j in heads:
            o = r_s[j, r, :]
            og = (o * lax.rsqrt(jnp.mean(o * o, axis=-1, keepdims=True) + RMS_EPS)
                  * nw * _silu(z_ref[r, j * hd:(j + 1) * hd]))
            y_ref[r, j * hd:(j + 1) * hd] = og.astype(BF16)

    merges = [stage(lv) for lv in range(MERGE_LEVELS)
              for stage in (stage_merge_products, stage_merge_apply)]
    _software_pipeline([stage_prep, stage_gram, stage_factor] + merges
                       + [stage_solve, stage_fold, stage_state, stage_norm], nchunk)


def _gdn(proj, small, rows, conv_w, a_log, dt_bias, norm_w, bsz, seq):
    nchunk = seq // GDN_CHUNK
    hd = HEAD_DIM
    c64 = GDN_CHUNK
    qo, ko = COL_GDN_Q // hd, COL_GDN_K // hd
    vo, zo = COL_GDN_V // (2 * hd), COL_GDN_Z // (2 * hd)
    cvo = (2 * GDN_QK_W) // (2 * hd)
    smem = pl.BlockSpec(memory_space=pltpu.SMEM)
    pad = (GDN_V_HEADS, N_SMALL - 2 * GDN_V_HEADS)
    gate_params = jnp.stack([jnp.pad(a_log, pad), jnp.pad(dt_bias, pad)])
    return pl.pallas_call(
        functools.partial(_gdn_kernel, nchunk=nchunk),
        name="gdn",
        grid=(bsz, GDN_QK_HEADS),
        in_specs=[smem, smem,
                  pl.BlockSpec((2, N_SMALL), lambda b, h: (0, 0)),
                  pl.BlockSpec((1, hd), lambda b, h: (0, 0)),
                  pl.BlockSpec((GDN_CONV, hd), lambda b, h: (0, h)),
                  pl.BlockSpec((GDN_CONV, hd), lambda b, h: (0, GDN_QK_HEADS + h)),
                  pl.BlockSpec((GDN_CONV, 2 * hd), lambda b, h: (0, cvo + h)),
                  pl.BlockSpec((seq, hd), lambda b, h: (b, qo + h)),
                  pl.BlockSpec((seq, hd), lambda b, h: (b, ko + h)),
                  pl.BlockSpec((seq, 2 * hd), lambda b, h: (b, vo + h)),
                  pl.BlockSpec((seq, 2 * hd), lambda b, h: (b, zo + h)),
                  pl.BlockSpec((seq, N_SMALL), lambda b, h: (b, 0)),
                  pl.BlockSpec((None, None, 4, nchunk, GDN_CHUNK), lambda b, h: (b, h, 0, 0, 0))],
        out_specs=pl.BlockSpec((seq, 2 * hd), lambda b, h: (b, h)),
        out_shape=jax.ShapeDtypeStruct((bsz * seq, GDN_V_W), BF16),
        scratch_shapes=[pltpu.VMEM((seq, hd), BF16),
                        pltpu.VMEM((seq, hd), BF16),
                        pltpu.VMEM((nchunk, 2 * c64, hd), BF16),
                        pltpu.VMEM((2, seq, 2 * hd), BF16),
                        pltpu.VMEM((2, seq, hd), BF16),
                        pltpu.VMEM((seq, 2 * c64), F32),
                        pltpu.VMEM((seq, 2 * c64), F32),
                        pltpu.VMEM((nchunk, 2 * c64), F32),
                        pltpu.VMEM((nchunk, 2 * c64, 2 * c64), F32),
                        pltpu.VMEM((seq, 2 * c64), F32),
                        pltpu.VMEM((nchunk, 3 * c64, hd), BF16),
                        pltpu.VMEM((seq, 4 * hd), BF16),
                        pltpu.VMEM((2, nchunk, 3 * c64, hd), BF16),
                        pltpu.VMEM((2, nchunk, hd, hd), F32),
                        pltpu.VMEM((2, seq, hd), F32),
                        pltpu.VMEM((2, hd, hd), F32)]
                       + [pltpu.VMEM((seq, 2 * c64), F32)] * (MERGE_LEVELS + 1)
                       + [pltpu.VMEM((seq, 2 * c64), BF16)] * MERGE_LEVELS,
        compiler_params=_vmem("gdn"),
    )(a_log, dt_bias, gate_params, norm_w, conv_w, conv_w, conv_w, proj, proj, proj, proj, small, rows)


def _merge_kernel(ya_ref, yb_ref, ga_ref, gb_ref, x_ref, g1_ref, sh2_ref, sc2_ref,
                  lng_ref, lnb_ref, wpm_ref, wpg_ref, wo_ref, x1_ref, h2_ref):
    pa = jnp.dot(ya_ref[...], wpm_ref[...], preferred_element_type=F32)
    pb = jnp.dot(yb_ref[...], wpg_ref[...], preferred_element_type=F32)
    merged = jax.nn.sigmoid(ga_ref[...]) * pa + jax.nn.sigmoid(gb_ref[...]) * pb
    y = jnp.dot(merged.astype(BF16), wo_ref[...], preferred_element_type=F32)
    x1 = _layer_norm(DEEPNORM_ALPHA * x_ref[...] + g1_ref[...] * y, lng_ref[...], lnb_ref[...])
    x1_ref[...] = x1
    h2_ref[...] = (x1 * (1.0 + sc2_ref[...]) + sh2_ref[...]).astype(BF16)


def _merge(ya, yb, proj, x2, mod3, ln_g, ln_b, wpm, wpg, wo, seq):
    t = x2.shape[0]
    tm = MERGE_TM
    per_b = seq // tm
    d = D_MODEL

    def modspec(k):
        return pl.BlockSpec((None, 1, d), lambda i: (i // per_b, 0, k))

    def const(shape):
        return pl.BlockSpec(shape, lambda i: (0, 0), pipeline_mode=pl.Buffered(1))

    return pl.pallas_call(
        _merge_kernel,
        name="merge",
        grid=(t // tm,),
        in_specs=[pl.BlockSpec((tm, MOBA_W), lambda i: (i, 0)),
                  pl.BlockSpec((tm, GDN_V_W), lambda i: (i, 0)),
                  pl.BlockSpec((tm, d), lambda i: (i, COL_GATE_A // d)),
                  pl.BlockSpec((tm, d), lambda i: (i, COL_GATE_B // d)),
                  pl.BlockSpec((tm, d), lambda i: (i, 0)),
                  modspec(2), modspec(3), modspec(4),
                  const((1, d)), const((1, d)),
                  const((MOBA_W, d)), const((GDN_V_W, d)), const((d, d))],
        out_specs=[pl.BlockSpec((tm, d), lambda i: (i, 0)),
                   pl.BlockSpec((tm, d), lambda i: (i, 0))],
        out_shape=[jax.ShapeDtypeStruct((t, d), F32), jax.ShapeDtypeStruct((t, d), BF16)],
        compiler_params=_vmem("merge"),
    )(ya, yb, proj, proj, x2, mod3, mod3, mod3, ln_g, ln_b, wpm, wpg, wo)


def _ffn_kernel(h_ref, x1_ref, g2_ref, lng_ref, lnb_ref, wg_ref, wu_ref, wo_ref, o_ref, acc_ref):
    f = pl.program_id(1)

    @pl.when(f == 0)
    def _():
        acc_ref[...] = jnp.zeros_like(acc_ref)

    h = h_ref[...]
    gate = jnp.dot(h, wg_ref[...], preferred_element_type=F32)
    up = jnp.dot(h, wu_ref[...], preferred_element_type=F32)
    act = (_silu(gate) * up).astype(BF16)
    acc_ref[...] += jnp.dot(act, wo_ref[...], preferred_element_type=F32)

    @pl.when(f == pl.num_programs(1) - 1)
    def _():
        r = DEEPNORM_ALPHA * x1_ref[...] + g2_ref[...] * acc_ref[...]
        o_ref[...] = _layer_norm(r, lng_ref[...], lnb_ref[...])


def _ffn(h2, x1, mod3, ln_g, ln_b, w_in, w_out, seq):
    t = h2.shape[0]
    tm, tf = FFN_TM, FFN_TF
    per_b = seq // tm
    d = D_MODEL
    nf = D_FF // tf
    return pl.pallas_call(
        _ffn_kernel,
        name="ffn",
        grid=(t // tm, nf),
        in_specs=[pl.BlockSpec((tm, d), lambda i, f: (i, 0)),
                  pl.BlockSpec((tm, d), lambda i, f: (i, 0)),
                  pl.BlockSpec((None, 1, d), lambda i, f: (i // per_b, 0, 5)),
                  pl.BlockSpec((1, d), lambda i, f: (0, 0)),
                  pl.BlockSpec((1, d), lambda i, f: (0, 0)),
                  pl.BlockSpec((d, tf), lambda i, f: (0, f)),
                  pl.BlockSpec((d, tf), lambda i, f: (0, nf + f)),
                  pl.BlockSpec((tf, d), lambda i, f: (f, 0))],
        out_specs=pl.BlockSpec((tm, d), lambda i, f: (i, 0)),
        out_shape=jax.ShapeDtypeStruct((t, d), F32),
        scratch_shapes=[pltpu.VMEM((tm, d), F32)],
        compiler_params=_vmem("ffn"),
    )(h2, x1, mod3, ln_g, ln_b, w_in, w_in, w_out)


def _rel_bucket(dist):
    max_exact = REL_BUCKETS // 2
    n = jnp.maximum(dist, 0)
    nf = jnp.maximum(n, 1).astype(F32)
    large = max_exact + (jnp.log(nf / max_exact) / math.log(REL_MAX_DIST / max_exact)
                         * (REL_BUCKETS - max_exact)).astype(jnp.int32)
    large = jnp.minimum(large, REL_BUCKETS - 1)
    return jnp.where(n < max_exact, n, large)


def _layer(x, c, w_ada, b_ada, w_in, conv_w, a_log, dt_bias, gdn_norm_w, rel_bias,
           w_proj_moba, w_proj_gdn, w_out, ln1_g, ln1_b, w_ffn_in, w_ffn_out, ln2_g, ln2_b):
    bsz, seq, d = x.shape
    t = bsz * seq
    x2 = x.reshape(t, d)

    mod = _ada_mod(c, w_ada, b_ada)
    mod3 = mod.reshape(bsz, 1, 6 * d)

    w_all = w_in.astype(BF16)
    n_gates = 2 * GDN_V_HEADS
    w_gate = w_all[:, N_HEAD + n_gates:]
    w_small = jnp.pad(w_all[:, N_HEAD:N_HEAD + n_gates], ((0, 0), (0, N_SMALL - n_gates)))

    proj, small = _in_proj(x2, mod3, w_all, w_gate, w_small, seq)

    ii = jnp.arange(MOBA_BLOCK, dtype=jnp.int32)
    dist = ii[None, :] - ii[:, None]
    bko = _rel_bucket(dist)
    bkp = _rel_bucket(dist + MOBA_BLOCK)
    ya = _moba(proj, rel_bias, bko, bkp, bsz, seq)

    nchunk = seq // GDN_CHUNK
    sm_t = small[:, :2 * GDN_V_HEADS].reshape(bsz, seq, 2, GDN_QK_HEADS, 2)
    rows = sm_t.transpose(0, 3, 2, 4, 1).reshape(bsz, GDN_QK_HEADS, 4, nchunk, GDN_CHUNK)
    yb = _gdn(proj, small, rows, conv_w, a_log, dt_bias, gdn_norm_w.reshape(1, HEAD_DIM),
              bsz, seq)

    x1, h2 = _merge(ya, yb, proj, x2, mod3, ln1_g.reshape(1, d), ln1_b.reshape(1, d),
                    w_proj_moba.astype(BF16), w_proj_gdn.astype(BF16), w_out.astype(BF16), seq)
    out = _ffn(h2, x1, mod3, ln2_g.reshape(1, d), ln2_b.reshape(1, d),
               w_ffn_in.astype(BF16), w_ffn_out.astype(BF16), seq)
    return out.reshape(bsz, seq, d)


def kernel(x, c, w_ada, b_ada, w_in, conv_w, a_log, dt_bias, gdn_norm_w, rel_bias, w_proj_moba,
           w_proj_gdn, w_out, ln1_g, ln1_b, w_ffn_in, w_ffn_out, ln2_g, ln2_b):
    depth = w_ada.shape[0]
    for l in range(depth):
        x = _layer(x, c, w_ada[l], b_ada[l], w_in[l], conv_w[l], a_log[l], dt_bias[l],
                   gdn_norm_w[l], rel_bias, w_proj_moba[l], w_proj_gdn[l], w_out[l],
                   ln1_g[l], ln1_b[l], w_ffn_in[l], w_ffn_out[l], ln2_g[l], ln2_b[l])
    return x
```

```python
import functools
import math

import jax
import jax.numpy as jnp
from jax import lax
from jax.experimental import pallas as pl
from jax.experimental.pallas import tpu as pltpu

F32 = jnp.float32
BF16 = jnp.bfloat16

D_MODEL = 2048
MOBA_HEADS = 8
HEAD_DIM = 128
MOBA_W = MOBA_HEADS * HEAD_DIM
MOBA_BLOCK = 256
MOBA_TOPK = 3
REL_BUCKETS = 32
REL_MAX_DIST = 128
GDN_QK_HEADS = 8
GDN_V_HEADS = 16
GDN_QK_W = GDN_QK_HEADS * HEAD_DIM
GDN_V_W = GDN_V_HEADS * HEAD_DIM
GDN_CONV = 4
GDN_CHUNK = 64
D_FF = 5632
DEEPNORM_ALPHA = 2.0 ** 0.25
LN_EPS = 1e-5
RMS_EPS = 1e-6
NEG_INF = -1e30

COL_GATE_A = 0
COL_GATE_B = 2048
COL_MOBA_Q = 4096
COL_MOBA_K = 5120
COL_MOBA_V = 6144
COL_GDN_Q = 7168
COL_GDN_K = 8192
COL_GDN_V = 9216
COL_GDN_Z = 11264
N_MAIN = 13312
N_GATE = 4096
N_HEAD = N_MAIN - N_GATE
N_SMALL = 128

V7X_VMEM_MIB = 64
SUBLANES = 8
BF16_SUBLANES = 16

ADA_TN = 1024
INPROJ_TM, INPROJ_TN = 1024, 1024
MERGE_TM = 256
FFN_TM, FFN_TF = 512, 512
VMEM_LIMIT_MIB = {"ada_mod": 40, "in_proj": 56, "moba": 48, "gdn": 58, "merge": 56, "ffn": 48}
assert max(VMEM_LIMIT_MIB.values()) < V7X_VMEM_MIB


def _vmem(name):
    return pltpu.CompilerParams(vmem_limit_bytes=VMEM_LIMIT_MIB[name] * 1024 * 1024)


def _silu(x):
    return x * jax.nn.sigmoid(x)


def _layer_norm(r, gain, bias):
    mu = jnp.mean(r, axis=-1, keepdims=True)
    d = r - mu
    var = jnp.mean(d * d, axis=-1, keepdims=True)
    return d * lax.rsqrt(var + LN_EPS) * gain + bias


def _ada_kernel(c_ref, w_ref, b_ref, o_ref):
    sc = _silu(c_ref[...])
    o_ref[...] = jnp.dot(sc, w_ref[...], precision=lax.Precision.HIGHEST,
                         preferred_element_type=F32) + b_ref[...]


def _ada_mod(c, w_ada, b_ada):
    bsz = c.shape[0]
    n = w_ada.shape[1]
    tn = ADA_TN
    return pl.pallas_call(
        _ada_kernel,
        name="ada_mod",
        grid=(n // tn,),
        in_specs=[pl.BlockSpec((bsz, D_MODEL), lambda j: (0, 0)),
                  pl.BlockSpec((D_MODEL, tn), lambda j: (0, j)),
                  pl.BlockSpec((1, tn), lambda j: (0, j))],
        out_specs=pl.BlockSpec((bsz, tn), lambda j: (0, j)),
        out_shape=jax.ShapeDtypeStruct((bsz, n), F32),
        compiler_params=_vmem("ada_mod"),
    )(c, w_ada, b_ada.reshape(1, n))


def _serpentine(i, j, n):
    return jnp.where(i % 2 == 0, j, n - 1 - j)


def _inproj_kernel(xt_ref, xb_ref, sh_ref, sc_ref, w_ref, wg_ref, ws_ref, o_ref, os_ref, h_ref, *,
                   n_head, n_tiles):
    @pl.when(pl.program_id(1) == 0)
    def _():
        half = xt_ref.shape[0]
        for x_ref, rows in ((xt_ref, pl.ds(0, half)), (xb_ref, pl.ds(half, half))):
            h = x_ref[...] * (1.0 + sc_ref[...]) + sh_ref[...]
            h_ref[rows, :] = h.astype(BF16)
        os_ref[...] = jnp.dot(h_ref[...], ws_ref[...], preferred_element_type=F32)

    col = _serpentine(pl.program_id(0), pl.program_id(1), n_tiles)

    @pl.when(col < n_head)
    def _():
        o_ref[...] = jnp.dot(h_ref[...], w_ref[...], preferred_element_type=F32)

    @pl.when(col >= n_head)
    def _():
        o_ref[...] = jnp.dot(h_ref[...], wg_ref[...], preferred_element_type=F32)


def _in_proj(x2, mod3, w_all, w_gate, w_small, seq):
    t = x2.shape[0]
    tm, tn = INPROJ_TM, INPROJ_TN
    per_b = seq // tm
    n_head, n_gate, n_tiles = N_HEAD // tn, N_GATE // tn, N_MAIN // tn
    n_rows = t // tm

    def col(i, j):
        return _serpentine(i, j, n_tiles)

    return pl.pallas_call(
        functools.partial(_inproj_kernel, n_head=n_head, n_tiles=n_tiles),
        name="in_proj",
        grid=(t // tm, n_tiles),
        in_specs=[pl.BlockSpec((tm // 2, D_MODEL), lambda i, j: (2 * i, 0)),
                  pl.BlockSpec((tm // 2, D_MODEL),
                               lambda i, j: (2 * jnp.minimum(i + (j == n_tiles - 1), n_rows - 1) + 1,
                                             0)),
                  pl.BlockSpec((None, 1, D_MODEL), lambda i, j: (i // per_b, 0, 0)),
                  pl.BlockSpec((None, 1, D_MODEL), lambda i, j: (i // per_b, 0, 1)),
                  pl.BlockSpec((D_MODEL, tn), lambda i, j: (0, jnp.minimum(col(i, j), n_head - 1))),
                  pl.BlockSpec((D_MODEL, tn), lambda i, j: (0, jnp.maximum(col(i, j) - n_head, 0))),
                  pl.BlockSpec((D_MODEL, N_SMALL), lambda i, j: (0, 0))],
        out_specs=[pl.BlockSpec((tm, tn), lambda i, j: (i, (col(i, j) + n_gate) % n_tiles)),
                   pl.BlockSpec((tm, N_SMALL), lambda i, j: (i, 0))],
        out_shape=[jax.ShapeDtypeStruct((t, N_MAIN), F32),
                   jax.ShapeDtypeStruct((t, N_SMALL), F32)],
        scratch_shapes=[pltpu.VMEM((tm, D_MODEL), BF16)],
        compiler_params=_vmem("in_proj"),
    )(x2, x2, mod3, mod3, w_all, w_gate, w_small)


def _moba_kernel(rel_ref, bko_ref, bkp_ref, q_ref, k_ref, v_ref, o_ref,
                 bias_own, bias_prev, kb_ref, vt_ref, *, nb):
    h = pl.program_id(0)
    blk = MOBA_BLOCK
    inv_scale = HEAD_DIM ** 0.5
    scale_log2e = HEAD_DIM ** -0.5 * math.log2(math.e)

    @pl.when(pl.program_id(1) == 0)
    def _():
        bo = bko_ref[...]
        bp = bkp_ref[...]
        acc_o = jnp.zeros((blk, blk), F32)
        acc_p = jnp.zeros((blk, blk), F32)
        for kk in range(REL_BUCKETS):
            val = rel_ref[kk, h] * inv_scale
            acc_o = jnp.where(bo == kk, val, acc_o)
            acc_p = jnp.where(bp == kk, val, acc_p)
        bias_own[...] = acc_o
        bias_prev[...] = acc_p

    bias_far = rel_ref[REL_BUCKETS - 1, h] * inv_scale
    kf = k_ref[...]
    kmean = jnp.mean(kf.reshape(nb, blk, HEAD_DIM), axis=1)
    kb_ref[...] = kf.astype(BF16)
    nt_dims = (((1,), (1,)), ((), ()))
    eye = (lax.broadcasted_iota(jnp.int32, (HEAD_DIM, HEAD_DIM), 0)
           == lax.broadcasted_iota(jnp.int32, (HEAD_DIM, HEAD_DIM), 1)).astype(BF16)
    vt_ref[:HEAD_DIM, :] = lax.dot_general(eye, v_ref[...].astype(BF16), nt_dims,
                                           preferred_element_type=F32).astype(BF16)
    pad_rows = vt_ref.shape[0] - HEAD_DIM
    vt_ref[HEAD_DIM:, :] = (lax.broadcasted_iota(jnp.int32, (pad_rows, vt_ref.shape[1]), 0)
                            == 0).astype(BF16)
    causal = (lax.broadcasted_iota(jnp.int32, (blk, blk), 0)
              <= lax.broadcasted_iota(jnp.int32, (blk, blk), 1))

    def scores(i):
        qi = q_ref[i * blk:(i + 1) * blk, :]
        qb = qi.astype(BF16)
        sel = None
        if i > MOBA_TOPK:
            route = lax.dot_general(kmean, qi, nt_dims, precision=lax.Precision.HIGHEST,
                                    preferred_element_type=F32)
            rc = [route[n:n + 1, :] for n in range(i)]
            sel = []
            for n in range(i):
                rank = jnp.zeros((1, blk), jnp.int32)
                for m in range(i):
                    if m == n:
                        continue
                    beats = (rc[m] >= rc[n]) if m < n else (rc[m] > rc[n])
                    rank = rank + beats.astype(jnp.int32)
                sel.append(rank < MOBA_TOPK)
        t_list = []
        for n in range(i + 1):
            t = lax.dot_general(kb_ref[n * blk:(n + 1) * blk, :], qb, nt_dims,
                                preferred_element_type=F32)
            if n == i:
                t = jnp.where(causal, t + bias_own[...], NEG_INF)
            else:
                if n == i - 1:
                    t = t + bias_prev[...]
                if sel is not None:
                    t = jnp.where(sel[n], t, NEG_INF)
            t_list.append(t)
        return t_list

    t_next = scores(0)
    for i in range(nb):
        t_list = t_next
        if i + 1 < nb:
            t_next = scores(i + 1)
        n_far = max(i - 1, 0)
        m_run = jnp.max(t_list[n_far], axis=0, keepdims=True)
        for t in t_list[n_far + 1:]:
            m_run = jnp.maximum(m_run, jnp.max(t, axis=0, keepdims=True))
        if n_far:
            m_far = jnp.max(t_list[0], axis=0, keepdims=True)
            for t in t_list[1:n_far]:
                m_far = jnp.maximum(m_far, jnp.max(t, axis=0, keepdims=True))
            m_run = jnp.maximum(m_run, m_far + bias_far)
        acc = jnp.zeros((vt_ref.shape[0], blk), F32)
        for n, t in enumerate(t_list):
            offset = m_run - bias_far if n < n_far else m_run
            p = jnp.exp2((t - offset) * scale_log2e)
            acc = acc + jnp.dot(vt_ref[:, n * blk:(n + 1) * blk], p.astype(BF16),
                                preferred_element_type=F32)
        out = acc[:HEAD_DIM] / acc[HEAD_DIM:HEAD_DIM + 1]
        o_ref[i * blk:(i + 1) * blk, :] = out.T.astype(BF16)


def _moba(proj, rel_bias, bko, bkp, bsz, seq):
    nb = seq // MOBA_BLOCK
    qo, ko, vo = COL_MOBA_Q // HEAD_DIM, COL_MOBA_K // HEAD_DIM, COL_MOBA_V // HEAD_DIM
    blk = MOBA_BLOCK
    return pl.pallas_call(
        functools.partial(_moba_kernel, nb=nb),
        name="moba",
        grid=(MOBA_HEADS, bsz),
        in_specs=[pl.BlockSpec(memory_space=pltpu.SMEM),
                  pl.BlockSpec((blk, blk), lambda h, b: (0, 0)),
                  pl.BlockSpec((blk, blk), lambda h, b: (0, 0)),
                  pl.BlockSpec((seq, HEAD_DIM), lambda h, b: (b, qo + h)),
                  pl.BlockSpec((seq, HEAD_DIM), lambda h, b: (b, ko + h)),
                  pl.BlockSpec((seq, HEAD_DIM), lambda h, b: (b, vo + h))],
        out_specs=pl.BlockSpec((seq, HEAD_DIM), lambda h, b: (b, h)),
        out_shape=jax.ShapeDtypeStruct((bsz * seq, MOBA_W), BF16),
        scratch_shapes=[pltpu.VMEM((blk, blk), F32), pltpu.VMEM((blk, blk), F32),
                        pltpu.VMEM((seq, HEAD_DIM), BF16),
                        pltpu.VMEM((HEAD_DIM + BF16_SUBLANES, seq), BF16)],
        compiler_params=_vmem("moba"),
    )(rel_bias, bko, bkp, proj, proj, proj)


def _conv_silu(xh, w):
    acc = xh[SUBLANES:, :] * w[GDN_CONV - 1:GDN_CONV, :]
    for s in range(1, GDN_CONV):
        acc = acc + pltpu.roll(xh, s, axis=0)[SUBLANES:, :] * w[GDN_CONV - 1 - s:GDN_CONV - s, :]
    return _silu(acc)


def _l2norm(x):
    return x * lax.rsqrt(jnp.sum(x * x, axis=-1, keepdims=True) + RMS_EPS)


def _softplus(x):
    return jnp.maximum(x, 0.0) + jnp.log1p(jnp.exp(-jnp.abs(x)))


ELIM_BLOCK = 8
MERGE_LEVELS = (GDN_CHUNK // ELIM_BLOCK).bit_length() - 1


def _block_diag_inverse(lpair):
    n = lpair.shape[0]
    nv = n // SUBLANES
    vpb = ELIM_BLOCK // SUBLANES
    rid = lax.broadcasted_iota(jnp.int32, (SUBLANES, 2 * n), 0)
    lid = lax.broadcasted_iota(jnp.int32, (SUBLANES, 2 * n), 1)
    t_rows = [(lid % n == rid + v * SUBLANES).astype(F32) for v in range(nv)]
    l_rows = [lpair[v * SUBLANES:(v + 1) * SUBLANES, :] for v in range(nv)]
    for m in range(n - 1):
        v0, s0 = divmod(m, SUBLANES)
        v_end = (v0 // vpb + 1) * vpb
        row = t_rows[v0][s0:s0 + 1, :]
        col = (lid // n) * n + m
        for v in range(v0 if s0 < SUBLANES - 1 else v0 + 1, v_end):
            t_rows[v] = t_rows[v] - jnp.take_along_axis(l_rows[v], col, axis=1) * row
    return jnp.concatenate(t_rows, axis=0)


def _lane_block_diag(pair):
    first = lax.broadcasted_iota(jnp.int32, pair.shape, 1) < pair.shape[1] // 2
    zero = jnp.zeros_like(pair)
    return jnp.concatenate([jnp.where(first, pair, zero), jnp.where(first, zero, pair)], axis=0)


def _merge_lower_products(lpair, tpair, k):
    n = lpair.shape[0]
    ri = lax.broadcasted_iota(jnp.int32, lpair.shape, 0)
    ci = lax.broadcasted_iota(jnp.int32, lpair.shape, 1) % n
    off = (ri // (2 * k) == ci // (2 * k)) & (ri // k > ci // k)
    lk = jnp.where(off, lpair, 0.0).astype(BF16)
    return jnp.dot(lk, _lane_block_diag(tpair.astype(BF16)), preferred_element_type=F32)


def _merge_apply(tpair, lt):
    return tpair - jnp.dot(tpair.astype(BF16), _lane_block_diag(lt), preferred_element_type=F32)


PIPELINE_UNROLL = 4


def _software_pipeline(stages, n):
    ns = len(stages)

    def run(it, lo, hi):
        conts = [stages[s](it - s) for s in reversed(range(lo, hi))]
        for cont in conts:
            if cont is not None:
                cont()

    for it in range(ns - 1):
        run(it, 0, it + 1)

    steady = n - (ns - 1)
    assert steady % PIPELINE_UNROLL == 0, (n, ns)

    def body(k, carry):
        first = ns - 1 + k * PIPELINE_UNROLL
        for u in range(PIPELINE_UNROLL):
            run(first + u, 0, ns)
        return carry

    lax.fori_loop(0, steady // PIPELINE_UNROLL, body, 0)
    for it in range(n, n + ns - 1):
        run(it, it - n + 1, ns)


def _lane_pick(x, lane, idx):
    return jnp.sum(jnp.where(lane == idx, x, 0.0), axis=-1, keepdims=True)


def _gdn_kernel(alog_ref, dtb_ref, gp_ref, nw_ref, cwq_ref, cwk_ref, cwv_ref,
                q_ref, k_ref, v_ref, z_ref, sm_ref, rw_ref, y_ref,
                kb_s, qn_s, kdp_s, rhs_s, qd_s, gcb_s, betab_s, gcr_s, gram_s, lm_s, lhs2_s,
                sol_s, mp_s, n_s, r_s, st_s, *merge_s, nchunk):
    tl_s, lt_s = merge_s[:MERGE_LEVELS + 1], merge_s[MERGE_LEVELS + 1:]
    hq = pl.program_id(1)
    c64 = GDN_CHUNK
    hd = HEAD_DIM
    heads = range(2)

    tri_u = (lax.broadcasted_iota(jnp.int32, (c64, c64), 0)
             <= lax.broadcasted_iota(jnp.int32, (c64, c64), 1)).astype(F32)
    gc_rows = []
    for j in heads:
        hv = 2 * hq + j
        a_neg_r = -jnp.exp(jnp.full((nchunk, c64), alog_ref[hv], F32))
        g_row = a_neg_r * _softplus(rw_ref[2 + j] + dtb_ref[hv])
        gc_rows.append(jnp.dot(g_row, tri_u, precision=lax.Precision.HIGHEST,
                               preferred_element_type=F32))
    gcr_s[...] = jnp.concatenate(gc_rows, axis=1)

    rows = lax.broadcasted_iota(jnp.int32, (c64, 2 * c64), 0)
    cols = lax.broadcasted_iota(jnp.int32, (c64, 2 * c64), 1) % c64
    tril = rows >= cols
    strict = rows > cols
    eye2 = (lax.broadcasted_iota(jnp.int32, (2 * c64, 2 * c64), 0)
            == lax.broadcasted_iota(jnp.int32, (2 * c64, 2 * c64), 1)).astype(BF16)
    nt_dims = (((1,), (1,)), ((), ()))

    def rows_of(c):
        start = c * c64
        return pl.ds(start if isinstance(c, int) else pl.multiple_of(start, c64), c64)

    def with_halo(x_ref, c):
        if isinstance(c, int) and c == 0:
            return jnp.concatenate([jnp.zeros((SUBLANES, x_ref.shape[1]), F32), x_ref[:c64, :]],
                                   axis=0)
        start = c * c64 - SUBLANES
        if not isinstance(c, int):
            start = pl.multiple_of(start, SUBLANES)
        return x_ref[pl.ds(start, c64 + SUBLANES), :]

    lane = lax.broadcasted_iota(jnp.int32, (c64, N_SMALL), 1)
    first = lane < c64
    pos = lax.broadcasted_iota(jnp.int32, (c64, N_SMALL), 0)

    def stage_prep(c):
        r = rows_of(c)
        qn = _l2norm(_conv_silu(with_halo(q_ref, c), cwq_ref[...])) * (hd ** -0.5)
        kn = _l2norm(_conv_silu(with_halo(k_ref, c), cwk_ref[...]))
        vc = _conv_silu(with_halo(v_ref, c), cwv_ref[...])
        qn_s[r, :] = qn.astype(BF16)
        kb_s[r, :] = kn.astype(BF16)
        sm = sm_ref[r, :]
        sig_all = jax.nn.sigmoid(sm)
        gc_all = -jnp.exp(gp_ref[0:1, :]) * _softplus(sm + gp_ref[1:2, :])
        sft = 1
        while sft < c64:
            gc_all = gc_all + jnp.where(pos >= sft, pltpu.roll(gc_all, sft, axis=0), 0.0)
            sft *= 2
        rest_all = gc_all[c64 - 1:, :] - gc_all
        betas, gc_cols = [], []
        for j in heads:
            hv = 2 * hq + j
            beta = _lane_pick(sig_all, lane, hv)
            gc_col = _lane_pick(gc_all, lane, GDN_V_HEADS + hv)
            rest_col = _lane_pick(rest_all, lane, GDN_V_HEADS + hv)
            eg = jnp.exp(gc_col)
            rhs_s[j, r, :hd] = (vc[:, j * hd:(j + 1) * hd] * beta).astype(BF16)
            rhs_s[j, r, hd:] = (kn * (beta * eg)).astype(BF16)
            qd_s[j, r, :] = (qn * eg).astype(BF16)
            kdp_s[c, j * c64:(j + 1) * c64, :] = (kn * jnp.exp(rest_col)).astype(BF16)
            betas.append(beta)
            gc_cols.append(gc_col)
        gcb_s[r, :] = jnp.where(first, gc_cols[0], gc_cols[1])
        betab_s[r, :] = jnp.where(first, betas[0], betas[1])

    def stage_gram(c):
        r = rows_of(c)
        kb = kb_s[r, :]
        kq = jnp.concatenate([kb, qn_s[r, :]], axis=0)
        gram = lax.dot_general(kq, jnp.concatenate([kb, kb], axis=0), nt_dims,
                               preferred_element_type=F32)
        kd_t = lax.dot_general(eye2, kdp_s[c], nt_dims, preferred_element_type=F32)

        def finish():
            gram_s[c] = gram
            lhs2_s[c, :2 * c64, :] = kd_t.astype(BF16)

        return finish

    def stage_factor(c):
        r = rows_of(c)
        dec = jnp.exp(jnp.where(tril, gcb_s[r, :] - gcr_s[pl.ds(c, 1), :], NEG_INF))
        lpair = jnp.where(strict, gram_s[c, :c64, :] * dec, 0.0) * betab_s[r, :]
        lm_s[r, :] = lpair
        lhs2_s[c, 2 * c64:, :] = (gram_s[c, c64:, :] * dec).astype(BF16)
        tl_s[0][r, :] = _block_diag_inverse(lpair)

    def stage_merge_products(level):
        def stage(c):
            r = rows_of(c)
            lt = _merge_lower_products(lm_s[r, :], tl_s[level][r, :], ELIM_BLOCK << level)

            def finish():
                lt_s[level][r, :] = lt.astype(BF16)

            return finish
        return stage

    def stage_merge_apply(level):
        def stage(c):
            r = rows_of(c)
            merged = _merge_apply(tl_s[level][r, :], lt_s[level][r, :])

            def finish():
                tl_s[level + 1][r, :] = merged

            return finish
        return stage

    def stage_solve(c):
        r = rows_of(c)
        zero = jnp.zeros((c64, 2 * hd), BF16)
        rhs_bd = jnp.concatenate([jnp.concatenate([rhs_s[0, r, :], zero], axis=1),
                                  jnp.concatenate([zero, rhs_s[1, r, :]], axis=1)], axis=0)
        sol = jnp.dot(tl_s[MERGE_LEVELS][r, :].astype(BF16), rhs_bd,
                      preferred_element_type=F32)

        def finish():
            sol_s[r, :] = sol.astype(BF16)

        return finish

    def stage_fold(c):
        r = rows_of(c)
        sol = sol_s[r, :]
        out = jnp.dot(lhs2_s[c], _lane_block_diag(sol), preferred_element_type=F32)

        def finish():
            for j in heads:
                u_col, w_col = 2 * j * hd, (2 * j + 1) * hd
                n_s[j, c] = out[:2 * c64, u_col:u_col + hd]
                mp_s[j, c, :2 * c64, :] = out[:2 * c64, w_col:w_col + hd].astype(BF16)
                mp_s[j, c, 2 * c64:, :] = (qd_s[j, r, :].astype(F32)
                                           - out[2 * c64:, w_col:w_col + hd]).astype(BF16)
                r_s[j, r, :] = out[2 * c64:, u_col:u_col + hd]

        return finish

    st_s[...] = jnp.zeros_like(st_s)
    nw = nw_ref[...]

    def stage_state(c):
        r = rows_of(c)
        sts = [st_s[j] for j in heads]
        outs = [jnp.dot(mp_s[j, c], sts[j].astype(BF16), preferred_element_type=F32)
                for j in heads]

        def finish():
            for j in heads:
                end = (j + 1) * c64
                gl = gcr_s[pl.ds(c, 1), end - 1:end]
                st_s[j] = sts[j] * jnp.exp(gl) - outs[j][:2 * c64] + n_s[j, c]
                r_s[j, r, :] = r_s[j, r, :] + outs[j][2 * c64:]

        return finish

    def stage_norm(c):
        r = rows_of(c)
        for j in heads:
            o = r_s[j, r, :]
            og = (o * lax.rsqrt(jnp.mean(o * o, axis=-1, keepdims=True) + RMS_EPS)
                  * nw * _silu(z_ref[r, j * hd:(j + 1) * hd]))
            y_ref[r, j * hd:(j + 1) * hd] = og.astype(BF16)

    merges = [stage(lv) for lv in range(MERGE_LEVELS)
              for stage in (stage_merge_products, stage_merge_apply)]
    _software_pipeline([stage_prep, stage_gram, stage_factor] + merges
                       + [stage_solve, stage_fold, stage_state, stage_norm], nchunk)


def _gdn(proj, small, rows, conv_w, a_log, dt_bias, norm_w, bsz, seq):
    nchunk = seq // GDN_CHUNK
    hd = HEAD_DIM
    c64 = GDN_CHUNK
    qo, ko = COL_GDN_Q // hd, COL_GDN_K // hd
    vo, zo = COL_GDN_V // (2 * hd), COL_GDN_Z // (2 * hd)
    cvo = (2 * GDN_QK_W) // (2 * hd)
    smem = pl.BlockSpec(memory_space=pltpu.SMEM)
    pad = (GDN_V_HEADS, N_SMALL - 2 * GDN_V_HEADS)
    gate_params = jnp.stack([jnp.pad(a_log, pad), jnp.pad(dt_bias, pad)])
    return pl.pallas_call(
        functools.partial(_gdn_kernel, nchunk=nchunk),
        name="gdn",
        grid=(bsz, GDN_QK_HEADS),
        in_specs=[smem, smem,
                  pl.BlockSpec((2, N_SMALL), lambda b, h: (0, 0)),
                  pl.BlockSpec((1, hd), lambda b, h: (0, 0)),
                  pl.BlockSpec((GDN_CONV, hd), lambda b, h: (0, h)),
                  pl.BlockSpec((GDN_CONV, hd), lambda b, h: (0, GDN_QK_HEADS + h)),
                  pl.BlockSpec((GDN_CONV, 2 * hd), lambda b, h: (0, cvo + h)),
                  pl.BlockSpec((seq, hd), lambda b, h: (b, qo + h)),
                  pl.BlockSpec((seq, hd), lambda b, h: (b, ko + h)),
                  pl.BlockSpec((seq, 2 * hd), lambda b, h: (b, vo + h)),
                  pl.BlockSpec((seq, 2 * hd), lambda b, h: (b, zo + h)),
                  pl.BlockSpec((seq, N_SMALL), lambda b, h: (b, 0)),
                  pl.BlockSpec((None, None, 4, nchunk, GDN_CHUNK), lambda b, h: (b, h, 0, 0, 0))],
        out_specs=pl.BlockSpec((seq, 2 * hd), lambda b, h: (b, h)),
        out_shape=jax.ShapeDtypeStruct((bsz * seq, GDN_V_W), BF16),
        scratch_shapes=[pltpu.VMEM((seq, hd), BF16),
                        pltpu.VMEM((seq, hd), BF16),
                        pltpu.VMEM((nchunk, 2 * c64, hd), BF16),
                        pltpu.VMEM((2, seq, 2 * hd), BF16),
                        pltpu.VMEM((2, seq, hd), BF16),
                        pltpu.VMEM((seq, 2 * c64), F32),
                        pltpu.VMEM((seq, 2 * c64), F32),
                        pltpu.VMEM((nchunk, 2 * c64), F32),
                        pltpu.VMEM((nchunk, 2 * c64, 2 * c64), F32),
                        pltpu.VMEM((seq, 2 * c64), F32),
                        pltpu.VMEM((nchunk, 3 * c64, hd), BF16),
                        pltpu.VMEM((seq, 4 * hd), BF16),
                        pltpu.VMEM((2, nchunk, 3 * c64, hd), BF16),
                        pltpu.VMEM((2, nchunk, hd, hd), F32),
                        pltpu.VMEM((2, seq, hd), F32),
                        pltpu.VMEM((2, hd, hd), F32)]
                       + [pltpu.VMEM((seq, 2 * c64), F32)] * (MERGE_LEVELS + 1)
                       + [pltpu.VMEM((seq, 2 * c64), BF16)] * MERGE_LEVELS,
        compiler_params=_vmem("gdn"),
    )(a_log, dt_bias, gate_params, norm_w, conv_w, conv_w, conv_w, proj, proj, proj, proj, small, rows)


def _merge_kernel(ya_ref, yb_ref, ga_ref, gb_ref, x_ref, g1_ref, sh2_ref, sc2_ref,
                  lng_ref, lnb_ref, wpm_ref, wpg_ref, wo_ref, x1_ref, h2_ref):
    pa = jnp.dot(ya_ref[...], wpm_ref[...], preferred_element_type=F32)
    pb = jnp.dot(yb_ref[...], wpg_ref[...], preferred_element_type=F32)
    merged = jax.nn.sigmoid(ga_ref[...]) * pa + jax.nn.sigmoid(gb_ref[...]) * pb
    y = jnp.dot(merged.astype(BF16), wo_ref[...], preferred_element_type=F32)
    x1 = _layer_norm(DEEPNORM_ALPHA * x_ref[...] + g1_ref[...] * y, lng_ref[...], lnb_ref[...])
    x1_ref[...] = x1
    h2_ref[...] = (x1 * (1.0 + sc2_ref[...]) + sh2_ref[...]).astype(BF16)


def _merge(ya, yb, proj, x2, mod3, ln_g, ln_b, wpm, wpg, wo, seq):
    t = x2.shape[0]
    tm = MERGE_TM
    per_b = seq // tm
    d = D_MODEL

    def modspec(k):
        return pl.BlockSpec((None, 1, d), lambda i: (i // per_b, 0, k))

    def const(shape):
        return pl.BlockSpec(shape, lambda i: (0, 0), pipeline_mode=pl.Buffered(1))

    return pl.pallas_call(
        _merge_kernel,
        name="merge",
        grid=(t // tm,),
        in_specs=[pl.BlockSpec((tm, MOBA_W), lambda i: (i, 0)),
                  pl.BlockSpec((tm, GDN_V_W), lambda i: (i, 0)),
                  pl.BlockSpec((tm, d), lambda i: (i, COL_GATE_A // d)),
                  pl.BlockSpec((tm, d), lambda i: (i, COL_GATE_B // d)),
                  pl.BlockSpec((tm, d), lambda i: (i, 0)),
                  modspec(2), modspec(3), modspec(4),
                  const((1, d)), const((1, d)),
                  const((MOBA_W, d)), const((GDN_V_W, d)), const((d, d))],
        out_specs=[pl.BlockSpec((tm, d), lambda i: (i, 0)),
                   pl.BlockSpec((tm, d), lambda i: (i, 0))],
        out_shape=[jax.ShapeDtypeStruct((t, d), F32), jax.ShapeDtypeStruct((t, d), BF16)],
        compiler_params=_vmem("merge"),
    )(ya, yb, proj, proj, x2, mod3, mod3, mod3, ln_g, ln_b, wpm, wpg, wo)


def _ffn_kernel(h_ref, x1_ref, g2_ref, lng_ref, lnb_ref, wg_ref, wu_ref, wo_ref, o_ref, acc_ref):
    f = pl.program_id(1)

    @pl.when(f == 0)
    def _():
        acc_ref[...] = jnp.zeros_like(acc_ref)

    h = h_ref[...]
    gate = jnp.dot(h, wg_ref[...], preferred_element_type=F32)
    up = jnp.dot(h, wu_ref[...], preferred_element_type=F32)
    act = (_silu(gate) * up).astype(BF16)
    acc_ref[...] += jnp.dot(act, wo_ref[...], preferred_element_type=F32)

    @pl.when(f == pl.num_programs(1) - 1)
    def _():
        r = DEEPNORM_ALPHA * x1_ref[...] + g2_ref[...] * acc_ref[...]
        o_ref[...] = _layer_norm(r, lng_ref[...], lnb_ref[...])


def _ffn(h2, x1, mod3, ln_g, ln_b, w_in, w_out, seq):
    t = h2.shape[0]
    tm, tf = FFN_TM, FFN_TF
    per_b = seq // tm
    d = D_MODEL
    nf = D_FF // tf
    return pl.pallas_call(
        _ffn_kernel,
        name="ffn",
        grid=(t // tm, nf),
        in_specs=[pl.BlockSpec((tm, d), lambda i, f: (i, 0)),
                  pl.BlockSpec((tm, d), lambda i, f: (i, 0)),
                  pl.BlockSpec((None, 1, d), lambda i, f: (i // per_b, 0, 5)),
                  pl.BlockSpec((1, d), lambda i, f: (0, 0)),
                  pl.BlockSpec((1, d), lambda i, f: (0, 0)),
                  pl.BlockSpec((d, tf), lambda i, f: (0, f)),
                  pl.BlockSpec((d, tf), lambda i, f: (0, nf + f)),
                  pl.BlockSpec((tf, d), lambda i, f: (f, 0))],
        out_specs=pl.BlockSpec((tm, d), lambda i, f: (i, 0)),
        out_shape=jax.ShapeDtypeStruct((t, d), F32),
        scratch_shapes=[pltpu.VMEM((tm, d), F32)],
        compiler_params=_vmem("ffn"),
    )(h2, x1, mod3, ln_g, ln_b, w_in, w_in, w_out)


def _rel_bucket(dist):
    max_exact = REL_BUCKETS // 2
    n = jnp.maximum(dist, 0)
    nf = jnp.maximum(n, 1).astype(F32)
    large = max_exact + (jnp.log(nf / max_exact) / math.log(REL_MAX_DIST / max_exact)
                         * (REL_BUCKETS - max_exact)).astype(jnp.int32)
    large = jnp.minimum(large, REL_BUCKETS - 1)
    return jnp.where(n < max_exact, n, large)


def _layer(x, c, w_ada, b_ada, w_in, conv_w, a_log, dt_bias, gdn_norm_w, rel_bias,
           w_proj_moba, w_proj_gdn, w_out, ln1_g, ln1_b, w_ffn_in, w_ffn_out, ln2_g, ln2_b):
    bsz, seq, d = x.shape
    t = bsz * seq
    x2 = x.reshape(t, d)

    mod = _ada_mod(c, w_ada, b_ada)
    mod3 = mod.reshape(bsz, 1, 6 * d)

    w_all = w_in.astype(BF16)
    n_gates = 2 * GDN_V_HEADS
    w_gate = w_all[:, N_HEAD + n_gates:]
    w_small = jnp.pad(w_all[:, N_HEAD:N_HEAD + n_gates], ((0, 0), (0, N_SMALL - n_gates)))

    proj, small = _in_proj(x2, mod3, w_all, w_gate, w_small, seq)

    ii = jnp.arange(MOBA_BLOCK, dtype=jnp.int32)
    dist = ii[None, :] - ii[:, None]
    bko = _rel_bucket(dist)
    bkp = _rel_bucket(dist + MOBA_BLOCK)
    ya = _moba(proj, rel_bias, bko, bkp, bsz, seq)

    nchunk = seq // GDN_CHUNK
    sm_t = small[:, :2 * GDN_V_HEADS].reshape(bsz, seq, 2, GDN_QK_HEADS, 2)
    rows = sm_t.transpose(0, 3, 2, 4, 1).reshape(bsz, GDN_QK_HEADS, 4, nchunk, GDN_CHUNK)
    yb = _gdn(proj, small, rows, conv_w, a_log, dt_bias, gdn_norm_w.reshape(1, HEAD_DIM),
              bsz, seq)

    x1, h2 = _merge(ya, yb, proj, x2, mod3, ln1_g.reshape(1, d), ln1_b.reshape(1, d),
                    w_proj_moba.astype(BF16), w_proj_gdn.astype(BF16), w_out.astype(BF16), seq)
    out = _ffn(h2, x1, mod3, ln2_g.reshape(1, d), ln2_b.reshape(1, d),
               w_ffn_in.astype(BF16), w_ffn_out.astype(BF16), seq)
    return out.reshape(bsz, seq, d)


def kernel(x, c, w_ada, b_ada, w_in, conv_w, a_log, dt_bias, gdn_norm_w, rel_bias, w_proj_moba,
           w_proj_gdn, w_out, ln1_g, ln1_b, w_ffn_in, w_ffn_out, ln2_g, ln2_b):
    depth = w_ada.shape[0]
    for l in range(depth):
        x = _layer(x, c, w_ada[l], b_ada[l], w_in[l], conv_w[l], a_log[l], dt_bias[l],
                   gdn_norm_w[l], rel_bias, w_proj_moba[l], w_proj_gdn[l], w_out[l],
                   ln1_g[l], ln1_b[l], w_ffn_in[l], w_ffn_out[l], ln2_g[l], ln2_b[l])
    return x
```

```python
import functools
import math

import jax
import jax.numpy as jnp
from jax import lax
from jax.experimental import pallas as pl
from jax.experimental.pallas import tpu as pltpu

F32 = jnp.float32
BF16 = jnp.bfloat16

D_MODEL = 2048
MOBA_HEADS = 8
HEAD_DIM = 128
MOBA_W = MOBA_HEADS * HEAD_DIM
MOBA_BLOCK = 256
MOBA_TOPK = 3
REL_BUCKETS = 32
REL_MAX_DIST = 128
GDN_QK_HEADS = 8
GDN_V_HEADS = 16
GDN_QK_W = GDN_QK_HEADS * HEAD_DIM
GDN_V_W = GDN_V_HEADS * HEAD_DIM
GDN_CONV = 4
GDN_CHUNK = 64
D_FF = 5632
DEEPNORM_ALPHA = 2.0 ** 0.25
LN_EPS = 1e-5
RMS_EPS = 1e-6
NEG_INF = -1e30

COL_GATE_A = 0
COL_GATE_B = 2048
COL_MOBA_Q = 4096
COL_MOBA_K = 5120
COL_MOBA_V = 6144
COL_GDN_Q = 7168
COL_GDN_K = 8192
COL_GDN_V = 9216
COL_GDN_Z = 11264
N_MAIN = 13312
N_GATE = 4096
N_HEAD = N_MAIN - N_GATE
N_SMALL = 128

V7X_VMEM_MIB = 64
SUBLANES = 8
BF16_SUBLANES = 16

ADA_TN = 1024
INPROJ_TM, INPROJ_TN = 1024, 1024
MERGE_TM = 256
FFN_TM, FFN_TF = 512, 512
VMEM_LIMIT_MIB = {"ada_mod": 40, "in_proj": 56, "moba": 48, "gdn": 58, "merge": 56, "ffn": 48}
assert max(VMEM_LIMIT_MIB.values()) < V7X_VMEM_MIB


def _vmem(name):
    return pltpu.CompilerParams(vmem_limit_bytes=VMEM_LIMIT_MIB[name] * 1024 * 1024)


def _silu(x):
    return x * jax.nn.sigmoid(x)


def _layer_norm(r, gain, bias):
    mu = jnp.mean(r, axis=-1, keepdims=True)
    d = r - mu
    var = jnp.mean(d * d, axis=-1, keepdims=True)
    return d * lax.rsqrt(var + LN_EPS) * gain + bias


def _ada_kernel(c_ref, w_ref, b_ref, o_ref):
    sc = _silu(c_ref[...])
    o_ref[...] = jnp.dot(sc, w_ref[...], precision=lax.Precision.HIGHEST,
                         preferred_element_type=F32) + b_ref[...]


def _ada_mod(c, w_ada, b_ada):
    bsz = c.shape[0]
    n = w_ada.shape[1]
    tn = ADA_TN
    return pl.pallas_call(
        _ada_kernel,
        name="ada_mod",
        grid=(n // tn,),
        in_specs=[pl.BlockSpec((bsz, D_MODEL), lambda j: (0, 0)),
                  pl.BlockSpec((D_MODEL, tn), lambda j: (0, j)),
                  pl.BlockSpec((1, tn), lambda j: (0, j))],
        out_specs=pl.BlockSpec((bsz, tn), lambda j: (0, j)),
        out_shape=jax.ShapeDtypeStruct((bsz, n), F32),
        compiler_params=_vmem("ada_mod"),
    )(c, w_ada, b_ada.reshape(1, n))


def _serpentine(i, j, n):
    return jnp.where(i % 2 == 0, j, n - 1 - j)


def _inproj_kernel(x_ref, sh_ref, sc_ref, w_ref, wg_ref, ws_ref, o_ref, os_ref, h_ref, *, n_head,
                   n_tiles):
    @pl.when(pl.program_id(1) == 0)
    def _():
        h = x_ref[...] * (1.0 + sc_ref[...]) + sh_ref[...]
        h_ref[...] = h.astype(BF16)
        os_ref[...] = jnp.dot(h_ref[...], ws_ref[...], preferred_element_type=F32)

    col = _serpentine(pl.program_id(0), pl.program_id(1), n_tiles)

    @pl.when(col < n_head)
    def _():
        o_ref[...] = jnp.dot(h_ref[...], w_ref[...], preferred_element_type=F32)

    @pl.when(col >= n_head)
    def _():
        o_ref[...] = jnp.dot(h_ref[...], wg_ref[...], preferred_element_type=F32)


def _in_proj(x2, mod3, w_all, w_gate, w_small, seq):
    t = x2.shape[0]
    tm, tn = INPROJ_TM, INPROJ_TN
    per_b = seq // tm
    n_head, n_gate, n_tiles = N_HEAD // tn, N_GATE // tn, N_MAIN // tn

    def col(i, j):
        return _serpentine(i, j, n_tiles)

    return pl.pallas_call(
        functools.partial(_inproj_kernel, n_head=n_head, n_tiles=n_tiles),
        name="in_proj",
        grid=(t // tm, n_tiles),
        in_specs=[pl.BlockSpec((tm, D_MODEL), lambda i, j: (i, 0)),
                  pl.BlockSpec((None, 1, D_MODEL), lambda i, j: (i // per_b, 0, 0)),
                  pl.BlockSpec((None, 1, D_MODEL), lambda i, j: (i // per_b, 0, 1)),
                  pl.BlockSpec((D_MODEL, tn), lambda i, j: (0, jnp.minimum(col(i, j), n_head - 1))),
                  pl.BlockSpec((D_MODEL, tn), lambda i, j: (0, jnp.maximum(col(i, j) - n_head, 0))),
                  pl.BlockSpec((D_MODEL, N_SMALL), lambda i, j: (0, 0))],
        out_specs=[pl.BlockSpec((tm, tn), lambda i, j: (i, (col(i, j) + n_gate) % n_tiles)),
                   pl.BlockSpec((tm, N_SMALL), lambda i, j: (i, 0))],
        out_shape=[jax.ShapeDtypeStruct((t, N_MAIN), F32),
                   jax.ShapeDtypeStruct((t, N_SMALL), F32)],
        scratch_shapes=[pltpu.VMEM((tm, D_MODEL), BF16)],
        compiler_params=_vmem("in_proj"),
    )(x2, mod3, mod3, w_all, w_gate, w_small)


def _moba_kernel(rel_ref, bko_ref, bkp_ref, q_ref, k_ref, v_ref, o_ref,
                 bias_own, bias_prev, kb_ref, vt_ref, *, nb):
    h = pl.program_id(0)
    blk = MOBA_BLOCK
    inv_scale = HEAD_DIM ** 0.5
    scale_log2e = HEAD_DIM ** -0.5 * math.log2(math.e)

    @pl.when(pl.program_id(1) == 0)
    def _():
        bo = bko_ref[...]
        bp = bkp_ref[...]
        acc_o = jnp.zeros((blk, blk), F32)
        acc_p = jnp.zeros((blk, blk), F32)
        for kk in range(REL_BUCKETS):
            val = rel_ref[kk, h] * inv_scale
            acc_o = jnp.where(bo == kk, val, acc_o)
            acc_p = jnp.where(bp == kk, val, acc_p)
        bias_own[...] = acc_o
        bias_prev[...] = acc_p

    bias_far = rel_ref[REL_BUCKETS - 1, h] * inv_scale
    kf = k_ref[...]
    kmean = jnp.mean(kf.reshape(nb, blk, HEAD_DIM), axis=1)
    kb_ref[...] = kf.astype(BF16)
    nt_dims = (((1,), (1,)), ((), ()))
    eye = (lax.broadcasted_iota(jnp.int32, (HEAD_DIM, HEAD_DIM), 0)
           == lax.broadcasted_iota(jnp.int32, (HEAD_DIM, HEAD_DIM), 1)).astype(BF16)
    vt_ref[:HEAD_DIM, :] = lax.dot_general(eye, v_ref[...].astype(BF16), nt_dims,
                                           preferred_element_type=F32).astype(BF16)
    pad_rows = vt_ref.shape[0] - HEAD_DIM
    vt_ref[HEAD_DIM:, :] = (lax.broadcasted_iota(jnp.int32, (pad_rows, vt_ref.shape[1]), 0)
                            == 0).astype(BF16)
    causal = (lax.broadcasted_iota(jnp.int32, (blk, blk), 0)
              <= lax.broadcasted_iota(jnp.int32, (blk, blk), 1))

    def scores(i):
        qi = q_ref[i * blk:(i + 1) * blk, :]
        qb = qi.astype(BF16)
        sel = None
        if i > MOBA_TOPK:
            route = lax.dot_general(kmean, qi, nt_dims, precision=lax.Precision.HIGHEST,
                                    preferred_element_type=F32)
            rc = [route[n:n + 1, :] for n in range(i)]
            sel = []
            for n in range(i):
                rank = jnp.zeros((1, blk), jnp.int32)
                for m in range(i):
                    if m == n:
                        continue
                    beats = (rc[m] >= rc[n]) if m < n else (rc[m] > rc[n])
                    rank = rank + beats.astype(jnp.int32)
                sel.append(rank < MOBA_TOPK)
        t_list = []
        for n in range(i + 1):
            t = lax.dot_general(kb_ref[n * blk:(n + 1) * blk, :], qb, nt_dims,
                                preferred_element_type=F32)
            if n == i:
                t = jnp.where(causal, t + bias_own[...], NEG_INF)
            else:
                if n == i - 1:
                    t = t + bias_prev[...]
                if sel is not None:
                    t = jnp.where(sel[n], t, NEG_INF)
            t_list.append(t)
        return t_list

    t_next = scores(0)
    for i in range(nb):
        t_list = t_next
        if i + 1 < nb:
            t_next = scores(i + 1)
        n_far = max(i - 1, 0)
        m_run = jnp.max(t_list[n_far], axis=0, keepdims=True)
        for t in t_list[n_far + 1:]:
            m_run = jnp.maximum(m_run, jnp.max(t, axis=0, keepdims=True))
        if n_far:
            m_far = jnp.max(t_list[0], axis=0, keepdims=True)
            for t in t_list[1:n_far]:
                m_far = jnp.maximum(m_far, jnp.max(t, axis=0, keepdims=True))
            m_run = jnp.maximum(m_run, m_far + bias_far)
        acc = jnp.zeros((vt_ref.shape[0], blk), F32)
        for n, t in enumerate(t_list):
            offset = m_run - bias_far if n < n_far else m_run
            p = jnp.exp2((t - offset) * scale_log2e)
            acc = acc + jnp.dot(vt_ref[:, n * blk:(n + 1) * blk], p.astype(BF16),
                                preferred_element_type=F32)
        out = acc[:HEAD_DIM] / acc[HEAD_DIM:HEAD_DIM + 1]
        o_ref[i * blk:(i + 1) * blk, :] = out.T.astype(BF16)


def _moba(proj, rel_bias, bko, bkp, bsz, seq):
    nb = seq // MOBA_BLOCK
    qo, ko, vo = COL_MOBA_Q // HEAD_DIM, COL_MOBA_K // HEAD_DIM, COL_MOBA_V // HEAD_DIM
    blk = MOBA_BLOCK
    return pl.pallas_call(
        functools.partial(_moba_kernel, nb=nb),
        name="moba",
        grid=(MOBA_HEADS, bsz),
        in_specs=[pl.BlockSpec(memory_space=pltpu.SMEM),
                  pl.BlockSpec((blk, blk), lambda h, b: (0, 0)),
                  pl.BlockSpec((blk, blk), lambda h, b: (0, 0)),
                  pl.BlockSpec((seq, HEAD_DIM), lambda h, b: (b, qo + h)),
                  pl.BlockSpec((seq, HEAD_DIM), lambda h, b: (b, ko + h)),
                  pl.BlockSpec((seq, HEAD_DIM), lambda h, b: (b, vo + h))],
        out_specs=pl.BlockSpec((seq, HEAD_DIM), lambda h, b: (b, h)),
        out_shape=jax.ShapeDtypeStruct((bsz * seq, MOBA_W), BF16),
        scratch_shapes=[pltpu.VMEM((blk, blk), F32), pltpu.VMEM((blk, blk), F32),
                        pltpu.VMEM((seq, HEAD_DIM), BF16),
                        pltpu.VMEM((HEAD_DIM + BF16_SUBLANES, seq), BF16)],
        compiler_params=_vmem("moba"),
    )(rel_bias, bko, bkp, proj, proj, proj)


def _conv_silu(xh, w):
    acc = xh[SUBLANES:, :] * w[GDN_CONV - 1:GDN_CONV, :]
    for s in range(1, GDN_CONV):
        acc = acc + pltpu.roll(xh, s, axis=0)[SUBLANES:, :] * w[GDN_CONV - 1 - s:GDN_CONV - s, :]
    return _silu(acc)


def _l2norm(x):
    return x * lax.rsqrt(jnp.sum(x * x, axis=-1, keepdims=True) + RMS_EPS)


def _softplus(x):
    return jnp.maximum(x, 0.0) + jnp.log1p(jnp.exp(-jnp.abs(x)))


ELIM_BLOCK = 8
MERGE_LEVELS = (GDN_CHUNK // ELIM_BLOCK).bit_length() - 1


def _block_diag_inverse(lpair):
    n = lpair.shape[0]
    nv = n // SUBLANES
    vpb = ELIM_BLOCK // SUBLANES
    rid = lax.broadcasted_iota(jnp.int32, (SUBLANES, 2 * n), 0)
    lid = lax.broadcasted_iota(jnp.int32, (SUBLANES, 2 * n), 1)
    t_rows = [(lid % n == rid + v * SUBLANES).astype(F32) for v in range(nv)]
    l_rows = [lpair[v * SUBLANES:(v + 1) * SUBLANES, :] for v in range(nv)]
    for m in range(n - 1):
        v0, s0 = divmod(m, SUBLANES)
        v_end = (v0 // vpb + 1) * vpb
        row = t_rows[v0][s0:s0 + 1, :]
        col = (lid // n) * n + m
        for v in range(v0 if s0 < SUBLANES - 1 else v0 + 1, v_end):
            t_rows[v] = t_rows[v] - jnp.take_along_axis(l_rows[v], col, axis=1) * row
    return jnp.concatenate(t_rows, axis=0)


def _lane_block_diag(pair):
    first = lax.broadcasted_iota(jnp.int32, pair.shape, 1) < pair.shape[1] // 2
    zero = jnp.zeros_like(pair)
    return jnp.concatenate([jnp.where(first, pair, zero), jnp.where(first, zero, pair)], axis=0)


def _merge_lower_products(lpair, tpair, k):
    n = lpair.shape[0]
    ri = lax.broadcasted_iota(jnp.int32, lpair.shape, 0)
    ci = lax.broadcasted_iota(jnp.int32, lpair.shape, 1) % n
    off = (ri // (2 * k) == ci // (2 * k)) & (ri // k > ci // k)
    lk = jnp.where(off, lpair, 0.0).astype(BF16)
    return jnp.dot(lk, _lane_block_diag(tpair.astype(BF16)), preferred_element_type=F32)


def _merge_apply(tpair, lt):
    return tpair - jnp.dot(tpair.astype(BF16), _lane_block_diag(lt), preferred_element_type=F32)


PIPELINE_UNROLL = 10


def _software_pipeline(stages, n):
    ns = len(stages)

    def run(it, lo, hi):
        conts = [stages[s](it - s) for s in reversed(range(lo, hi))]
        for cont in conts:
            if cont is not None:
                cont()

    for it in range(ns - 1):
        run(it, 0, it + 1)

    steady = n - (ns - 1)
    assert steady % PIPELINE_UNROLL == 0, (n, ns)

    def body(k, carry):
        first = ns - 1 + k * PIPELINE_UNROLL
        for u in range(PIPELINE_UNROLL):
            run(first + u, 0, ns)
        return carry

    lax.fori_loop(0, steady // PIPELINE_UNROLL, body, 0)
    for it in range(n, n + ns - 1):
        run(it, it - n + 1, ns)


def _lane_pick(x, lane, idx):
    return jnp.sum(jnp.where(lane == idx, x, 0.0), axis=-1, keepdims=True)


def _gdn_kernel(alog_ref, dtb_ref, gp_ref, nw_ref, cwq_ref, cwk_ref, cwv_ref,
                q_ref, k_ref, v_ref, z_ref, sm_ref, rw_ref, y_ref,
                kb_s, qn_s, kdp_s, rhs_s, qd_s, gcb_s, betab_s, gcr_s, gram_s, lm_s, lhs2_s,
                sol_s, mp_s, n_s, r_s, st_s, *merge_s, nchunk):
    tl_s, lt_s = merge_s[:MERGE_LEVELS + 1], merge_s[MERGE_LEVELS + 1:]
    hq = pl.program_id(1)
    c64 = GDN_CHUNK
    hd = HEAD_DIM
    heads = range(2)

    tri_u = (lax.broadcasted_iota(jnp.int32, (c64, c64), 0)
             <= lax.broadcasted_iota(jnp.int32, (c64, c64), 1)).astype(F32)
    gc_rows = []
    for j in heads:
        hv = 2 * hq + j
        a_neg_r = -jnp.exp(jnp.full((nchunk, c64), alog_ref[hv], F32))
        g_row = a_neg_r * _softplus(rw_ref[2 + j] + dtb_ref[hv])
        gc_rows.append(jnp.dot(g_row, tri_u, precision=lax.Precision.HIGHEST,
                               preferred_element_type=F32))
    gcr_s[...] = jnp.concatenate(gc_rows, axis=1)

    rows = lax.broadcasted_iota(jnp.int32, (c64, 2 * c64), 0)
    cols = lax.broadcasted_iota(jnp.int32, (c64, 2 * c64), 1) % c64
    tril = rows >= cols
    strict = rows > cols
    eye2 = (lax.broadcasted_iota(jnp.int32, (2 * c64, 2 * c64), 0)
            == lax.broadcasted_iota(jnp.int32, (2 * c64, 2 * c64), 1)).astype(BF16)
    nt_dims = (((1,), (1,)), ((), ()))

    def rows_of(c):
        start = c * c64
        return pl.ds(start if isinstance(c, int) else pl.multiple_of(start, c64), c64)

    def with_halo(x_ref, c):
        if isinstance(c, int) and c == 0:
            return jnp.concatenate([jnp.zeros((SUBLANES, x_ref.shape[1]), F32), x_ref[:c64, :]],
                                   axis=0)
        start = c * c64 - SUBLANES
        if not isinstance(c, int):
            start = pl.multiple_of(start, SUBLANES)
        return x_ref[pl.ds(start, c64 + SUBLANES), :]

    lane = lax.broadcasted_iota(jnp.int32, (c64, N_SMALL), 1)
    first = lane < c64
    pos = lax.broadcasted_iota(jnp.int32, (c64, N_SMALL), 0)

    def stage_prep(c):
        r = rows_of(c)
        qn = _l2norm(_conv_silu(with_halo(q_ref, c), cwq_ref[...])) * (hd ** -0.5)
        kn = _l2norm(_conv_silu(with_halo(k_ref, c), cwk_ref[...]))
        vc = _conv_silu(with_halo(v_ref, c), cwv_ref[...])
        qn_s[r, :] = qn.astype(BF16)
        kb_s[r, :] = kn.astype(BF16)
        sm = sm_ref[r, :]
        sig_all = jax.nn.sigmoid(sm)
        gc_all = -jnp.exp(gp_ref[0:1, :]) * _softplus(sm + gp_ref[1:2, :])
        sft = 1
        while sft < c64:
            gc_all = gc_all + jnp.where(pos >= sft, pltpu.roll(gc_all, sft, axis=0), 0.0)
            sft *= 2
        rest_all = gc_all[c64 - 1:, :] - gc_all
        betas, gc_cols = [], []
        for j in heads:
            hv = 2 * hq + j
            beta = _lane_pick(sig_all, lane, hv)
            gc_col = _lane_pick(gc_all, lane, GDN_V_HEADS + hv)
            rest_col = _lane_pick(rest_all, lane, GDN_V_HEADS + hv)
            eg = jnp.exp(gc_col)
            rhs_s[j, r, :hd] = (vc[:, j * hd:(j + 1) * hd] * beta).astype(BF16)
            rhs_s[j, r, hd:] = (kn * (beta * eg)).astype(BF16)
            qd_s[j, r, :] = (qn * eg).astype(BF16)
            kdp_s[c, j * c64:(j + 1) * c64, :] = (kn * jnp.exp(rest_col)).astype(BF16)
            betas.append(beta)
            gc_cols.append(gc_col)
        gcb_s[r, :] = jnp.where(first, gc_cols[0], gc_cols[1])
        betab_s[r, :] = jnp.where(first, betas[0], betas[1])

    def stage_gram(c):
        r = rows_of(c)
        kb = kb_s[r, :]
        kq = jnp.concatenate([kb, qn_s[r, :]], axis=0)
        gram = lax.dot_general(kq, jnp.concatenate([kb, kb], axis=0), nt_dims,
                               preferred_element_type=F32)
        kd_t = lax.dot_general(eye2, kdp_s[c], nt_dims, preferred_element_type=F32)

        def finish():
            gram_s[c] = gram
            lhs2_s[c, :2 * c64, :] = kd_t.astype(BF16)

        return finish

    def stage_factor(c):
        r = rows_of(c)
        dec = jnp.exp(jnp.where(tril, gcb_s[r, :] - gcr_s[pl.ds(c, 1), :], NEG_INF))
        lpair = jnp.where(strict, gram_s[c, :c64, :] * dec, 0.0) * betab_s[r, :]
        lm_s[r, :] = lpair
        lhs2_s[c, 2 * c64:, :] = (gram_s[c, c64:, :] * dec).astype(BF16)
        tl_s[0][r, :] = _block_diag_inverse(lpair)

    def stage_merge_products(level):
        def stage(c):
            r = rows_of(c)
            lt = _merge_lower_products(lm_s[r, :], tl_s[level][r, :], ELIM_BLOCK << level)

            def finish():
                lt_s[level][r, :] = lt.astype(BF16)

            return finish
        return stage

    def stage_merge_apply(level):
        def stage(c):
            r = rows_of(c)
            merged = _merge_apply(tl_s[level][r, :], lt_s[level][r, :])

            def finish():
                tl_s[level + 1][r, :] = merged

            return finish
        return stage

    def stage_solve(c):
        r = rows_of(c)
        zero = jnp.zeros((c64, 2 * hd), BF16)
        rhs_bd = jnp.concatenate([jnp.concatenate([rhs_s[0, r, :], zero], axis=1),
                                  jnp.concatenate([zero, rhs_s[1, r, :]], axis=1)], axis=0)
        sol = jnp.dot(tl_s[MERGE_LEVELS][r, :].astype(BF16), rhs_bd,
                      preferred_element_type=F32)

        def finish():
            sol_s[r, :] = sol.astype(BF16)

        return finish

    def stage_fold(c):
        r = rows_of(c)
        sol = sol_s[r, :]
        out = jnp.dot(lhs2_s[c], _lane_block_diag(sol), preferred_element_type=F32)

        def finish():
            for j in heads:
                u_col, w_col = 2 * j * hd, (2 * j + 1) * hd
                n_s[j, c] = out[:2 * c64, u_col:u_col + hd]
                mp_s[j, c, :2 * c64, :] = out[:2 * c64, w_col:w_col + hd].astype(BF16)
                mp_s[j, c, 2 * c64:, :] = (qd_s[j, r, :].astype(F32)
                                           - out[2 * c64:, w_col:w_col + hd]).astype(BF16)
                r_s[j, r, :] = out[2 * c64:, u_col:u_col + hd]

        return finish

    st_s[...] = jnp.zeros_like(st_s)
    nw = nw_ref[...]

    def stage_state(c):
        r = rows_of(c)
        sts = [st_s[j] for j in heads]
        outs = [jnp.dot(mp_s[j, c], sts[j].astype(BF16), preferred_element_type=F32)
                for j in heads]

        def finish():
            for j in heads:
                end = (j + 1) * c64
                gl = gcr_s[pl.ds(c, 1), end - 1:end]
                st_s[j] = sts[j] * jnp.exp(gl) - outs[j][:2 * c64] + n_s[j, c]
                r_s[j, r, :] = r_s[j, r, :] + outs[j][2 * c64:]

        return finish

    def stage_norm(c):
        r = rows_of(c)
        for j in heads:
            o = r_s[j, r, :]
            og = (o * lax.rsqrt(jnp.mean(o * o, axis=-1, keepdims=True) + RMS_EPS)
                  * nw * _silu(z_ref[r, j * hd:(j + 1) * hd]))
            y_ref[r, j * hd:(j + 1) * hd] = og.astype(BF16)

    merges = [stage(lv) for lv in range(MERGE_LEVELS)
              for stage in (stage_merge_products, stage_merge_apply)]
    _software_pipeline([stage_prep, stage_gram, stage_factor] + merges
                       + [stage_solve, stage_fold, stage_state, stage_norm], nchunk)


def _gdn(proj, small, rows, conv_w, a_log, dt_bias, norm_w, bsz, seq):
    nchunk = seq // GDN_CHUNK
    hd = HEAD_DIM
    c64 = GDN_CHUNK
    qo, ko = COL_GDN_Q // hd, COL_GDN_K // hd
    vo, zo = COL_GDN_V // (2 * hd), COL_GDN_Z // (2 * hd)
    cvo = (2 * GDN_QK_W) // (2 * hd)
    smem = pl.BlockSpec(memory_space=pltpu.SMEM)
    pad = (GDN_V_HEADS, N_SMALL - 2 * GDN_V_HEADS)
    gate_params = jnp.stack([jnp.pad(a_log, pad), jnp.pad(dt_bias, pad)])
    return pl.pallas_call(
        functools.partial(_gdn_kernel, nchunk=nchunk),
        name="gdn",
        grid=(bsz, GDN_QK_HEADS),
        in_specs=[smem, smem,
                  pl.BlockSpec((2, N_SMALL), lambda b, h: (0, 0)),
                  pl.BlockSpec((1, hd), lambda b, h: (0, 0)),
                  pl.BlockSpec((GDN_CONV, hd), lambda b, h: (0, h)),
                  pl.BlockSpec((GDN_CONV, hd), lambda b, h: (0, GDN_QK_HEADS + h)),
                  pl.BlockSpec((GDN_CONV, 2 * hd), lambda b, h: (0, cvo + h)),
                  pl.BlockSpec((seq, hd), lambda b, h: (b, qo + h)),
                  pl.BlockSpec((seq, hd), lambda b, h: (b, ko + h)),
                  pl.BlockSpec((seq, 2 * hd), lambda b, h: (b, vo + h)),
                  pl.BlockSpec((seq, 2 * hd), lambda b, h: (b, zo + h)),
                  pl.BlockSpec((seq, N_SMALL), lambda b, h: (b, 0)),
                  pl.BlockSpec((None, None, 4, nchunk, GDN_CHUNK), lambda b, h: (b, h, 0, 0, 0))],
        out_specs=pl.BlockSpec((seq, 2 * hd), lambda b, h: (b, h)),
        out_shape=jax.ShapeDtypeStruct((bsz * seq, GDN_V_W), BF16),
        scratch_shapes=[pltpu.VMEM((seq, hd), BF16),
                        pltpu.VMEM((seq, hd), BF16),
                        pltpu.VMEM((nchunk, 2 * c64, hd), BF16),
                        pltpu.VMEM((2, seq, 2 * hd), BF16),
                        pltpu.VMEM((2, seq, hd), BF16),
                        pltpu.VMEM((seq, 2 * c64), F32),
                        pltpu.VMEM((seq, 2 * c64), F32),
                        pltpu.VMEM((nchunk, 2 * c64), F32),
                        pltpu.VMEM((nchunk, 2 * c64, 2 * c64), F32),
                        pltpu.VMEM((seq, 2 * c64), F32),
                        pltpu.VMEM((nchunk, 3 * c64, hd), BF16),
                        pltpu.VMEM((seq, 4 * hd), BF16),
                        pltpu.VMEM((2, nchunk, 3 * c64, hd), BF16),
                        pltpu.VMEM((2, nchunk, hd, hd), F32),
                        pltpu.VMEM((2, seq, hd), F32),
                        pltpu.VMEM((2, hd, hd), F32)]
                       + [pltpu.VMEM((seq, 2 * c64), F32)] * (MERGE_LEVELS + 1)
                       + [pltpu.VMEM((seq, 2 * c64), BF16)] * MERGE_LEVELS,
        compiler_params=_vmem("gdn"),
    )(a_log, dt_bias, gate_params, norm_w, conv_w, conv_w, conv_w, proj, proj, proj, proj, small, rows)


def _merge_kernel(ya_ref, yb_ref, ga_ref, gb_ref, x_ref, g1_ref, sh2_ref, sc2_ref,
                  lng_ref, lnb_ref, wpm_ref, wpg_ref, wo_ref, x1_ref, h2_ref):
    pa = jnp.dot(ya_ref[...], wpm_ref[...], preferred_element_type=F32)
    pb = jnp.dot(yb_ref[...], wpg_ref[...], preferred_element_type=F32)
    merged = jax.nn.sigmoid(ga_ref[...]) * pa + jax.nn.sigmoid(gb_ref[...]) * pb
    y = jnp.dot(merged.astype(BF16), wo_ref[...], preferred_element_type=F32)
    x1 = _layer_norm(DEEPNORM_ALPHA * x_ref[...] + g1_ref[...] * y, lng_ref[...], lnb_ref[...])
    x1_ref[...] = x1
    h2_ref[...] = (x1 * (1.0 + sc2_ref[...]) + sh2_ref[...]).astype(BF16)


def _merge(ya, yb, proj, x2, mod3, ln_g, ln_b, wpm, wpg, wo, seq):
    t = x2.shape[0]
    tm = MERGE_TM
    per_b = seq // tm
    d = D_MODEL

    def modspec(k):
        return pl.BlockSpec((None, 1, d), lambda i: (i // per_b, 0, k))

    def const(shape):
        return pl.BlockSpec(shape, lambda i: (0, 0), pipeline_mode=pl.Buffered(1))

    return pl.pallas_call(
        _merge_kernel,
        name="merge",
        grid=(t // tm,),
        in_specs=[pl.BlockSpec((tm, MOBA_W), lambda i: (i, 0)),
                  pl.BlockSpec((tm, GDN_V_W), lambda i: (i, 0)),
                  pl.BlockSpec((tm, d), lambda i: (i, COL_GATE_A // d)),
                  pl.BlockSpec((tm, d), lambda i: (i, COL_GATE_B // d)),
                  pl.BlockSpec((tm, d), lambda i: (i, 0)),
                  modspec(2), modspec(3), modspec(4),
                  const((1, d)), const((1, d)),
                  const((MOBA_W, d)), const((GDN_V_W, d)), const((d, d))],
        out_specs=[pl.BlockSpec((tm, d), lambda i: (i, 0)),
                   pl.BlockSpec((tm, d), lambda i: (i, 0))],
        out_shape=[jax.ShapeDtypeStruct((t, d), F32), jax.ShapeDtypeStruct((t, d), BF16)],
        compiler_params=_vmem("merge"),
    )(ya, yb, proj, proj, x2, mod3, mod3, mod3, ln_g, ln_b, wpm, wpg, wo)


def _ffn_kernel(h_ref, x1_ref, g2_ref, lng_ref, lnb_ref, wg_ref, wu_ref, wo_ref, o_ref, acc_ref):
    f = pl.program_id(1)

    @pl.when(f == 0)
    def _():
        acc_ref[...] = jnp.zeros_like(acc_ref)

    h = h_ref[...]
    gate = jnp.dot(h, wg_ref[...], preferred_element_type=F32)
    up = jnp.dot(h, wu_ref[...], preferred_element_type=F32)
    act = (_silu(gate) * up).astype(BF16)
    acc_ref[...] += jnp.dot(act, wo_ref[...], preferred_element_type=F32)

    @pl.when(f == pl.num_programs(1) - 1)
    def _():
        r = DEEPNORM_ALPHA * x1_ref[...] + g2_ref[...] * acc_ref[...]
        o_ref[...] = _layer_norm(r, lng_ref[...], lnb_ref[...])


def _ffn(h2, x1, mod3, ln_g, ln_b, w_in, w_out, seq):
    t = h2.shape[0]
    tm, tf = FFN_TM, FFN_TF
    per_b = seq // tm
    d = D_MODEL
    nf = D_FF // tf
    return pl.pallas_call(
        _ffn_kernel,
        name="ffn",
        grid=(t // tm, nf),
        in_specs=[pl.BlockSpec((tm, d), lambda i, f: (i, 0)),
                  pl.BlockSpec((tm, d), lambda i, f: (i, 0)),
                  pl.BlockSpec((None, 1, d), lambda i, f: (i // per_b, 0, 5)),
                  pl.BlockSpec((1, d), lambda i, f: (0, 0)),
                  pl.BlockSpec((1, d), lambda i, f: (0, 0)),
                  pl.BlockSpec((d, tf), lambda i, f: (0, f)),
                  pl.BlockSpec((d, tf), lambda i, f: (0, nf + f)),
                  pl.BlockSpec((tf, d), lambda i, f: (f, 0))],
        out_specs=pl.BlockSpec((tm, d), lambda i, f: (i, 0)),
        out_shape=jax.ShapeDtypeStruct((t, d), F32),
        scratch_shapes=[pltpu.VMEM((tm, d), F32)],
        compiler_params=_vmem("ffn"),
    )(h2, x1, mod3, ln_g, ln_b, w_in, w_in, w_out)


def _rel_bucket(dist):
    max_exact = REL_BUCKETS // 2
    n = jnp.maximum(dist, 0)
    nf = jnp.maximum(n, 1).astype(F32)
    large = max_exact + (jnp.log(nf / max_exact) / math.log(REL_MAX_DIST / max_exact)
                         * (REL_BUCKETS - max_exact)).astype(jnp.int32)
    large = jnp.minimum(large, REL_BUCKETS - 1)
    return jnp.where(n < max_exact, n, large)


def _layer(x, c, w_ada, b_ada, w_in, conv_w, a_log, dt_bias, gdn_norm_w, rel_bias,
           w_proj_moba, w_proj_gdn, w_out, ln1_g, ln1_b, w_ffn_in, w_ffn_out, ln2_g, ln2_b):
    bsz, seq, d = x.shape
    t = bsz * seq
    x2 = x.reshape(t, d)

    mod = _ada_mod(c, w_ada, b_ada)
    mod3 = mod.reshape(bsz, 1, 6 * d)

    w_all = w_in.astype(BF16)
    n_gates = 2 * GDN_V_HEADS
    w_gate = w_all[:, N_HEAD + n_gates:]
    w_small = jnp.pad(w_all[:, N_HEAD:N_HEAD + n_gates], ((0, 0), (0, N_SMALL - n_gates)))

    proj, small = _in_proj(x2, mod3, w_all, w_gate, w_small, seq)

    ii = jnp.arange(MOBA_BLOCK, dtype=jnp.int32)
    dist = ii[None, :] - ii[:, None]
    bko = _rel_bucket(dist)
    bkp = _rel_bucket(dist + MOBA_BLOCK)
    ya = _moba(proj, rel_bias, bko, bkp, bsz, seq)

    nchunk = seq // GDN_CHUNK
    sm_t = small[:, :2 * GDN_V_HEADS].reshape(bsz, seq, 2, GDN_QK_HEADS, 2)
    rows = sm_t.transpose(0, 3, 2, 4, 1).reshape(bsz, GDN_QK_HEADS, 4, nchunk, GDN_CHUNK)
    yb = _gdn(proj, small, rows, conv_w, a_log, dt_bias, gdn_norm_w.reshape(1, HEAD_DIM),
              bsz, seq)

    x1, h2 = _merge(ya, yb, proj, x2, mod3, ln1_g.reshape(1, d), ln1_b.reshape(1, d),
                    w_proj_moba.astype(BF16), w_proj_gdn.astype(BF16), w_out.astype(BF16), seq)
    out = _ffn(h2, x1, mod3, ln2_g.reshape(1, d), ln2_b.reshape(1, d),
               w_ffn_in.astype(BF16), w_ffn_out.astype(BF16), seq)
    return out.reshape(bsz, seq, d)


def kernel(x, c, w_ada, b_ada, w_in, conv_w, a_log, dt_bias, gdn_norm_w, rel_bias, w_proj_moba,
           w_proj_gdn, w_out, ln1_g, ln1_b, w_ffn_in, w_ffn_out, ln2_g, ln2_b):
    depth = w_ada.shape[0]
    for l in range(depth):
        x = _layer(x, c, w_ada[l], b_ada[l], w_in[l], conv_w[l], a_log[l], dt_bias[l],
                   gdn_norm_w[l], rel_bias, w_proj_moba[l], w_proj_gdn[l], w_out[l],
                   ln1_g[l], ln1_b[l], w_ffn_in[l], w_ffn_out[l], ln2_g[l], ln2_b[l])
    return x
```

```python
import functools
import math

import jax
import jax.numpy as jnp
from jax import lax
from jax.experimental import pallas as pl
from jax.experimental.pallas import tpu as pltpu

F32 = jnp.float32
BF16 = jnp.bfloat16

D_MODEL = 2048
MOBA_HEADS = 8
HEAD_DIM = 128
MOBA_W = MOBA_HEADS * HEAD_DIM
MOBA_BLOCK = 256
MOBA_TOPK = 3
REL_BUCKETS = 32
REL_MAX_DIST = 128
GDN_QK_HEADS = 8
GDN_V_HEADS = 16
GDN_QK_W = GDN_QK_HEADS * HEAD_DIM
GDN_V_W = GDN_V_HEADS * HEAD_DIM
GDN_CONV = 4
GDN_CHUNK = 64
D_FF = 5632
DEEPNORM_ALPHA = 2.0 ** 0.25
LN_EPS = 1e-5
RMS_EPS = 1e-6
NEG_INF = -1e30

COL_GATE_A = 0
COL_GATE_B = 2048
COL_MOBA_Q = 4096
COL_MOBA_K = 5120
COL_MOBA_V = 6144
COL_GDN_Q = 7168
COL_GDN_K = 8192
COL_GDN_V = 9216
COL_GDN_Z = 11264
N_MAIN = 13312
N_GATE = 4096
N_HEAD = N_MAIN - N_GATE
N_SMALL = 128

V7X_VMEM_MIB = 64
SUBLANES = 8
BF16_SUBLANES = 16

ADA_TN = 1024
INPROJ_TM, INPROJ_TN = 1024, 1024
MERGE_TM = 256
FFN_TM, FFN_TF = 512, 512
VMEM_LIMIT_MIB = {"ada_mod": 40, "in_proj": 56, "moba": 48, "gdn": 58, "merge": 56, "ffn": 48}
assert max(VMEM_LIMIT_MIB.values()) < V7X_VMEM_MIB


def _vmem(name):
    return pltpu.CompilerParams(vmem_limit_bytes=VMEM_LIMIT_MIB[name] * 1024 * 1024)


def _silu(x):
    return x * jax.nn.sigmoid(x)


def _layer_norm(r, gain, bias):
    mu = jnp.mean(r, axis=-1, keepdims=True)
    d = r - mu
    var = jnp.mean(d * d, axis=-1, keepdims=True)
    return d * lax.rsqrt(var + LN_EPS) * gain + bias


def _ada_kernel(c_ref, w_ref, b_ref, o_ref):
    sc = _silu(c_ref[...])
    w = w_ref[...]
    sc_hi = sc.astype(BF16)
    sc_lo = (sc - sc_hi.astype(F32)).astype(BF16)
    w_hi = w.astype(BF16)
    w_lo = (w - w_hi.astype(F32)).astype(BF16)
    acc = jnp.dot(sc_hi, w_hi, preferred_element_type=F32)
    acc = acc + jnp.dot(sc_hi, w_lo, preferred_element_type=F32)
    acc = acc + jnp.dot(sc_lo, w_hi, preferred_element_type=F32)
    o_ref[...] = acc + b_ref[...]


def _ada_mod(c, w_ada, b_ada):
    bsz = c.shape[0]
    n = w_ada.shape[1]
    tn = ADA_TN
    return pl.pallas_call(
        _ada_kernel,
        name="ada_mod",
        grid=(n // tn,),
        in_specs=[pl.BlockSpec((bsz, D_MODEL), lambda j: (0, 0)),
                  pl.BlockSpec((D_MODEL, tn), lambda j: (0, j)),
                  pl.BlockSpec((1, tn), lambda j: (0, j))],
        out_specs=pl.BlockSpec((bsz, tn), lambda j: (0, j)),
        out_shape=jax.ShapeDtypeStruct((bsz, n), F32),
        compiler_params=_vmem("ada_mod"),
    )(c, w_ada, b_ada.reshape(1, n))


def _serpentine(i, j, n):
    return jnp.where(i % 2 == 0, j, n - 1 - j)


def _inproj_kernel(x_ref, sh_ref, sc_ref, w_ref, wg_ref, ws_ref, o_ref, os_ref, h_ref, *, n_head,
                   n_tiles):
    @pl.when(pl.program_id(1) == 0)
    def _():
        h = x_ref[...] * (1.0 + sc_ref[...]) + sh_ref[...]
        h_ref[...] = h.astype(BF16)
        os_ref[...] = jnp.dot(h_ref[...], ws_ref[...], preferred_element_type=F32)

    col = _serpentine(pl.program_id(0), pl.program_id(1), n_tiles)

    @pl.when(col < n_head)
    def _():
        o_ref[...] = jnp.dot(h_ref[...], w_ref[...], preferred_element_type=F32)

    @pl.when(col >= n_head)
    def _():
        o_ref[...] = jnp.dot(h_ref[...], wg_ref[...], preferred_element_type=F32)


def _in_proj(x2, mod3, w_all, w_gate, w_small, seq):
    t = x2.shape[0]
    tm, tn = INPROJ_TM, INPROJ_TN
    per_b = seq // tm
    n_head, n_gate, n_tiles = N_HEAD // tn, N_GATE // tn, N_MAIN // tn

    def col(i, j):
        return _serpentine(i, j, n_tiles)

    return pl.pallas_call(
        functools.partial(_inproj_kernel, n_head=n_head, n_tiles=n_tiles),
        name="in_proj",
        grid=(t // tm, n_tiles),
        in_specs=[pl.BlockSpec((tm, D_MODEL), lambda i, j: (i, 0)),
                  pl.BlockSpec((None, 1, D_MODEL), lambda i, j: (i // per_b, 0, 0)),
                  pl.BlockSpec((None, 1, D_MODEL), lambda i, j: (i // per_b, 0, 1)),
                  pl.BlockSpec((D_MODEL, tn), lambda i, j: (0, jnp.minimum(col(i, j), n_head - 1))),
                  pl.BlockSpec((D_MODEL, tn), lambda i, j: (0, jnp.maximum(col(i, j) - n_head, 0))),
                  pl.BlockSpec((D_MODEL, N_SMALL), lambda i, j: (0, 0))],
        out_specs=[pl.BlockSpec((tm, tn), lambda i, j: (i, (col(i, j) + n_gate) % n_tiles)),
                   pl.BlockSpec((tm, N_SMALL), lambda i, j: (i, 0))],
        out_shape=[jax.ShapeDtypeStruct((t, N_MAIN), F32),
                   jax.ShapeDtypeStruct((t, N_SMALL), F32)],
        scratch_shapes=[pltpu.VMEM((tm, D_MODEL), BF16)],
        compiler_params=_vmem("in_proj"),
    )(x2, mod3, mod3, w_all, w_gate, w_small)


def _moba_kernel(rel_ref, bko_ref, bkp_ref, q_ref, k_ref, v_ref, o_ref,
                 bias_own, bias_prev, kb_ref, vt_ref, *, nb):
    h = pl.program_id(0)
    blk = MOBA_BLOCK
    inv_scale = HEAD_DIM ** 0.5
    scale_log2e = HEAD_DIM ** -0.5 * math.log2(math.e)

    @pl.when(pl.program_id(1) == 0)
    def _():
        bo = bko_ref[...]
        bp = bkp_ref[...]
        acc_o = jnp.zeros((blk, blk), F32)
        acc_p = jnp.zeros((blk, blk), F32)
        for kk in range(REL_BUCKETS):
            val = rel_ref[kk, h] * inv_scale
            acc_o = jnp.where(bo == kk, val, acc_o)
            acc_p = jnp.where(bp == kk, val, acc_p)
        bias_own[...] = acc_o
        bias_prev[...] = acc_p

    bias_far = rel_ref[REL_BUCKETS - 1, h] * inv_scale
    kf = k_ref[...]
    kmean = jnp.mean(kf.reshape(nb, blk, HEAD_DIM), axis=1)
    kb_ref[...] = kf.astype(BF16)
    nt_dims = (((1,), (1,)), ((), ()))
    eye = (lax.broadcasted_iota(jnp.int32, (HEAD_DIM, HEAD_DIM), 0)
           == lax.broadcasted_iota(jnp.int32, (HEAD_DIM, HEAD_DIM), 1)).astype(BF16)
    vt_ref[:HEAD_DIM, :] = lax.dot_general(eye, v_ref[...].astype(BF16), nt_dims,
                                           preferred_element_type=F32).astype(BF16)
    pad_rows = vt_ref.shape[0] - HEAD_DIM
    vt_ref[HEAD_DIM:, :] = (lax.broadcasted_iota(jnp.int32, (pad_rows, vt_ref.shape[1]), 0)
                            == 0).astype(BF16)
    causal = (lax.broadcasted_iota(jnp.int32, (blk, blk), 0)
              <= lax.broadcasted_iota(jnp.int32, (blk, blk), 1))

    def scores(i):
        qi = q_ref[i * blk:(i + 1) * blk, :]
        qb = qi.astype(BF16)
        sel = None
        if i > MOBA_TOPK:
            route = lax.dot_general(kmean, qi, nt_dims, precision=lax.Precision.HIGHEST,
                                    preferred_element_type=F32)
            rc = [route[n:n + 1, :] for n in range(i)]
            sel = []
            for n in range(i):
                rank = jnp.zeros((1, blk), jnp.int32)
                for m in range(i):
                    if m == n:
                        continue
                    beats = (rc[m] >= rc[n]) if m < n else (rc[m] > rc[n])
                    rank = rank + beats.astype(jnp.int32)
                sel.append(rank < MOBA_TOPK)
        t_list = []
        for n in range(i + 1):
            t = lax.dot_general(kb_ref[n * blk:(n + 1) * blk, :], qb, nt_dims,
                                preferred_element_type=F32)
            if n == i:
                t = jnp.where(causal, t + bias_own[...], NEG_INF)
            else:
                if n == i - 1:
                    t = t + bias_prev[...]
                if sel is not None:
                    t = jnp.where(sel[n], t, NEG_INF)
            t_list.append(t)
        return t_list

    t_next = scores(0)
    for i in range(nb):
        t_list = t_next
        if i + 1 < nb:
            t_next = scores(i + 1)
        n_far = max(i - 1, 0)
        m_run = jnp.max(t_list[n_far], axis=0, keepdims=True)
        for t in t_list[n_far + 1:]:
            m_run = jnp.maximum(m_run, jnp.max(t, axis=0, keepdims=True))
        if n_far:
            m_far = jnp.max(t_list[0], axis=0, keepdims=True)
            for t in t_list[1:n_far]:
                m_far = jnp.maximum(m_far, jnp.max(t, axis=0, keepdims=True))
            m_run = jnp.maximum(m_run, m_far + bias_far)
        acc = jnp.zeros((vt_ref.shape[0], blk), F32)
        for n, t in enumerate(t_list):
            offset = m_run - bias_far if n < n_far else m_run
            p = jnp.exp2((t - offset) * scale_log2e)
            acc = acc + jnp.dot(vt_ref[:, n * blk:(n + 1) * blk], p.astype(BF16),
                                preferred_element_type=F32)
        out = acc[:HEAD_DIM] / acc[HEAD_DIM:HEAD_DIM + 1]
        o_ref[i * blk:(i + 1) * blk, :] = out.T.astype(BF16)


def _moba(proj, rel_bias, bko, bkp, bsz, seq):
    nb = seq // MOBA_BLOCK
    qo, ko, vo = COL_MOBA_Q // HEAD_DIM, COL_MOBA_K // HEAD_DIM, COL_MOBA_V // HEAD_DIM
    blk = MOBA_BLOCK
    return pl.pallas_call(
        functools.partial(_moba_kernel, nb=nb),
        name="moba",
        grid=(MOBA_HEADS, bsz),
        in_specs=[pl.BlockSpec(memory_space=pltpu.SMEM),
                  pl.BlockSpec((blk, blk), lambda h, b: (0, 0)),
                  pl.BlockSpec((blk, blk), lambda h, b: (0, 0)),
                  pl.BlockSpec((seq, HEAD_DIM), lambda h, b: (b, qo + h)),
                  pl.BlockSpec((seq, HEAD_DIM), lambda h, b: (b, ko + h)),
                  pl.BlockSpec((seq, HEAD_DIM), lambda h, b: (b, vo + h))],
        out_specs=pl.BlockSpec((seq, HEAD_DIM), lambda h, b: (b, h)),
        out_shape=jax.ShapeDtypeStruct((bsz * seq, MOBA_W), BF16),
        scratch_shapes=[pltpu.VMEM((blk, blk), F32), pltpu.VMEM((blk, blk), F32),
                        pltpu.VMEM((seq, HEAD_DIM), BF16),
                        pltpu.VMEM((HEAD_DIM + BF16_SUBLANES, seq), BF16)],
        compiler_params=_vmem("moba"),
    )(rel_bias, bko, bkp, proj, proj, proj)


def _conv_silu(xh, w):
    acc = xh[SUBLANES:, :] * w[GDN_CONV - 1:GDN_CONV, :]
    for s in range(1, GDN_CONV):
        acc = acc + pltpu.roll(xh, s, axis=0)[SUBLANES:, :] * w[GDN_CONV - 1 - s:GDN_CONV - s, :]
    return _silu(acc)


def _l2norm(x):
    return x * lax.rsqrt(jnp.sum(x * x, axis=-1, keepdims=True) + RMS_EPS)


def _softplus(x):
    return jnp.maximum(x, 0.0) + jnp.log1p(jnp.exp(-jnp.abs(x)))


ELIM_BLOCK = 8
MERGE_LEVELS = (GDN_CHUNK // ELIM_BLOCK).bit_length() - 1


def _block_diag_inverse(lpair):
    n = lpair.shape[0]
    nv = n // SUBLANES
    vpb = ELIM_BLOCK // SUBLANES
    rid = lax.broadcasted_iota(jnp.int32, (SUBLANES, 2 * n), 0)
    lid = lax.broadcasted_iota(jnp.int32, (SUBLANES, 2 * n), 1)
    t_rows = [(lid % n == rid + v * SUBLANES).astype(F32) for v in range(nv)]
    l_rows = [lpair[v * SUBLANES:(v + 1) * SUBLANES, :] for v in range(nv)]
    for m in range(n - 1):
        v0, s0 = divmod(m, SUBLANES)
        v_end = (v0 // vpb + 1) * vpb
        row = t_rows[v0][s0:s0 + 1, :]
        col = (lid // n) * n + m
        for v in range(v0 if s0 < SUBLANES - 1 else v0 + 1, v_end):
            t_rows[v] = t_rows[v] - jnp.take_along_axis(l_rows[v], col, axis=1) * row
    return jnp.concatenate(t_rows, axis=0)


def _lane_block_diag(pair):
    first = lax.broadcasted_iota(jnp.int32, pair.shape, 1) < pair.shape[1] // 2
    zero = jnp.zeros_like(pair)
    return jnp.concatenate([jnp.where(first, pair, zero), jnp.where(first, zero, pair)], axis=0)


def _merge_lower_products(lpair, tpair, k):
    n = lpair.shape[0]
    ri = lax.broadcasted_iota(jnp.int32, lpair.shape, 0)
    ci = lax.broadcasted_iota(jnp.int32, lpair.shape, 1) % n
    off = (ri // (2 * k) == ci // (2 * k)) & (ri // k > ci // k)
    lk = jnp.where(off, lpair, 0.0).astype(BF16)
    return jnp.dot(lk, _lane_block_diag(tpair.astype(BF16)), preferred_element_type=F32)


def _merge_apply(tpair, lt):
    return tpair - jnp.dot(tpair.astype(BF16), _lane_block_diag(lt), preferred_element_type=F32)


PIPELINE_UNROLL = 4


def _software_pipeline(stages, n):
    ns = len(stages)

    def run(it, lo, hi):
        conts = [stages[s](it - s) for s in reversed(range(lo, hi))]
        for cont in conts:
            if cont is not None:
                cont()

    for it in range(ns - 1):
        run(it, 0, it + 1)

    steady = n - (ns - 1)
    assert steady % PIPELINE_UNROLL == 0, (n, ns)

    def body(k, carry):
        first = ns - 1 + k * PIPELINE_UNROLL
        for u in range(PIPELINE_UNROLL):
            run(first + u, 0, ns)
        return carry

    lax.fori_loop(0, steady // PIPELINE_UNROLL, body, 0)
    for it in range(n, n + ns - 1):
        run(it, it - n + 1, ns)


def _lane_pick(x, lane, idx):
    return jnp.sum(jnp.where(lane == idx, x, 0.0), axis=-1, keepdims=True)


def _gdn_kernel(alog_ref, dtb_ref, gp_ref, nw_ref, cwq_ref, cwk_ref, cwv_ref,
                q_ref, k_ref, v_ref, z_ref, sm_ref, rw_ref, y_ref,
                kb_s, qn_s, kdp_s, rhs_s, qd_s, gcb_s, betab_s, gcr_s, gram_s, lm_s, lhs2_s,
                sol_s, mp_s, n_s, r_s, st_s, *merge_s, nchunk):
    tl_s, lt_s = merge_s[:MERGE_LEVELS + 1], merge_s[MERGE_LEVELS + 1:]
    hq = pl.program_id(1)
    c64 = GDN_CHUNK
    hd = HEAD_DIM
    heads = range(2)

    tri_u = (lax.broadcasted_iota(jnp.int32, (c64, c64), 0)
             <= lax.broadcasted_iota(jnp.int32, (c64, c64), 1)).astype(F32)
    gc_rows = []
    for j in heads:
        hv = 2 * hq + j
        a_neg_r = -jnp.exp(jnp.full((nchunk, c64), alog_ref[hv], F32))
        g_row = a_neg_r * _softplus(rw_ref[2 + j] + dtb_ref[hv])
        gc_rows.append(jnp.dot(g_row, tri_u, precision=lax.Precision.HIGHEST,
                               preferred_element_type=F32))
    gcr_s[...] = jnp.concatenate(gc_rows, axis=1)

    rows = lax.broadcasted_iota(jnp.int32, (c64, 2 * c64), 0)
    cols = lax.broadcasted_iota(jnp.int32, (c64, 2 * c64), 1) % c64
    tril = rows >= cols
    strict = rows > cols
    eye2 = (lax.broadcasted_iota(jnp.int32, (2 * c64, 2 * c64), 0)
            == lax.broadcasted_iota(jnp.int32, (2 * c64, 2 * c64), 1)).astype(BF16)
    nt_dims = (((1,), (1,)), ((), ()))

    def rows_of(c):
        start = c * c64
        return pl.ds(start if isinstance(c, int) else pl.multiple_of(start, c64), c64)

    def with_halo(x_ref, c):
        if isinstance(c, int) and c == 0:
            return jnp.concatenate([jnp.zeros((SUBLANES, x_ref.shape[1]), F32), x_ref[:c64, :]],
                                   axis=0)
        start = c * c64 - SUBLANES
        if not isinstance(c, int):
            start = pl.multiple_of(start, SUBLANES)
        return x_ref[pl.ds(start, c64 + SUBLANES), :]

    lane = lax.broadcasted_iota(jnp.int32, (c64, N_SMALL), 1)
    first = lane < c64
    pos = lax.broadcasted_iota(jnp.int32, (c64, N_SMALL), 0)

    def stage_prep(c):
        r = rows_of(c)
        qn = _l2norm(_conv_silu(with_halo(q_ref, c), cwq_ref[...])) * (hd ** -0.5)
        kn = _l2norm(_conv_silu(with_halo(k_ref, c), cwk_ref[...]))
        vc = _conv_silu(with_halo(v_ref, c), cwv_ref[...])
        qn_s[r, :] = qn.astype(BF16)
        kb_s[r, :] = kn.astype(BF16)
        sm = sm_ref[r, :]
        sig_all = jax.nn.sigmoid(sm)
        gc_all = -jnp.exp(gp_ref[0:1, :]) * _softplus(sm + gp_ref[1:2, :])
        sft = 1
        while sft < c64:
            gc_all = gc_all + jnp.where(pos >= sft, pltpu.roll(gc_all, sft, axis=0), 0.0)
            sft *= 2
        rest_all = gc_all[c64 - 1:, :] - gc_all
        betas, gc_cols = [], []
        for j in heads:
            hv = 2 * hq + j
            beta = _lane_pick(sig_all, lane, hv)
            gc_col = _lane_pick(gc_all, lane, GDN_V_HEADS + hv)
            rest_col = _lane_pick(rest_all, lane, GDN_V_HEADS + hv)
            eg = jnp.exp(gc_col)
            rhs_s[j, r, :hd] = (vc[:, j * hd:(j + 1) * hd] * beta).astype(BF16)
            rhs_s[j, r, hd:] = (kn * (beta * eg)).astype(BF16)
            qd_s[j, r, :] = (qn * eg).astype(BF16)
            kdp_s[c, j * c64:(j + 1) * c64, :] = (kn * jnp.exp(rest_col)).astype(BF16)
            betas.append(beta)
            gc_cols.append(gc_col)
        gcb_s[r, :] = jnp.where(first, gc_cols[0], gc_cols[1])
        betab_s[r, :] = jnp.where(first, betas[0], betas[1])

    def stage_gram(c):
        r = rows_of(c)
        kb = kb_s[r, :]
        kq = jnp.concatenate([kb, qn_s[r, :]], axis=0)
        gram = lax.dot_general(kq, jnp.concatenate([kb, kb], axis=0), nt_dims,
                               preferred_element_type=F32)
        kd_t = lax.dot_general(eye2, kdp_s[c], nt_dims, preferred_element_type=F32)

        def finish():
            gram_s[c] = gram
            lhs2_s[c, :2 * c64, :] = kd_t.astype(BF16)

        return finish

    def stage_factor(c):
        r = rows_of(c)
        dec = jnp.exp(jnp.where(tril, gcb_s[r, :] - gcr_s[pl.ds(c, 1), :], NEG_INF))
        lpair = jnp.where(strict, gram_s[c, :c64, :] * dec, 0.0) * betab_s[r, :]
        lm_s[r, :] = lpair
        lhs2_s[c, 2 * c64:, :] = (gram_s[c, c64:, :] * dec).astype(BF16)
        tl_s[0][r, :] = _block_diag_inverse(lpair)

    def stage_merge_products(level):
        def stage(c):
            r = rows_of(c)
            lt = _merge_lower_products(lm_s[r, :], tl_s[level][r, :], ELIM_BLOCK << level)

            def finish():
                lt_s[level][r, :] = lt.astype(BF16)

            return finish
        return stage

    def stage_merge_apply(level):
        def stage(c):
            r = rows_of(c)
            merged = _merge_apply(tl_s[level][r, :], lt_s[level][r, :])

            def finish():
                tl_s[level + 1][r, :] = merged

            return finish
        return stage

    def stage_solve(c):
        r = rows_of(c)
        zero = jnp.zeros((c64, 2 * hd), BF16)
        rhs_bd = jnp.concatenate([jnp.concatenate([rhs_s[0, r, :], zero], axis=1),
                                  jnp.concatenate([zero, rhs_s[1, r, :]], axis=1)], axis=0)
        sol = jnp.dot(tl_s[MERGE_LEVELS][r, :].astype(BF16), rhs_bd,
                      preferred_element_type=F32)

        def finish():
            sol_s[r, :] = sol.astype(BF16)

        return finish

    def stage_fold(c):
        r = rows_of(c)
        sol = sol_s[r, :]
        out = jnp.dot(lhs2_s[c], _lane_block_diag(sol), preferred_element_type=F32)

        def finish():
            for j in heads:
                u_col, w_col = 2 * j * hd, (2 * j + 1) * hd
                n_s[j, c] = out[:2 * c64, u_col:u_col + hd]
                mp_s[j, c, :2 * c64, :] = out[:2 * c64, w_col:w_col + hd].astype(BF16)
                mp_s[j, c, 2 * c64:, :] = (qd_s[j, r, :].astype(F32)
                                           - out[2 * c64:, w_col:w_col + hd]).astype(BF16)
                r_s[j, r, :] = out[2 * c64:, u_col:u_col + hd]

        return finish

    st_s[...] = jnp.zeros_like(st_s)
    nw = nw_ref[...]

    def stage_state(c):
        r = rows_of(c)
        sts = [st_s[j] for j in heads]
        outs = [jnp.dot(mp_s[j, c], sts[j].astype(BF16), preferred_element_type=F32)
                for j in heads]

        def finish():
            for j in heads:
                end = (j + 1) * c64
                gl = gcr_s[pl.ds(c, 1), end - 1:end]
                st_s[j] = sts[j] * jnp.exp(gl) - outs[j][:2 * c64] + n_s[j, c]
                r_s[j, r, :] = r_s[j, r, :] + outs[j][2 * c64:]

        return finish

    def stage_norm(c):
        r = rows_of(c)
        for j in heads:
            o = r_s[j, r, :]
            og = (o * lax.rsqrt(jnp.mean(o * o, axis=-1, keepdims=True) + RMS_EPS)
                  * nw * _silu(z_ref[r, j * hd:(j + 1) * hd]))
            y_ref[r, j * hd:(j + 1) * hd] = og.astype(BF16)

    merges = [stage(lv) for lv in range(MERGE_LEVELS)
              for stage in (stage_merge_products, stage_merge_apply)]
    _software_pipeline([stage_prep, stage_gram, stage_factor] + merges
                       + [stage_solve, stage_fold, stage_state, stage_norm], nchunk)


def _gdn(proj, small, rows, conv_w, a_log, dt_bias, norm_w, bsz, seq):
    nchunk = seq // GDN_CHUNK
    hd = HEAD_DIM
    c64 = GDN_CHUNK
    qo, ko = COL_GDN_Q // hd, COL_GDN_K // hd
    vo, zo = COL_GDN_V // (2 * hd), COL_GDN_Z // (2 * hd)
    cvo = (2 * GDN_QK_W) // (2 * hd)
    smem = pl.BlockSpec(memory_space=pltpu.SMEM)
    pad = (GDN_V_HEADS, N_SMALL - 2 * GDN_V_HEADS)
    gate_params = jnp.stack([jnp.pad(a_log, pad), jnp.pad(dt_bias, pad)])
    return pl.pallas_call(
        functools.partial(_gdn_kernel, nchunk=nchunk),
        name="gdn",
        grid=(bsz, GDN_QK_HEADS),
        in_specs=[smem, smem,
                  pl.BlockSpec((2, N_SMALL), lambda b, h: (0, 0)),
                  pl.BlockSpec((1, hd), lambda b, h: (0, 0)),
                  pl.BlockSpec((GDN_CONV, hd), lambda b, h: (0, h)),
                  pl.BlockSpec((GDN_CONV, hd), lambda b, h: (0, GDN_QK_HEADS + h)),
                  pl.BlockSpec((GDN_CONV, 2 * hd), lambda b, h: (0, cvo + h)),
                  pl.BlockSpec((seq, hd), lambda b, h: (b, qo + h)),
                  pl.BlockSpec((seq, hd), lambda b, h: (b, ko + h)),
                  pl.BlockSpec((seq, 2 * hd), lambda b, h: (b, vo + h)),
                  pl.BlockSpec((seq, 2 * hd), lambda b, h: (b, zo + h)),
                  pl.BlockSpec((seq, N_SMALL), lambda b, h: (b, 0)),
                  pl.BlockSpec((None, None, 4, nchunk, GDN_CHUNK), lambda b, h: (b, h, 0, 0, 0))],
        out_specs=pl.BlockSpec((seq, 2 * hd), lambda b, h: (b, h)),
        out_shape=jax.ShapeDtypeStruct((bsz * seq, GDN_V_W), BF16),
        scratch_shapes=[pltpu.VMEM((seq, hd), BF16),
                        pltpu.VMEM((seq, hd), BF16),
                        pltpu.VMEM((nchunk, 2 * c64, hd), BF16),
                        pltpu.VMEM((2, seq, 2 * hd), BF16),
                        pltpu.VMEM((2, seq, hd), BF16),
                        pltpu.VMEM((seq, 2 * c64), F32),
                        pltpu.VMEM((seq, 2 * c64), F32),
                        pltpu.VMEM((nchunk, 2 * c64), F32),
                        pltpu.VMEM((nchunk, 2 * c64, 2 * c64), F32),
                        pltpu.VMEM((seq, 2 * c64), F32),
                        pltpu.VMEM((nchunk, 3 * c64, hd), BF16),
                        pltpu.VMEM((seq, 4 * hd), BF16),
                        pltpu.VMEM((2, nchunk, 3 * c64, hd), BF16),
                        pltpu.VMEM((2, nchunk, hd, hd), F32),
                        pltpu.VMEM((2, seq, hd), F32),
                        pltpu.VMEM((2, hd, hd), F32)]
                       + [pltpu.VMEM((seq, 2 * c64), F32)] * (MERGE_LEVELS + 1)
                       + [pltpu.VMEM((seq, 2 * c64), BF16)] * MERGE_LEVELS,
        compiler_params=_vmem("gdn"),
    )(a_log, dt_bias, gate_params, norm_w, conv_w, conv_w, conv_w, proj, proj, proj, proj, small, rows)


def _merge_kernel(ya_ref, yb_ref, ga_ref, gb_ref, x_ref, g1_ref, sh2_ref, sc2_ref,
                  lng_ref, lnb_ref, wpm_ref, wpg_ref, wo_ref, x1_ref, h2_ref):
    pa = jnp.dot(ya_ref[...], wpm_ref[...], preferred_element_type=F32)
    pb = jnp.dot(yb_ref[...], wpg_ref[...], preferred_element_type=F32)
    merged = jax.nn.sigmoid(ga_ref[...]) * pa + jax.nn.sigmoid(gb_ref[...]) * pb
    y = jnp.dot(merged.astype(BF16), wo_ref[...], preferred_element_type=F32)
    x1 = _layer_norm(DEEPNORM_ALPHA * x_ref[...] + g1_ref[...] * y, lng_ref[...], lnb_ref[...])
    x1_ref[...] = x1
    h2_ref[...] = (x1 * (1.0 + sc2_ref[...]) + sh2_ref[...]).astype(BF16)


def _merge(ya, yb, proj, x2, mod3, ln_g, ln_b, wpm, wpg, wo, seq):
    t = x2.shape[0]
    tm = MERGE_TM
    per_b = seq // tm
    d = D_MODEL

    def modspec(k):
        return pl.BlockSpec((None, 1, d), lambda i: (i // per_b, 0, k))

    def const(shape):
        return pl.BlockSpec(shape, lambda i: (0, 0), pipeline_mode=pl.Buffered(1))

    return pl.pallas_call(
        _merge_kernel,
        name="merge",
        grid=(t // tm,),
        in_specs=[pl.BlockSpec((tm, MOBA_W), lambda i: (i, 0)),
                  pl.BlockSpec((tm, GDN_V_W), lambda i: (i, 0)),
                  pl.BlockSpec((tm, d), lambda i: (i, COL_GATE_A // d)),
                  pl.BlockSpec((tm, d), lambda i: (i, COL_GATE_B // d)),
                  pl.BlockSpec((tm, d), lambda i: (i, 0)),
                  modspec(2), modspec(3), modspec(4),
                  const((1, d)), const((1, d)),
                  const((MOBA_W, d)), const((GDN_V_W, d)), const((d, d))],
        out_specs=[pl.BlockSpec((tm, d), lambda i: (i, 0)),
                   pl.BlockSpec((tm, d), lambda i: (i, 0))],
        out_shape=[jax.ShapeDtypeStruct((t, d), F32), jax.ShapeDtypeStruct((t, d), BF16)],
        compiler_params=_vmem("merge"),
    )(ya, yb, proj, proj, x2, mod3, mod3, mod3, ln_g, ln_b, wpm, wpg, wo)


def _ffn_kernel(h_ref, x1_ref, g2_ref, lng_ref, lnb_ref, wg_ref, wu_ref, wo_ref, o_ref, acc_ref):
    f = pl.program_id(1)

    @pl.when(f == 0)
    def _():
        acc_ref[...] = jnp.zeros_like(acc_ref)

    h = h_ref[...]
    gate = jnp.dot(h, wg_ref[...], preferred_element_type=F32)
    up = jnp.dot(h, wu_ref[...], preferred_element_type=F32)
    act = (_silu(gate) * up).astype(BF16)
    acc_ref[...] += jnp.dot(act, wo_ref[...], preferred_element_type=F32)

    @pl.when(f == pl.num_programs(1) - 1)
    def _():
        r = DEEPNORM_ALPHA * x1_ref[...] + g2_ref[...] * acc_ref[...]
        o_ref[...] = _layer_norm(r, lng_ref[...], lnb_ref[...])


def _ffn(h2, x1, mod3, ln_g, ln_b, w_in, w_out, seq):
    t = h2.shape[0]
    tm, tf = FFN_TM, FFN_TF
    per_b = seq // tm
    d = D_MODEL
    nf = D_FF // tf
    return pl.pallas_call(
        _ffn_kernel,
        name="ffn",
        grid=(t // tm, nf),
        in_specs=[pl.BlockSpec((tm, d), lambda i, f: (i, 0)),
                  pl.BlockSpec((tm, d), lambda i, f: (i, 0)),
                  pl.BlockSpec((None, 1, d), lambda i, f: (i // per_b, 0, 5)),
                  pl.BlockSpec((1, d), lambda i, f: (0, 0)),
                  pl.BlockSpec((1, d), lambda i, f: (0, 0)),
                  pl.BlockSpec((d, tf), lambda i, f: (0, f)),
                  pl.BlockSpec((d, tf), lambda i, f: (0, nf + f)),
                  pl.BlockSpec((tf, d), lambda i, f: (f, 0))],
        out_specs=pl.BlockSpec((tm, d), lambda i, f: (i, 0)),
        out_shape=jax.ShapeDtypeStruct((t, d), F32),
        scratch_shapes=[pltpu.VMEM((tm, d), F32)],
        compiler_params=_vmem("ffn"),
    )(h2, x1, mod3, ln_g, ln_b, w_in, w_in, w_out)


def _rel_bucket(dist):
    max_exact = REL_BUCKETS // 2
    n = jnp.maximum(dist, 0)
    nf = jnp.maximum(n, 1).astype(F32)
    large = max_exact + (jnp.log(nf / max_exact) / math.log(REL_MAX_DIST / max_exact)
                         * (REL_BUCKETS - max_exact)).astype(jnp.int32)
    large = jnp.minimum(large, REL_BUCKETS - 1)
    return jnp.where(n < max_exact, n, large)


def _layer(x, c, w_ada, b_ada, w_in, conv_w, a_log, dt_bias, gdn_norm_w, rel_bias,
           w_proj_moba, w_proj_gdn, w_out, ln1_g, ln1_b, w_ffn_in, w_ffn_out, ln2_g, ln2_b):
    bsz, seq, d = x.shape
    t = bsz * seq
    x2 = x.reshape(t, d)

    mod = _ada_mod(c, w_ada, b_ada)
    mod3 = mod.reshape(bsz, 1, 6 * d)

    w_all = w_in.astype(BF16)
    n_gates = 2 * GDN_V_HEADS
    w_gate = w_all[:, N_HEAD + n_gates:]
    w_small = jnp.pad(w_all[:, N_HEAD:N_HEAD + n_gates], ((0, 0), (0, N_SMALL - n_gates)))

    proj, small = _in_proj(x2, mod3, w_all, w_gate, w_small, seq)

    ii = jnp.arange(MOBA_BLOCK, dtype=jnp.int32)
    dist = ii[None, :] - ii[:, None]
    bko = _rel_bucket(dist)
    bkp = _rel_bucket(dist + MOBA_BLOCK)
    ya = _moba(proj, rel_bias, bko, bkp, bsz, seq)

    nchunk = seq // GDN_CHUNK
    sm_t = small[:, :2 * GDN_V_HEADS].reshape(bsz, seq, 2, GDN_QK_HEADS, 2)
    rows = sm_t.transpose(0, 3, 2, 4, 1).reshape(bsz, GDN_QK_HEADS, 4, nchunk, GDN_CHUNK)
    yb = _gdn(proj, small, rows, conv_w, a_log, dt_bias, gdn_norm_w.reshape(1, HEAD_DIM),
              bsz, seq)

    x1, h2 = _merge(ya, yb, proj, x2, mod3, ln1_g.reshape(1, d), ln1_b.reshape(1, d),
                    w_proj_moba.astype(BF16), w_proj_gdn.astype(BF16), w_out.astype(BF16), seq)
    out = _ffn(h2, x1, mod3, ln2_g.reshape(1, d), ln2_b.reshape(1, d),
               w_ffn_in.astype(BF16), w_ffn_out.astype(BF16), seq)
    return out.reshape(bsz, seq, d)


def kernel(x, c, w_ada, b_ada, w_in, conv_w, a_log, dt_bias, gdn_norm_w, rel_bias, w_proj_moba,
           w_proj_gdn, w_out, ln1_g, ln1_b, w_ffn_in, w_ffn_out, ln2_g, ln2_b):
    depth = w_ada.shape[0]
    for l in range(depth):
        x = _layer(x, c, w_ada[l], b_ada[l], w_in[l], conv_w[l], a_log[l], dt_bias[l],
                   gdn_norm_w[l], rel_bias, w_proj_moba[l], w_proj_gdn[l], w_out[l],
                   ln1_g[l], ln1_b[l], w_ffn_in[l], w_ffn_out[l], ln2_g[l], ln2_b[l])
    return x
```

```python
import functools
import math

import jax
import jax.numpy as jnp
from jax import lax
from jax.experimental import pallas as pl
from jax.experimental.pallas import tpu as pltpu

F32 = jnp.float32
BF16 = jnp.bfloat16

D_MODEL = 2048
MOBA_HEADS = 8
HEAD_DIM = 128
MOBA_W = MOBA_HEADS * HEAD_DIM
MOBA_BLOCK = 256
MOBA_TOPK = 3
REL_BUCKETS = 32
REL_MAX_DIST = 128
GDN_QK_HEADS = 8
GDN_V_HEADS = 16
GDN_QK_W = GDN_QK_HEADS * HEAD_DIM
GDN_V_W = GDN_V_HEADS * HEAD_DIM
GDN_CONV = 4
GDN_CHUNK = 64
D_FF = 5632
DEEPNORM_ALPHA = 2.0 ** 0.25
LN_EPS = 1e-5
RMS_EPS = 1e-6
NEG_INF = -1e30

COL_GATE_A = 0
COL_GATE_B = 2048
COL_MOBA_Q = 4096
COL_MOBA_K = 5120
COL_MOBA_V = 6144
COL_GDN_Q = 7168
COL_GDN_K = 8192
COL_GDN_V = 9216
COL_GDN_Z = 11264
N_MAIN = 13312
N_GATE = 4096
N_HEAD = N_MAIN - N_GATE
N_SMALL = 128

V7X_VMEM_MIB = 64
SUBLANES = 8
BF16_SUBLANES = 16

ADA_TN = 1024
INPROJ_TM, INPROJ_TN = 1024, 1024
MERGE_TM = 256
FFN_TM, FFN_TF = 512, 512
VMEM_LIMIT_MIB = {"ada_mod": 40, "in_proj": 56, "moba": 48, "gdn": 58, "merge": 56, "ffn": 48}
assert max(VMEM_LIMIT_MIB.values()) < V7X_VMEM_MIB


def _vmem(name):
    return pltpu.CompilerParams(vmem_limit_bytes=VMEM_LIMIT_MIB[name] * 1024 * 1024)


def _silu(x):
    return x * jax.nn.sigmoid(x)


def _layer_norm(r, gain, bias):
    mu = jnp.mean(r, axis=-1, keepdims=True)
    d = r - mu
    var = jnp.mean(d * d, axis=-1, keepdims=True)
    return d * lax.rsqrt(var + LN_EPS) * gain + bias


def _ada_kernel(c_ref, w_ref, b_ref, o_ref):
    sc = _silu(c_ref[...])
    w = w_ref[...]
    sc_hi = sc.astype(BF16)
    sc_lo = (sc - sc_hi.astype(F32)).astype(BF16)
    w_hi = w.astype(BF16)
    w_lo = (w - w_hi.astype(F32)).astype(BF16)
    acc = jnp.dot(sc_hi, w_hi, preferred_element_type=F32)
    acc = acc + jnp.dot(sc_hi, w_lo, preferred_element_type=F32)
    acc = acc + jnp.dot(sc_lo, w_hi, preferred_element_type=F32)
    o_ref[...] = acc + b_ref[...]


def _ada_mod(c, w_ada, b_ada):
    bsz = c.shape[0]
    n = w_ada.shape[1]
    tn = ADA_TN
    return pl.pallas_call(
        _ada_kernel,
        name="ada_mod",
        grid=(n // tn,),
        in_specs=[pl.BlockSpec((bsz, D_MODEL), lambda j: (0, 0)),
                  pl.BlockSpec((D_MODEL, tn), lambda j: (0, j)),
                  pl.BlockSpec((1, tn), lambda j: (0, j))],
        out_specs=pl.BlockSpec((bsz, tn), lambda j: (0, j)),
        out_shape=jax.ShapeDtypeStruct((bsz, n), F32),
        compiler_params=_vmem("ada_mod"),
    )(c, w_ada, b_ada.reshape(1, n))


def _serpentine(i, j, n):
    return jnp.where(i % 2 == 0, j, n - 1 - j)


def _inproj_kernel(x_ref, sh_ref, sc_ref, w_ref, wg_ref, ws_ref, o_ref, os_ref, h_ref, *, n_head,
                   n_tiles):
    @pl.when(pl.program_id(1) == 0)
    def _():
        h = x_ref[...] * (1.0 + sc_ref[...]) + sh_ref[...]
        h_ref[...] = h.astype(BF16)
        os_ref[...] = jnp.dot(h_ref[...], ws_ref[...], preferred_element_type=F32)

    col = _serpentine(pl.program_id(0), pl.program_id(1), n_tiles)

    @pl.when(col < n_head)
    def _():
        o_ref[...] = jnp.dot(h_ref[...], w_ref[...], preferred_element_type=F32)

    @pl.when(col >= n_head)
    def _():
        o_ref[...] = jnp.dot(h_ref[...], wg_ref[...], preferred_element_type=F32)


def _in_proj(x2, mod3, w_all, w_gate, w_small, seq):
    t = x2.shape[0]
    tm, tn = INPROJ_TM, INPROJ_TN
    per_b = seq // tm
    n_head, n_gate, n_tiles = N_HEAD // tn, N_GATE // tn, N_MAIN // tn

    def col(i, j):
        return _serpentine(i, j, n_tiles)

    return pl.pallas_call(
        functools.partial(_inproj_kernel, n_head=n_head, n_tiles=n_tiles),
        name="in_proj",
        grid=(t // tm, n_tiles),
        in_specs=[pl.BlockSpec((tm, D_MODEL), lambda i, j: (i, 0)),
                  pl.BlockSpec((None, 1, D_MODEL), lambda i, j: (i // per_b, 0, 0)),
                  pl.BlockSpec((None, 1, D_MODEL), lambda i, j: (i // per_b, 0, 1)),
                  pl.BlockSpec((D_MODEL, tn), lambda i, j: (0, jnp.minimum(col(i, j), n_head - 1))),
                  pl.BlockSpec((D_MODEL, tn), lambda i, j: (0, jnp.maximum(col(i, j) - n_head, 0))),
                  pl.BlockSpec((D_MODEL, N_SMALL), lambda i, j: (0, 0))],
        out_specs=[pl.BlockSpec((tm, tn), lambda i, j: (i, (col(i, j) + n_gate) % n_tiles)),
                   pl.BlockSpec((tm, N_SMALL), lambda i, j: (i, 0))],
        out_shape=[jax.ShapeDtypeStruct((t, N_MAIN), F32),
                   jax.ShapeDtypeStruct((t, N_SMALL), F32)],
        scratch_shapes=[pltpu.VMEM((tm, D_MODEL), BF16)],
        compiler_params=_vmem("in_proj"),
    )(x2, mod3, mod3, w_all, w_gate, w_small)


def _moba_kernel(rel_ref, bko_ref, bkp_ref, q_ref, k_ref, v_ref, o_ref,
                 bias_own, bias_prev, kb_ref, vt_ref, *, nb):
    h = pl.program_id(0)
    blk = MOBA_BLOCK
    inv_scale = HEAD_DIM ** 0.5
    scale_log2e = HEAD_DIM ** -0.5 * math.log2(math.e)

    @pl.when(pl.program_id(1) == 0)
    def _():
        bo = bko_ref[...]
        bp = bkp_ref[...]
        acc_o = jnp.zeros((blk, blk), F32)
        acc_p = jnp.zeros((blk, blk), F32)
        for kk in range(REL_BUCKETS):
            val = rel_ref[kk, h] * inv_scale
            acc_o = jnp.where(bo == kk, val, acc_o)
            acc_p = jnp.where(bp == kk, val, acc_p)
        bias_own[...] = acc_o
        bias_prev[...] = acc_p

    bias_far = rel_ref[REL_BUCKETS - 1, h] * inv_scale
    kf = k_ref[...]
    kmean = jnp.mean(kf.reshape(nb, blk, HEAD_DIM), axis=1)
    kb_ref[...] = kf.astype(BF16)
    nt_dims = (((1,), (1,)), ((), ()))
    eye = (lax.broadcasted_iota(jnp.int32, (HEAD_DIM, HEAD_DIM), 0)
           == lax.broadcasted_iota(jnp.int32, (HEAD_DIM, HEAD_DIM), 1)).astype(BF16)
    vt_ref[:HEAD_DIM, :] = lax.dot_general(eye, v_ref[...].astype(BF16), nt_dims,
                                           preferred_element_type=F32).astype(BF16)
    pad_rows = vt_ref.shape[0] - HEAD_DIM
    vt_ref[HEAD_DIM:, :] = (lax.broadcasted_iota(jnp.int32, (pad_rows, vt_ref.shape[1]), 0)
                            == 0).astype(BF16)
    causal = (lax.broadcasted_iota(jnp.int32, (blk, blk), 0)
              <= lax.broadcasted_iota(jnp.int32, (blk, blk), 1))

    def scores(i):
        qi = q_ref[i * blk:(i + 1) * blk, :]
        qb = qi.astype(BF16)
        sel = None
        if i > MOBA_TOPK:
            route = lax.dot_general(kmean, qi, nt_dims, precision=lax.Precision.HIGHEST,
                                    preferred_element_type=F32)
            rc = [route[n:n + 1, :] for n in range(i)]
            sel = []
            for n in range(i):
                rank = jnp.zeros((1, blk), jnp.int32)
                for m in range(i):
                    if m == n:
                        continue
                    beats = (rc[m] >= rc[n]) if m < n else (rc[m] > rc[n])
                    rank = rank + beats.astype(jnp.int32)
                sel.append(rank < MOBA_TOPK)
        t_list = []
        for n in range(i + 1):
            t = lax.dot_general(kb_ref[n * blk:(n + 1) * blk, :], qb, nt_dims,
                                preferred_element_type=F32)
            if n == i:
                t = jnp.where(causal, t + bias_own[...], NEG_INF)
            else:
                if n == i - 1:
                    t = t + bias_prev[...]
                if sel is not None:
                    t = jnp.where(sel[n], t, NEG_INF)
            t_list.append(t)
        return t_list

    t_next = scores(0)
    for i in range(nb):
        t_list = t_next
        if i + 1 < nb:
            t_next = scores(i + 1)
        n_far = max(i - 1, 0)
        m_run = jnp.max(t_list[n_far], axis=0, keepdims=True)
        for t in t_list[n_far + 1:]:
            m_run = jnp.maximum(m_run, jnp.max(t, axis=0, keepdims=True))
        if n_far:
            m_far = jnp.max(t_list[0], axis=0, keepdims=True)
            for t in t_list[1:n_far]:
                m_far = jnp.maximum(m_far, jnp.max(t, axis=0, keepdims=True))
            m_run = jnp.maximum(m_run, m_far + bias_far)
        acc = jnp.zeros((vt_ref.shape[0], blk), F32)
        for n, t in enumerate(t_list):
            offset = m_run - bias_far if n < n_far else m_run
            p = jnp.exp2((t - offset) * scale_log2e)
            acc = acc + jnp.dot(vt_ref[:, n * blk:(n + 1) * blk], p.astype(BF16),
                                preferred_element_type=F32)
        out = acc[:HEAD_DIM] / acc[HEAD_DIM:HEAD_DIM + 1]
        o_ref[i * blk:(i + 1) * blk, :] = out.T.astype(BF16)


def _moba(proj, rel_bias, bko, bkp, bsz, seq):
    nb = seq // MOBA_BLOCK
    qo, ko, vo = COL_MOBA_Q // HEAD_DIM, COL_MOBA_K // HEAD_DIM, COL_MOBA_V // HEAD_DIM
    blk = MOBA_BLOCK
    return pl.pallas_call(
        functools.partial(_moba_kernel, nb=nb),
        name="moba",
        grid=(MOBA_HEADS, bsz),
        in_specs=[pl.BlockSpec(memory_space=pltpu.SMEM),
                  pl.BlockSpec((blk, blk), lambda h, b: (0, 0)),
                  pl.BlockSpec((blk, blk), lambda h, b: (0, 0)),
                  pl.BlockSpec((seq, HEAD_DIM), lambda h, b: (b, qo + h)),
                  pl.BlockSpec((seq, HEAD_DIM), lambda h, b: (b, ko + h)),
                  pl.BlockSpec((seq, HEAD_DIM), lambda h, b: (b, vo + h))],
        out_specs=pl.BlockSpec((seq, HEAD_DIM), lambda h, b: (b, h)),
        out_shape=jax.ShapeDtypeStruct((bsz * seq, MOBA_W), BF16),
        scratch_shapes=[pltpu.VMEM((blk, blk), F32), pltpu.VMEM((blk, blk), F32),
                        pltpu.VMEM((seq, HEAD_DIM), BF16),
                        pltpu.VMEM((HEAD_DIM + BF16_SUBLANES, seq), BF16)],
        compiler_params=_vmem("moba"),
    )(rel_bias, bko, bkp, proj, proj, proj)


def _conv_silu(xh, w):
    assert GDN_CONV == 4
    x1 = pltpu.roll(xh, 1, axis=0)
    u = xh * w[1:2, :] + x1 * w[0:1, :]
    acc = (xh * w[3:4, :] + x1 * w[2:3, :] + pltpu.roll(u, 2, axis=0))[SUBLANES:, :]
    return _silu(acc)


def _l2norm(x):
    return x * lax.rsqrt(jnp.sum(x * x, axis=-1, keepdims=True) + RMS_EPS)


def _softplus(x):
    return jnp.maximum(x, 0.0) + jnp.log1p(jnp.exp(-jnp.abs(x)))


ELIM_BLOCK = 8
MERGE_LEVELS = (GDN_CHUNK // ELIM_BLOCK).bit_length() - 1


def _block_diag_inverse(lpair):
    n = lpair.shape[0]
    nv = n // SUBLANES
    vpb = ELIM_BLOCK // SUBLANES
    rid = lax.broadcasted_iota(jnp.int32, (SUBLANES, 2 * n), 0)
    lid = lax.broadcasted_iota(jnp.int32, (SUBLANES, 2 * n), 1)
    t_rows = [(lid % n == rid + v * SUBLANES).astype(F32) for v in range(nv)]
    l_rows = [lpair[v * SUBLANES:(v + 1) * SUBLANES, :] for v in range(nv)]
    for m in range(n - 1):
        v0, s0 = divmod(m, SUBLANES)
        v_end = (v0 // vpb + 1) * vpb
        row = t_rows[v0][s0:s0 + 1, :]
        col = (lid // n) * n + m
        for v in range(v0 if s0 < SUBLANES - 1 else v0 + 1, v_end):
            t_rows[v] = t_rows[v] - jnp.take_along_axis(l_rows[v], col, axis=1) * row
    return jnp.concatenate(t_rows, axis=0)


def _lane_block_diag(pair):
    first = lax.broadcasted_iota(jnp.int32, pair.shape, 1) < pair.shape[1] // 2
    zero = jnp.zeros_like(pair)
    return jnp.concatenate([jnp.where(first, pair, zero), jnp.where(first, zero, pair)], axis=0)


def _merge_lower_products(lpair, tpair, k):
    n = lpair.shape[0]
    ri = lax.broadcasted_iota(jnp.int32, lpair.shape, 0)
    ci = lax.broadcasted_iota(jnp.int32, lpair.shape, 1) % n
    off = (ri // (2 * k) == ci // (2 * k)) & (ri // k > ci // k)
    lk = jnp.where(off, lpair, 0.0).astype(BF16)
    return jnp.dot(lk, _lane_block_diag(tpair.astype(BF16)), preferred_element_type=F32)


def _merge_apply(tpair, lt):
    return tpair - jnp.dot(tpair.astype(BF16), _lane_block_diag(lt), preferred_element_type=F32)


PIPELINE_UNROLL = 4


def _software_pipeline(stages, n):
    ns = len(stages)

    def run(it, lo, hi):
        conts = [stages[s](it - s) for s in reversed(range(lo, hi))]
        for cont in conts:
            if cont is not None:
                cont()

    for it in range(ns - 1):
        run(it, 0, it + 1)

    steady = n - (ns - 1)
    assert steady % PIPELINE_UNROLL == 0, (n, ns)

    def body(k, carry):
        first = ns - 1 + k * PIPELINE_UNROLL
        for u in range(PIPELINE_UNROLL):
            run(first + u, 0, ns)
        return carry

    lax.fori_loop(0, steady // PIPELINE_UNROLL, body, 0)
    for it in range(n, n + ns - 1):
        run(it, it - n + 1, ns)


def _lane_pick(x, lane, idx):
    return jnp.sum(jnp.where(lane == idx, x, 0.0), axis=-1, keepdims=True)


def _gdn_kernel(alog_ref, dtb_ref, gp_ref, nw_ref, cwq_ref, cwk_ref, cwv_ref,
                q_ref, k_ref, v_ref, z_ref, sm_ref, rw_ref, y_ref,
                kb_s, qn_s, kdp_s, rhs_s, qd_s, gcb_s, betab_s, gcr_s, gram_s, lm_s, lhs2_s,
                sol_s, mp_s, n_s, r_s, st_s, *merge_s, nchunk):
    tl_s, lt_s = merge_s[:MERGE_LEVELS + 1], merge_s[MERGE_LEVELS + 1:]
    hq = pl.program_id(1)
    c64 = GDN_CHUNK
    hd = HEAD_DIM
    heads = range(2)

    tri_u = (lax.broadcasted_iota(jnp.int32, (c64, c64), 0)
             <= lax.broadcasted_iota(jnp.int32, (c64, c64), 1)).astype(F32)
    gc_rows = []
    for j in heads:
        hv = 2 * hq + j
        a_neg_r = -jnp.exp(jnp.full((nchunk, c64), alog_ref[hv], F32))
        g_row = a_neg_r * _softplus(rw_ref[2 + j] + dtb_ref[hv])
        gc_rows.append(jnp.dot(g_row, tri_u, precision=lax.Precision.HIGHEST,
                               preferred_element_type=F32))
    gcr_s[...] = jnp.concatenate(gc_rows, axis=1)

    rows = lax.broadcasted_iota(jnp.int32, (c64, 2 * c64), 0)
    cols = lax.broadcasted_iota(jnp.int32, (c64, 2 * c64), 1) % c64
    tril = rows >= cols
    strict = rows > cols
    eye2 = (lax.broadcasted_iota(jnp.int32, (2 * c64, 2 * c64), 0)
            == lax.broadcasted_iota(jnp.int32, (2 * c64, 2 * c64), 1)).astype(BF16)
    nt_dims = (((1,), (1,)), ((), ()))

    def rows_of(c):
        start = c * c64
        return pl.ds(start if isinstance(c, int) else pl.multiple_of(start, c64), c64)

    def with_halo(x_ref, c):
        if isinstance(c, int) and c == 0:
            return jnp.concatenate([jnp.zeros((SUBLANES, x_ref.shape[1]), F32), x_ref[:c64, :]],
                                   axis=0)
        start = c * c64 - SUBLANES
        if not isinstance(c, int):
            start = pl.multiple_of(start, SUBLANES)
        return x_ref[pl.ds(start, c64 + SUBLANES), :]

    lane = lax.broadcasted_iota(jnp.int32, (c64, N_SMALL), 1)
    first = lane < c64
    pos = lax.broadcasted_iota(jnp.int32, (c64, N_SMALL), 0)

    def stage_prep(c):
        r = rows_of(c)
        qn = _l2norm(_conv_silu(with_halo(q_ref, c), cwq_ref[...])) * (hd ** -0.5)
        kn = _l2norm(_conv_silu(with_halo(k_ref, c), cwk_ref[...]))
        vc = _conv_silu(with_halo(v_ref, c), cwv_ref[...])
        qn_s[r, :] = qn.astype(BF16)
        kb_s[r, :] = kn.astype(BF16)
        sm = sm_ref[r, :]
        sig_all = jax.nn.sigmoid(sm)
        gc_all = -jnp.exp(gp_ref[0:1, :]) * _softplus(sm + gp_ref[1:2, :])
        sft = 1
        while sft < c64:
            gc_all = gc_all + jnp.where(pos >= sft, pltpu.roll(gc_all, sft, axis=0), 0.0)
            sft *= 2
        rest_all = gc_all[c64 - 1:, :] - gc_all
        betas, gc_cols = [], []
        for j in heads:
            hv = 2 * hq + j
            beta = _lane_pick(sig_all, lane, hv)
            gc_col = _lane_pick(gc_all, lane, GDN_V_HEADS + hv)
            rest_col = _lane_pick(rest_all, lane, GDN_V_HEADS + hv)
            eg = jnp.exp(gc_col)
            rhs_s[j, r, :hd] = (vc[:, j * hd:(j + 1) * hd] * beta).astype(BF16)
            rhs_s[j, r, hd:] = (kn * (beta * eg)).astype(BF16)
            qd_s[j, r, :] = (qn * eg).astype(BF16)
            kdp_s[c, j * c64:(j + 1) * c64, :] = (kn * jnp.exp(rest_col)).astype(BF16)
            betas.append(beta)
            gc_cols.append(gc_col)
        gcb_s[r, :] = jnp.where(first, gc_cols[0], gc_cols[1])
        betab_s[r, :] = jnp.where(first, betas[0], betas[1])

    def stage_gram(c):
        r = rows_of(c)
        kb = kb_s[r, :]
        kq = jnp.concatenate([kb, qn_s[r, :]], axis=0)
        gram = lax.dot_general(kq, jnp.concatenate([kb, kb], axis=0), nt_dims,
                               preferred_element_type=F32)
        kd_t = lax.dot_general(eye2, kdp_s[c], nt_dims, preferred_element_type=F32)

        def finish():
            gram_s[c] = gram
            lhs2_s[c, :2 * c64, :] = kd_t.astype(BF16)

        return finish

    def stage_factor(c):
        r = rows_of(c)
        dec = jnp.exp(jnp.where(tril, gcb_s[r, :] - gcr_s[pl.ds(c, 1), :], NEG_INF))
        lpair = jnp.where(strict, gram_s[c, :c64, :] * dec, 0.0) * betab_s[r, :]
        lm_s[r, :] = lpair
        lhs2_s[c, 2 * c64:, :] = (gram_s[c, c64:, :] * dec).astype(BF16)
        tl_s[0][r, :] = _block_diag_inverse(lpair)

    def stage_merge_products(level):
        def stage(c):
            r = rows_of(c)
            lt = _merge_lower_products(lm_s[r, :], tl_s[level][r, :], ELIM_BLOCK << level)

            def finish():
                lt_s[level][r, :] = lt.astype(BF16)

            return finish
        return stage

    def stage_merge_apply(level):
        def stage(c):
            r = rows_of(c)
            merged = _merge_apply(tl_s[level][r, :], lt_s[level][r, :])

            def finish():
                tl_s[level + 1][r, :] = merged

            return finish
        return stage

    def stage_solve(c):
        r = rows_of(c)
        zero = jnp.zeros((c64, 2 * hd), BF16)
        rhs_bd = jnp.concatenate([jnp.concatenate([rhs_s[0, r, :], zero], axis=1),
                                  jnp.concatenate([zero, rhs_s[1, r, :]], axis=1)], axis=0)
        sol = jnp.dot(tl_s[MERGE_LEVELS][r, :].astype(BF16), rhs_bd,
                      preferred_element_type=F32)

        def finish():
            sol_s[r, :] = sol.astype(BF16)

        return finish

    def stage_fold(c):
        r = rows_of(c)
        sol = sol_s[r, :]
        out = jnp.dot(lhs2_s[c], _lane_block_diag(sol), preferred_element_type=F32)

        def finish():
            for j in heads:
                u_col, w_col = 2 * j * hd, (2 * j + 1) * hd
                n_s[j, c] = out[:2 * c64, u_col:u_col + hd]
                mp_s[j, c, :2 * c64, :] = out[:2 * c64, w_col:w_col + hd].astype(BF16)
                mp_s[j, c, 2 * c64:, :] = (qd_s[j, r, :].astype(F32)
                                           - out[2 * c64:, w_col:w_col + hd]).astype(BF16)
                r_s[j, r, :] = out[2 * c64:, u_col:u_col + hd]

        return finish

    st_s[...] = jnp.zeros_like(st_s)
    nw = nw_ref[...]

    def stage_state(c):
        r = rows_of(c)
        sts = [st_s[j] for j in heads]
        outs = [jnp.dot(mp_s[j, c], sts[j].astype(BF16), preferred_element_type=F32)
                for j in heads]

        def finish():
            for j in heads:
                end = (j + 1) * c64
                gl = gcr_s[pl.ds(c, 1), end - 1:end]
                st_s[j] = sts[j] * jnp.exp(gl) - outs[j][:2 * c64] + n_s[j, c]
                r_s[j, r, :] = r_s[j, r, :] + outs[j][2 * c64:]

        return finish

    def stage_norm(c):
        r = rows_of(c)
        for j in heads:
            o = r_s[j, r, :]
            og = (o * lax.rsqrt(jnp.mean(o * o, axis=-1, keepdims=True) + RMS_EPS)
                  * nw * _silu(z_ref[r, j * hd:(j + 1) * hd]))
            y_ref[r, j * hd:(j + 1) * hd] = og.astype(BF16)

    merges = [stage(lv) for lv in range(MERGE_LEVELS)
              for stage in (stage_merge_products, stage_merge_apply)]
    _software_pipeline([stage_prep, stage_gram, stage_factor] + merges
                       + [stage_solve, stage_fold, stage_state, stage_norm], nchunk)


def _gdn(proj, small, rows, conv_w, a_log, dt_bias, norm_w, bsz, seq):
    nchunk = seq // GDN_CHUNK
    hd = HEAD_DIM
    c64 = GDN_CHUNK
    qo, ko = COL_GDN_Q // hd, COL_GDN_K // hd
    vo, zo = COL_GDN_V // (2 * hd), COL_GDN_Z // (2 * hd)
    cvo = (2 * GDN_QK_W) // (2 * hd)
    smem = pl.BlockSpec(memory_space=pltpu.SMEM)
    pad = (GDN_V_HEADS, N_SMALL - 2 * GDN_V_HEADS)
    gate_params = jnp.stack([jnp.pad(a_log, pad), jnp.pad(dt_bias, pad)])
    return pl.pallas_call(
        functools.partial(_gdn_kernel, nchunk=nchunk),
        name="gdn",
        grid=(bsz, GDN_QK_HEADS),
        in_specs=[smem, smem,
                  pl.BlockSpec((2, N_SMALL), lambda b, h: (0, 0)),
                  pl.BlockSpec((1, hd), lambda b, h: (0, 0)),
                  pl.BlockSpec((GDN_CONV, hd), lambda b, h: (0, h)),
                  pl.BlockSpec((GDN_CONV, hd), lambda b, h: (0, GDN_QK_HEADS + h)),
                  pl.BlockSpec((GDN_CONV, 2 * hd), lambda b, h: (0, cvo + h)),
                  pl.BlockSpec((seq, hd), lambda b, h: (b, qo + h)),
                  pl.BlockSpec((seq, hd), lambda b, h: (b, ko + h)),
                  pl.BlockSpec((seq, 2 * hd), lambda b, h: (b, vo + h)),
                  pl.BlockSpec((seq, 2 * hd), lambda b, h: (b, zo + h)),
                  pl.BlockSpec((seq, N_SMALL), lambda b, h: (b, 0)),
                  pl.BlockSpec((None, None, 4, nchunk, GDN_CHUNK), lambda b, h: (b, h, 0, 0, 0))],
        out_specs=pl.BlockSpec((seq, 2 * hd), lambda b, h: (b, h)),
        out_shape=jax.ShapeDtypeStruct((bsz * seq, GDN_V_W), BF16),
        scratch_shapes=[pltpu.VMEM((seq, hd), BF16),
                        pltpu.VMEM((seq, hd), BF16),
                        pltpu.VMEM((nchunk, 2 * c64, hd), BF16),
                        pltpu.VMEM((2, seq, 2 * hd), BF16),
                        pltpu.VMEM((2, seq, hd), BF16),
                        pltpu.VMEM((seq, 2 * c64), F32),
                        pltpu.VMEM((seq, 2 * c64), F32),
                        pltpu.VMEM((nchunk, 2 * c64), F32),
                        pltpu.VMEM((nchunk, 2 * c64, 2 * c64), F32),
                        pltpu.VMEM((seq, 2 * c64), F32),
                        pltpu.VMEM((nchunk, 3 * c64, hd), BF16),
                        pltpu.VMEM((seq, 4 * hd), BF16),
                        pltpu.VMEM((2, nchunk, 3 * c64, hd), BF16),
                        pltpu.VMEM((2, nchunk, hd, hd), F32),
                        pltpu.VMEM((2, seq, hd), F32),
                        pltpu.VMEM((2, hd, hd), F32)]
                       + [pltpu.VMEM((seq, 2 * c64), F32)] * (MERGE_LEVELS + 1)
                       + [pltpu.VMEM((seq, 2 * c64), BF16)] * MERGE_LEVELS,
        compiler_params=_vmem("gdn"),
    )(a_log, dt_bias, gate_params, norm_w, conv_w, conv_w, conv_w, proj, proj, proj, proj, small, rows)


def _merge_kernel(ya_ref, yb_ref, ga_ref, gb_ref, x_ref, g1_ref, sh2_ref, sc2_ref,
                  lng_ref, lnb_ref, wpm_ref, wpg_ref, wo_ref, x1_ref, h2_ref):
    pa = jnp.dot(ya_ref[...], wpm_ref[...], preferred_element_type=F32)
    pb = jnp.dot(yb_ref[...], wpg_ref[...], preferred_element_type=F32)
    merged = jax.nn.sigmoid(ga_ref[...]) * pa + jax.nn.sigmoid(gb_ref[...]) * pb
    y = jnp.dot(merged.astype(BF16), wo_ref[...], preferred_element_type=F32)
    x1 = _layer_norm(DEEPNORM_ALPHA * x_ref[...] + g1_ref[...] * y, lng_ref[...], lnb_ref[...])
    x1_ref[...] = x1
    h2_ref[...] = (x1 * (1.0 + sc2_ref[...]) + sh2_ref[...]).astype(BF16)


def _merge(ya, yb, proj, x2, mod3, ln_g, ln_b, wpm, wpg, wo, seq):
    t = x2.shape[0]
    tm = MERGE_TM
    per_b = seq // tm
    d = D_MODEL

    def modspec(k):
        return pl.BlockSpec((None, 1, d), lambda i: (i // per_b, 0, k))

    def const(shape):
        return pl.BlockSpec(shape, lambda i: (0, 0), pipeline_mode=pl.Buffered(1))

    return pl.pallas_call(
        _merge_kernel,
        name="merge",
        grid=(t // tm,),
        in_specs=[pl.BlockSpec((tm, MOBA_W), lambda i: (i, 0)),
                  pl.BlockSpec((tm, GDN_V_W), lambda i: (i, 0)),
                  pl.BlockSpec((tm, d), lambda i: (i, COL_GATE_A // d)),
                  pl.BlockSpec((tm, d), lambda i: (i, COL_GATE_B // d)),
                  pl.BlockSpec((tm, d), lambda i: (i, 0)),
                  modspec(2), modspec(3), modspec(4),
                  const((1, d)), const((1, d)),
                  const((MOBA_W, d)), const((GDN_V_W, d)), const((d, d))],
        out_specs=[pl.BlockSpec((tm, d), lambda i: (i, 0)),
                   pl.BlockSpec((tm, d), lambda i: (i, 0))],
        out_shape=[jax.ShapeDtypeStruct((t, d), F32), jax.ShapeDtypeStruct((t, d), BF16)],
        compiler_params=_vmem("merge"),
    )(ya, yb, proj, proj, x2, mod3, mod3, mod3, ln_g, ln_b, wpm, wpg, wo)


def _ffn_kernel(h_ref, x1_ref, g2_ref, lng_ref, lnb_ref, wg_ref, wu_ref, wo_ref, o_ref, acc_ref):
    f = pl.program_id(1)

    @pl.when(f == 0)
    def _():
        acc_ref[...] = jnp.zeros_like(acc_ref)

    h = h_ref[...]
    gate = jnp.dot(h, wg_ref[...], preferred_element_type=F32)
    up = jnp.dot(h, wu_ref[...], preferred_element_type=F32)
    act = (_silu(gate) * up).astype(BF16)
    acc_ref[...] += jnp.dot(act, wo_ref[...], preferred_element_type=F32)

    @pl.when(f == pl.num_programs(1) - 1)
    def _():
        r = DEEPNORM_ALPHA * x1_ref[...] + g2_ref[...] * acc_ref[...]
        o_ref[...] = _layer_norm(r, lng_ref[...], lnb_ref[...])


def _ffn(h2, x1, mod3, ln_g, ln_b, w_in, w_out, seq):
    t = h2.shape[0]
    tm, tf = FFN_TM, FFN_TF
    per_b = seq // tm
    d = D_MODEL
    nf = D_FF // tf
    return pl.pallas_call(
        _ffn_kernel,
        name="ffn",
        grid=(t // tm, nf),
        in_specs=[pl.BlockSpec((tm, d), lambda i, f: (i, 0)),
                  pl.BlockSpec((tm, d), lambda i, f: (i, 0)),
                  pl.BlockSpec((None, 1, d), lambda i, f: (i // per_b, 0, 5)),
                  pl.BlockSpec((1, d), lambda i, f: (0, 0)),
                  pl.BlockSpec((1, d), lambda i, f: (0, 0)),
                  pl.BlockSpec((d, tf), lambda i, f: (0, f)),
                  pl.BlockSpec((d, tf), lambda i, f: (0, nf + f)),
                  pl.BlockSpec((tf, d), lambda i, f: (f, 0))],
        out_specs=pl.BlockSpec((tm, d), lambda i, f: (i, 0)),
        out_shape=jax.ShapeDtypeStruct((t, d), F32),
        scratch_shapes=[pltpu.VMEM((tm, d), F32)],
        compiler_params=_vmem("ffn"),
    )(h2, x1, mod3, ln_g, ln_b, w_in, w_in, w_out)


def _rel_bucket(dist):
    max_exact = REL_BUCKETS // 2
    n = jnp.maximum(dist, 0)
    nf = jnp.maximum(n, 1).astype(F32)
    large = max_exact + (jnp.log(nf / max_exact) / math.log(REL_MAX_DIST / max_exact)
                         * (REL_BUCKETS - max_exact)).astype(jnp.int32)
    large = jnp.minimum(large, REL_BUCKETS - 1)
    return jnp.where(n < max_exact, n, large)


def _layer(x, c, w_ada, b_ada, w_in, conv_w, a_log, dt_bias, gdn_norm_w, rel_bias,
           w_proj_moba, w_proj_gdn, w_out, ln1_g, ln1_b, w_ffn_in, w_ffn_out, ln2_g, ln2_b):
    bsz, seq, d = x.shape
    t = bsz * seq
    x2 = x.reshape(t, d)

    mod = _ada_mod(c, w_ada, b_ada)
    mod3 = mod.reshape(bsz, 1, 6 * d)

    w_all = w_in.astype(BF16)
    n_gates = 2 * GDN_V_HEADS
    w_gate = w_all[:, N_HEAD + n_gates:]
    w_small = jnp.pad(w_all[:, N_HEAD:N_HEAD + n_gates], ((0, 0), (0, N_SMALL - n_gates)))

    proj, small = _in_proj(x2, mod3, w_all, w_gate, w_small, seq)

    ii = jnp.arange(MOBA_BLOCK, dtype=jnp.int32)
    dist = ii[None, :] - ii[:, None]
    bko = _rel_bucket(dist)
    bkp = _rel_bucket(dist + MOBA_BLOCK)
    ya = _moba(proj, rel_bias, bko, bkp, bsz, seq)

    nchunk = seq // GDN_CHUNK
    sm_t = small[:, :2 * GDN_V_HEADS].reshape(bsz, seq, 2, GDN_QK_HEADS, 2)
    rows = sm_t.transpose(0, 3, 2, 4, 1).reshape(bsz, GDN_QK_HEADS, 4, nchunk, GDN_CHUNK)
    yb = _gdn(proj, small, rows, conv_w, a_log, dt_bias, gdn_norm_w.reshape(1, HEAD_DIM),
              bsz, seq)

    x1, h2 = _merge(ya, yb, proj, x2, mod3, ln1_g.reshape(1, d), ln1_b.reshape(1, d),
                    w_proj_moba.astype(BF16), w_proj_gdn.astype(BF16), w_out.astype(BF16), seq)
    out = _ffn(h2, x1, mod3, ln2_g.reshape(1, d), ln2_b.reshape(1, d),
               w_ffn_in.astype(BF16), w_ffn_out.astype(BF16), seq)
    return out.reshape(bsz, seq, d)


def kernel(x, c, w_ada, b_ada, w_in, conv_w, a_log, dt_bias, gdn_norm_w, rel_bias, w_proj_moba,
           w_proj_gdn, w_out, ln1_g, ln1_b, w_ffn_in, w_ffn_out, ln2_g, ln2_b):
    depth = w_ada.shape[0]
    for l in range(depth):
        x = _layer(x, c, w_ada[l], b_ada[l], w_in[l], conv_w[l], a_log[l], dt_bias[l],
                   gdn_norm_w[l], rel_bias, w_proj_moba[l], w_proj_gdn[l], w_out[l],
                   ln1_g[l], ln1_b[l], w_ffn_in[l], w_ffn_out[l], ln2_g[l], ln2_b[l])
    return x
```

```python
import functools
import math

import jax
import jax.numpy as jnp
from jax import lax
from jax.experimental import pallas as pl
from jax.experimental.pallas import tpu as pltpu

F32 = jnp.float32
BF16 = jnp.bfloat16

D_MODEL = 2048
MOBA_HEADS = 8
HEAD_DIM = 128
MOBA_W = MOBA_HEADS * HEAD_DIM
MOBA_BLOCK = 256
MOBA_TOPK = 3
REL_BUCKETS = 32
REL_MAX_DIST = 128
GDN_QK_HEADS = 8
GDN_V_HEADS = 16
GDN_QK_W = GDN_QK_HEADS * HEAD_DIM
GDN_V_W = GDN_V_HEADS * HEAD_DIM
GDN_CONV = 4
GDN_CHUNK = 64
D_FF = 5632
DEEPNORM_ALPHA = 2.0 ** 0.25
LN_EPS = 1e-5
RMS_EPS = 1e-6
NEG_INF = -1e30

COL_GATE_A = 0
COL_GATE_B = 2048
COL_MOBA_Q = 4096
COL_MOBA_K = 5120
COL_MOBA_V = 6144
COL_GDN_Q = 7168
COL_GDN_K = 8192
COL_GDN_V = 9216
COL_GDN_Z = 11264
N_MAIN = 13312
N_GATE = 4096
N_HEAD = N_MAIN - N_GATE
N_SMALL = 128

V7X_VMEM_MIB = 64
SUBLANES = 8
BF16_SUBLANES = 16

ADA_TN = 1024
INPROJ_TM, INPROJ_TN = 1024, 1024
MERGE_TM = 256
FFN_TM, FFN_TF = 512, 512
VMEM_LIMIT_MIB = {"ada_mod": 40, "in_proj": 56, "moba": 48, "gdn": 58, "merge": 56, "ffn": 48}
assert max(VMEM_LIMIT_MIB.values()) < V7X_VMEM_MIB


def _vmem(name):
    return pltpu.CompilerParams(vmem_limit_bytes=VMEM_LIMIT_MIB[name] * 1024 * 1024)


def _silu(x):
    h = 0.5 * x
    return h + h * jnp.tanh(h)


def _layer_norm(r, gain, bias):
    mu = jnp.mean(r, axis=-1, keepdims=True)
    d = r - mu
    var = jnp.mean(d * d, axis=-1, keepdims=True)
    return d * lax.rsqrt(var + LN_EPS) * gain + bias


def _ada_kernel(c_ref, w_ref, b_ref, o_ref):
    sc = _silu(c_ref[...])
    w = w_ref[...]
    sc_hi = sc.astype(BF16)
    sc_lo = (sc - sc_hi.astype(F32)).astype(BF16)
    w_hi = w.astype(BF16)
    w_lo = (w - w_hi.astype(F32)).astype(BF16)
    acc = jnp.dot(sc_hi, w_hi, preferred_element_type=F32)
    acc = acc + jnp.dot(sc_hi, w_lo, preferred_element_type=F32)
    acc = acc + jnp.dot(sc_lo, w_hi, preferred_element_type=F32)
    o_ref[...] = acc + b_ref[...]


def _ada_mod(c, w_ada, b_ada):
    bsz = c.shape[0]
    n = w_ada.shape[1]
    tn = ADA_TN
    return pl.pallas_call(
        _ada_kernel,
        name="ada_mod",
        grid=(n // tn,),
        in_specs=[pl.BlockSpec((bsz, D_MODEL), lambda j: (0, 0)),
                  pl.BlockSpec((D_MODEL, tn), lambda j: (0, j)),
                  pl.BlockSpec((1, tn), lambda j: (0, j))],
        out_specs=pl.BlockSpec((bsz, tn), lambda j: (0, j)),
        out_shape=jax.ShapeDtypeStruct((bsz, n), F32),
        compiler_params=_vmem("ada_mod"),
    )(c, w_ada, b_ada.reshape(1, n))


def _serpentine(i, j, n):
    return jnp.where(i % 2 == 0, j, n - 1 - j)


def _inproj_kernel(x_ref, sh_ref, sc_ref, w_ref, wg_ref, ws_ref, o_ref, os_ref, h_ref, *, n_head,
                   n_tiles):
    @pl.when(pl.program_id(1) == 0)
    def _():
        h = x_ref[...] * (1.0 + sc_ref[...]) + sh_ref[...]
        h_ref[...] = h.astype(BF16)
        os_ref[...] = jnp.dot(h_ref[...], ws_ref[...], preferred_element_type=F32)

    col = _serpentine(pl.program_id(0), pl.program_id(1), n_tiles)

    @pl.when(col < n_head)
    def _():
        o_ref[...] = jnp.dot(h_ref[...], w_ref[...], preferred_element_type=F32)

    @pl.when(col >= n_head)
    def _():
        o_ref[...] = jnp.dot(h_ref[...], wg_ref[...], preferred_element_type=F32)


def _in_proj(x2, mod3, w_all, w_gate, w_small, seq):
    t = x2.shape[0]
    tm, tn = INPROJ_TM, INPROJ_TN
    per_b = seq // tm
    n_head, n_gate, n_tiles = N_HEAD // tn, N_GATE // tn, N_MAIN // tn

    def col(i, j):
        return _serpentine(i, j, n_tiles)

    return pl.pallas_call(
        functools.partial(_inproj_kernel, n_head=n_head, n_tiles=n_tiles),
        name="in_proj",
        grid=(t // tm, n_tiles),
        in_specs=[pl.BlockSpec((tm, D_MODEL), lambda i, j: (i, 0)),
                  pl.BlockSpec((None, 1, D_MODEL), lambda i, j: (i // per_b, 0, 0)),
                  pl.BlockSpec((None, 1, D_MODEL), lambda i, j: (i // per_b, 0, 1)),
                  pl.BlockSpec((D_MODEL, tn), lambda i, j: (0, jnp.minimum(col(i, j), n_head - 1))),
                  pl.BlockSpec((D_MODEL, tn), lambda i, j: (0, jnp.maximum(col(i, j) - n_head, 0))),
                  pl.BlockSpec((D_MODEL, N_SMALL), lambda i, j: (0, 0))],
        out_specs=[pl.BlockSpec((tm, tn), lambda i, j: (i, (col(i, j) + n_gate) % n_tiles)),
                   pl.BlockSpec((tm, N_SMALL), lambda i, j: (i, 0))],
        out_shape=[jax.ShapeDtypeStruct((t, N_MAIN), F32),
                   jax.ShapeDtypeStruct((t, N_SMALL), F32)],
        scratch_shapes=[pltpu.VMEM((tm, D_MODEL), BF16)],
        compiler_params=_vmem("in_proj"),
    )(x2, mod3, mod3, w_all, w_gate, w_small)


def _moba_kernel(rel_ref, bko_ref, bkp_ref, q_ref, k_ref, v_ref, o_ref,
                 bias_own, bias_prev, kb_ref, vt_ref, *, nb):
    h = pl.program_id(0)
    blk = MOBA_BLOCK
    inv_scale = HEAD_DIM ** 0.5
    scale_log2e = HEAD_DIM ** -0.5 * math.log2(math.e)

    @pl.when(pl.program_id(1) == 0)
    def _():
        bo = bko_ref[...]
        bp = bkp_ref[...]
        acc_o = jnp.zeros((blk, blk), F32)
        acc_p = jnp.zeros((blk, blk), F32)
        for kk in range(REL_BUCKETS):
            val = rel_ref[kk, h] * inv_scale
            acc_o = jnp.where(bo == kk, val, acc_o)
            acc_p = jnp.where(bp == kk, val, acc_p)
        bias_own[...] = acc_o
        bias_prev[...] = acc_p

    bias_far = rel_ref[REL_BUCKETS - 1, h] * inv_scale
    kf = k_ref[...]
    kmean = jnp.mean(kf.reshape(nb, blk, HEAD_DIM), axis=1)
    kb_ref[...] = kf.astype(BF16)
    nt_dims = (((1,), (1,)), ((), ()))
    eye = (lax.broadcasted_iota(jnp.int32, (HEAD_DIM, HEAD_DIM), 0)
           == lax.broadcasted_iota(jnp.int32, (HEAD_DIM, HEAD_DIM), 1)).astype(BF16)
    vt_ref[:HEAD_DIM, :] = lax.dot_general(eye, v_ref[...].astype(BF16), nt_dims,
                                           preferred_element_type=F32).astype(BF16)
    pad_rows = vt_ref.shape[0] - HEAD_DIM
    vt_ref[HEAD_DIM:, :] = (lax.broadcasted_iota(jnp.int32, (pad_rows, vt_ref.shape[1]), 0)
                            == 0).astype(BF16)
    causal = (lax.broadcasted_iota(jnp.int32, (blk, blk), 0)
              <= lax.broadcasted_iota(jnp.int32, (blk, blk), 1))

    def scores(i):
        qi = q_ref[i * blk:(i + 1) * blk, :]
        qb = qi.astype(BF16)
        sel = None
        if i > MOBA_TOPK:
            route = lax.dot_general(kmean, qi, nt_dims, precision=lax.Precision.HIGHEST,
                                    preferred_element_type=F32)
            rc = [route[n:n + 1, :] for n in range(i)]
            sel = []
            for n in range(i):
                rank = jnp.zeros((1, blk), jnp.int32)
                for m in range(i):
                    if m == n:
                        continue
                    beats = (rc[m] >= rc[n]) if m < n else (rc[m] > rc[n])
                    rank = rank + beats.astype(jnp.int32)
                sel.append(rank < MOBA_TOPK)
        t_list = []
        for n in range(i + 1):
            t = lax.dot_general(kb_ref[n * blk:(n + 1) * blk, :], qb, nt_dims,
                                preferred_element_type=F32)
            if n == i:
                t = jnp.where(causal, t + bias_own[...], NEG_INF)
            else:
                if n == i - 1:
                    t = t + bias_prev[...]
                if sel is not None:
                    t = jnp.where(sel[n], t, NEG_INF)
            t_list.append(t)
        return t_list

    t_next = scores(0)
    for i in range(nb):
        t_list = t_next
        if i + 1 < nb:
            t_next = scores(i + 1)
        n_far = max(i - 1, 0)
        m_run = jnp.max(t_list[n_far], axis=0, keepdims=True)
        for t in t_list[n_far + 1:]:
            m_run = jnp.maximum(m_run, jnp.max(t, axis=0, keepdims=True))
        if n_far:
            m_far = jnp.max(t_list[0], axis=0, keepdims=True)
            for t in t_list[1:n_far]:
                m_far = jnp.maximum(m_far, jnp.max(t, axis=0, keepdims=True))
            m_run = jnp.maximum(m_run, m_far + bias_far)
        acc = jnp.zeros((vt_ref.shape[0], blk), F32)
        for n, t in enumerate(t_list):
            offset = m_run - bias_far if n < n_far else m_run
            p = jnp.exp2((t - offset) * scale_log2e)
            acc = acc + jnp.dot(vt_ref[:, n * blk:(n + 1) * blk], p.astype(BF16),
                                preferred_element_type=F32)
        out = acc[:HEAD_DIM] / acc[HEAD_DIM:HEAD_DIM + 1]
        o_ref[i * blk:(i + 1) * blk, :] = out.T.astype(BF16)


def _moba(proj, rel_bias, bko, bkp, bsz, seq):
    nb = seq // MOBA_BLOCK
    qo, ko, vo = COL_MOBA_Q // HEAD_DIM, COL_MOBA_K // HEAD_DIM, COL_MOBA_V // HEAD_DIM
    blk = MOBA_BLOCK
    return pl.pallas_call(
        functools.partial(_moba_kernel, nb=nb),
        name="moba",
        grid=(MOBA_HEADS, bsz),
        in_specs=[pl.BlockSpec(memory_space=pltpu.SMEM),
                  pl.BlockSpec((blk, blk), lambda h, b: (0, 0)),
                  pl.BlockSpec((blk, blk), lambda h, b: (0, 0)),
                  pl.BlockSpec((seq, HEAD_DIM), lambda h, b: (b, qo + h)),
                  pl.BlockSpec((seq, HEAD_DIM), lambda h, b: (b, ko + h)),
                  pl.BlockSpec((seq, HEAD_DIM), lambda h, b: (b, vo + h))],
        out_specs=pl.BlockSpec((seq, HEAD_DIM), lambda h, b: (b, h)),
        out_shape=jax.ShapeDtypeStruct((bsz * seq, MOBA_W), BF16),
        scratch_shapes=[pltpu.VMEM((blk, blk), F32), pltpu.VMEM((blk, blk), F32),
                        pltpu.VMEM((seq, HEAD_DIM), BF16),
                        pltpu.VMEM((HEAD_DIM + BF16_SUBLANES, seq), BF16)],
        compiler_params=_vmem("moba"),
    )(rel_bias, bko, bkp, proj, proj, proj)


def _conv_silu(xh, w):
    assert GDN_CONV == 4
    x1 = pltpu.roll(xh, 1, axis=0)
    u = xh * w[1:2, :] + x1 * w[0:1, :]
    acc = (xh * w[3:4, :] + x1 * w[2:3, :] + pltpu.roll(u, 2, axis=0))[SUBLANES:, :]
    return _silu(acc)


def _l2norm(x):
    return x * lax.rsqrt(jnp.sum(x * x, axis=-1, keepdims=True) + RMS_EPS)


def _softplus(x):
    return jnp.maximum(x, 0.0) + jnp.log1p(jnp.exp(-jnp.abs(x)))


ELIM_BLOCK = 8
MERGE_LEVELS = (GDN_CHUNK // ELIM_BLOCK).bit_length() - 1


def _block_diag_inverse(lpair):
    n = lpair.shape[0]
    nv = n // SUBLANES
    vpb = ELIM_BLOCK // SUBLANES
    rid = lax.broadcasted_iota(jnp.int32, (SUBLANES, 2 * n), 0)
    lid = lax.broadcasted_iota(jnp.int32, (SUBLANES, 2 * n), 1)
    t_rows = [(lid % n == rid + v * SUBLANES).astype(F32) for v in range(nv)]
    l_rows = [lpair[v * SUBLANES:(v + 1) * SUBLANES, :] for v in range(nv)]
    for m in range(n - 1):
        v0, s0 = divmod(m, SUBLANES)
        v_end = (v0 // vpb + 1) * vpb
        row = t_rows[v0][s0:s0 + 1, :]
        col = (lid // n) * n + m
        for v in range(v0 if s0 < SUBLANES - 1 else v0 + 1, v_end):
            t_rows[v] = t_rows[v] - jnp.take_along_axis(l_rows[v], col, axis=1) * row
    return jnp.concatenate(t_rows, axis=0)


def _lane_block_diag(pair):
    first = lax.broadcasted_iota(jnp.int32, pair.shape, 1) < pair.shape[1] // 2
    zero = jnp.zeros_like(pair)
    return jnp.concatenate([jnp.where(first, pair, zero), jnp.where(first, zero, pair)], axis=0)


def _merge_lower_products(lpair, tpair, k):
    n = lpair.shape[0]
    ri = lax.broadcasted_iota(jnp.int32, lpair.shape, 0)
    ci = lax.broadcasted_iota(jnp.int32, lpair.shape, 1) % n
    off = (ri // (2 * k) == ci // (2 * k)) & (ri // k > ci // k)
    lk = jnp.where(off, lpair, 0.0).astype(BF16)
    return jnp.dot(lk, _lane_block_diag(tpair.astype(BF16)), preferred_element_type=F32)


def _merge_apply(tpair, lt):
    return tpair - jnp.dot(tpair.astype(BF16), _lane_block_diag(lt), preferred_element_type=F32)


PIPELINE_UNROLL = 4


def _software_pipeline(stages, n):
    ns = len(stages)

    def run(it, lo, hi):
        conts = [stages[s](it - s) for s in reversed(range(lo, hi))]
        for cont in conts:
            if cont is not None:
                cont()

    for it in range(ns - 1):
        run(it, 0, it + 1)

    steady = n - (ns - 1)
    assert steady % PIPELINE_UNROLL == 0, (n, ns)

    def body(k, carry):
        first = ns - 1 + k * PIPELINE_UNROLL
        for u in range(PIPELINE_UNROLL):
            run(first + u, 0, ns)
        return carry

    lax.fori_loop(0, steady // PIPELINE_UNROLL, body, 0)
    for it in range(n, n + ns - 1):
        run(it, it - n + 1, ns)


def _lane_pick(x, lane, idx):
    return jnp.sum(jnp.where(lane == idx, x, 0.0), axis=-1, keepdims=True)


def _gdn_kernel(alog_ref, dtb_ref, gp_ref, nw_ref, cwq_ref, cwk_ref, cwv_ref,
                q_ref, k_ref, v_ref, z_ref, sm_ref, rw_ref, y_ref,
                kb_s, qn_s, kdp_s, rhs_s, qd_s, gcb_s, betab_s, gcr_s, gram_s, lm_s, lhs2_s,
                sol_s, mp_s, n_s, r_s, st_s, *merge_s, nchunk):
    tl_s, lt_s = merge_s[:MERGE_LEVELS + 1], merge_s[MERGE_LEVELS + 1:]
    hq = pl.program_id(1)
    c64 = GDN_CHUNK
    hd = HEAD_DIM
    heads = range(2)

    tri_u = (lax.broadcasted_iota(jnp.int32, (c64, c64), 0)
             <= lax.broadcasted_iota(jnp.int32, (c64, c64), 1)).astype(F32)
    gc_rows = []
    for j in heads:
        hv = 2 * hq + j
        a_neg_r = -jnp.exp(jnp.full((nchunk, c64), alog_ref[hv], F32))
        g_row = a_neg_r * _softplus(rw_ref[2 + j] + dtb_ref[hv])
        gc_rows.append(jnp.dot(g_row, tri_u, precision=lax.Precision.HIGHEST,
                               preferred_element_type=F32))
    gcr_s[...] = jnp.concatenate(gc_rows, axis=1)

    rows = lax.broadcasted_iota(jnp.int32, (c64, 2 * c64), 0)
    cols = lax.broadcasted_iota(jnp.int32, (c64, 2 * c64), 1) % c64
    tril = rows >= cols
    strict = rows > cols
    eye2 = (lax.broadcasted_iota(jnp.int32, (2 * c64, 2 * c64), 0)
            == lax.broadcasted_iota(jnp.int32, (2 * c64, 2 * c64), 1)).astype(BF16)
    nt_dims = (((1,), (1,)), ((), ()))

    def rows_of(c):
        start = c * c64
        return pl.ds(start if isinstance(c, int) else pl.multiple_of(start, c64), c64)

    def with_halo(x_ref, c):
        if isinstance(c, int) and c == 0:
            return jnp.concatenate([jnp.zeros((SUBLANES, x_ref.shape[1]), F32), x_ref[:c64, :]],
                                   axis=0)
        start = c * c64 - SUBLANES
        if not isinstance(c, int):
            start = pl.multiple_of(start, SUBLANES)
        return x_ref[pl.ds(start, c64 + SUBLANES), :]

    lane = lax.broadcasted_iota(jnp.int32, (c64, N_SMALL), 1)
    first = lane < c64
    pos = lax.broadcasted_iota(jnp.int32, (c64, N_SMALL), 0)

    def stage_prep(c):
        r = rows_of(c)
        qn = _l2norm(_conv_silu(with_halo(q_ref, c), cwq_ref[...])) * (hd ** -0.5)
        kn = _l2norm(_conv_silu(with_halo(k_ref, c), cwk_ref[...]))
        vc = _conv_silu(with_halo(v_ref, c), cwv_ref[...])
        qn_s[r, :] = qn.astype(BF16)
        kb_s[r, :] = kn.astype(BF16)
        sm = sm_ref[r, :]
        sig_all = jax.nn.sigmoid(sm)
        gc_all = -jnp.exp(gp_ref[0:1, :]) * _softplus(sm + gp_ref[1:2, :])
        sft = 1
        while sft < c64:
            gc_all = gc_all + jnp.where(pos >= sft, pltpu.roll(gc_all, sft, axis=0), 0.0)
            sft *= 2
        rest_all = gc_all[c64 - 1:, :] - gc_all
        betas, gc_cols = [], []
        for j in heads:
            hv = 2 * hq + j
            beta = _lane_pick(sig_all, lane, hv)
            gc_col = _lane_pick(gc_all, lane, GDN_V_HEADS + hv)
            rest_col = _lane_pick(rest_all, lane, GDN_V_HEADS + hv)
            eg = jnp.exp(gc_col)
            rhs_s[j, r, :hd] = (vc[:, j * hd:(j + 1) * hd] * beta).astype(BF16)
            rhs_s[j, r, hd:] = (kn * (beta * eg)).astype(BF16)
            qd_s[j, r, :] = (qn * eg).astype(BF16)
            kdp_s[c, j * c64:(j + 1) * c64, :] = (kn * jnp.exp(rest_col)).astype(BF16)
            betas.append(beta)
            gc_cols.append(gc_col)
        gcb_s[r, :] = jnp.where(first, gc_cols[0], gc_cols[1])
        betab_s[r, :] = jnp.where(first, betas[0], betas[1])

    def stage_gram(c):
        r = rows_of(c)
        kb = kb_s[r, :]
        kq = jnp.concatenate([kb, qn_s[r, :]], axis=0)
        gram = lax.dot_general(kq, jnp.concatenate([kb, kb], axis=0), nt_dims,
                               preferred_element_type=F32)
        kd_t = lax.dot_general(eye2, kdp_s[c], nt_dims, preferred_element_type=F32)

        def finish():
            gram_s[c] = gram
            lhs2_s[c, :2 * c64, :] = kd_t.astype(BF16)

        return finish

    def stage_factor(c):
        r = rows_of(c)
        dec = jnp.exp(jnp.where(tril, gcb_s[r, :] - gcr_s[pl.ds(c, 1), :], NEG_INF))
        lpair = jnp.where(strict, gram_s[c, :c64, :] * dec, 0.0) * betab_s[r, :]
        lm_s[r, :] = lpair
        lhs2_s[c, 2 * c64:, :] = (gram_s[c, c64:, :] * dec).astype(BF16)
        tl_s[0][r, :] = _block_diag_inverse(lpair)

    def stage_merge_products(level):
        def stage(c):
            r = rows_of(c)
            lt = _merge_lower_products(lm_s[r, :], tl_s[level][r, :], ELIM_BLOCK << level)

            def finish():
                lt_s[level][r, :] = lt.astype(BF16)

            return finish
        return stage

    def stage_merge_apply(level):
        def stage(c):
            r = rows_of(c)
            merged = _merge_apply(tl_s[level][r, :], lt_s[level][r, :])

            def finish():
                tl_s[level + 1][r, :] = merged

            return finish
        return stage

    def stage_solve(c):
        r = rows_of(c)
        zero = jnp.zeros((c64, 2 * hd), BF16)
        rhs_bd = jnp.concatenate([jnp.concatenate([rhs_s[0, r, :], zero], axis=1),
                                  jnp.concatenate([zero, rhs_s[1, r, :]], axis=1)], axis=0)
        sol = jnp.dot(tl_s[MERGE_LEVELS][r, :].astype(BF16), rhs_bd,
                      preferred_element_type=F32)

        def finish():
            sol_s[r, :] = sol.astype(BF16)

        return finish

    def stage_fold(c):
        r = rows_of(c)
        sol = sol_s[r, :]
        out = jnp.dot(lhs2_s[c], _lane_block_diag(sol), preferred_element_type=F32)

        def finish():
            for j in heads:
                u_col, w_col = 2 * j * hd, (2 * j + 1) * hd
                n_s[j, c] = out[:2 * c64, u_col:u_col + hd]
                mp_s[j, c, :2 * c64, :] = out[:2 * c64, w_col:w_col + hd].astype(BF16)
                mp_s[j, c, 2 * c64:, :] = (qd_s[j, r, :].astype(F32)
                                           - out[2 * c64:, w_col:w_col + hd]).astype(BF16)
                r_s[j, r, :] = out[2 * c64:, u_col:u_col + hd]

        return finish

    st_s[...] = jnp.zeros_like(st_s)
    nw = nw_ref[...]

    def stage_state(c):
        r = rows_of(c)
        sts = [st_s[j] for j in heads]
        outs = [jnp.dot(mp_s[j, c], sts[j].astype(BF16), preferred_element_type=F32)
                for j in heads]

        def finish():
            for j in heads:
                end = (j + 1) * c64
                gl = gcr_s[pl.ds(c, 1), end - 1:end]
                st_s[j] = sts[j] * jnp.exp(gl) - outs[j][:2 * c64] + n_s[j, c]
                r_s[j, r, :] = r_s[j, r, :] + outs[j][2 * c64:]

        return finish

    def stage_norm(c):
        r = rows_of(c)
        for j in heads:
            o = r_s[j, r, :]
            og = (o * lax.rsqrt(jnp.mean(o * o, axis=-1, keepdims=True) + RMS_EPS)
                  * nw * _silu(z_ref[r, j * hd:(j + 1) * hd]))
            y_ref[r, j * hd:(j + 1) * hd] = og.astype(BF16)

    merges = [stage(lv) for lv in range(MERGE_LEVELS)
              for stage in (stage_merge_products, stage_merge_apply)]
    _software_pipeline([stage_prep, stage_gram, stage_factor] + merges
                       + [stage_solve, stage_fold, stage_state, stage_norm], nchunk)


def _gdn(proj, small, rows, conv_w, a_log, dt_bias, norm_w, bsz, seq):
    nchunk = seq // GDN_CHUNK
    hd = HEAD_DIM
    c64 = GDN_CHUNK
    qo, ko = COL_GDN_Q // hd, COL_GDN_K // hd
    vo, zo = COL_GDN_V // (2 * hd), COL_GDN_Z // (2 * hd)
    cvo = (2 * GDN_QK_W) // (2 * hd)
    smem = pl.BlockSpec(memory_space=pltpu.SMEM)
    pad = (GDN_V_HEADS, N_SMALL - 2 * GDN_V_HEADS)
    gate_params = jnp.stack([jnp.pad(a_log, pad), jnp.pad(dt_bias, pad)])
    return pl.pallas_call(
        functools.partial(_gdn_kernel, nchunk=nchunk),
        name="gdn",
        grid=(bsz, GDN_QK_HEADS),
        in_specs=[smem, smem,
                  pl.BlockSpec((2, N_SMALL), lambda b, h: (0, 0)),
                  pl.BlockSpec((1, hd), lambda b, h: (0, 0)),
                  pl.BlockSpec((GDN_CONV, hd), lambda b, h: (0, h)),
                  pl.BlockSpec((GDN_CONV, hd), lambda b, h: (0, GDN_QK_HEADS + h)),
                  pl.BlockSpec((GDN_CONV, 2 * hd), lambda b, h: (0, cvo + h)),
                  pl.BlockSpec((seq, hd), lambda b, h: (b, qo + h)),
                  pl.BlockSpec((seq, hd), lambda b, h: (b, ko + h)),
                  pl.BlockSpec((seq, 2 * hd), lambda b, h: (b, vo + h)),
                  pl.BlockSpec((seq, 2 * hd), lambda b, h: (b, zo + h)),
                  pl.BlockSpec((seq, N_SMALL), lambda b, h: (b, 0)),
                  pl.BlockSpec((None, None, 4, nchunk, GDN_CHUNK), lambda b, h: (b, h, 0, 0, 0))],
        out_specs=pl.BlockSpec((seq, 2 * hd), lambda b, h: (b, h)),
        out_shape=jax.ShapeDtypeStruct((bsz * seq, GDN_V_W), BF16),
        scratch_shapes=[pltpu.VMEM((seq, hd), BF16),
                        pltpu.VMEM((seq, hd), BF16),
                        pltpu.VMEM((nchunk, 2 * c64, hd), BF16),
                        pltpu.VMEM((2, seq, 2 * hd), BF16),
                        pltpu.VMEM((2, seq, hd), BF16),
                        pltpu.VMEM((seq, 2 * c64), F32),
                        pltpu.VMEM((seq, 2 * c64), F32),
                        pltpu.VMEM((nchunk, 2 * c64), F32),
                        pltpu.VMEM((nchunk, 2 * c64, 2 * c64), F32),
                        pltpu.VMEM((seq, 2 * c64), F32),
                        pltpu.VMEM((nchunk, 3 * c64, hd), BF16),
                        pltpu.VMEM((seq, 4 * hd), BF16),
                        pltpu.VMEM((2, nchunk, 3 * c64, hd), BF16),
                        pltpu.VMEM((2, nchunk, hd, hd), F32),
                        pltpu.VMEM((2, seq, hd), F32),
                        pltpu.VMEM((2, hd, hd), F32)]
                       + [pltpu.VMEM((seq, 2 * c64), F32)] * (MERGE_LEVELS + 1)
                       + [pltpu.VMEM((seq, 2 * c64), BF16)] * MERGE_LEVELS,
        compiler_params=_vmem("gdn"),
    )(a_log, dt_bias, gate_params, norm_w, conv_w, conv_w, conv_w, proj, proj, proj, proj, small, rows)


def _merge_kernel(ya_ref, yb_ref, ga_ref, gb_ref, x_ref, g1_ref, sh2_ref, sc2_ref,
                  lng_ref, lnb_ref, wpm_ref, wpg_ref, wo_ref, x1_ref, h2_ref):
    pa = jnp.dot(ya_ref[...], wpm_ref[...], preferred_element_type=F32)
    pb = jnp.dot(yb_ref[...], wpg_ref[...], preferred_element_type=F32)
    merged = jax.nn.sigmoid(ga_ref[...]) * pa + jax.nn.sigmoid(gb_ref[...]) * pb
    y = jnp.dot(merged.astype(BF16), wo_ref[...], preferred_element_type=F32)
    x1 = _layer_norm(DEEPNORM_ALPHA * x_ref[...] + g1_ref[...] * y, lng_ref[...], lnb_ref[...])
    x1_ref[...] = x1
    h2_ref[...] = (x1 * (1.0 + sc2_ref[...]) + sh2_ref[...]).astype(BF16)


def _merge(ya, yb, proj, x2, mod3, ln_g, ln_b, wpm, wpg, wo, seq):
    t = x2.shape[0]
    tm = MERGE_TM
    per_b = seq // tm
    d = D_MODEL

    def modspec(k):
        return pl.BlockSpec((None, 1, d), lambda i: (i // per_b, 0, k))

    def const(shape):
        return pl.BlockSpec(shape, lambda i: (0, 0), pipeline_mode=pl.Buffered(1))

    return pl.pallas_call(
        _merge_kernel,
        name="merge",
        grid=(t // tm,),
        in_specs=[pl.BlockSpec((tm, MOBA_W), lambda i: (i, 0)),
                  pl.BlockSpec((tm, GDN_V_W), lambda i: (i, 0)),
                  pl.BlockSpec((tm, d), lambda i: (i, COL_GATE_A // d)),
                  pl.BlockSpec((tm, d), lambda i: (i, COL_GATE_B // d)),
                  pl.BlockSpec((tm, d), lambda i: (i, 0)),
                  modspec(2), modspec(3), modspec(4),
                  const((1, d)), const((1, d)),
                  const((MOBA_W, d)), const((GDN_V_W, d)), const((d, d))],
        out_specs=[pl.BlockSpec((tm, d), lambda i: (i, 0)),
                   pl.BlockSpec((tm, d), lambda i: (i, 0))],
        out_shape=[jax.ShapeDtypeStruct((t, d), F32), jax.ShapeDtypeStruct((t, d), BF16)],
        compiler_params=_vmem("merge"),
    )(ya, yb, proj, proj, x2, mod3, mod3, mod3, ln_g, ln_b, wpm, wpg, wo)


def _ffn_kernel(h_ref, x1_ref, g2_ref, lng_ref, lnb_ref, wg_ref, wu_ref, wo_ref, o_ref, acc_ref):
    f = pl.program_id(1)

    @pl.when(f == 0)
    def _():
        acc_ref[...] = jnp.zeros_like(acc_ref)

    h = h_ref[...]
    gate = jnp.dot(h, wg_ref[...], preferred_element_type=F32)
    up = jnp.dot(h, wu_ref[...], preferred_element_type=F32)
    act = (_silu(gate) * up).astype(BF16)
    acc_ref[...] += jnp.dot(act, wo_ref[...], preferred_element_type=F32)

    @pl.when(f == pl.num_programs(1) - 1)
    def _():
        r = DEEPNORM_ALPHA * x1_ref[...] + g2_ref[...] * acc_ref[...]
        o_ref[...] = _layer_norm(r, lng_ref[...], lnb_ref[...])


def _ffn(h2, x1, mod3, ln_g, ln_b, w_in, w_out, seq):
    t = h2.shape[0]
    tm, tf = FFN_TM, FFN_TF
    per_b = seq // tm
    d = D_MODEL
    nf = D_FF // tf
    return pl.pallas_call(
        _ffn_kernel,
        name="ffn",
        grid=(t // tm, nf),
        in_specs=[pl.BlockSpec((tm, d), lambda i, f: (i, 0)),
                  pl.BlockSpec((tm, d), lambda i, f: (i, 0)),
                  pl.BlockSpec((None, 1, d), lambda i, f: (i // per_b, 0, 5)),
                  pl.BlockSpec((1, d), lambda i, f: (0, 0)),
                  pl.BlockSpec((1, d), lambda i, f: (0, 0)),
                  pl.BlockSpec((d, tf), lambda i, f: (0, f)),
                  pl.BlockSpec((d, tf), lambda i, f: (0, nf + f)),
                  pl.BlockSpec((tf, d), lambda i, f: (f, 0))],
        out_specs=pl.BlockSpec((tm, d), lambda i, f: (i, 0)),
        out_shape=jax.ShapeDtypeStruct((t, d), F32),
        scratch_shapes=[pltpu.VMEM((tm, d), F32)],
        compiler_params=_vmem("ffn"),
    )(h2, x1, mod3, ln_g, ln_b, w_in, w_in, w_out)


def _rel_bucket(dist):
    max_exact = REL_BUCKETS // 2
    n = jnp.maximum(dist, 0)
    nf = jnp.maximum(n, 1).astype(F32)
    large = max_exact + (jnp.log(nf / max_exact) / math.log(REL_MAX_DIST / max_exact)
                         * (REL_BUCKETS - max_exact)).astype(jnp.int32)
    large = jnp.minimum(large, REL_BUCKETS - 1)
    return jnp.where(n < max_exact, n, large)


def _layer(x, c, w_ada, b_ada, w_in, conv_w, a_log, dt_bias, gdn_norm_w, rel_bias,
           w_proj_moba, w_proj_gdn, w_out, ln1_g, ln1_b, w_ffn_in, w_ffn_out, ln2_g, ln2_b):
    bsz, seq, d = x.shape
    t = bsz * seq
    x2 = x.reshape(t, d)

    mod = _ada_mod(c, w_ada, b_ada)
    mod3 = mod.reshape(bsz, 1, 6 * d)

    w_all = w_in.astype(BF16)
    n_gates = 2 * GDN_V_HEADS
    w_gate = w_all[:, N_HEAD + n_gates:]
    w_small = jnp.pad(w_all[:, N_HEAD:N_HEAD + n_gates], ((0, 0), (0, N_SMALL - n_gates)))

    proj, small = _in_proj(x2, mod3, w_all, w_gate, w_small, seq)

    ii = jnp.arange(MOBA_BLOCK, dtype=jnp.int32)
    dist = ii[None, :] - ii[:, None]
    bko = _rel_bucket(dist)
    bkp = _rel_bucket(dist + MOBA_BLOCK)
    ya = _moba(proj, rel_bias, bko, bkp, bsz, seq)

    nchunk = seq // GDN_CHUNK
    sm_t = small[:, :2 * GDN_V_HEADS].reshape(bsz, seq, 2, GDN_QK_HEADS, 2)
    rows = sm_t.transpose(0, 3, 2, 4, 1).reshape(bsz, GDN_QK_HEADS, 4, nchunk, GDN_CHUNK)
    yb = _gdn(proj, small, rows, conv_w, a_log, dt_bias, gdn_norm_w.reshape(1, HEAD_DIM),
              bsz, seq)

    x1, h2 = _merge(ya, yb, proj, x2, mod3, ln1_g.reshape(1, d), ln1_b.reshape(1, d),
                    w_proj_moba.astype(BF16), w_proj_gdn.astype(BF16), w_out.astype(BF16), seq)
    out = _ffn(h2, x1, mod3, ln2_g.reshape(1, d), ln2_b.reshape(1, d),
               w_ffn_in.astype(BF16), w_ffn_out.astype(BF16), seq)
    return out.reshape(bsz, seq, d)


def kernel(x, c, w_ada, b_ada, w_in, conv_w, a_log, dt_bias, gdn_norm_w, rel_bias, w_proj_moba,
           w_proj_gdn, w_out, ln1_g, ln1_b, w_ffn_in, w_ffn_out, ln2_g, ln2_b):
    depth = w_ada.shape[0]
    for l in range(depth):
        x = _layer(x, c, w_ada[l], b_ada[l], w_in[l], conv_w[l], a_log[l], dt_bias[l],
                   gdn_norm_w[l], rel_bias, w_proj_moba[l], w_proj_gdn[l], w_out[l],
                   ln1_g[l], ln1_b[l], w_ffn_in[l], w_ffn_out[l], ln2_g[l], ln2_b[l])
    return x
```

```python
import functools
import math

import jax
import jax.numpy as jnp
from jax import lax
from jax.experimental import pallas as pl
from jax.experimental.pallas import tpu as pltpu

F32 = jnp.float32
BF16 = jnp.bfloat16

D_MODEL = 2048
MOBA_HEADS = 8
HEAD_DIM = 128
MOBA_W = MOBA_HEADS * HEAD_DIM
MOBA_BLOCK = 256
MOBA_TOPK = 3
REL_BUCKETS = 32
REL_MAX_DIST = 128
GDN_QK_HEADS = 8
GDN_V_HEADS = 16
GDN_QK_W = GDN_QK_HEADS * HEAD_DIM
GDN_V_W = GDN_V_HEADS * HEAD_DIM
GDN_CONV = 4
GDN_CHUNK = 64
D_FF = 5632
DEEPNORM_ALPHA = 2.0 ** 0.25
LN_EPS = 1e-5
RMS_EPS = 1e-6
NEG_INF = -1e30

COL_GATE_A = 0
COL_GATE_B = 2048
COL_MOBA_Q = 4096
COL_MOBA_K = 5120
COL_MOBA_V = 6144
COL_GDN_Q = 7168
COL_GDN_K = 8192
COL_GDN_V = 9216
COL_GDN_Z = 11264
N_MAIN = 13312
N_GATE = 4096
N_HEAD = N_MAIN - N_GATE
N_SMALL = 128

V7X_VMEM_MIB = 64
SUBLANES = 8
BF16_SUBLANES = 16

ADA_TN = 1024
INPROJ_TM, INPROJ_TN = 1024, 1024
MERGE_TM = 256
FFN_TM, FFN_TF = 512, 512
VMEM_LIMIT_MIB = {"ada_mod": 40, "in_proj": 56, "moba": 48, "gdn": 58, "merge": 56, "ffn": 48}
assert max(VMEM_LIMIT_MIB.values()) < V7X_VMEM_MIB


def _vmem(name):
    return pltpu.CompilerParams(vmem_limit_bytes=VMEM_LIMIT_MIB[name] * 1024 * 1024)


def _sigmoid(x):
    return 0.5 + 0.5 * jnp.tanh(0.5 * x)


def _silu(x):
    h = 0.5 * x
    return h + h * jnp.tanh(h)


def _layer_norm(r, gain, bias):
    mu = jnp.mean(r, axis=-1, keepdims=True)
    d = r - mu
    var = jnp.mean(d * d, axis=-1, keepdims=True)
    return d * lax.rsqrt(var + LN_EPS) * gain + bias


def _ada_kernel(c_ref, w_ref, b_ref, o_ref):
    sc = _silu(c_ref[...])
    w = w_ref[...]
    sc_hi = sc.astype(BF16)
    sc_lo = (sc - sc_hi.astype(F32)).astype(BF16)
    w_hi = w.astype(BF16)
    w_lo = (w - w_hi.astype(F32)).astype(BF16)
    acc = jnp.dot(sc_hi, w_hi, preferred_element_type=F32)
    acc = acc + jnp.dot(sc_hi, w_lo, preferred_element_type=F32)
    acc = acc + jnp.dot(sc_lo, w_hi, preferred_element_type=F32)
    o_ref[...] = acc + b_ref[...]


def _ada_mod(c, w_ada, b_ada):
    bsz = c.shape[0]
    n = w_ada.shape[1]
    tn = ADA_TN
    return pl.pallas_call(
        _ada_kernel,
        name="ada_mod",
        grid=(n // tn,),
        in_specs=[pl.BlockSpec((bsz, D_MODEL), lambda j: (0, 0)),
                  pl.BlockSpec((D_MODEL, tn), lambda j: (0, j)),
                  pl.BlockSpec((1, tn), lambda j: (0, j))],
        out_specs=pl.BlockSpec((bsz, tn), lambda j: (0, j)),
        out_shape=jax.ShapeDtypeStruct((bsz, n), F32),
        compiler_params=_vmem("ada_mod"),
    )(c, w_ada, b_ada.reshape(1, n))


def _serpentine(i, j, n):
    return jnp.where(i % 2 == 0, j, n - 1 - j)


def _inproj_kernel(x_ref, sh_ref, sc_ref, w_ref, wg_ref, ws_ref, o_ref, os_ref, h_ref, *, n_head,
                   n_tiles):
    @pl.when(pl.program_id(1) == 0)
    def _():
        h = x_ref[...] * (1.0 + sc_ref[...]) + sh_ref[...]
        h_ref[...] = h.astype(BF16)
        os_ref[...] = jnp.dot(h_ref[...], ws_ref[...], preferred_element_type=F32)

    col = _serpentine(pl.program_id(0), pl.program_id(1), n_tiles)

    @pl.when(col < n_head)
    def _():
        o_ref[...] = jnp.dot(h_ref[...], w_ref[...], preferred_element_type=F32)

    @pl.when(col >= n_head)
    def _():
        o_ref[...] = jnp.dot(h_ref[...], wg_ref[...], preferred_element_type=F32)


def _in_proj(x2, mod3, w_all, w_gate, w_small, seq):
    t = x2.shape[0]
    tm, tn = INPROJ_TM, INPROJ_TN
    per_b = seq // tm
    n_head, n_gate, n_tiles = N_HEAD // tn, N_GATE // tn, N_MAIN // tn

    def col(i, j):
        return _serpentine(i, j, n_tiles)

    return pl.pallas_call(
        functools.partial(_inproj_kernel, n_head=n_head, n_tiles=n_tiles),
        name="in_proj",
        grid=(t // tm, n_tiles),
        in_specs=[pl.BlockSpec((tm, D_MODEL), lambda i, j: (i, 0)),
                  pl.BlockSpec((None, 1, D_MODEL), lambda i, j: (i // per_b, 0, 0)),
                  pl.BlockSpec((None, 1, D_MODEL), lambda i, j: (i // per_b, 0, 1)),
                  pl.BlockSpec((D_MODEL, tn), lambda i, j: (0, jnp.minimum(col(i, j), n_head - 1))),
                  pl.BlockSpec((D_MODEL, tn), lambda i, j: (0, jnp.maximum(col(i, j) - n_head, 0))),
                  pl.BlockSpec((D_MODEL, N_SMALL), lambda i, j: (0, 0))],
        out_specs=[pl.BlockSpec((tm, tn), lambda i, j: (i, (col(i, j) + n_gate) % n_tiles)),
                   pl.BlockSpec((tm, N_SMALL), lambda i, j: (i, 0))],
        out_shape=[jax.ShapeDtypeStruct((t, N_MAIN), F32),
                   jax.ShapeDtypeStruct((t, N_SMALL), F32)],
        scratch_shapes=[pltpu.VMEM((tm, D_MODEL), BF16)],
        compiler_params=_vmem("in_proj"),
    )(x2, mod3, mod3, w_all, w_gate, w_small)


def _moba_kernel(rel_ref, bko_ref, bkp_ref, q_ref, k_ref, v_ref, o_ref,
                 bias_own, bias_prev, kb_ref, vt_ref, *, nb):
    h = pl.program_id(0)
    blk = MOBA_BLOCK
    inv_scale = HEAD_DIM ** 0.5
    scale_log2e = HEAD_DIM ** -0.5 * math.log2(math.e)

    @pl.when(pl.program_id(1) == 0)
    def _():
        bo = bko_ref[...]
        bp = bkp_ref[...]
        acc_o = jnp.zeros((blk, blk), F32)
        acc_p = jnp.zeros((blk, blk), F32)
        for kk in range(REL_BUCKETS):
            val = rel_ref[kk, h] * inv_scale
            acc_o = jnp.where(bo == kk, val, acc_o)
            acc_p = jnp.where(bp == kk, val, acc_p)
        bias_own[...] = acc_o
        bias_prev[...] = acc_p

    bias_far = rel_ref[REL_BUCKETS - 1, h] * inv_scale
    kf = k_ref[...]
    kmean = jnp.mean(kf.reshape(nb, blk, HEAD_DIM), axis=1)
    kb_ref[...] = kf.astype(BF16)
    nt_dims = (((1,), (1,)), ((), ()))
    eye = (lax.broadcasted_iota(jnp.int32, (HEAD_DIM, HEAD_DIM), 0)
           == lax.broadcasted_iota(jnp.int32, (HEAD_DIM, HEAD_DIM), 1)).astype(BF16)
    vt_ref[:HEAD_DIM, :] = lax.dot_general(eye, v_ref[...].astype(BF16), nt_dims,
                                           preferred_element_type=F32).astype(BF16)
    pad_rows = vt_ref.shape[0] - HEAD_DIM
    vt_ref[HEAD_DIM:, :] = (lax.broadcasted_iota(jnp.int32, (pad_rows, vt_ref.shape[1]), 0)
                            == 0).astype(BF16)
    causal = (lax.broadcasted_iota(jnp.int32, (blk, blk), 0)
              <= lax.broadcasted_iota(jnp.int32, (blk, blk), 1))

    def scores(i):
        qi = q_ref[i * blk:(i + 1) * blk, :]
        qb = qi.astype(BF16)
        sel = None
        if i > MOBA_TOPK:
            route = lax.dot_general(kmean, qi, nt_dims, precision=lax.Precision.HIGHEST,
                                    preferred_element_type=F32)
            rc = [route[n:n + 1, :] for n in range(i)]
            sel = []
            for n in range(i):
                rank = jnp.zeros((1, blk), jnp.int32)
                for m in range(i):
                    if m == n:
                        continue
                    beats = (rc[m] >= rc[n]) if m < n else (rc[m] > rc[n])
                    rank = rank + beats.astype(jnp.int32)
                sel.append(rank < MOBA_TOPK)
        t_list = []
        for n in range(i + 1):
            t = lax.dot_general(kb_ref[n * blk:(n + 1) * blk, :], qb, nt_dims,
                                preferred_element_type=F32)
            if n == i:
                t = jnp.where(causal, t + bias_own[...], NEG_INF)
            else:
                if n == i - 1:
                    t = t + bias_prev[...]
                if sel is not None:
                    t = jnp.where(sel[n], t, NEG_INF)
            t_list.append(t)
        return t_list

    t_next = scores(0)
    for i in range(nb):
        t_list = t_next
        if i + 1 < nb:
            t_next = scores(i + 1)
        n_far = max(i - 1, 0)
        m_run = jnp.max(t_list[n_far], axis=0, keepdims=True)
        for t in t_list[n_far + 1:]:
            m_run = jnp.maximum(m_run, jnp.max(t, axis=0, keepdims=True))
        if n_far:
            m_far = jnp.max(t_list[0], axis=0, keepdims=True)
            for t in t_list[1:n_far]:
                m_far = jnp.maximum(m_far, jnp.max(t, axis=0, keepdims=True))
            m_run = jnp.maximum(m_run, m_far + bias_far)
        acc = jnp.zeros((vt_ref.shape[0], blk), F32)
        for n, t in enumerate(t_list):
            offset = m_run - bias_far if n < n_far else m_run
            p = jnp.exp2((t - offset) * scale_log2e)
            acc = acc + jnp.dot(vt_ref[:, n * blk:(n + 1) * blk], p.astype(BF16),
                                preferred_element_type=F32)
        out = acc[:HEAD_DIM] / acc[HEAD_DIM:HEAD_DIM + 1]
        o_ref[i * blk:(i + 1) * blk, :] = out.T.astype(BF16)


def _moba(proj, rel_bias, bko, bkp, bsz, seq):
    nb = seq // MOBA_BLOCK
    qo, ko, vo = COL_MOBA_Q // HEAD_DIM, COL_MOBA_K // HEAD_DIM, COL_MOBA_V // HEAD_DIM
    blk = MOBA_BLOCK
    return pl.pallas_call(
        functools.partial(_moba_kernel, nb=nb),
        name="moba",
        grid=(MOBA_HEADS, bsz),
        in_specs=[pl.BlockSpec(memory_space=pltpu.SMEM),
                  pl.BlockSpec((blk, blk), lambda h, b: (0, 0)),
                  pl.BlockSpec((blk, blk), lambda h, b: (0, 0)),
                  pl.BlockSpec((seq, HEAD_DIM), lambda h, b: (b, qo + h)),
                  pl.BlockSpec((seq, HEAD_DIM), lambda h, b: (b, ko + h)),
                  pl.BlockSpec((seq, HEAD_DIM), lambda h, b: (b, vo + h))],
        out_specs=pl.BlockSpec((seq, HEAD_DIM), lambda h, b: (b, h)),
        out_shape=jax.ShapeDtypeStruct((bsz * seq, MOBA_W), BF16),
        scratch_shapes=[pltpu.VMEM((blk, blk), F32), pltpu.VMEM((blk, blk), F32),
                        pltpu.VMEM((seq, HEAD_DIM), BF16),
                        pltpu.VMEM((HEAD_DIM + BF16_SUBLANES, seq), BF16)],
        compiler_params=_vmem("moba"),
    )(rel_bias, bko, bkp, proj, proj, proj)


def _conv_silu(xh, w):
    assert GDN_CONV == 4
    x1 = pltpu.roll(xh, 1, axis=0)
    u = xh * w[1:2, :] + x1 * w[0:1, :]
    acc = (xh * w[3:4, :] + x1 * w[2:3, :] + pltpu.roll(u, 2, axis=0))[SUBLANES:, :]
    return _silu(acc)


def _l2norm(x):
    return x * lax.rsqrt(jnp.sum(x * x, axis=-1, keepdims=True) + RMS_EPS)


def _softplus(x):
    return jnp.maximum(x, 0.0) + jnp.log1p(jnp.exp(-jnp.abs(x)))


ELIM_BLOCK = 8
MERGE_LEVELS = (GDN_CHUNK // ELIM_BLOCK).bit_length() - 1


def _block_diag_inverse(lpair):
    n = lpair.shape[0]
    nv = n // SUBLANES
    vpb = ELIM_BLOCK // SUBLANES
    rid = lax.broadcasted_iota(jnp.int32, (SUBLANES, 2 * n), 0)
    lid = lax.broadcasted_iota(jnp.int32, (SUBLANES, 2 * n), 1)
    t_rows = [(lid % n == rid + v * SUBLANES).astype(F32) for v in range(nv)]
    l_rows = [lpair[v * SUBLANES:(v + 1) * SUBLANES, :] for v in range(nv)]
    for m in range(n - 1):
        v0, s0 = divmod(m, SUBLANES)
        v_end = (v0 // vpb + 1) * vpb
        row = t_rows[v0][s0:s0 + 1, :]
        col = (lid // n) * n + m
        for v in range(v0 if s0 < SUBLANES - 1 else v0 + 1, v_end):
            t_rows[v] = t_rows[v] - jnp.take_along_axis(l_rows[v], col, axis=1) * row
    return jnp.concatenate(t_rows, axis=0)


def _lane_block_diag(pair):
    first = lax.broadcasted_iota(jnp.int32, pair.shape, 1) < pair.shape[1] // 2
    zero = jnp.zeros_like(pair)
    return jnp.concatenate([jnp.where(first, pair, zero), jnp.where(first, zero, pair)], axis=0)


def _merge_lower_products(lpair, tpair, k):
    n = lpair.shape[0]
    ri = lax.broadcasted_iota(jnp.int32, lpair.shape, 0)
    ci = lax.broadcasted_iota(jnp.int32, lpair.shape, 1) % n
    off = (ri // (2 * k) == ci // (2 * k)) & (ri // k > ci // k)
    lk = jnp.where(off, lpair, 0.0).astype(BF16)
    return jnp.dot(lk, _lane_block_diag(tpair.astype(BF16)), preferred_element_type=F32)


def _merge_apply(tpair, lt):
    return tpair - jnp.dot(tpair.astype(BF16), _lane_block_diag(lt), preferred_element_type=F32)


PIPELINE_UNROLL = 4


def _software_pipeline(stages, n):
    ns = len(stages)

    def run(it, lo, hi):
        conts = [stages[s](it - s) for s in reversed(range(lo, hi))]
        for cont in conts:
            if cont is not None:
                cont()

    for it in range(ns - 1):
        run(it, 0, it + 1)

    steady = n - (ns - 1)
    assert steady % PIPELINE_UNROLL == 0, (n, ns)

    def body(k, carry):
        first = ns - 1 + k * PIPELINE_UNROLL
        for u in range(PIPELINE_UNROLL):
            run(first + u, 0, ns)
        return carry

    lax.fori_loop(0, steady // PIPELINE_UNROLL, body, 0)
    for it in range(n, n + ns - 1):
        run(it, it - n + 1, ns)


def _lane_pick(x, lane, idx):
    return jnp.sum(jnp.where(lane == idx, x, 0.0), axis=-1, keepdims=True)


def _gdn_kernel(alog_ref, dtb_ref, gp_ref, nw_ref, cwq_ref, cwk_ref, cwv_ref,
                q_ref, k_ref, v_ref, z_ref, sm_ref, rw_ref, y_ref,
                kb_s, qn_s, kdp_s, rhs_s, qd_s, gcb_s, betab_s, gcr_s, gram_s, lm_s, lhs2_s,
                sol_s, mp_s, n_s, r_s, st_s, *merge_s, nchunk):
    tl_s, lt_s = merge_s[:MERGE_LEVELS + 1], merge_s[MERGE_LEVELS + 1:]
    hq = pl.program_id(1)
    c64 = GDN_CHUNK
    hd = HEAD_DIM
    heads = range(2)

    tri_u = (lax.broadcasted_iota(jnp.int32, (c64, c64), 0)
             <= lax.broadcasted_iota(jnp.int32, (c64, c64), 1)).astype(F32)
    gc_rows = []
    for j in heads:
        hv = 2 * hq + j
        a_neg_r = -jnp.exp(jnp.full((nchunk, c64), alog_ref[hv], F32))
        g_row = a_neg_r * _softplus(rw_ref[2 + j] + dtb_ref[hv])
        gc_rows.append(jnp.dot(g_row, tri_u, precision=lax.Precision.HIGHEST,
                               preferred_element_type=F32))
    gcr_s[...] = jnp.concatenate(gc_rows, axis=1)

    rows = lax.broadcasted_iota(jnp.int32, (c64, 2 * c64), 0)
    cols = lax.broadcasted_iota(jnp.int32, (c64, 2 * c64), 1) % c64
    tril = rows >= cols
    strict = rows > cols
    eye2 = (lax.broadcasted_iota(jnp.int32, (2 * c64, 2 * c64), 0)
            == lax.broadcasted_iota(jnp.int32, (2 * c64, 2 * c64), 1)).astype(BF16)
    nt_dims = (((1,), (1,)), ((), ()))

    def rows_of(c):
        start = c * c64
        return pl.ds(start if isinstance(c, int) else pl.multiple_of(start, c64), c64)

    def with_halo(x_ref, c):
        if isinstance(c, int) and c == 0:
            return jnp.concatenate([jnp.zeros((SUBLANES, x_ref.shape[1]), F32), x_ref[:c64, :]],
                                   axis=0)
        start = c * c64 - SUBLANES
        if not isinstance(c, int):
            start = pl.multiple_of(start, SUBLANES)
        return x_ref[pl.ds(start, c64 + SUBLANES), :]

    lane = lax.broadcasted_iota(jnp.int32, (c64, N_SMALL), 1)
    first = lane < c64
    pos = lax.broadcasted_iota(jnp.int32, (c64, N_SMALL), 0)

    def stage_prep(c):
        r = rows_of(c)
        qn = _l2norm(_conv_silu(with_halo(q_ref, c), cwq_ref[...])) * (hd ** -0.5)
        kn = _l2norm(_conv_silu(with_halo(k_ref, c), cwk_ref[...]))
        vc = _conv_silu(with_halo(v_ref, c), cwv_ref[...])
        qn_s[r, :] = qn.astype(BF16)
        kb_s[r, :] = kn.astype(BF16)
        sm = sm_ref[r, :]
        sig_all = _sigmoid(sm)
        gc_all = -jnp.exp(gp_ref[0:1, :]) * _softplus(sm + gp_ref[1:2, :])
        sft = 1
        while sft < c64:
            gc_all = gc_all + jnp.where(pos >= sft, pltpu.roll(gc_all, sft, axis=0), 0.0)
            sft *= 2
        rest_all = gc_all[c64 - 1:, :] - gc_all
        betas, gc_cols = [], []
        for j in heads:
            hv = 2 * hq + j
            beta = _lane_pick(sig_all, lane, hv)
            gc_col = _lane_pick(gc_all, lane, GDN_V_HEADS + hv)
            rest_col = _lane_pick(rest_all, lane, GDN_V_HEADS + hv)
            eg = jnp.exp(gc_col)
            rhs_s[j, r, :hd] = (vc[:, j * hd:(j + 1) * hd] * beta).astype(BF16)
            rhs_s[j, r, hd:] = (kn * (beta * eg)).astype(BF16)
            qd_s[j, r, :] = (qn * eg).astype(BF16)
            kdp_s[c, j * c64:(j + 1) * c64, :] = (kn * jnp.exp(rest_col)).astype(BF16)
            betas.append(beta)
            gc_cols.append(gc_col)
        gcb_s[r, :] = jnp.where(first, gc_cols[0], gc_cols[1])
        betab_s[r, :] = jnp.where(first, betas[0], betas[1])

    def stage_gram(c):
        r = rows_of(c)
        kb = kb_s[r, :]
        kq = jnp.concatenate([kb, qn_s[r, :]], axis=0)
        gram = lax.dot_general(kq, jnp.concatenate([kb, kb], axis=0), nt_dims,
                               preferred_element_type=F32)
        kd_t = lax.dot_general(eye2, kdp_s[c], nt_dims, preferred_element_type=F32)

        def finish():
            gram_s[c] = gram
            lhs2_s[c, :2 * c64, :] = kd_t.astype(BF16)

        return finish

    def stage_factor(c):
        r = rows_of(c)
        dec = jnp.exp(jnp.where(tril, gcb_s[r, :] - gcr_s[pl.ds(c, 1), :], NEG_INF))
        lpair = jnp.where(strict, gram_s[c, :c64, :] * dec, 0.0) * betab_s[r, :]
        lm_s[r, :] = lpair
        lhs2_s[c, 2 * c64:, :] = (gram_s[c, c64:, :] * dec).astype(BF16)
        tl_s[0][r, :] = _block_diag_inverse(lpair)

    def stage_merge_products(level):
        def stage(c):
            r = rows_of(c)
            lt = _merge_lower_products(lm_s[r, :], tl_s[level][r, :], ELIM_BLOCK << level)

            def finish():
                lt_s[level][r, :] = lt.astype(BF16)

            return finish
        return stage

    def stage_merge_apply(level):
        def stage(c):
            r = rows_of(c)
            merged = _merge_apply(tl_s[level][r, :], lt_s[level][r, :])

            def finish():
                tl_s[level + 1][r, :] = merged

            return finish
        return stage

    def stage_solve(c):
        r = rows_of(c)
        zero = jnp.zeros((c64, 2 * hd), BF16)
        rhs_bd = jnp.concatenate([jnp.concatenate([rhs_s[0, r, :], zero], axis=1),
                                  jnp.concatenate([zero, rhs_s[1, r, :]], axis=1)], axis=0)
        sol = jnp.dot(tl_s[MERGE_LEVELS][r, :].astype(BF16), rhs_bd,
                      preferred_element_type=F32)

        def finish():
            sol_s[r, :] = sol.astype(BF16)

        return finish

    def stage_fold(c):
        r = rows_of(c)
        sol = sol_s[r, :]
        out = jnp.dot(lhs2_s[c], _lane_block_diag(sol), preferred_element_type=F32)

        def finish():
            for j in heads:
                u_col, w_col = 2 * j * hd, (2 * j + 1) * hd
                n_s[j, c] = out[:2 * c64, u_col:u_col + hd]
                mp_s[j, c, :2 * c64, :] = out[:2 * c64, w_col:w_col + hd].astype(BF16)
                mp_s[j, c, 2 * c64:, :] = (qd_s[j, r, :].astype(F32)
                                           - out[2 * c64:, w_col:w_col + hd]).astype(BF16)
                r_s[j, r, :] = out[2 * c64:, u_col:u_col + hd]

        return finish

    st_s[...] = jnp.zeros_like(st_s)
    nw = nw_ref[...]

    def stage_state(c):
        r = rows_of(c)
        sts = [st_s[j] for j in heads]
        outs = [jnp.dot(mp_s[j, c], sts[j].astype(BF16), preferred_element_type=F32)
                for j in heads]

        def finish():
            for j in heads:
                end = (j + 1) * c64
                gl = gcr_s[pl.ds(c, 1), end - 1:end]
                st_s[j] = sts[j] * jnp.exp(gl) - outs[j][:2 * c64] + n_s[j, c]
                r_s[j, r, :] = r_s[j, r, :] + outs[j][2 * c64:]

        return finish

    def stage_norm(c):
        r = rows_of(c)
        for j in heads:
            o = r_s[j, r, :]
            og = (o * lax.rsqrt(jnp.mean(o * o, axis=-1, keepdims=True) + RMS_EPS)
                  * nw * _silu(z_ref[r, j * hd:(j + 1) * hd]))
            y_ref[r, j * hd:(j + 1) * hd] = og.astype(BF16)

    merges = [stage(lv) for lv in range(MERGE_LEVELS)
              for stage in (stage_merge_products, stage_merge_apply)]
    _software_pipeline([stage_prep, stage_gram, stage_factor] + merges
                       + [stage_solve, stage_fold, stage_state, stage_norm], nchunk)


def _gdn(proj, small, rows, conv_w, a_log, dt_bias, norm_w, bsz, seq):
    nchunk = seq // GDN_CHUNK
    hd = HEAD_DIM
    c64 = GDN_CHUNK
    qo, ko = COL_GDN_Q // hd, COL_GDN_K // hd
    vo, zo = COL_GDN_V // (2 * hd), COL_GDN_Z // (2 * hd)
    cvo = (2 * GDN_QK_W) // (2 * hd)
    smem = pl.BlockSpec(memory_space=pltpu.SMEM)
    pad = (GDN_V_HEADS, N_SMALL - 2 * GDN_V_HEADS)
    gate_params = jnp.stack([jnp.pad(a_log, pad), jnp.pad(dt_bias, pad)])
    return pl.pallas_call(
        functools.partial(_gdn_kernel, nchunk=nchunk),
        name="gdn",
        grid=(bsz, GDN_QK_HEADS),
        in_specs=[smem, smem,
                  pl.BlockSpec((2, N_SMALL), lambda b, h: (0, 0)),
                  pl.BlockSpec((1, hd), lambda b, h: (0, 0)),
                  pl.BlockSpec((GDN_CONV, hd), lambda b, h: (0, h)),
                  pl.BlockSpec((GDN_CONV, hd), lambda b, h: (0, GDN_QK_HEADS + h)),
                  pl.BlockSpec((GDN_CONV, 2 * hd), lambda b, h: (0, cvo + h)),
                  pl.BlockSpec((seq, hd), lambda b, h: (b, qo + h)),
                  pl.BlockSpec((seq, hd), lambda b, h: (b, ko + h)),
                  pl.BlockSpec((seq, 2 * hd), lambda b, h: (b, vo + h)),
                  pl.BlockSpec((seq, 2 * hd), lambda b, h: (b, zo + h)),
                  pl.BlockSpec((seq, N_SMALL), lambda b, h: (b, 0)),
                  pl.BlockSpec((None, None, 4, nchunk, GDN_CHUNK), lambda b, h: (b, h, 0, 0, 0))],
        out_specs=pl.BlockSpec((seq, 2 * hd), lambda b, h: (b, h)),
        out_shape=jax.ShapeDtypeStruct((bsz * seq, GDN_V_W), BF16),
        scratch_shapes=[pltpu.VMEM((seq, hd), BF16),
                        pltpu.VMEM((seq, hd), BF16),
                        pltpu.VMEM((nchunk, 2 * c64, hd), BF16),
                        pltpu.VMEM((2, seq, 2 * hd), BF16),
                        pltpu.VMEM((2, seq, hd), BF16),
                        pltpu.VMEM((seq, 2 * c64), F32),
                        pltpu.VMEM((seq, 2 * c64), F32),
                        pltpu.VMEM((nchunk, 2 * c64), F32),
                        pltpu.VMEM((nchunk, 2 * c64, 2 * c64), F32),
                        pltpu.VMEM((seq, 2 * c64), F32),
                        pltpu.VMEM((nchunk, 3 * c64, hd), BF16),
                        pltpu.VMEM((seq, 4 * hd), BF16),
                        pltpu.VMEM((2, nchunk, 3 * c64, hd), BF16),
                        pltpu.VMEM((2, nchunk, hd, hd), F32),
                        pltpu.VMEM((2, seq, hd), F32),
                        pltpu.VMEM((2, hd, hd), F32)]
                       + [pltpu.VMEM((seq, 2 * c64), F32)] * (MERGE_LEVELS + 1)
                       + [pltpu.VMEM((seq, 2 * c64), BF16)] * MERGE_LEVELS,
        compiler_params=_vmem("gdn"),
    )(a_log, dt_bias, gate_params, norm_w, conv_w, conv_w, conv_w, proj, proj, proj, proj, small, rows)


def _merge_kernel(ya_ref, yb_ref, ga_ref, gb_ref, x_ref, g1_ref, sh2_ref, sc2_ref,
                  lng_ref, lnb_ref, wpm_ref, wpg_ref, wo_ref, x1_ref, h2_ref):
    pa = jnp.dot(ya_ref[...], wpm_ref[...], preferred_element_type=F32)
    pb = jnp.dot(yb_ref[...], wpg_ref[...], preferred_element_type=F32)
    merged = _sigmoid(ga_ref[...]) * pa + _sigmoid(gb_ref[...]) * pb
    y = jnp.dot(merged.astype(BF16), wo_ref[...], preferred_element_type=F32)
    x1 = _layer_norm(DEEPNORM_ALPHA * x_ref[...] + g1_ref[...] * y, lng_ref[...], lnb_ref[...])
    x1_ref[...] = x1
    h2_ref[...] = (x1 * (1.0 + sc2_ref[...]) + sh2_ref[...]).astype(BF16)


def _merge(ya, yb, proj, x2, mod3, ln_g, ln_b, wpm, wpg, wo, seq):
    t = x2.shape[0]
    tm = MERGE_TM
    per_b = seq // tm
    d = D_MODEL

    def modspec(k):
        return pl.BlockSpec((None, 1, d), lambda i: (i // per_b, 0, k))

    def const(shape):
        return pl.BlockSpec(shape, lambda i: (0, 0), pipeline_mode=pl.Buffered(1))

    return pl.pallas_call(
        _merge_kernel,
        name="merge",
        grid=(t // tm,),
        in_specs=[pl.BlockSpec((tm, MOBA_W), lambda i: (i, 0)),
                  pl.BlockSpec((tm, GDN_V_W), lambda i: (i, 0)),
                  pl.BlockSpec((tm, d), lambda i: (i, COL_GATE_A // d)),
                  pl.BlockSpec((tm, d), lambda i: (i, COL_GATE_B // d)),
                  pl.BlockSpec((tm, d), lambda i: (i, 0)),
                  modspec(2), modspec(3), modspec(4),
                  const((1, d)), const((1, d)),
                  const((MOBA_W, d)), const((GDN_V_W, d)), const((d, d))],
        out_specs=[pl.BlockSpec((tm, d), lambda i: (i, 0)),
                   pl.BlockSpec((tm, d), lambda i: (i, 0))],
        out_shape=[jax.ShapeDtypeStruct((t, d), F32), jax.ShapeDtypeStruct((t, d), BF16)],
        compiler_params=_vmem("merge"),
    )(ya, yb, proj, proj, x2, mod3, mod3, mod3, ln_g, ln_b, wpm, wpg, wo)


def _ffn_kernel(h_ref, x1_ref, g2_ref, lng_ref, lnb_ref, wg_ref, wu_ref, wo_ref, o_ref, acc_ref):
    f = pl.program_id(1)

    @pl.when(f == 0)
    def _():
        acc_ref[...] = jnp.zeros_like(acc_ref)

    h = h_ref[...]
    gate = jnp.dot(h, wg_ref[...], preferred_element_type=F32)
    up = jnp.dot(h, wu_ref[...], preferred_element_type=F32)
    act = (_silu(gate) * up).astype(BF16)
    acc_ref[...] += jnp.dot(act, wo_ref[...], preferred_element_type=F32)

    @pl.when(f == pl.num_programs(1) - 1)
    def _():
        r = DEEPNORM_ALPHA * x1_ref[...] + g2_ref[...] * acc_ref[...]
        o_ref[...] = _layer_norm(r, lng_ref[...], lnb_ref[...])


def _ffn(h2, x1, mod3, ln_g, ln_b, w_in, w_out, seq):
    t = h2.shape[0]
    tm, tf = FFN_TM, FFN_TF
    per_b = seq // tm
    d = D_MODEL
    nf = D_FF // tf
    return pl.pallas_call(
        _ffn_kernel,
        name="ffn",
        grid=(t // tm, nf),
        in_specs=[pl.BlockSpec((tm, d), lambda i, f: (i, 0)),
                  pl.BlockSpec((tm, d), lambda i, f: (i, 0)),
                  pl.BlockSpec((None, 1, d), lambda i, f: (i // per_b, 0, 5)),
                  pl.BlockSpec((1, d), lambda i, f: (0, 0)),
                  pl.BlockSpec((1, d), lambda i, f: (0, 0)),
                  pl.BlockSpec((d, tf), lambda i, f: (0, f)),
                  pl.BlockSpec((d, tf), lambda i, f: (0, nf + f)),
                  pl.BlockSpec((tf, d), lambda i, f: (f, 0))],
        out_specs=pl.BlockSpec((tm, d), lambda i, f: (i, 0)),
        out_shape=jax.ShapeDtypeStruct((t, d), F32),
        scratch_shapes=[pltpu.VMEM((tm, d), F32)],
        compiler_params=_vmem("ffn"),
    )(h2, x1, mod3, ln_g, ln_b, w_in, w_in, w_out)


def _rel_bucket(dist):
    max_exact = REL_BUCKETS // 2
    n = jnp.maximum(dist, 0)
    nf = jnp.maximum(n, 1).astype(F32)
    large = max_exact + (jnp.log(nf / max_exact) / math.log(REL_MAX_DIST / max_exact)
                         * (REL_BUCKETS - max_exact)).astype(jnp.int32)
    large = jnp.minimum(large, REL_BUCKETS - 1)
    return jnp.where(n < max_exact, n, large)


def _layer(x, c, w_ada, b_ada, w_in, conv_w, a_log, dt_bias, gdn_norm_w, rel_bias,
           w_proj_moba, w_proj_gdn, w_out, ln1_g, ln1_b, w_ffn_in, w_ffn_out, ln2_g, ln2_b):
    bsz, seq, d = x.shape
    t = bsz * seq
    x2 = x.reshape(t, d)

    mod = _ada_mod(c, w_ada, b_ada)
    mod3 = mod.reshape(bsz, 1, 6 * d)

    w_all = w_in.astype(BF16)
    n_gates = 2 * GDN_V_HEADS
    w_gate = w_all[:, N_HEAD + n_gates:]
    w_small = jnp.pad(w_all[:, N_HEAD:N_HEAD + n_gates], ((0, 0), (0, N_SMALL - n_gates)))

    proj, small = _in_proj(x2, mod3, w_all, w_gate, w_small, seq)

    ii = jnp.arange(MOBA_BLOCK, dtype=jnp.int32)
    dist = ii[None, :] - ii[:, None]
    bko = _rel_bucket(dist)
    bkp = _rel_bucket(dist + MOBA_BLOCK)
    ya = _moba(proj, rel_bias, bko, bkp, bsz, seq)

    nchunk = seq // GDN_CHUNK
    sm_t = small[:, :2 * GDN_V_HEADS].reshape(bsz, seq, 2, GDN_QK_HEADS, 2)
    rows = sm_t.transpose(0, 3, 2, 4, 1).reshape(bsz, GDN_QK_HEADS, 4, nchunk, GDN_CHUNK)
    yb = _gdn(proj, small, rows, conv_w, a_log, dt_bias, gdn_norm_w.reshape(1, HEAD_DIM),
              bsz, seq)

    x1, h2 = _merge(ya, yb, proj, x2, mod3, ln1_g.reshape(1, d), ln1_b.reshape(1, d),
                    w_proj_moba.astype(BF16), w_proj_gdn.astype(BF16), w_out.astype(BF16), seq)
    out = _ffn(h2, x1, mod3, ln2_g.reshape(1, d), ln2_b.reshape(1, d),
               w_ffn_in.astype(BF16), w_ffn_out.astype(BF16), seq)
    return out.reshape(bsz, seq, d)


def kernel(x, c, w_ada, b_ada, w_in, conv_w, a_log, dt_bias, gdn_norm_w, rel_bias, w_proj_moba,
           w_proj_gdn, w_out, ln1_g, ln1_b, w_ffn_in, w_ffn_out, ln2_g, ln2_b):
    depth = w_ada.shape[0]
    for l in range(depth):
        x = _layer(x, c, w_ada[l], b_ada[l], w_in[l], conv_w[l], a_log[l], dt_bias[l],
                   gdn_norm_w[l], rel_bias, w_proj_moba[l], w_proj_gdn[l], w_out[l],
                   ln1_g[l], ln1_b[l], w_ffn_in[l], w_ffn_out[l], ln2_g[l], ln2_b[l])
    return x
```

```python
import functools
import math

import jax
import jax.numpy as jnp
from jax import lax
from jax.experimental import pallas as pl
from jax.experimental.pallas import tpu as pltpu

F32 = jnp.float32
BF16 = jnp.bfloat16

D_MODEL = 2048
MOBA_HEADS = 8
HEAD_DIM = 128
MOBA_W = MOBA_HEADS * HEAD_DIM
MOBA_BLOCK = 256
MOBA_TOPK = 3
REL_BUCKETS = 32
REL_MAX_DIST = 128
GDN_QK_HEADS = 8
GDN_V_HEADS = 16
GDN_QK_W = GDN_QK_HEADS * HEAD_DIM
GDN_V_W = GDN_V_HEADS * HEAD_DIM
GDN_CONV = 4
GDN_CHUNK = 64
D_FF = 5632
DEEPNORM_ALPHA = 2.0 ** 0.25
LN_EPS = 1e-5
RMS_EPS = 1e-6
NEG_INF = -1e30

COL_GATE_A = 0
COL_GATE_B = 2048
COL_MOBA_Q = 4096
COL_MOBA_K = 5120
COL_MOBA_V = 6144
COL_GDN_Q = 7168
COL_GDN_K = 8192
COL_GDN_V = 9216
COL_GDN_Z = 11264
N_MAIN = 13312
N_GATE = 4096
N_HEAD = N_MAIN - N_GATE
N_SMALL = 128

V7X_VMEM_MIB = 64
SUBLANES = 8
BF16_SUBLANES = 16

ADA_TN = 1024
INPROJ_TM, INPROJ_TN = 1024, 1024
MERGE_TM = 256
FFN_TM, FFN_TF = 512, 512
VMEM_LIMIT_MIB = {"ada_mod": 40, "in_proj": 56, "moba": 48, "gdn": 58, "merge": 56, "ffn": 48}
assert max(VMEM_LIMIT_MIB.values()) < V7X_VMEM_MIB


def _vmem(name):
    return pltpu.CompilerParams(vmem_limit_bytes=VMEM_LIMIT_MIB[name] * 1024 * 1024)


def _sigmoid(x):
    return 0.5 + 0.5 * jnp.tanh(0.5 * x)


def _silu(x):
    h = 0.5 * x
    return h + h * jnp.tanh(h)


def _layer_norm(r, gain, bias):
    mu = jnp.mean(r, axis=-1, keepdims=True)
    d = r - mu
    var = jnp.mean(d * d, axis=-1, keepdims=True)
    return d * lax.rsqrt(var + LN_EPS) * gain + bias


def _ada_kernel(c_ref, w_ref, b_ref, o_ref):
    sc = _silu(c_ref[...])
    w = w_ref[...]
    sc_hi = sc.astype(BF16)
    sc_lo = (sc - sc_hi.astype(F32)).astype(BF16)
    w_hi = w.astype(BF16)
    w_lo = (w - w_hi.astype(F32)).astype(BF16)
    acc = jnp.dot(sc_hi, w_hi, preferred_element_type=F32)
    acc = acc + jnp.dot(sc_hi, w_lo, preferred_element_type=F32)
    acc = acc + jnp.dot(sc_lo, w_hi, preferred_element_type=F32)
    o_ref[...] = acc + b_ref[...]


def _ada_mod(c, w_ada, b_ada):
    bsz = c.shape[0]
    n = w_ada.shape[1]
    tn = ADA_TN
    return pl.pallas_call(
        _ada_kernel,
        name="ada_mod",
        grid=(n // tn,),
        in_specs=[pl.BlockSpec((bsz, D_MODEL), lambda j: (0, 0)),
                  pl.BlockSpec((D_MODEL, tn), lambda j: (0, j)),
                  pl.BlockSpec((1, tn), lambda j: (0, j))],
        out_specs=pl.BlockSpec((bsz, tn), lambda j: (0, j)),
        out_shape=jax.ShapeDtypeStruct((bsz, n), F32),
        compiler_params=_vmem("ada_mod"),
    )(c, w_ada, b_ada.reshape(1, n))


def _serpentine(i, j, n):
    return jnp.where(i % 2 == 0, j, n - 1 - j)


def _inproj_kernel(x_ref, sh_ref, sc_ref, w_ref, wg_ref, ws_ref, o_ref, os_ref, h_ref, *, n_head,
                   n_tiles):
    @pl.when(pl.program_id(1) == 0)
    def _():
        h = x_ref[...] * (1.0 + sc_ref[...]) + sh_ref[...]
        h_ref[...] = h.astype(BF16)
        os_ref[...] = jnp.dot(h_ref[...], ws_ref[...], preferred_element_type=F32)

    col = _serpentine(pl.program_id(0), pl.program_id(1), n_tiles)

    @pl.when(col < n_head)
    def _():
        o_ref[...] = jnp.dot(h_ref[...], w_ref[...], preferred_element_type=F32)

    @pl.when(col >= n_head)
    def _():
        o_ref[...] = jnp.dot(h_ref[...], wg_ref[...], preferred_element_type=F32)


def _in_proj(x2, mod3, w_all, w_gate, w_small, seq):
    t = x2.shape[0]
    tm, tn = INPROJ_TM, INPROJ_TN
    per_b = seq // tm
    n_head, n_gate, n_tiles = N_HEAD // tn, N_GATE // tn, N_MAIN // tn

    def col(i, j):
        return _serpentine(i, j, n_tiles)

    return pl.pallas_call(
        functools.partial(_inproj_kernel, n_head=n_head, n_tiles=n_tiles),
        name="in_proj",
        grid=(t // tm, n_tiles),
        in_specs=[pl.BlockSpec((tm, D_MODEL), lambda i, j: (i, 0)),
                  pl.BlockSpec((None, 1, D_MODEL), lambda i, j: (i // per_b, 0, 0)),
                  pl.BlockSpec((None, 1, D_MODEL), lambda i, j: (i // per_b, 0, 1)),
                  pl.BlockSpec((D_MODEL, tn), lambda i, j: (0, jnp.minimum(col(i, j), n_head - 1))),
                  pl.BlockSpec((D_MODEL, tn), lambda i, j: (0, jnp.maximum(col(i, j) - n_head, 0))),
                  pl.BlockSpec((D_MODEL, N_SMALL), lambda i, j: (0, 0))],
        out_specs=[pl.BlockSpec((tm, tn), lambda i, j: (i, (col(i, j) + n_gate) % n_tiles)),
                   pl.BlockSpec((tm, N_SMALL), lambda i, j: (i, 0))],
        out_shape=[jax.ShapeDtypeStruct((t, N_MAIN), F32),
                   jax.ShapeDtypeStruct((t, N_SMALL), F32)],
        scratch_shapes=[pltpu.VMEM((tm, D_MODEL), BF16)],
        compiler_params=_vmem("in_proj"),
    )(x2, mod3, mod3, w_all, w_gate, w_small)


def _moba_kernel(rel_ref, bko_ref, bkp_ref, q_ref, k_ref, v_ref, o_ref,
                 bias_own, bias_prev, kb_ref, vt_ref, *, nb):
    h = pl.program_id(0)
    blk = MOBA_BLOCK
    inv_scale = HEAD_DIM ** 0.5
    scale_log2e = HEAD_DIM ** -0.5 * math.log2(math.e)

    @pl.when(pl.program_id(1) == 0)
    def _():
        bo = bko_ref[...]
        bp = bkp_ref[...]
        acc_o = jnp.zeros((blk, blk), F32)
        acc_p = jnp.zeros((blk, blk), F32)
        for kk in range(REL_BUCKETS):
            val = rel_ref[kk, h] * inv_scale
            acc_o = jnp.where(bo == kk, val, acc_o)
            acc_p = jnp.where(bp == kk, val, acc_p)
        bias_own[...] = acc_o
        bias_prev[...] = acc_p

    bias_far = rel_ref[REL_BUCKETS - 1, h] * inv_scale
    kf = k_ref[...]
    kmean = jnp.mean(kf.reshape(nb, blk, HEAD_DIM), axis=1)
    kb_ref[...] = kf.astype(BF16)
    nt_dims = (((1,), (1,)), ((), ()))
    eye = (lax.broadcasted_iota(jnp.int32, (HEAD_DIM, HEAD_DIM), 0)
           == lax.broadcasted_iota(jnp.int32, (HEAD_DIM, HEAD_DIM), 1)).astype(BF16)
    vt_ref[:HEAD_DIM, :] = lax.dot_general(eye, v_ref[...].astype(BF16), nt_dims,
                                           preferred_element_type=F32).astype(BF16)
    pad_rows = vt_ref.shape[0] - HEAD_DIM
    vt_ref[HEAD_DIM:, :] = (lax.broadcasted_iota(jnp.int32, (pad_rows, vt_ref.shape[1]), 0)
                            == 0).astype(BF16)
    causal = (lax.broadcasted_iota(jnp.int32, (blk, blk), 0)
              <= lax.broadcasted_iota(jnp.int32, (blk, blk), 1))

    def scores(i):
        qi = q_ref[i * blk:(i + 1) * blk, :]
        qb = qi.astype(BF16)
        sel = None
        if i > MOBA_TOPK:
            route = lax.dot_general(kmean, qi, nt_dims, precision=lax.Precision.HIGHEST,
                                    preferred_element_type=F32)
            rc = [route[n:n + 1, :] for n in range(i)]
            sel = []
            for n in range(i):
                rank = jnp.zeros((1, blk), jnp.int32)
                for m in range(i):
                    if m == n:
                        continue
                    beats = (rc[m] >= rc[n]) if m < n else (rc[m] > rc[n])
                    rank = rank + beats.astype(jnp.int32)
                sel.append(rank < MOBA_TOPK)
        t_list = []
        for n in range(i + 1):
            t = lax.dot_general(kb_ref[n * blk:(n + 1) * blk, :], qb, nt_dims,
                                preferred_element_type=F32)
            if n == i:
                t = jnp.where(causal, t + bias_own[...], NEG_INF)
            else:
                if n == i - 1:
                    t = t + bias_prev[...]
                if sel is not None:
                    t = jnp.where(sel[n], t, NEG_INF)
            t_list.append(t)
        return t_list

    t_next = scores(0)
    for i in range(nb):
        t_list = t_next
        if i + 1 < nb:
            t_next = scores(i + 1)
        n_far = max(i - 1, 0)
        m_run = jnp.max(t_list[n_far], axis=0, keepdims=True)
        for t in t_list[n_far + 1:]:
            m_run = jnp.maximum(m_run, jnp.max(t, axis=0, keepdims=True))
        if n_far:
            m_far = jnp.max(t_list[0], axis=0, keepdims=True)
            for t in t_list[1:n_far]:
                m_far = jnp.maximum(m_far, jnp.max(t, axis=0, keepdims=True))
            m_run = jnp.maximum(m_run, m_far + bias_far)
        acc = jnp.zeros((vt_ref.shape[0], blk), F32)
        for n, t in enumerate(t_list):
            offset = m_run - bias_far if n < n_far else m_run
            p = jnp.exp2((t - offset) * scale_log2e)
            acc = acc + jnp.dot(vt_ref[:, n * blk:(n + 1) * blk], p.astype(BF16),
                                preferred_element_type=F32)
        out = acc[:HEAD_DIM] / acc[HEAD_DIM:HEAD_DIM + 1]
        o_ref[i * blk:(i + 1) * blk, :] = out.T.astype(BF16)


def _moba(proj, rel_bias, bko, bkp, bsz, seq):
    nb = seq // MOBA_BLOCK
    qo, ko, vo = COL_MOBA_Q // HEAD_DIM, COL_MOBA_K // HEAD_DIM, COL_MOBA_V // HEAD_DIM
    blk = MOBA_BLOCK
    return pl.pallas_call(
        functools.partial(_moba_kernel, nb=nb),
        name="moba",
        grid=(MOBA_HEADS, bsz),
        in_specs=[pl.BlockSpec(memory_space=pltpu.SMEM),
                  pl.BlockSpec((blk, blk), lambda h, b: (0, 0)),
                  pl.BlockSpec((blk, blk), lambda h, b: (0, 0)),
                  pl.BlockSpec((seq, HEAD_DIM), lambda h, b: (b, qo + h)),
                  pl.BlockSpec((seq, HEAD_DIM), lambda h, b: (b, ko + h)),
                  pl.BlockSpec((seq, HEAD_DIM), lambda h, b: (b, vo + h))],
        out_specs=pl.BlockSpec((seq, HEAD_DIM), lambda h, b: (b, h)),
        out_shape=jax.ShapeDtypeStruct((bsz * seq, MOBA_W), BF16),
        scratch_shapes=[pltpu.VMEM((blk, blk), F32), pltpu.VMEM((blk, blk), F32),
                        pltpu.VMEM((seq, HEAD_DIM), BF16),
                        pltpu.VMEM((HEAD_DIM + BF16_SUBLANES, seq), BF16)],
        compiler_params=_vmem("moba"),
    )(rel_bias, bko, bkp, proj, proj, proj)


def _conv_silu(xh, w):
    assert GDN_CONV == 4
    x1 = pltpu.roll(xh, 1, axis=0)
    u = xh * w[1:2, :] + x1 * w[0:1, :]
    acc = (xh * w[3:4, :] + x1 * w[2:3, :] + pltpu.roll(u, 2, axis=0))[SUBLANES:, :]
    return _silu(acc)


def _l2norm(x):
    return x * lax.rsqrt(jnp.sum(x * x, axis=-1, keepdims=True) + RMS_EPS)


def _softplus(x):
    return jnp.maximum(x, 0.0) + jnp.log1p(jnp.exp(-jnp.abs(x)))


ELIM_BLOCK = 8
MERGE_LEVELS = (GDN_CHUNK // ELIM_BLOCK).bit_length() - 1


def _block_diag_inverse(lpair):
    n = lpair.shape[0]
    nv = n // SUBLANES
    vpb = ELIM_BLOCK // SUBLANES
    rid = lax.broadcasted_iota(jnp.int32, (SUBLANES, 2 * n), 0)
    lid = lax.broadcasted_iota(jnp.int32, (SUBLANES, 2 * n), 1)
    t_rows = [(lid % n == rid + v * SUBLANES).astype(F32) for v in range(nv)]
    l_rows = [lpair[v * SUBLANES:(v + 1) * SUBLANES, :] for v in range(nv)]
    for m in range(n - 1):
        v0, s0 = divmod(m, SUBLANES)
        v_end = (v0 // vpb + 1) * vpb
        row = t_rows[v0][s0:s0 + 1, :]
        col = (lid // n) * n + m
        for v in range(v0 if s0 < SUBLANES - 1 else v0 + 1, v_end):
            t_rows[v] = t_rows[v] - jnp.take_along_axis(l_rows[v], col, axis=1) * row
    return jnp.concatenate(t_rows, axis=0)


def _lane_block_diag(pair):
    first = lax.broadcasted_iota(jnp.int32, pair.shape, 1) < pair.shape[1] // 2
    zero = jnp.zeros_like(pair)
    return jnp.concatenate([jnp.where(first, pair, zero), jnp.where(first, zero, pair)], axis=0)


def _merge_lower_products(lpair, tpair, k):
    n = lpair.shape[0]
    ri = lax.broadcasted_iota(jnp.int32, lpair.shape, 0)
    ci = lax.broadcasted_iota(jnp.int32, lpair.shape, 1) % n
    off = (ri // (2 * k) == ci // (2 * k)) & (ri // k > ci // k)
    lk = jnp.where(off, lpair, 0.0).astype(BF16)
    return jnp.dot(lk, _lane_block_diag(tpair.astype(BF16)), preferred_element_type=F32)


def _merge_apply(tpair, lt):
    return tpair - jnp.dot(tpair.astype(BF16), _lane_block_diag(lt), preferred_element_type=F32)


PIPELINE_UNROLL = 4


def _software_pipeline(stages, n):
    ns = len(stages)

    def run(it, lo, hi):
        conts = [stages[s](it - s) for s in reversed(range(lo, hi))]
        for cont in conts:
            if cont is not None:
                cont()

    for it in range(ns - 1):
        run(it, 0, it + 1)

    steady = n - (ns - 1)
    assert steady % PIPELINE_UNROLL == 0, (n, ns)

    def body(k, carry):
        first = ns - 1 + k * PIPELINE_UNROLL
        for u in range(PIPELINE_UNROLL):
            run(first + u, 0, ns)
        return carry

    lax.fori_loop(0, steady // PIPELINE_UNROLL, body, 0)
    for it in range(n, n + ns - 1):
        run(it, it - n + 1, ns)


def _lane_pick(x, lane, idx):
    return jnp.sum(jnp.where(lane == idx, x, 0.0), axis=-1, keepdims=True)


def _gdn_kernel(alog_ref, dtb_ref, gp_ref, nw_ref, cwq_ref, cwk_ref, cwv_ref,
                q_ref, k_ref, v_ref, z_ref, sm_ref, rw_ref, y_ref,
                kb_s, qn_s, kdp_s, rhs_s, qd_s, gcb_s, betab_s, gcr_s, gram_s, lm_s, lhs2_s,
                sol_s, mp_s, n_s, r_s, st_s, *merge_s, nchunk):
    tl_s, lt_s = merge_s[:MERGE_LEVELS + 1], merge_s[MERGE_LEVELS + 1:]
    hq = pl.program_id(1)
    c64 = GDN_CHUNK
    hd = HEAD_DIM
    heads = range(2)

    tri_u = (lax.broadcasted_iota(jnp.int32, (c64, c64), 0)
             <= lax.broadcasted_iota(jnp.int32, (c64, c64), 1)).astype(F32)
    gc_rows = []
    for j in heads:
        hv = 2 * hq + j
        a_neg_r = -jnp.exp(jnp.full((nchunk, c64), alog_ref[hv], F32))
        g_row = a_neg_r * _softplus(rw_ref[2 + j] + dtb_ref[hv])
        gc_rows.append(jnp.dot(g_row, tri_u, precision=lax.Precision.HIGHEST,
                               preferred_element_type=F32))
    gcr_s[...] = jnp.concatenate(gc_rows, axis=1)

    rows = lax.broadcasted_iota(jnp.int32, (c64, 2 * c64), 0)
    cols = lax.broadcasted_iota(jnp.int32, (c64, 2 * c64), 1) % c64
    tril = rows >= cols
    strict = rows > cols
    eye2 = (lax.broadcasted_iota(jnp.int32, (2 * c64, 2 * c64), 0)
            == lax.broadcasted_iota(jnp.int32, (2 * c64, 2 * c64), 1)).astype(BF16)
    nt_dims = (((1,), (1,)), ((), ()))

    def rows_of(c):
        start = c * c64
        return pl.ds(start if isinstance(c, int) else pl.multiple_of(start, c64), c64)

    def with_halo(x_ref, c):
        if isinstance(c, int) and c == 0:
            return jnp.concatenate([jnp.zeros((SUBLANES, x_ref.shape[1]), F32), x_ref[:c64, :]],
                                   axis=0)
        start = c * c64 - SUBLANES
        if not isinstance(c, int):
            start = pl.multiple_of(start, SUBLANES)
        return x_ref[pl.ds(start, c64 + SUBLANES), :]

    lane = lax.broadcasted_iota(jnp.int32, (c64, N_SMALL), 1)
    first = lane < c64
    pos = lax.broadcasted_iota(jnp.int32, (c64, N_SMALL), 0)

    def stage_prep(c):
        r = rows_of(c)
        qn = _l2norm(_conv_silu(with_halo(q_ref, c), cwq_ref[...])) * (hd ** -0.5)
        kn = _l2norm(_conv_silu(with_halo(k_ref, c), cwk_ref[...]))
        vc = _conv_silu(with_halo(v_ref, c), cwv_ref[...])
        qn_s[r, :] = qn.astype(BF16)
        kb_s[r, :] = kn.astype(BF16)
        sm = sm_ref[r, :]
        sig_all = _sigmoid(sm)
        gc_all = -jnp.exp(gp_ref[0:1, :]) * _softplus(sm + gp_ref[1:2, :])
        sft = 1
        while sft < c64:
            gc_all = gc_all + jnp.where(pos >= sft, pltpu.roll(gc_all, sft, axis=0), 0.0)
            sft *= 2
        rest_all = gc_all[c64 - 1:, :] - gc_all
        betas, gc_cols = [], []
        for j in heads:
            hv = 2 * hq + j
            beta = _lane_pick(sig_all, lane, hv)
            gc_col = _lane_pick(gc_all, lane, GDN_V_HEADS + hv)
            rest_col = _lane_pick(rest_all, lane, GDN_V_HEADS + hv)
            eg = jnp.exp(gc_col)
            rhs_s[j, r, :hd] = (vc[:, j * hd:(j + 1) * hd] * beta).astype(BF16)
            rhs_s[j, r, hd:] = (kn * (beta * eg)).astype(BF16)
            qd_s[j, r, :] = (qn * eg).astype(BF16)
            kdp_s[c, j * c64:(j + 1) * c64, :] = (kn * jnp.exp(rest_col)).astype(BF16)
            betas.append(beta)
            gc_cols.append(gc_col)
        gcb_s[r, :] = jnp.where(first, gc_cols[0], gc_cols[1])
        betab_s[r, :] = jnp.where(first, betas[0], betas[1])

    def stage_gram(c):
        r = rows_of(c)
        kb = kb_s[r, :]
        kq = jnp.concatenate([kb, qn_s[r, :]], axis=0)
        gram = lax.dot_general(kq, jnp.concatenate([kb, kb], axis=0), nt_dims,
                               preferred_element_type=F32)
        kd_t = lax.dot_general(eye2, kdp_s[c], nt_dims, preferred_element_type=F32)

        def finish():
            gram_s[c] = gram
            lhs2_s[c, :2 * c64, :] = kd_t.astype(BF16)

        return finish

    def stage_factor(c):
        r = rows_of(c)
        dec = jnp.exp(jnp.where(tril, gcb_s[r, :] - gcr_s[pl.ds(c, 1), :], NEG_INF))
        lpair = jnp.where(strict, gram_s[c, :c64, :] * dec, 0.0) * betab_s[r, :]
        lm_s[r, :] = lpair
        lhs2_s[c, 2 * c64:, :] = (gram_s[c, c64:, :] * dec).astype(BF16)
        tl_s[0][r, :] = _block_diag_inverse(lpair)

    def stage_merge_products(level):
        def stage(c):
            r = rows_of(c)
            lt = _merge_lower_products(lm_s[r, :], tl_s[level][r, :], ELIM_BLOCK << level)

            def finish():
                lt_s[level][r, :] = lt.astype(BF16)

            return finish
        return stage

    def stage_merge_apply(level):
        def stage(c):
            r = rows_of(c)
            merged = _merge_apply(tl_s[level][r, :], lt_s[level][r, :])

            def finish():
                tl_s[level + 1][r, :] = merged

            return finish
        return stage

    def stage_solve(c):
        r = rows_of(c)
        zero = jnp.zeros((c64, 2 * hd), BF16)
        rhs_bd = jnp.concatenate([jnp.concatenate([rhs_s[0, r, :], zero], axis=1),
                                  jnp.concatenate([zero, rhs_s[1, r, :]], axis=1)], axis=0)
        sol = jnp.dot(tl_s[MERGE_LEVELS][r, :].astype(BF16), rhs_bd,
                      preferred_element_type=F32)

        def finish():
            sol_s[r, :] = sol.astype(BF16)

        return finish

    def stage_fold(c):
        r = rows_of(c)
        sol = sol_s[r, :]
        out = jnp.dot(lhs2_s[c], _lane_block_diag(sol), preferred_element_type=F32)

        def finish():
            for j in heads:
                u_col, w_col = 2 * j * hd, (2 * j + 1) * hd
                n_s[j, c] = out[:2 * c64, u_col:u_col + hd]
                mp_s[j, c, :2 * c64, :] = out[:2 * c64, w_col:w_col + hd].astype(BF16)
                mp_s[j, c, 2 * c64:, :] = (qd_s[j, r, :].astype(F32)
                                           - out[2 * c64:, w_col:w_col + hd]).astype(BF16)
                r_s[j, r, :] = out[2 * c64:, u_col:u_col + hd]

        return finish

    st_s[...] = jnp.zeros_like(st_s)
    nw = nw_ref[...]

    def stage_state(c):
        r = rows_of(c)
        sts = [st_s[j] for j in heads]
        outs = [jnp.dot(mp_s[j, c], sts[j].astype(BF16), preferred_element_type=F32)
                for j in heads]

        def finish():
            for j in heads:
                end = (j + 1) * c64
                gl = gcr_s[pl.ds(c, 1), end - 1:end]
                st_s[j] = sts[j] * jnp.exp(gl) - outs[j][:2 * c64] + n_s[j, c]
                r_s[j, r, :] = r_s[j, r, :] + outs[j][2 * c64:]

        return finish

    def stage_norm(c):
        r = rows_of(c)
        for j in heads:
            o = r_s[j, r, :]
            og = (o * lax.rsqrt(jnp.mean(o * o, axis=-1, keepdims=True) + RMS_EPS)
                  * nw * _silu(z_ref[r, j * hd:(j + 1) * hd]))
            y_ref[r, j * hd:(j + 1) * hd] = og.astype(BF16)

    merges = [stage(lv) for lv in range(MERGE_LEVELS)
              for stage in (stage_merge_products, stage_merge_apply)]
    _software_pipeline([stage_prep, stage_gram, stage_factor] + merges
                       + [stage_solve, stage_fold, stage_state, stage_norm], nchunk)


def _gdn(proj, small, rows, conv_w, a_log, dt_bias, norm_w, bsz, seq):
    nchunk = seq // GDN_CHUNK
    hd = HEAD_DIM
    c64 = GDN_CHUNK
    qo, ko = COL_GDN_Q // hd, COL_GDN_K // hd
    vo, zo = COL_GDN_V // (2 * hd), COL_GDN_Z // (2 * hd)
    cvo = (2 * GDN_QK_W) // (2 * hd)
    smem = pl.BlockSpec(memory_space=pltpu.SMEM)
    pad = (GDN_V_HEADS, N_SMALL - 2 * GDN_V_HEADS)
    gate_params = jnp.stack([jnp.pad(a_log, pad), jnp.pad(dt_bias, pad)])
    return pl.pallas_call(
        functools.partial(_gdn_kernel, nchunk=nchunk),
        name="gdn",
        grid=(bsz, GDN_QK_HEADS),
        in_specs=[smem, smem,
                  pl.BlockSpec((2, N_SMALL), lambda b, h: (0, 0)),
                  pl.BlockSpec((1, hd), lambda b, h: (0, 0)),
                  pl.BlockSpec((GDN_CONV, hd), lambda b, h: (0, h)),
                  pl.BlockSpec((GDN_CONV, hd), lambda b, h: (0, GDN_QK_HEADS + h)),
                  pl.BlockSpec((GDN_CONV, 2 * hd), lambda b, h: (0, cvo + h)),
                  pl.BlockSpec((seq, hd), lambda b, h: (b, qo + h)),
                  pl.BlockSpec((seq, hd), lambda b, h: (b, ko + h)),
                  pl.BlockSpec((seq, 2 * hd), lambda b, h: (b, vo + h)),
                  pl.BlockSpec((seq, 2 * hd), lambda b, h: (b, zo + h)),
                  pl.BlockSpec((seq, N_SMALL), lambda b, h: (b, 0)),
                  pl.BlockSpec((None, None, 4, nchunk, GDN_CHUNK), lambda b, h: (b, h, 0, 0, 0))],
        out_specs=pl.BlockSpec((seq, 2 * hd), lambda b, h: (b, h)),
        out_shape=jax.ShapeDtypeStruct((bsz * seq, GDN_V_W), BF16),
        scratch_shapes=[pltpu.VMEM((seq, hd), BF16),
                        pltpu.VMEM((seq, hd), BF16),
                        pltpu.VMEM((nchunk, 2 * c64, hd), BF16),
                        pltpu.VMEM((2, seq, 2 * hd), BF16),
                        pltpu.VMEM((2, seq, hd), BF16),
                        pltpu.VMEM((seq, 2 * c64), F32),
                        pltpu.VMEM((seq, 2 * c64), F32),
                        pltpu.VMEM((nchunk, 2 * c64), F32),
                        pltpu.VMEM((nchunk, 2 * c64, 2 * c64), F32),
                        pltpu.VMEM((seq, 2 * c64), F32),
                        pltpu.VMEM((nchunk, 3 * c64, hd), BF16),
                        pltpu.VMEM((seq, 4 * hd), BF16),
                        pltpu.VMEM((2, nchunk, 3 * c64, hd), BF16),
                        pltpu.VMEM((2, nchunk, hd, hd), F32),
                        pltpu.VMEM((2, seq, hd), F32),
                        pltpu.VMEM((2, hd, hd), F32)]
                       + [pltpu.VMEM((seq, 2 * c64), F32)] * (MERGE_LEVELS + 1)
                       + [pltpu.VMEM((seq, 2 * c64), BF16)] * MERGE_LEVELS,
        compiler_params=_vmem("gdn"),
    )(a_log, dt_bias, gate_params, norm_w, conv_w, conv_w, conv_w, proj, proj, proj, proj, small, rows)


def _merge_kernel(ya_ref, yb_ref, ga_ref, gb_ref, x_ref, g1_ref,
                  lng_ref, lnb_ref, wpm_ref, wpg_ref, wo_ref, x1_ref):
    pa = jnp.dot(ya_ref[...], wpm_ref[...], preferred_element_type=F32)
    pb = jnp.dot(yb_ref[...], wpg_ref[...], preferred_element_type=F32)
    merged = _sigmoid(ga_ref[...]) * pa + _sigmoid(gb_ref[...]) * pb
    y = jnp.dot(merged.astype(BF16), wo_ref[...], preferred_element_type=F32)
    x1_ref[...] = _layer_norm(DEEPNORM_ALPHA * x_ref[...] + g1_ref[...] * y,
                              lng_ref[...], lnb_ref[...])


def _merge(ya, yb, proj, x2, mod3, ln_g, ln_b, wpm, wpg, wo, seq):
    t = x2.shape[0]
    tm = MERGE_TM
    per_b = seq // tm
    d = D_MODEL

    def modspec(k):
        return pl.BlockSpec((None, 1, d), lambda i: (i // per_b, 0, k))

    def const(shape):
        return pl.BlockSpec(shape, lambda i: (0, 0), pipeline_mode=pl.Buffered(1))

    return pl.pallas_call(
        _merge_kernel,
        name="merge",
        grid=(t // tm,),
        in_specs=[pl.BlockSpec((tm, MOBA_W), lambda i: (i, 0)),
                  pl.BlockSpec((tm, GDN_V_W), lambda i: (i, 0)),
                  pl.BlockSpec((tm, d), lambda i: (i, COL_GATE_A // d)),
                  pl.BlockSpec((tm, d), lambda i: (i, COL_GATE_B // d)),
                  pl.BlockSpec((tm, d), lambda i: (i, 0)),
                  modspec(2),
                  const((1, d)), const((1, d)),
                  const((MOBA_W, d)), const((GDN_V_W, d)), const((d, d))],
        out_specs=pl.BlockSpec((tm, d), lambda i: (i, 0)),
        out_shape=jax.ShapeDtypeStruct((t, d), F32),
        compiler_params=_vmem("merge"),
    )(ya, yb, proj, proj, x2, mod3, ln_g, ln_b, wpm, wpg, wo)


def _ffn_kernel(x1_ref, sh2_ref, sc2_ref, g2_ref, lng_ref, lnb_ref, wg_ref, wu_ref, wo_ref, o_ref,
                acc_ref, h_ref):
    f = pl.program_id(1)

    @pl.when(f == 0)
    def _():
        acc_ref[...] = jnp.zeros_like(acc_ref)
        h_ref[...] = (x1_ref[...] * (1.0 + sc2_ref[...]) + sh2_ref[...]).astype(BF16)

    h = h_ref[...]
    gate = jnp.dot(h, wg_ref[...], preferred_element_type=F32)
    up = jnp.dot(h, wu_ref[...], preferred_element_type=F32)
    act = (_silu(gate) * up).astype(BF16)
    acc_ref[...] += jnp.dot(act, wo_ref[...], preferred_element_type=F32)

    @pl.when(f == pl.num_programs(1) - 1)
    def _():
        r = DEEPNORM_ALPHA * x1_ref[...] + g2_ref[...] * acc_ref[...]
        o_ref[...] = _layer_norm(r, lng_ref[...], lnb_ref[...])


def _ffn(x1, mod3, ln_g, ln_b, w_in, w_out, seq):
    t = x1.shape[0]
    tm, tf = FFN_TM, FFN_TF
    per_b = seq // tm
    d = D_MODEL
    nf = D_FF // tf
    return pl.pallas_call(
        _ffn_kernel,
        name="ffn",
        grid=(t // tm, nf),
        in_specs=[pl.BlockSpec((tm, d), lambda i, f: (i, 0)),
                  pl.BlockSpec((None, 1, d), lambda i, f: (i // per_b, 0, 3)),
                  pl.BlockSpec((None, 1, d), lambda i, f: (i // per_b, 0, 4)),
                  pl.BlockSpec((None, 1, d), lambda i, f: (i // per_b, 0, 5)),
                  pl.BlockSpec((1, d), lambda i, f: (0, 0)),
                  pl.BlockSpec((1, d), lambda i, f: (0, 0)),
                  pl.BlockSpec((d, tf), lambda i, f: (0, f)),
                  pl.BlockSpec((d, tf), lambda i, f: (0, nf + f)),
                  pl.BlockSpec((tf, d), lambda i, f: (f, 0))],
        out_specs=pl.BlockSpec((tm, d), lambda i, f: (i, 0)),
        out_shape=jax.ShapeDtypeStruct((t, d), F32),
        scratch_shapes=[pltpu.VMEM((tm, d), F32), pltpu.VMEM((tm, d), BF16)],
        compiler_params=_vmem("ffn"),
    )(x1, mod3, mod3, mod3, ln_g, ln_b, w_in, w_in, w_out)


def _rel_bucket(dist):
    max_exact = REL_BUCKETS // 2
    n = jnp.maximum(dist, 0)
    nf = jnp.maximum(n, 1).astype(F32)
    large = max_exact + (jnp.log(nf / max_exact) / math.log(REL_MAX_DIST / max_exact)
                         * (REL_BUCKETS - max_exact)).astype(jnp.int32)
    large = jnp.minimum(large, REL_BUCKETS - 1)
    return jnp.where(n < max_exact, n, large)


def _layer(x, c, w_ada, b_ada, w_in, conv_w, a_log, dt_bias, gdn_norm_w, rel_bias,
           w_proj_moba, w_proj_gdn, w_out, ln1_g, ln1_b, w_ffn_in, w_ffn_out, ln2_g, ln2_b):
    bsz, seq, d = x.shape
    t = bsz * seq
    x2 = x.reshape(t, d)

    mod = _ada_mod(c, w_ada, b_ada)
    mod3 = mod.reshape(bsz, 1, 6 * d)

    w_all = w_in.astype(BF16)
    n_gates = 2 * GDN_V_HEADS
    w_gate = w_all[:, N_HEAD + n_gates:]
    w_small = jnp.pad(w_all[:, N_HEAD:N_HEAD + n_gates], ((0, 0), (0, N_SMALL - n_gates)))

    proj, small = _in_proj(x2, mod3, w_all, w_gate, w_small, seq)

    ii = jnp.arange(MOBA_BLOCK, dtype=jnp.int32)
    dist = ii[None, :] - ii[:, None]
    bko = _rel_bucket(dist)
    bkp = _rel_bucket(dist + MOBA_BLOCK)
    ya = _moba(proj, rel_bias, bko, bkp, bsz, seq)

    nchunk = seq // GDN_CHUNK
    sm_t = small[:, :2 * GDN_V_HEADS].reshape(bsz, seq, 2, GDN_QK_HEADS, 2)
    rows = sm_t.transpose(0, 3, 2, 4, 1).reshape(bsz, GDN_QK_HEADS, 4, nchunk, GDN_CHUNK)
    yb = _gdn(proj, small, rows, conv_w, a_log, dt_bias, gdn_norm_w.reshape(1, HEAD_DIM),
              bsz, seq)

    x1 = _merge(ya, yb, proj, x2, mod3, ln1_g.reshape(1, d), ln1_b.reshape(1, d),
                w_proj_moba.astype(BF16), w_proj_gdn.astype(BF16), w_out.astype(BF16), seq)
    out = _ffn(x1, mod3, ln2_g.reshape(1, d), ln2_b.reshape(1, d),
               w_ffn_in.astype(BF16), w_ffn_out.astype(BF16), seq)
    return out.reshape(bsz, seq, d)


def kernel(x, c, w_ada, b_ada, w_in, conv_w, a_log, dt_bias, gdn_norm_w, rel_bias, w_proj_moba,
           w_proj_gdn, w_out, ln1_g, ln1_b, w_ffn_in, w_ffn_out, ln2_g, ln2_b):
    depth = w_ada.shape[0]
    for l in range(depth):
        x = _layer(x, c, w_ada[l], b_ada[l], w_in[l], conv_w[l], a_log[l], dt_bias[l],
                   gdn_norm_w[l], rel_bias, w_proj_moba[l], w_proj_gdn[l], w_out[l],
                   ln1_g[l], ln1_b[l], w_ffn_in[l], w_ffn_out[l], ln2_g[l], ln2_b[l])
    return x
```
